```python
import math
import jax
import jax.numpy as jnp
from jax import lax
import numpy as np

D_MODEL = 1024
BATCH = 2
SEQ = 16384
DEPTH = 2

HEAD_DIM = 64
ATTN_HQ = 8
ATTN_HKV = 2
ATTN_GROUP = ATTN_HQ // ATTN_HKV
WINDOW = 128
ATTN_BLOCK = 128
ROT_DIM = HEAD_DIM // 4
ROPE_THETA = 500000.0
DN_H = 4
DN_DK = 64
DN_DV = 64
DN_CONV = 4
DN_CHUNK = 64
CV_GROUPS = 4
CV_C = CV_GROUPS * HEAD_DIM
CV_K = 31
ATTN_QW = ATTN_HQ * HEAD_DIM
ATTN_KVW = ATTN_HKV * HEAD_DIM
DN_KW = DN_H * DN_DK
DN_VW = DN_H * DN_DV
D_MIX = ATTN_QW + DN_VW + CV_C
IN_SIZES = (ATTN_QW, ATTN_KVW, ATTN_KVW, DN_KW, DN_KW, DN_VW, DN_H, DN_H, DN_VW, 2 * CV_C)
N_IN = ATTN_QW + 2 * ATTN_KVW + 2 * DN_KW + 2 * DN_VW + 2 * DN_H + 2 * CV_C
D_FF = 2816
N_EXPERTS = 8
TOP_K = 2
D_FF_EXPERT = 1024
N_DENSE = (DEPTH + 1) // 2
N_MOE = DEPTH // 2
EPS = 1e-6

kernel_name = 'hymba_style_swa_deltanet_conformer_moe_adaln'


def rmsnorm(x, g):
    xf = x.astype(jnp.float32)
    y = xf * lax.rsqrt(jnp.mean(xf * xf, axis=-1, keepdims=True) + EPS)
    return (y * g.astype(jnp.float32)).astype(x.dtype)


def layernorm(x, g, b):
    xf = x.astype(jnp.float32)
    mu = jnp.mean(xf, axis=-1, keepdims=True)
    xc = xf - mu
    y = xc * lax.rsqrt(jnp.mean(xc * xc, axis=-1, keepdims=True) + EPS)
    return (y * g.astype(jnp.float32) + b.astype(jnp.float32)).astype(x.dtype)


def l2norm(x):
    return x * lax.rsqrt(jnp.sum(x * x, axis=-1, keepdims=True) + EPS)


def causal_dwconv(x, w):
    K, C = w.shape
    return lax.conv_general_dilated(x, w[:, None, :].astype(x.dtype), window_strides=(1,), padding=[(K - 1, 0)],
                                    dimension_numbers=('NWC', 'WIO', 'NWC'), feature_group_count=C)


def partial_rope(x, cos, sin):
    half = ROT_DIM // 2
    x1 = x[..., :half].astype(jnp.float32)
    x2 = x[..., half:ROT_DIM].astype(jnp.float32)
    r1 = (x1 * cos - x2 * sin).astype(x.dtype)
    r2 = (x2 * cos + x1 * sin).astype(x.dtype)
    return jnp.concatenate([r1, r2, x[..., ROT_DIM:]], axis=-1)


def swa_attention(q, k, v, sinks):
    B, S = q.shape[0], q.shape[1]
    NB = S // ATTN_BLOCK
    qb = q.reshape(B, NB, ATTN_BLOCK, ATTN_HKV, ATTN_GROUP, HEAD_DIM)

    def band(t):
        tp = jnp.pad(t, ((0, 0), (ATTN_BLOCK, 0), (0, 0), (0, 0)))
        tb = tp.reshape(B, NB + 1, ATTN_BLOCK, ATTN_HKV, HEAD_DIM)
        return jnp.concatenate([tb[:, :-1], tb[:, 1:]], axis=2)

    kb, vb = band(k), band(v)
    s = jnp.einsum('bnqhgd,bnkhd->bnhgqk', qb, kb, preferred_element_type=jnp.float32) * (HEAD_DIM ** -0.5)
    qi = jnp.arange(ATTN_BLOCK)[:, None]
    kj = jnp.arange(2 * ATTN_BLOCK)[None, :]
    diff = qi + ATTN_BLOCK - kj
    band_ok = (diff >= 0) & (diff < WINDOW)
    key_pos = jnp.arange(NB)[:, None] * ATTN_BLOCK - ATTN_BLOCK + kj
    valid = band_ok[None] & (key_pos >= 0)[:, None, :]
    s = jnp.where(valid[None, :, None, None], s, -jnp.inf)
    sink = sinks.astype(jnp.float32).reshape(1, 1, ATTN_HKV, ATTN_GROUP, 1, 1)
    m = jnp.maximum(jnp.max(s, axis=-1, keepdims=True), sink)
    p = jnp.exp(s - m)
    p = p / (jnp.sum(p, axis=-1, keepdims=True) + jnp.exp(sink - m))
    o = jnp.einsum('bnhgqk,bnkhd->bnqhgd', p.astype(v.dtype), vb)
    return o.reshape(B, S, ATTN_QW)


def gated_delta_rule(q, k, v, g, beta):
    B, S, H, DK = q.shape
    DV = v.shape[-1]
    C = DN_CHUNK
    N = S // C
    q = q * (DK ** -0.5)

    def chunk(t):
        return t.reshape(B, N, C, H, t.shape[-1]).transpose(0, 3, 1, 2, 4)

    qc, kc, vc = chunk(q), chunk(k), chunk(v)
    gc = jnp.cumsum(g.reshape(B, N, C, H).transpose(0, 3, 1, 2), axis=-1)
    bc = beta.reshape(B, N, C, H).transpose(0, 3, 1, 2)
    tril_incl = jnp.tril(jnp.ones((C, C), dtype=bool))
    strict = jnp.tril(jnp.ones((C, C), dtype=bool), -1)
    decay = jnp.exp(jnp.where(tril_incl, gc[..., :, None] - gc[..., None, :], -jnp.inf))
    kbeta = kc * bc[..., None]
    L = jnp.where(strict, jnp.einsum('bhnid,bhnjd->bhnij', kbeta, kc) * decay, 0.0)
    eye = jnp.eye(C, dtype=jnp.float32)
    T = lax.linalg.triangular_solve(eye + L, jnp.broadcast_to(eye, L.shape), left_side=True, lower=True,
                                    unit_diagonal=True)
    u = jnp.einsum('bhnij,bhnjd->bhnid', T, vc * bc[..., None])
    w = jnp.einsum('bhnij,bhnjd->bhnid', T, kbeta * jnp.exp(gc)[..., None])
    a_intra = jnp.einsum('bhnid,bhnjd->bhnij', qc, kc) * decay
    qg = qc * jnp.exp(gc)[..., None]
    kg = kc * jnp.exp(gc[..., -1:] - gc)[..., None]
    glast = jnp.exp(gc[..., -1])

    def step(state, inp):
        q_i, k_i, u_i, w_i, a_i, gl = inp
        v_new = u_i - jnp.einsum('bhck,bhkv->bhcv', w_i, state)
        o = jnp.einsum('bhck,bhkv->bhcv', q_i, state) + jnp.einsum('bhij,bhjv->bhiv', a_i, v_new)
        state = state * gl[..., None, None] + jnp.einsum('bhck,bhcv->bhkv', k_i, v_new)
        return state, o

    xs = tuple(jnp.moveaxis(t, 2, 0) for t in (qg, kg, u, w, a_intra, glast))
    s0 = jnp.zeros((B, H, DK, DV), jnp.float32)
    _, o = lax.scan(step, s0, xs)
    o = jnp.moveaxis(o, 0, 2).transpose(0, 2, 3, 1, 4)
    return o.reshape(B, S, H, DV)


def token_mixer(h, cos, sin, w_in, sinks, dn_conv_w, dn_a_log, dn_dt_bias, dn_norm_g,
                cv_dw_w, cv_dw_b, cv_ln_g, cv_ln_b, w_out):
    B, S, _ = h.shape
    f32 = jnp.float32
    proj = h @ w_in
    idx = np.cumsum(IN_SIZES)[:-1].tolist()
    aq, ak, av, dq, dk, dv, db, da, dz, cv = jnp.split(proj, idx, axis=-1)

    q = partial_rope(aq.reshape(B, S, ATTN_HQ, HEAD_DIM), cos, sin)
    k = partial_rope(ak.reshape(B, S, ATTN_HKV, HEAD_DIM), cos, sin)
    v = av.reshape(B, S, ATTN_HKV, HEAD_DIM)
    y_attn = swa_attention(q, k, v, sinks)

    qkv = jax.nn.silu(causal_dwconv(jnp.concatenate([dq, dk, dv], axis=-1), dn_conv_w))
    dq, dk, dv = jnp.split(qkv, [DN_KW, 2 * DN_KW], axis=-1)
    qd = l2norm(dq.reshape(B, S, DN_H, DN_DK).astype(f32))
    kd = l2norm(dk.reshape(B, S, DN_H, DN_DK).astype(f32))
    vd = dv.reshape(B, S, DN_H, DN_DV).astype(f32)
    beta = jax.nn.sigmoid(db.astype(f32))
    g = -jnp.exp(dn_a_log.astype(f32)) * jax.nn.softplus(da.astype(f32) + dn_dt_bias.astype(f32))
    o = gated_delta_rule(qd, kd, vd, g, beta)
    o = o * lax.rsqrt(jnp.mean(o * o, axis=-1, keepdims=True) + EPS) * dn_norm_g.astype(f32)
    o = o * jax.nn.silu(dz.reshape(B, S, DN_H, DN_DV).astype(f32))
    y_dn = o.reshape(B, S, DN_VW).astype(h.dtype)

    ca, cb = jnp.split(cv, 2, axis=-1)
    u = ca * jax.nn.sigmoid(cb)
    u = causal_dwconv(u, cv_dw_w) + cv_dw_b
    y_cv = jax.nn.silu(layernorm(u, cv_ln_g, cv_ln_b))

    return jnp.concatenate([y_attn, y_dn, y_cv], axis=-1) @ w_out


def swiglu(h, w_gate, w_up, w_down):
    return (jax.nn.silu(h @ w_gate) * (h @ w_up)) @ w_down


def moe_ffn(h, router_w, router_b, w_gate, w_up, w_down):
    B, S, D = h.shape
    t = h.reshape(B * S, D)
    logits = (t @ router_w).astype(jnp.float32) + router_b.astype(jnp.float32)
    top_v, top_i = lax.top_k(logits, TOP_K)
    top_p = jax.nn.softmax(top_v, axis=-1)
    combine = jnp.sum(jax.nn.one_hot(top_i, N_EXPERTS, dtype=jnp.float32) * top_p[..., None], axis=1)
    out = jnp.zeros_like(t)
    for e in range(N_EXPERTS):
        out = out + combine[:, e:e + 1].astype(t.dtype) * swiglu(t, w_gate[e], w_up[e], w_down[e])
    return out.reshape(B, S, D)


def modulate(x, g, shift, scale):
    return rmsnorm(x, g) * (1.0 + scale[:, None, :]) + shift[:, None, :]


def setup_inputs(seed: int = 0) -> dict:
    key = jax.random.key(seed)
    ks = jax.random.split(key, 32)
    f32 = jnp.float32

    def nrm(k, shape, s):
        return jax.random.normal(k, shape, f32) * s

    def gain(k, shape):
        return 1.0 + 0.02 * jax.random.normal(k, shape, f32)

    dt = jnp.exp(jax.random.uniform(ks[10], (DEPTH, DN_H), f32, math.log(1e-3), math.log(1e-1)))
    return {
        'x': nrm(ks[0], (BATCH, SEQ, D_MODEL), 1.0),
        'c': nrm(ks[1], (BATCH, D_MODEL), 1.0),
        'positions': jnp.broadcast_to(jnp.arange(SEQ, dtype=jnp.int32), (BATCH, SEQ)),
        'ada_w': nrm(ks[2], (DEPTH, 2, D_MODEL, 3 * D_MODEL), D_MODEL ** -0.5),
        'ada_b': nrm(ks[3], (DEPTH, 2, 3 * D_MODEL), 0.01),
        'norm_mix_g': gain(ks[4], (DEPTH, D_MODEL)),
        'norm_ffn_g': gain(ks[5], (DEPTH, D_MODEL)),
        'w_in': nrm(ks[6], (DEPTH, D_MODEL, N_IN), D_MODEL ** -0.5),
        'attn_sinks': nrm(ks[7], (DEPTH, ATTN_HQ), 0.5),
        'dn_conv_w': nrm(ks[8], (DEPTH, DN_CONV, 2 * DN_KW + DN_VW), DN_CONV ** -0.5),
        'dn_a_log': jnp.log(jax.random.uniform(ks[9], (DEPTH, DN_H), f32, 1.0, 16.0)),
        'dn_dt_bias': jnp.log(jnp.expm1(dt)),
        'dn_norm_g': gain(ks[11], (DEPTH, DN_DV)),
        'cv_dw_w': nrm(ks[12], (DEPTH, CV_K, CV_C), CV_K ** -0.5),
        'cv_dw_b': nrm(ks[13], (DEPTH, CV_C), 0.01),
        'cv_ln_g': gain(ks[14], (DEPTH, CV_C)),
        'cv_ln_b': nrm(ks[15], (DEPTH, CV_C), 0.01),
        'w_out': nrm(ks[16], (DEPTH, D_MIX, D_MODEL), D_MIX ** -0.5),
        'ffn_w_gate': nrm(ks[17], (N_DENSE, D_MODEL, D_FF), D_MODEL ** -0.5),
        'ffn_w_up': nrm(ks[18], (N_DENSE, D_MODEL, D_FF), D_MODEL ** -0.5),
        'ffn_w_down': nrm(ks[19], (N_DENSE, D_FF, D_MODEL), D_FF ** -0.5),
        'router_w': nrm(ks[20], (N_MOE, D_MODEL, N_EXPERTS), D_MODEL ** -0.5),
        'router_b': nrm(ks[21], (N_MOE, N_EXPERTS), 0.01),
        'moe_w_gate': nrm(ks[22], (N_MOE, N_EXPERTS, D_MODEL, D_FF_EXPERT), D_MODEL ** -0.5),
        'moe_w_up': nrm(ks[23], (N_MOE, N_EXPERTS, D_MODEL, D_FF_EXPERT), D_MODEL ** -0.5),
        'moe_w_down': nrm(ks[24], (N_MOE, N_EXPERTS, D_FF_EXPERT, D_MODEL), D_FF_EXPERT ** -0.5),
        'final_norm_g': gain(ks[25], (D_MODEL,)),
    }


def reference(x, c, positions, ada_w, ada_b, norm_mix_g, norm_ffn_g, w_in, attn_sinks, dn_conv_w, dn_a_log,
              dn_dt_bias, dn_norm_g, cv_dw_w, cv_dw_b, cv_ln_g, cv_ln_b, w_out, ffn_w_gate, ffn_w_up, ffn_w_down,
              router_w, router_b, moe_w_gate, moe_w_up, moe_w_down, final_norm_g):
    inv_freq = ROPE_THETA ** (-jnp.arange(0, ROT_DIM, 2, dtype=jnp.float32) / ROT_DIM)
    ang = positions.astype(jnp.float32)[..., None] * inv_freq
    cos = jnp.cos(ang)[:, :, None, :]
    sin = jnp.sin(ang)[:, :, None, :]
    c_act = jax.nn.silu(c)
    for l in range(DEPTH):
        mod = jnp.einsum('bd,kde->kbe', c_act, ada_w[l]) + ada_b[l][:, None, :]
        sh1, sc1, gt1 = jnp.split(mod[0], 3, axis=-1)
        sh2, sc2, gt2 = jnp.split(mod[1], 3, axis=-1)
        h = modulate(x, norm_mix_g[l], sh1, sc1)
        y = token_mixer(h, cos, sin, w_in[l], attn_sinks[l], dn_conv_w[l], dn_a_log[l], dn_dt_bias[l],
                        dn_norm_g[l], cv_dw_w[l], cv_dw_b[l], cv_ln_g[l], cv_ln_b[l], w_out[l])
        x = x + gt1[:, None, :] * y
        h = modulate(x, norm_ffn_g[l], sh2, sc2)
        if l % 2 == 0:
            j = l // 2
            f = swiglu(h, ffn_w_gate[j], ffn_w_up[j], ffn_w_down[j])
        else:
            j = l // 2
            f = moe_ffn(h, router_w[j], router_b[j], moe_w_gate[j], moe_w_up[j], moe_w_down[j])
        x = x + gt2[:, None, :] * f
    return rmsnorm(x, final_norm_g)
```

```python
import functools

import numpy as np
import jax
import jax.numpy as jnp
from jax import lax
from jax.experimental import pallas as pl
from jax.experimental.pallas import tpu as pltpu

F32 = jnp.float32
BF16 = jnp.bfloat16
HIGHEST = lax.Precision.HIGHEST

HEAD_DIM = 64
ATTN_HQ = 8
ATTN_HKV = 2
ATTN_GROUP = ATTN_HQ // ATTN_HKV
ATTN_BLOCK = 128
ROT_DIM = HEAD_DIM // 4
ROPE_THETA = 500000.0
DN_H = 4
DN_D = 64
DN_CONV = 4
DN_CHUNK = 64
CV_C = 256
CV_K = 31
ATTN_QW = ATTN_HQ * HEAD_DIM
ATTN_KVW = ATTN_HKV * HEAD_DIM
ATTN_W = ATTN_QW + 2 * ATTN_KVW
DN_W = DN_H * DN_D
N_EXPERTS = 8
EPS = 1e-6
LANES = 128
VMEM_LIMIT = 48 * 1024 * 1024

TM_PROJ = 512
TQ_ATTN = 512
R_DN = 256
R_CV = 512
TM_FFN = 512
CV_HALO = 32
DN_HALO = 8


def _dot(a, b, precision=None):
    return jnp.dot(a, b, preferred_element_type=F32, precision=precision)


def _dot_nt(a, b):
    return lax.dot_general(a, b, (((1,), (1,)), ((), ())), preferred_element_type=F32)


def _dot_tn(a, b):
    return lax.dot_general(a, b, (((0,), (0,)), ((), ())), preferred_element_type=F32)


def _sigmoid(x):
    return 1.0 / (1.0 + jnp.exp(-x))


def _silu(x):
    return x * _sigmoid(x)


def _rms_modulate(x, g, sc, sh):
    y = x * lax.rsqrt(jnp.mean(x * x, axis=-1, keepdims=True) + EPS)
    return (y * g) * (1.0 + sc) + sh


def _adaln_kernel(c_ref, w_ref, b_ref, o_ref):
    o_ref[0] = _dot(_silu(c_ref[...]), w_ref[0], HIGHEST) + b_ref[0]


def _adaln(c, ada_w, ada_b):
    depth, _, d, d3 = ada_w.shape
    b = c.shape[0]
    rows = 8
    cp = jnp.zeros((rows, d), F32).at[:b].set(c)
    w = ada_w.reshape(depth * 2, d, d3)
    bias = ada_b.reshape(depth * 2, 1, d3)
    tn = 1024
    out = pl.pallas_call(
        _adaln_kernel,
        grid=(depth * 2, d3 // tn),
        in_specs=[
            pl.BlockSpec((rows, d), lambda i, j: (0, 0)),
            pl.BlockSpec((1, d, tn), lambda i, j: (i, 0, j)),
            pl.BlockSpec((1, 1, tn), lambda i, j: (i, 0, j)),
        ],
        out_specs=pl.BlockSpec((1, rows, tn), lambda i, j: (i, 0, j)),
        out_shape=jax.ShapeDtypeStruct((depth * 2, rows, d3), F32),
        compiler_params=pltpu.CompilerParams(vmem_limit_bytes=VMEM_LIMIT),
        name="adaln",
    )(cp, w, bias)
    return out[:, :b].reshape(depth, 2, b, d3)


def _inproj_kernel(x_ref, g_ref, sc_ref, sh_ref, rc_ref, ra_ref, rb_ref, w_ref,
                   attn_ref, dnqkv_ref, gate_ref, dz_ref, cv_ref):
    h = _rms_modulate(x_ref[0], g_ref[...], sc_ref[0], sh_ref[0]).astype(BF16)
    a = _dot(h, w_ref[:, 0:ATTN_W])
    rc, ra, rb = rc_ref[0], ra_ref[0], rb_ref[0]
    half = ROT_DIM // 2
    for j in range((ATTN_QW + ATTN_KVW) // LANES):
        t = a[:, j * LANES:(j + 1) * LANES]
        r = t * rc + pltpu.roll(t, LANES - half, 1) * ra + pltpu.roll(t, half, 1) * rb
        if j < ATTN_QW // LANES:
            r = r * (HEAD_DIM ** -0.5)
        attn_ref[0, :, j * LANES:(j + 1) * LANES] = r.astype(BF16)
    attn_ref[0, :, ATTN_QW + ATTN_KVW:ATTN_W] = a[:, ATTN_QW + ATTN_KVW:ATTN_W].astype(BF16)
    o = ATTN_W
    dnqkv_ref[0] = _dot(h, w_ref[:, o:o + 3 * DN_W])
    o += 3 * DN_W
    gate_ref[0] = _dot(h, w_ref[:, o:o + LANES])
    o += LANES
    dz_ref[0] = _dot(h, w_ref[:, o:o + DN_W])
    o += DN_W
    cv_ref[0] = _dot(h, w_ref[:, o:o + 2 * CV_C])


def _inproj(x, g, sc, sh, rc, ra, rb, w):
    b, s, d = x.shape
    tm = TM_PROJ
    n = w.shape[1]
    row = lambda width: pl.BlockSpec((1, tm, width), lambda bi, i: (bi, i, 0))
    vec = pl.BlockSpec((1, 1, d), lambda bi, i: (bi, 0, 0))
    widths = (ATTN_W, 3 * DN_W, LANES, DN_W, 2 * CV_C)
    dtypes = (BF16, F32, F32, F32, F32)
    return pl.pallas_call(
        _inproj_kernel,
        grid=(b, s // tm),
        in_specs=[row(d), pl.BlockSpec((1, d), lambda bi, i: (0, 0)), vec, vec,
                  row(LANES), row(LANES), row(LANES),
                  pl.BlockSpec((d, n), lambda bi, i: (0, 0))],
        out_specs=[row(wd) for wd in widths],
        out_shape=[jax.ShapeDtypeStruct((b, s, wd), dt) for wd, dt in zip(widths, dtypes)],
        compiler_params=pltpu.CompilerParams(vmem_limit_bytes=VMEM_LIMIT),
        name="inproj",
    )(x, g, sc, sh, rc, ra, rb, w)


def _attn_kernel(sink_ref, cur_ref, prev_ref, o_ref):
    i = pl.program_id(1)
    blk = ATTN_BLOCK
    cur = cur_ref[0]
    prev = prev_ref[0]
    kv_all = jnp.concatenate([prev[:, ATTN_QW:], cur[:, ATTN_QW:]], axis=0)
    qi = lax.broadcasted_iota(jnp.int32, (blk, 2 * blk), 0)
    kj = lax.broadcasted_iota(jnp.int32, (blk, 2 * blk), 1)
    diff = qi + blk - kj
    band = (diff >= 0) & (diff < blk)
    band_first = band & (kj >= jnp.where(i == 0, blk, 0))
    for r in range(cur.shape[0] // blk):
        valid = band_first if r == 0 else band
        for j in range(ATTN_HKV):
            kb = kv_all[r * blk:(r + 2) * blk, j * HEAD_DIM:(j + 1) * HEAD_DIM]
            vb = kv_all[r * blk:(r + 2) * blk, ATTN_KVW + j * HEAD_DIM:ATTN_KVW + (j + 1) * HEAD_DIM]
            for gq in range(ATTN_GROUP):
                hq = j * ATTN_GROUP + gq
                q = cur[r * blk:(r + 1) * blk, hq * HEAD_DIM:(hq + 1) * HEAD_DIM]
                sc = jnp.where(valid, _dot_nt(q, kb), -jnp.inf)
                sink = sink_ref[hq]
                m = jnp.maximum(jnp.max(sc, axis=-1, keepdims=True), sink)
                p = jnp.exp(sc - m)
                den = jnp.sum(p, axis=-1, keepdims=True) + jnp.exp(sink - m)
                o = _dot(p.astype(BF16), vb) / den
                o_ref[0, r * blk:(r + 1) * blk, hq * HEAD_DIM:(hq + 1) * HEAD_DIM] = o.astype(BF16)


def _attention(attn, sinks):
    b, s, _ = attn.shape
    tq = TQ_ATTN
    per = tq // ATTN_BLOCK
    return pl.pallas_call(
        _attn_kernel,
        grid=(b, s // tq),
        in_specs=[
            pl.BlockSpec(memory_space=pltpu.SMEM),
            pl.BlockSpec((1, tq, ATTN_W), lambda bi, i: (bi, i, 0)),
            pl.BlockSpec((1, ATTN_BLOCK, ATTN_W), lambda bi, i: (bi, jnp.maximum(i * per - 1, 0), 0)),
        ],
        out_specs=pl.BlockSpec((1, tq, ATTN_QW), lambda bi, i: (bi, i, 0)),
        out_shape=jax.ShapeDtypeStruct((b, s, ATTN_QW), BF16),
        compiler_params=pltpu.CompilerParams(vmem_limit_bytes=VMEM_LIMIT),
        name="attention",
    )(sinks, attn, attn)


def _level_masks(n):
    r = lax.broadcasted_iota(jnp.int32, (n, n), 0)
    c = lax.broadcasted_iota(jnp.int32, (n, n), 1)
    masks = []
    b = 1
    while b < n:
        masks.append((r // (2 * b) == c // (2 * b)) & ((r // b) % 2 == 1) & ((c // b) % 2 == 0))
        b *= 2
    return r == c, masks


def _unit_lower_inverse(lower, eye, masks):
    t = eye.astype(F32) - jnp.where(masks[0], lower, 0.0)
    for m in masks[1:]:
        tb = t.astype(BF16)
        t = t - _dot(tb, _dot(jnp.where(m, lower, 0.0).astype(BF16), tb).astype(BF16))
    return t


def _deltanet_kernel(cur_ref, prev_ref, gate_ref, dz_ref, cw_ref, alog_ref, dtb_ref, ng_ref,
                     expand_ref, tril_ref, slow_ref, ones_ref, o_ref,
                     xext, q_s, k_s, v_s, b_s, g_s, o_s, state):
    i = pl.program_id(1)
    rows = cur_ref.shape[1]
    ck = DN_CHUNK

    @pl.when(i == 0)
    def _():
        state[...] = jnp.zeros_like(state)

    xext[0:DN_HALO, :] = jnp.where(i > 0, prev_ref[0], 0.0)
    xext[DN_HALO:DN_HALO + rows, :] = cur_ref[0]
    acc = cw_ref[0:1, :] * xext[pl.ds(DN_HALO - DN_CONV + 1, rows), :]
    for t in range(1, DN_CONV):
        acc = acc + cw_ref[t:t + 1, :] * xext[pl.ds(DN_HALO - DN_CONV + 1 + t, rows), :]
    qkv = _silu(acc)
    ones_blk = ones_ref[...]

    def l2n(t):
        ss = _dot((t * t).astype(BF16), ones_blk)
        return t * lax.rsqrt(ss + EPS)

    q_s[...] = l2n(qkv[:, 0:DN_W]) * (DN_D ** -0.5)
    k_s[...] = l2n(qkv[:, DN_W:2 * DN_W])
    v_s[...] = qkv[:, 2 * DN_W:3 * DN_W]
    ge = _dot(gate_ref[0], expand_ref[...], HIGHEST)
    b_s[...] = _sigmoid(ge[:, 0:DN_W])
    da = ge[:, DN_W:2 * DN_W] + dtb_ref[...]
    softplus = jnp.maximum(da, 0.0) + jnp.log1p(jnp.exp(-jnp.abs(da)))
    g_s[...] = -jnp.exp(alog_ref[...]) * softplus

    tril = tril_ref[...]
    slow = slow_ref[...]
    ri = lax.broadcasted_iota(jnp.int32, (ck, DN_W), 0)
    ci = lax.broadcasted_iota(jnp.int32, (ck, DN_W), 1) % ck
    incl = ri >= ci
    r2 = lax.broadcasted_iota(jnp.int32, (ck, ck), 0)
    c2 = lax.broadcasted_iota(jnp.int32, (ck, ck), 1)
    strict = r2 > c2
    eye, masks = _level_masks(ck)

    def chunk(c, carry):
        r0 = pl.multiple_of(c * ck, ck)
        q = q_s[pl.ds(r0, ck), :]
        k = k_s[pl.ds(r0, ck), :]
        v = v_s[pl.ds(r0, ck), :]
        beta = b_s[pl.ds(r0, ck), :]
        g = g_s[pl.ds(r0, ck), :]
        gc = _dot(tril, g, HIGHEST)
        gdiff = _dot(tril, g * slow, HIGHEST)
        decay = jnp.exp(jnp.where(incl, gdiff, -jnp.inf))
        egc = jnp.exp(gc)
        glast = gc[ck - 1:ck, :]
        ekg = jnp.exp(glast - gc)
        egl = jnp.exp(glast)
        kbeta = k * beta
        vbeta = (v * beta).astype(BF16)
        kbg = (kbeta * egc).astype(BF16)
        qg = (q * egc).astype(BF16)
        kg = (k * ekg).astype(BF16)
        kb16 = k.astype(BF16)
        kbeta16 = kbeta.astype(BF16)
        q16 = q.astype(BF16)
        for h in range(DN_H):
            sl = slice(h * DN_D, (h + 1) * DN_D)
            dec = decay[:, sl]
            lower = jnp.where(strict, _dot_nt(kbeta16[:, sl], kb16[:, sl]) * dec, 0.0)
            tinv = _unit_lower_inverse(lower, eye, masks).astype(BF16)
            u = _dot(tinv, vbeta[:, sl])
            w = _dot(tinv, kbg[:, sl])
            a_intra = (_dot_nt(q16[:, sl], kb16[:, sl]) * dec).astype(BF16)
            st = state[h]
            st16 = st.astype(BF16)
            v_new = (u - _dot(w.astype(BF16), st16)).astype(BF16)
            o = _dot(qg[:, sl], st16) + _dot(a_intra, v_new)
            state[h] = st * egl[:, sl] + _dot_tn(kg[:, sl], v_new)
            o_s[pl.ds(r0, ck), sl] = o
        return carry

    lax.fori_loop(0, rows // ck, chunk, 0)
    o = o_s[...]
    ms = _dot((o * o).astype(BF16), ones_blk) * (1.0 / DN_D)
    y = o * lax.rsqrt(ms + EPS) * ng_ref[...]
    o_ref[0] = (y * _silu(dz_ref[0])).astype(BF16)


def _deltanet(dnqkv, gate, dz, conv_w, a_log, dt_bias, norm_g):
    b, s, _ = dnqkv.shape
    rows = R_DN
    ck = DN_CHUNK
    lane_head = np.arange(2 * DN_W) // DN_D
    expand = (np.arange(LANES)[:, None] == lane_head[None, :]).astype(np.float32)
    tril = np.tril(np.ones((ck, ck), np.float32))
    slow = (np.arange(ck)[:, None] > (np.arange(DN_W)[None, :] % ck)).astype(np.float32)
    ones_blk = (np.arange(DN_W)[:, None] // DN_D == np.arange(DN_W)[None, :] // DN_D).astype(np.float32)
    rep = lambda t: jnp.repeat(t.astype(F32), DN_D).reshape(1, DN_W)
    const = lambda shape: pl.BlockSpec(shape, lambda bi, i: (0,) * len(shape))
    row = lambda width: pl.BlockSpec((1, rows, width), lambda bi, i: (bi, i, 0))
    per = rows // DN_HALO
    return pl.pallas_call(
        _deltanet_kernel,
        grid=(b, s // rows),
        in_specs=[
            row(3 * DN_W),
            pl.BlockSpec((1, DN_HALO, 3 * DN_W), lambda bi, i: (bi, jnp.maximum(i * per - 1, 0), 0)),
            row(LANES), row(DN_W),
            const((DN_CONV, 3 * DN_W)), const((1, DN_W)), const((1, DN_W)), const((1, DN_W)),
            const((LANES, 2 * DN_W)), const((ck, ck)), const((ck, DN_W)), const((DN_W, DN_W)),
        ],
        out_specs=row(DN_W),
        out_shape=jax.ShapeDtypeStruct((b, s, DN_W), BF16),
        scratch_shapes=[
            pltpu.VMEM((rows + DN_HALO, 3 * DN_W), F32),
            pltpu.VMEM((rows, DN_W), F32), pltpu.VMEM((rows, DN_W), F32), pltpu.VMEM((rows, DN_W), F32),
            pltpu.VMEM((rows, DN_W), F32), pltpu.VMEM((rows, DN_W), F32), pltpu.VMEM((rows, DN_W), F32),
            pltpu.VMEM((DN_H, DN_D, DN_D), F32),
        ],
        compiler_params=pltpu.CompilerParams(vmem_limit_bytes=VMEM_LIMIT),
        name="deltanet",
    )(dnqkv, dnqkv, gate, dz, conv_w, rep(a_log), rep(dt_bias), jnp.tile(norm_g.astype(F32), DN_H).reshape(1, DN_W),
      jnp.asarray(expand), jnp.asarray(tril), jnp.asarray(slow), jnp.asarray(ones_blk, dtype=BF16))


def _conformer_kernel(cur_ref, prev_ref, w_ref, b_ref, lg_ref, lb_ref, o_ref, u_s):
    i = pl.program_id(1)
    rows = cur_ref.shape[1]
    prev = jnp.where(i > 0, prev_ref[0], 0.0)
    u_s[0:CV_HALO, :] = prev[:, 0:CV_C] * _sigmoid(prev[:, CV_C:])
    cur = cur_ref[0]
    u_s[CV_HALO:CV_HALO + rows, :] = cur[:, 0:CV_C] * _sigmoid(cur[:, CV_C:])
    base = CV_HALO - CV_K + 1
    acc = b_ref[...] + w_ref[0:1, :] * u_s[pl.ds(base, rows), :]
    for t in range(1, CV_K):
        acc = acc + w_ref[t:t + 1, :] * u_s[pl.ds(base + t, rows), :]
    mu = jnp.mean(acc, axis=-1, keepdims=True)
    xc = acc - mu
    y = xc * lax.rsqrt(jnp.mean(xc * xc, axis=-1, keepdims=True) + EPS)
    o_ref[0] = _silu(y * lg_ref[...] + lb_ref[...]).astype(BF16)


def _conformer(cv, w, bias, ln_g, ln_b):
    b, s, _ = cv.shape
    rows = R_CV
    per = rows // CV_HALO
    const = lambda shape: pl.BlockSpec(shape, lambda bi, i: (0,) * len(shape))
    return pl.pallas_call(
        _conformer_kernel,
        grid=(b, s // rows),
        in_specs=[
            pl.BlockSpec((1, rows, 2 * CV_C), lambda bi, i: (bi, i, 0)),
            pl.BlockSpec((1, CV_HALO, 2 * CV_C), lambda bi, i: (bi, jnp.maximum(i * per - 1, 0), 0)),
            const((CV_K, CV_C)), const((1, CV_C)), const((1, CV_C)), const((1, CV_C)),
        ],
        out_specs=pl.BlockSpec((1, rows, CV_C), lambda bi, i: (bi, i, 0)),
        out_shape=jax.ShapeDtypeStruct((b, s, CV_C), BF16),
        scratch_shapes=[pltpu.VMEM((rows + CV_HALO, CV_C), F32)],
        compiler_params=pltpu.CompilerParams(vmem_limit_bytes=VMEM_LIMIT),
        name="conformer",
    )(cv, cv, w, bias.reshape(1, CV_C), ln_g.reshape(1, CV_C), ln_b.reshape(1, CV_C))


def _mix_residual(x_ref, ya_ref, yd_ref, yc_ref, wa_ref, wd_ref, wc_ref, gt1_ref):
    y = _dot(ya_ref[...], wa_ref[...]) + _dot(yd_ref[...], wd_ref[...]) + _dot(yc_ref[...], wc_ref[...])
    return x_ref[...] + gt1_ref[0] * y


def _finish(x1, acc, gt2_ref, fg_ref, o_ref, final_norm):
    x2 = x1 + gt2_ref[0] * acc
    if final_norm:
        x2 = x2 * lax.rsqrt(jnp.mean(x2 * x2, axis=-1, keepdims=True) + EPS) * fg_ref[...]
    o_ref[...] = x2


def _dense_ffn_kernel(x_ref, ya_ref, yd_ref, yc_ref, wa_ref, wd_ref, wc_ref, gt1_ref, g_ref, sc_ref, sh_ref, gt2_ref,
                      fg_ref, wg_ref, wu_ref, wdn_ref, o_ref, x1_s, h_s, acc_s, *, final_norm):
    f = pl.program_id(1)

    @pl.when(f == 0)
    def _():
        x1 = _mix_residual(x_ref, ya_ref, yd_ref, yc_ref, wa_ref, wd_ref, wc_ref, gt1_ref)
        x1_s[...] = x1
        h_s[...] = _rms_modulate(x1, g_ref[...], sc_ref[0], sh_ref[0]).astype(BF16)
        acc_s[...] = jnp.zeros_like(acc_s)

    h = h_s[...]
    a = (_silu(_dot(h, wg_ref[...])) * _dot(h, wu_ref[...])).astype(BF16)
    acc_s[...] += _dot(a, wdn_ref[...])

    @pl.when(f == pl.num_programs(1) - 1)
    def _():
        _finish(x1_s[...], acc_s[...], gt2_ref, fg_ref, o_ref, final_norm)


def _moe_ffn_kernel(x_ref, ya_ref, yd_ref, yc_ref, wa_ref, wd_ref, wc_ref, gt1_ref, g_ref, sc_ref, sh_ref, gt2_ref,
                    fg_ref, rw_ref, rb_ref, wg_ref, wu_ref, wdn_ref, o_ref, x1_s, h_s, acc_s, comb_s, *, final_norm):
    e = pl.program_id(1)
    lane = lax.broadcasted_iota(jnp.int32, comb_s.shape, 1)

    @pl.when(e == 0)
    def _():
        x1 = _mix_residual(x_ref, ya_ref, yd_ref, yc_ref, wa_ref, wd_ref, wc_ref, gt1_ref)
        x1_s[...] = x1
        h = _rms_modulate(x1, g_ref[...], sc_ref[0], sh_ref[0])
        h_s[...] = h.astype(BF16)
        acc_s[...] = jnp.zeros_like(acc_s)
        logits = jnp.where(lane < N_EXPERTS, _dot(h, rw_ref[...], HIGHEST) + rb_ref[...], -jnp.inf)
        m1 = jnp.max(logits, axis=-1, keepdims=True)
        i1 = jnp.min(jnp.where(logits == m1, lane, LANES), axis=-1, keepdims=True)
        rest = jnp.where(lane == i1, -jnp.inf, logits)
        m2 = jnp.max(rest, axis=-1, keepdims=True)
        i2 = jnp.min(jnp.where(rest == m2, lane, LANES), axis=-1, keepdims=True)
        e2 = jnp.exp(m2 - m1)
        comb_s[...] = jnp.where(lane == i1, 1.0 / (1.0 + e2), 0.0) + jnp.where(lane == i2, e2 / (1.0 + e2), 0.0)

    h = h_s[...]
    a = (_silu(_dot(h, wg_ref[0])) * _dot(h, wu_ref[0])).astype(BF16)
    cw = jnp.sum(jnp.where(lane == e, comb_s[...], 0.0), axis=-1, keepdims=True)
    acc_s[...] += cw * _dot(a, wdn_ref[0])

    @pl.when(e == pl.num_programs(1) - 1)
    def _():
        _finish(x1_s[...], acc_s[...], gt2_ref, fg_ref, o_ref, final_norm)


def _mixer_common_specs(d, tm, s):
    tok = lambda width: pl.BlockSpec((tm, width), lambda i, f: (i, 0))
    const = lambda shape: pl.BlockSpec(shape, lambda i, f: (0,) * len(shape))
    vec = pl.BlockSpec((1, 1, d), lambda i, f: ((i * tm) // s, 0, 0))
    return [tok(d), tok(ATTN_QW), tok(DN_W), tok(CV_C),
            const((ATTN_QW, d)), const((DN_W, d)), const((CV_C, d)),
            vec, const((1, d)), vec, vec, vec, const((1, d))], tok(d)


def _dense_ffn(x, ys, w_out, mods, g, final_g, wg, wu, wdn, s, final_norm):
    t, d = x.shape
    tm = TM_FFN
    ff = wg.shape[1]
    tf = ff // 2
    common, out_spec = _mixer_common_specs(d, tm, s)
    gt1, sc, sh, gt2 = mods
    return pl.pallas_call(
        functools.partial(_dense_ffn_kernel, final_norm=final_norm),
        grid=(t // tm, ff // tf),
        in_specs=common + [
            pl.BlockSpec((d, tf), lambda i, f: (0, f)),
            pl.BlockSpec((d, tf), lambda i, f: (0, f)),
            pl.BlockSpec((tf, d), lambda i, f: (f, 0)),
        ],
        out_specs=out_spec,
        out_shape=jax.ShapeDtypeStruct((t, d), F32),
        scratch_shapes=[pltpu.VMEM((tm, d), F32), pltpu.VMEM((tm, d), BF16), pltpu.VMEM((tm, d), F32)],
        compiler_params=pltpu.CompilerParams(vmem_limit_bytes=VMEM_LIMIT),
        name="dense_ffn",
    )(x, *ys, *w_out, gt1, g, sc, sh, gt2, final_g, wg, wu, wdn)


def _moe_ffn(x, ys, w_out, mods, g, final_g, rw, rb, wg, wu, wdn, s, final_norm):
    t, d = x.shape
    tm = TM_FFN
    n_e, _, ffe = wg.shape
    common, out_spec = _mixer_common_specs(d, tm, s)
    gt1, sc, sh, gt2 = mods
    return pl.pallas_call(
        functools.partial(_moe_ffn_kernel, final_norm=final_norm),
        grid=(t // tm, n_e),
        in_specs=common + [
            pl.BlockSpec((d, LANES), lambda i, e: (0, 0)),
            pl.BlockSpec((1, LANES), lambda i, e: (0, 0)),
            pl.BlockSpec((1, d, ffe), lambda i, e: (e, 0, 0)),
            pl.BlockSpec((1, d, ffe), lambda i, e: (e, 0, 0)),
            pl.BlockSpec((1, ffe, d), lambda i, e: (e, 0, 0)),
        ],
        out_specs=out_spec,
        out_shape=jax.ShapeDtypeStruct((t, d), F32),
        scratch_shapes=[pltpu.VMEM((tm, d), F32), pltpu.VMEM((tm, d), BF16), pltpu.VMEM((tm, d), F32),
                        pltpu.VMEM((tm, LANES), F32)],
        compiler_params=pltpu.CompilerParams(vmem_limit_bytes=VMEM_LIMIT),
        name="moe_ffn",
    )(x, *ys, *w_out, gt1, g, sc, sh, gt2, final_g, rw, rb, wg, wu, wdn)


def _rope_tables(positions):
    half = ROT_DIM // 2
    inv_freq = ROPE_THETA ** (-jnp.arange(0, ROT_DIM, 2, dtype=F32) / ROT_DIM)
    ang = positions.astype(F32)[..., None] * inv_freq
    cos, sin = jnp.cos(ang), jnp.sin(ang)
    pad = lambda t, lo, hi, val: jnp.pad(t, ((0, 0), (0, 0), (lo, hi)), constant_values=val)
    rest = HEAD_DIM - ROT_DIM
    c = pad(jnp.concatenate([cos, cos], -1), 0, rest, 1.0)
    a = pad(-sin, 0, HEAD_DIM - half, 0.0)
    b = pad(sin, half, rest, 0.0)
    two = lambda t: jnp.concatenate([t, t], -1)
    return two(c), two(a), two(b)


def _pack_w_in(w):
    d = w.shape[0]
    o1 = ATTN_W
    o2 = o1 + 3 * DN_W
    o3 = o2 + 2 * DN_H
    o4 = o3 + DN_W
    gates = jnp.zeros((d, LANES), w.dtype).at[:, :2 * DN_H].set(w[:, o2:o3])
    return jnp.concatenate([w[:, :o2], gates, w[:, o3:o4], w[:, o4:]], axis=1).astype(BF16)


def kernel(x, c, positions, ada_w, ada_b, norm_mix_g, norm_ffn_g, w_in, attn_sinks, dn_conv_w, dn_a_log, dn_dt_bias, dn_norm_g, cv_dw_w, cv_dw_b, cv_ln_g, cv_ln_b, w_out, ffn_w_gate, ffn_w_up, ffn_w_down, router_w, router_b, moe_w_gate, moe_w_up, moe_w_down, final_norm_g):
    b, s, d = x.shape
    depth = w_in.shape[0]
    mod = _adaln(c, ada_w, ada_b)
    rc, ra, rb = _rope_tables(positions)
    final_g = final_norm_g.reshape(1, d)
    vec = lambda t: t.reshape(b, 1, d)
    for l in range(depth):
        sh1, sc1, gt1 = (vec(t) for t in jnp.split(mod[l, 0], 3, axis=-1))
        sh2, sc2, gt2 = (vec(t) for t in jnp.split(mod[l, 1], 3, axis=-1))
        attn, dnqkv, gate, dz, cv = _inproj(x, norm_mix_g[l].reshape(1, d), sc1, sh1, rc, ra, rb, _pack_w_in(w_in[l]))
        y_attn = _attention(attn, attn_sinks[l].astype(F32))
        y_dn = _deltanet(dnqkv, gate, dz, dn_conv_w[l], dn_a_log[l], dn_dt_bias[l], dn_norm_g[l])
        y_cv = _conformer(cv, cv_dw_w[l], cv_dw_b[l], cv_ln_g[l], cv_ln_b[l])
        t = b * s
        ys = (y_attn.reshape(t, ATTN_QW), y_dn.reshape(t, DN_W), y_cv.reshape(t, CV_C))
        wo = w_out[l].astype(BF16)
        wos = (wo[:ATTN_QW], wo[ATTN_QW:ATTN_QW + DN_W], wo[ATTN_QW + DN_W:])
        mods = (gt1, sc2, sh2, gt2)
        g2 = norm_ffn_g[l].reshape(1, d)
        last = l == depth - 1
        j = l // 2
        if l % 2 == 0:
            x2 = _dense_ffn(x.reshape(t, d), ys, wos, mods, g2, final_g, ffn_w_gate[j].astype(BF16),
                            ffn_w_up[j].astype(BF16), ffn_w_down[j].astype(BF16), s, last)
        else:
            rw = jnp.zeros((d, LANES), F32).at[:, :N_EXPERTS].set(router_w[j])
            rbias = jnp.zeros((1, LANES), F32).at[0, :N_EXPERTS].set(router_b[j])
            x2 = _moe_ffn(x.reshape(t, d), ys, wos, mods, g2, final_g, rw, rbias, moe_w_gate[j].astype(BF16),
                          moe_w_up[j].astype(BF16), moe_w_down[j].astype(BF16), s, last)
        x = x2.reshape(b, s, d)
    return x
```

```python
import functools

import numpy as np
import jax
import jax.numpy as jnp
from jax import lax
from jax.experimental import pallas as pl
from jax.experimental.pallas import tpu as pltpu

F32 = jnp.float32
BF16 = jnp.bfloat16
HIGHEST = lax.Precision.HIGHEST

HEAD_DIM = 64
ATTN_HQ = 8
ATTN_HKV = 2
ATTN_GROUP = ATTN_HQ // ATTN_HKV
ATTN_BLOCK = 128
ROT_DIM = HEAD_DIM // 4
ROPE_THETA = 500000.0
DN_H = 4
DN_D = 64
DN_CONV = 4
DN_CHUNK = 64
CV_C = 256
CV_K = 31
ATTN_QW = ATTN_HQ * HEAD_DIM
ATTN_KVW = ATTN_HKV * HEAD_DIM
ATTN_W = ATTN_QW + 2 * ATTN_KVW
DN_W = DN_H * DN_D
N_EXPERTS = 8
EPS = 1e-6
LANES = 128
VMEM_LIMIT = 48 * 1024 * 1024

TM_PROJ = 512
TQ_ATTN = 512
R_DN = 512
R_CV = 512
TM_FFN = 512
CV_HALO = 32
DN_HALO = 8


def _dot(a, b, precision=None):
    return jnp.dot(a, b, preferred_element_type=F32, precision=precision)


def _dot_nt(a, b):
    return lax.dot_general(a, b, (((1,), (1,)), ((), ())), preferred_element_type=F32)


def _dot_tn(a, b):
    return lax.dot_general(a, b, (((0,), (0,)), ((), ())), preferred_element_type=F32)


def _sigmoid(x):
    return 1.0 / (1.0 + jnp.exp(-x))


def _silu(x):
    return x * _sigmoid(x)


def _rms_modulate(x, g, sc, sh):
    y = x * lax.rsqrt(jnp.mean(x * x, axis=-1, keepdims=True) + EPS)
    return (y * g) * (1.0 + sc) + sh


def _adaln_kernel(c_ref, w_ref, b_ref, o_ref):
    o_ref[0] = _dot(_silu(c_ref[...]), w_ref[0], HIGHEST) + b_ref[0]


def _adaln(c, ada_w, ada_b):
    depth, _, d, d3 = ada_w.shape
    b = c.shape[0]
    rows = 8
    cp = jnp.zeros((rows, d), F32).at[:b].set(c)
    w = ada_w.reshape(depth * 2, d, d3)
    bias = ada_b.reshape(depth * 2, 1, d3)
    tn = 1024
    out = pl.pallas_call(
        _adaln_kernel,
        grid=(depth * 2, d3 // tn),
        in_specs=[
            pl.BlockSpec((rows, d), lambda i, j: (0, 0)),
            pl.BlockSpec((1, d, tn), lambda i, j: (i, 0, j)),
            pl.BlockSpec((1, 1, tn), lambda i, j: (i, 0, j)),
        ],
        out_specs=pl.BlockSpec((1, rows, tn), lambda i, j: (i, 0, j)),
        out_shape=jax.ShapeDtypeStruct((depth * 2, rows, d3), F32),
        compiler_params=pltpu.CompilerParams(vmem_limit_bytes=VMEM_LIMIT),
        name="adaln",
    )(cp, w, bias)
    return out[:, :b].reshape(depth, 2, b, d3)


def _inproj_kernel(x_ref, g_ref, sc_ref, sh_ref, rc_ref, ra_ref, rb_ref, w_ref,
                   attn_ref, dnqkv_ref, gate_ref, dz_ref, cv_ref):
    h = _rms_modulate(x_ref[0], g_ref[...], sc_ref[0], sh_ref[0]).astype(BF16)
    a = _dot(h, w_ref[:, 0:ATTN_W])
    rc, ra, rb = rc_ref[0], ra_ref[0], rb_ref[0]
    half = ROT_DIM // 2
    for j in range((ATTN_QW + ATTN_KVW) // LANES):
        t = a[:, j * LANES:(j + 1) * LANES]
        r = t * rc + pltpu.roll(t, LANES - half, 1) * ra + pltpu.roll(t, half, 1) * rb
        if j < ATTN_QW // LANES:
            r = r * (HEAD_DIM ** -0.5)
        attn_ref[0, :, j * LANES:(j + 1) * LANES] = r.astype(BF16)
    attn_ref[0, :, ATTN_QW + ATTN_KVW:ATTN_W] = a[:, ATTN_QW + ATTN_KVW:ATTN_W].astype(BF16)
    o = ATTN_W
    dnqkv_ref[0] = _dot(h, w_ref[:, o:o + 3 * DN_W])
    o += 3 * DN_W
    gate_ref[0] = _dot(h, w_ref[:, o:o + LANES])
    o += LANES
    dz_ref[0] = _dot(h, w_ref[:, o:o + DN_W])
    o += DN_W
    cv_ref[0] = _dot(h, w_ref[:, o:o + 2 * CV_C])


def _inproj(x, g, sc, sh, rc, ra, rb, w):
    b, s, d = x.shape
    tm = TM_PROJ
    n = w.shape[1]
    row = lambda width: pl.BlockSpec((1, tm, width), lambda bi, i: (bi, i, 0))
    vec = pl.BlockSpec((1, 1, d), lambda bi, i: (bi, 0, 0))
    widths = (ATTN_W, 3 * DN_W, LANES, DN_W, 2 * CV_C)
    dtypes = (BF16, F32, F32, F32, F32)
    return pl.pallas_call(
        _inproj_kernel,
        grid=(b, s // tm),
        in_specs=[row(d), pl.BlockSpec((1, d), lambda bi, i: (0, 0)), vec, vec,
                  row(LANES), row(LANES), row(LANES),
                  pl.BlockSpec((d, n), lambda bi, i: (0, 0))],
        out_specs=[row(wd) for wd in widths],
        out_shape=[jax.ShapeDtypeStruct((b, s, wd), dt) for wd, dt in zip(widths, dtypes)],
        compiler_params=pltpu.CompilerParams(vmem_limit_bytes=VMEM_LIMIT),
        name="inproj",
    )(x, g, sc, sh, rc, ra, rb, w)


def _attn_kernel(sink_ref, cur_ref, prev_ref, o_ref):
    i = pl.program_id(1)
    blk = ATTN_BLOCK
    cur = cur_ref[0]
    prev = prev_ref[0]
    kv_all = jnp.concatenate([prev[:, ATTN_QW:], cur[:, ATTN_QW:]], axis=0)
    qi = lax.broadcasted_iota(jnp.int32, (blk, 2 * blk), 0)
    kj = lax.broadcasted_iota(jnp.int32, (blk, 2 * blk), 1)
    diff = qi + blk - kj
    band = (diff >= 0) & (diff < blk)
    band_first = band & (kj >= jnp.where(i == 0, blk, 0))
    for r in range(cur.shape[0] // blk):
        valid = band_first if r == 0 else band
        for j in range(ATTN_HKV):
            kb = kv_all[r * blk:(r + 2) * blk, j * HEAD_DIM:(j + 1) * HEAD_DIM]
            vb = kv_all[r * blk:(r + 2) * blk, ATTN_KVW + j * HEAD_DIM:ATTN_KVW + (j + 1) * HEAD_DIM]
            for gq in range(ATTN_GROUP):
                hq = j * ATTN_GROUP + gq
                q = cur[r * blk:(r + 1) * blk, hq * HEAD_DIM:(hq + 1) * HEAD_DIM]
                sc = jnp.where(valid, _dot_nt(q, kb), -jnp.inf)
                sink = sink_ref[hq]
                m = jnp.maximum(jnp.max(sc, axis=-1, keepdims=True), sink)
                p = jnp.exp(sc - m)
                den = jnp.sum(p, axis=-1, keepdims=True) + jnp.exp(sink - m)
                o = _dot(p.astype(BF16), vb) / den
                o_ref[0, r * blk:(r + 1) * blk, hq * HEAD_DIM:(hq + 1) * HEAD_DIM] = o.astype(BF16)


def _attention(attn, sinks):
    b, s, _ = attn.shape
    tq = TQ_ATTN
    per = tq // ATTN_BLOCK
    return pl.pallas_call(
        _attn_kernel,
        grid=(b, s // tq),
        in_specs=[
            pl.BlockSpec(memory_space=pltpu.SMEM),
            pl.BlockSpec((1, tq, ATTN_W), lambda bi, i: (bi, i, 0)),
            pl.BlockSpec((1, ATTN_BLOCK, ATTN_W), lambda bi, i: (bi, jnp.maximum(i * per - 1, 0), 0)),
        ],
        out_specs=pl.BlockSpec((1, tq, ATTN_QW), lambda bi, i: (bi, i, 0)),
        out_shape=jax.ShapeDtypeStruct((b, s, ATTN_QW), BF16),
        compiler_params=pltpu.CompilerParams(vmem_limit_bytes=VMEM_LIMIT),
        name="attention",
    )(sinks, attn, attn)


def _level_masks(n):
    r = lax.broadcasted_iota(jnp.int32, (n, n), 0)
    c = lax.broadcasted_iota(jnp.int32, (n, n), 1)
    masks = []
    b = 1
    while b < n:
        masks.append((r // (2 * b) == c // (2 * b)) & ((r // b) % 2 == 1) & ((c // b) % 2 == 0))
        b *= 2
    return r == c, masks


def _unit_lower_inverse(lower, eye, masks):
    t = eye.astype(F32) - jnp.where(masks[0], lower, 0.0)
    for m in masks[1:]:
        tb = t.astype(BF16)
        t = t - _dot(tb, _dot(jnp.where(m, lower, 0.0).astype(BF16), tb).astype(BF16))
    return t


def _deltanet_kernel(cur_ref, prev_ref, gate_ref, dz_ref, cw_ref, alog_ref, dtb_ref, ng_ref,
                     expand_ref, tril_ref, slow_ref, ones_ref, o_ref,
                     xext, q_s, k_s, v_s, b_s, g_s, o_s, state):
    i = pl.program_id(1)
    rows = cur_ref.shape[1]
    ck = DN_CHUNK

    @pl.when(i == 0)
    def _():
        state[...] = jnp.zeros_like(state)

    xext[0:DN_HALO, :] = jnp.where(i > 0, prev_ref[0], 0.0)
    xext[DN_HALO:DN_HALO + rows, :] = cur_ref[0]
    acc = cw_ref[0:1, :] * xext[pl.ds(DN_HALO - DN_CONV + 1, rows), :]
    for t in range(1, DN_CONV):
        acc = acc + cw_ref[t:t + 1, :] * xext[pl.ds(DN_HALO - DN_CONV + 1 + t, rows), :]
    qkv = _silu(acc)
    ones_blk = ones_ref[...]

    def l2n(t):
        ss = _dot((t * t).astype(BF16), ones_blk)
        return t * lax.rsqrt(ss + EPS)

    q_s[...] = l2n(qkv[:, 0:DN_W]) * (DN_D ** -0.5)
    k_s[...] = l2n(qkv[:, DN_W:2 * DN_W])
    v_s[...] = qkv[:, 2 * DN_W:3 * DN_W]
    ge = _dot(gate_ref[0], expand_ref[...], HIGHEST)
    b_s[...] = _sigmoid(ge[:, 0:DN_W])
    da = ge[:, DN_W:2 * DN_W] + dtb_ref[...]
    softplus = jnp.maximum(da, 0.0) + jnp.log1p(jnp.exp(-jnp.abs(da)))
    g_s[...] = -jnp.exp(alog_ref[...]) * softplus

    tril = tril_ref[...]
    slow = slow_ref[...]
    ri = lax.broadcasted_iota(jnp.int32, (ck, DN_W), 0)
    ci = lax.broadcasted_iota(jnp.int32, (ck, DN_W), 1) % ck
    incl = ri >= ci
    r2 = lax.broadcasted_iota(jnp.int32, (ck, ck), 0)
    c2 = lax.broadcasted_iota(jnp.int32, (ck, ck), 1)
    strict = r2 > c2
    eye, masks = _level_masks(ck)

    nc = rows // ck
    lanes = [slice(h * DN_D, (h + 1) * DN_D) for h in range(DN_H)]
    kbeta16, k16, q16, dec, vbeta, kbg, qg, kg, egl = ([] for _ in range(9))
    for c in range(nc):
        r0 = c * ck
        q = q_s[r0:r0 + ck, :]
        k = k_s[r0:r0 + ck, :]
        beta = b_s[r0:r0 + ck, :]
        g = g_s[r0:r0 + ck, :]
        gc = _dot(tril, g, HIGHEST)
        gdiff = _dot(tril, g * slow, HIGHEST)
        decay = jnp.exp(jnp.where(incl, gdiff, -jnp.inf))
        egc = jnp.exp(gc)
        glast = gc[ck - 1:ck, :]
        kbeta = k * beta
        per_head = ((kbeta16, kbeta.astype(BF16)), (k16, k.astype(BF16)), (q16, q.astype(BF16)), (dec, decay),
                    (vbeta, (v_s[r0:r0 + ck, :] * beta).astype(BF16)), (kbg, (kbeta * egc).astype(BF16)),
                    (qg, q * egc), (kg, (k * jnp.exp(glast - gc)).astype(BF16)), (egl, jnp.exp(glast)))
        for dst, full in per_head:
            dst.extend(full[:, sl] for sl in lanes)
    n_inst = nc * DN_H
    inst = range(n_inst)
    lower = [jnp.where(strict, _dot_nt(kbeta16[n], k16[n]) * dec[n], 0.0) for n in inst]
    a_intra = [(_dot_nt(q16[n], k16[n]) * dec[n]).astype(BF16) for n in inst]
    lower16 = [t.astype(BF16) for t in lower]
    tinv = [eye.astype(F32) - jnp.where(masks[0], lower[n], 0.0) for n in inst]
    for m in masks[1:]:
        t16 = [t.astype(BF16) for t in tinv]
        x16 = [_dot(jnp.where(m, lower16[n], jnp.zeros_like(lower16[n])), t16[n]).astype(BF16) for n in inst]
        tinv = [tinv[n] - _dot(t16[n], x16[n]) for n in inst]
    t16 = [t.astype(BF16) for t in tinv]
    w16 = [_dot(t16[n], kbg[n]).astype(BF16) for n in inst]
    u16 = [_dot(t16[n], vbeta[n]).astype(BF16) for n in inst]
    kw16 = [_dot_tn(kg[n], w16[n]).astype(BF16) for n in inst]
    ku = [_dot_tn(kg[n], u16[n]) for n in inst]
    qeff16 = [(qg[n] - _dot(a_intra[n], w16[n])).astype(BF16) for n in inst]
    au = [_dot(a_intra[n], u16[n]) for n in inst]
    st = [state[h] for h in range(DN_H)]
    for c in range(nc):
        outs = []
        for h in range(DN_H):
            n = c * DN_H + h
            s16 = st[h].astype(BF16)
            outs.append(_dot(qeff16[n], s16) + au[n])
            st[h] = st[h] * egl[n] - _dot(kw16[n], s16) + ku[n]
        o_s[c * ck:(c + 1) * ck, :] = jnp.concatenate(outs, axis=1)
    for h in range(DN_H):
        state[h] = st[h]
    o = o_s[...]
    ms = _dot((o * o).astype(BF16), ones_blk) * (1.0 / DN_D)
    y = o * lax.rsqrt(ms + EPS) * ng_ref[...]
    o_ref[0] = (y * _silu(dz_ref[0])).astype(BF16)


def _deltanet(dnqkv, gate, dz, conv_w, a_log, dt_bias, norm_g):
    b, s, _ = dnqkv.shape
    rows = R_DN
    ck = DN_CHUNK
    lane_head = np.arange(2 * DN_W) // DN_D
    expand = (np.arange(LANES)[:, None] == lane_head[None, :]).astype(np.float32)
    tril = np.tril(np.ones((ck, ck), np.float32))
    slow = (np.arange(ck)[:, None] > (np.arange(DN_W)[None, :] % ck)).astype(np.float32)
    ones_blk = (np.arange(DN_W)[:, None] // DN_D == np.arange(DN_W)[None, :] // DN_D).astype(np.float32)
    rep = lambda t: jnp.repeat(t.astype(F32), DN_D).reshape(1, DN_W)
    const = lambda shape: pl.BlockSpec(shape, lambda bi, i: (0,) * len(shape))
    row = lambda width: pl.BlockSpec((1, rows, width), lambda bi, i: (bi, i, 0))
    per = rows // DN_HALO
    return pl.pallas_call(
        _deltanet_kernel,
        grid=(b, s // rows),
        in_specs=[
            row(3 * DN_W),
            pl.BlockSpec((1, DN_HALO, 3 * DN_W), lambda bi, i: (bi, jnp.maximum(i * per - 1, 0), 0)),
            row(LANES), row(DN_W),
            const((DN_CONV, 3 * DN_W)), const((1, DN_W)), const((1, DN_W)), const((1, DN_W)),
            const((LANES, 2 * DN_W)), const((ck, ck)), const((ck, DN_W)), const((DN_W, DN_W)),
        ],
        out_specs=row(DN_W),
        out_shape=jax.ShapeDtypeStruct((b, s, DN_W), BF16),
        scratch_shapes=[
            pltpu.VMEM((rows + DN_HALO, 3 * DN_W), F32),
            pltpu.VMEM((rows, DN_W), F32), pltpu.VMEM((rows, DN_W), F32), pltpu.VMEM((rows, DN_W), F32),
            pltpu.VMEM((rows, DN_W), F32), pltpu.VMEM((rows, DN_W), F32), pltpu.VMEM((rows, DN_W), F32),
            pltpu.VMEM((DN_H, DN_D, DN_D), F32),
        ],
        compiler_params=pltpu.CompilerParams(vmem_limit_bytes=VMEM_LIMIT),
        name="deltanet",
    )(dnqkv, dnqkv, gate, dz, conv_w, rep(a_log), rep(dt_bias), jnp.tile(norm_g.astype(F32), DN_H).reshape(1, DN_W),
      jnp.asarray(expand), jnp.asarray(tril), jnp.asarray(slow), jnp.asarray(ones_blk, dtype=BF16))


def _conformer_kernel(cur_ref, prev_ref, w_ref, b_ref, lg_ref, lb_ref, o_ref, u_s):
    i = pl.program_id(1)
    rows = cur_ref.shape[1]
    prev = jnp.where(i > 0, prev_ref[0], 0.0)
    u_s[0:CV_HALO, :] = prev[:, 0:CV_C] * _sigmoid(prev[:, CV_C:])
    cur = cur_ref[0]
    u_s[CV_HALO:CV_HALO + rows, :] = cur[:, 0:CV_C] * _sigmoid(cur[:, CV_C:])
    base = CV_HALO - CV_K + 1
    acc = b_ref[...] + w_ref[0:1, :] * u_s[pl.ds(base, rows), :]
    for t in range(1, CV_K):
        acc = acc + w_ref[t:t + 1, :] * u_s[pl.ds(base + t, rows), :]
    mu = jnp.mean(acc, axis=-1, keepdims=True)
    xc = acc - mu
    y = xc * lax.rsqrt(jnp.mean(xc * xc, axis=-1, keepdims=True) + EPS)
    o_ref[0] = _silu(y * lg_ref[...] + lb_ref[...]).astype(BF16)


def _conformer(cv, w, bias, ln_g, ln_b):
    b, s, _ = cv.shape
    rows = R_CV
    per = rows // CV_HALO
    const = lambda shape: pl.BlockSpec(shape, lambda bi, i: (0,) * len(shape))
    return pl.pallas_call(
        _conformer_kernel,
        grid=(b, s // rows),
        in_specs=[
            pl.BlockSpec((1, rows, 2 * CV_C), lambda bi, i: (bi, i, 0)),
            pl.BlockSpec((1, CV_HALO, 2 * CV_C), lambda bi, i: (bi, jnp.maximum(i * per - 1, 0), 0)),
            const((CV_K, CV_C)), const((1, CV_C)), const((1, CV_C)), const((1, CV_C)),
        ],
        out_specs=pl.BlockSpec((1, rows, CV_C), lambda bi, i: (bi, i, 0)),
        out_shape=jax.ShapeDtypeStruct((b, s, CV_C), BF16),
        scratch_shapes=[pltpu.VMEM((rows + CV_HALO, CV_C), F32)],
        compiler_params=pltpu.CompilerParams(vmem_limit_bytes=VMEM_LIMIT),
        name="conformer",
    )(cv, cv, w, bias.reshape(1, CV_C), ln_g.reshape(1, CV_C), ln_b.reshape(1, CV_C))


def _mix_residual(x_ref, ya_ref, yd_ref, yc_ref, wa_ref, wd_ref, wc_ref, gt1_ref):
    y = _dot(ya_ref[...], wa_ref[...]) + _dot(yd_ref[...], wd_ref[...]) + _dot(yc_ref[...], wc_ref[...])
    return x_ref[...] + gt1_ref[0] * y


def _finish(x1, acc, gt2_ref, fg_ref, o_ref, final_norm):
    x2 = x1 + gt2_ref[0] * acc
    if final_norm:
        x2 = x2 * lax.rsqrt(jnp.mean(x2 * x2, axis=-1, keepdims=True) + EPS) * fg_ref[...]
    o_ref[...] = x2


def _dense_ffn_kernel(x_ref, ya_ref, yd_ref, yc_ref, wa_ref, wd_ref, wc_ref, gt1_ref, g_ref, sc_ref, sh_ref, gt2_ref,
                      fg_ref, wg_ref, wu_ref, wdn_ref, o_ref, x1_s, h_s, acc_s, *, final_norm):
    f = pl.program_id(1)

    @pl.when(f == 0)
    def _():
        x1 = _mix_residual(x_ref, ya_ref, yd_ref, yc_ref, wa_ref, wd_ref, wc_ref, gt1_ref)
        x1_s[...] = x1
        h_s[...] = _rms_modulate(x1, g_ref[...], sc_ref[0], sh_ref[0]).astype(BF16)
        acc_s[...] = jnp.zeros_like(acc_s)

    h = h_s[...]
    a = (_silu(_dot(h, wg_ref[...])) * _dot(h, wu_ref[...])).astype(BF16)
    acc_s[...] += _dot(a, wdn_ref[...])

    @pl.when(f == pl.num_programs(1) - 1)
    def _():
        _finish(x1_s[...], acc_s[...], gt2_ref, fg_ref, o_ref, final_norm)


def _moe_ffn_kernel(x_ref, ya_ref, yd_ref, yc_ref, wa_ref, wd_ref, wc_ref, gt1_ref, g_ref, sc_ref, sh_ref, gt2_ref,
                    fg_ref, rw_ref, rb_ref, wg_ref, wu_ref, wdn_ref, o_ref, x1_s, h_s, acc_s, comb_s, *, final_norm):
    e = pl.program_id(1)
    lane = lax.broadcasted_iota(jnp.int32, comb_s.shape, 1)

    @pl.when(e == 0)
    def _():
        x1 = _mix_residual(x_ref, ya_ref, yd_ref, yc_ref, wa_ref, wd_ref, wc_ref, gt1_ref)
        x1_s[...] = x1
        h = _rms_modulate(x1, g_ref[...], sc_ref[0], sh_ref[0])
        h_s[...] = h.astype(BF16)
        acc_s[...] = jnp.zeros_like(acc_s)
        logits = jnp.where(lane < N_EXPERTS, _dot(h, rw_ref[...], HIGHEST) + rb_ref[...], -jnp.inf)
        m1 = jnp.max(logits, axis=-1, keepdims=True)
        i1 = jnp.min(jnp.where(logits == m1, lane, LANES), axis=-1, keepdims=True)
        rest = jnp.where(lane == i1, -jnp.inf, logits)
        m2 = jnp.max(rest, axis=-1, keepdims=True)
        i2 = jnp.min(jnp.where(rest == m2, lane, LANES), axis=-1, keepdims=True)
        e2 = jnp.exp(m2 - m1)
        comb_s[...] = jnp.where(lane == i1, 1.0 / (1.0 + e2), 0.0) + jnp.where(lane == i2, e2 / (1.0 + e2), 0.0)

    h = h_s[...]
    a = (_silu(_dot(h, wg_ref[0])) * _dot(h, wu_ref[0])).astype(BF16)
    cw = jnp.sum(jnp.where(lane == e, comb_s[...], 0.0), axis=-1, keepdims=True)
    acc_s[...] += cw * _dot(a, wdn_ref[0])

    @pl.when(e == pl.num_programs(1) - 1)
    def _():
        _finish(x1_s[...], acc_s[...], gt2_ref, fg_ref, o_ref, final_norm)


def _mixer_common_specs(d, tm, s):
    tok = lambda width: pl.BlockSpec((tm, width), lambda i, f: (i, 0))
    const = lambda shape: pl.BlockSpec(shape, lambda i, f: (0,) * len(shape))
    vec = pl.BlockSpec((1, 1, d), lambda i, f: ((i * tm) // s, 0, 0))
    return [tok(d), tok(ATTN_QW), tok(DN_W), tok(CV_C),
            const((ATTN_QW, d)), const((DN_W, d)), const((CV_C, d)),
            vec, const((1, d)), vec, vec, vec, const((1, d))], tok(d)


def _dense_ffn(x, ys, w_out, mods, g, final_g, wg, wu, wdn, s, final_norm):
    t, d = x.shape
    tm = TM_FFN
    ff = wg.shape[1]
    tf = ff // 2
    common, out_spec = _mixer_common_specs(d, tm, s)
    gt1, sc, sh, gt2 = mods
    return pl.pallas_call(
        functools.partial(_dense_ffn_kernel, final_norm=final_norm),
        grid=(t // tm, ff // tf),
        in_specs=common + [
            pl.BlockSpec((d, tf), lambda i, f: (0, f)),
            pl.BlockSpec((d, tf), lambda i, f: (0, f)),
            pl.BlockSpec((tf, d), lambda i, f: (f, 0)),
        ],
        out_specs=out_spec,
        out_shape=jax.ShapeDtypeStruct((t, d), F32),
        scratch_shapes=[pltpu.VMEM((tm, d), F32), pltpu.VMEM((tm, d), BF16), pltpu.VMEM((tm, d), F32)],
        compiler_params=pltpu.CompilerParams(vmem_limit_bytes=VMEM_LIMIT),
        name="dense_ffn",
    )(x, *ys, *w_out, gt1, g, sc, sh, gt2, final_g, wg, wu, wdn)


def _moe_ffn(x, ys, w_out, mods, g, final_g, rw, rb, wg, wu, wdn, s, final_norm):
    t, d = x.shape
    tm = TM_FFN
    n_e, _, ffe = wg.shape
    common, out_spec = _mixer_common_specs(d, tm, s)
    gt1, sc, sh, gt2 = mods
    return pl.pallas_call(
        functools.partial(_moe_ffn_kernel, final_norm=final_norm),
        grid=(t // tm, n_e),
        in_specs=common + [
            pl.BlockSpec((d, LANES), lambda i, e: (0, 0)),
            pl.BlockSpec((1, LANES), lambda i, e: (0, 0)),
            pl.BlockSpec((1, d, ffe), lambda i, e: (e, 0, 0)),
            pl.BlockSpec((1, d, ffe), lambda i, e: (e, 0, 0)),
            pl.BlockSpec((1, ffe, d), lambda i, e: (e, 0, 0)),
        ],
        out_specs=out_spec,
        out_shape=jax.ShapeDtypeStruct((t, d), F32),
        scratch_shapes=[pltpu.VMEM((tm, d), F32), pltpu.VMEM((tm, d), BF16), pltpu.VMEM((tm, d), F32),
                        pltpu.VMEM((tm, LANES), F32)],
        compiler_params=pltpu.CompilerParams(vmem_limit_bytes=VMEM_LIMIT),
        name="moe_ffn",
    )(x, *ys, *w_out, gt1, g, sc, sh, gt2, final_g, rw, rb, wg, wu, wdn)


def _rope_tables(positions):
    half = ROT_DIM // 2
    inv_freq = ROPE_THETA ** (-jnp.arange(0, ROT_DIM, 2, dtype=F32) / ROT_DIM)
    ang = positions.astype(F32)[..., None] * inv_freq
    cos, sin = jnp.cos(ang), jnp.sin(ang)
    pad = lambda t, lo, hi, val: jnp.pad(t, ((0, 0), (0, 0), (lo, hi)), constant_values=val)
    rest = HEAD_DIM - ROT_DIM
    c = pad(jnp.concatenate([cos, cos], -1), 0, rest, 1.0)
    a = pad(-sin, 0, HEAD_DIM - half, 0.0)
    b = pad(sin, half, rest, 0.0)
    two = lambda t: jnp.concatenate([t, t], -1)
    return two(c), two(a), two(b)


def _pack_w_in(w):
    d = w.shape[0]
    o1 = ATTN_W
    o2 = o1 + 3 * DN_W
    o3 = o2 + 2 * DN_H
    o4 = o3 + DN_W
    gates = jnp.zeros((d, LANES), w.dtype).at[:, :2 * DN_H].set(w[:, o2:o3])
    return jnp.concatenate([w[:, :o2], gates, w[:, o3:o4], w[:, o4:]], axis=1).astype(BF16)


def kernel(x, c, positions, ada_w, ada_b, norm_mix_g, norm_ffn_g, w_in, attn_sinks, dn_conv_w, dn_a_log, dn_dt_bias, dn_norm_g, cv_dw_w, cv_dw_b, cv_ln_g, cv_ln_b, w_out, ffn_w_gate, ffn_w_up, ffn_w_down, router_w, router_b, moe_w_gate, moe_w_up, moe_w_down, final_norm_g):
    b, s, d = x.shape
    depth = w_in.shape[0]
    mod = _adaln(c, ada_w, ada_b)
    rc, ra, rb = _rope_tables(positions)
    final_g = final_norm_g.reshape(1, d)
    vec = lambda t: t.reshape(b, 1, d)
    for l in range(depth):
        sh1, sc1, gt1 = (vec(t) for t in jnp.split(mod[l, 0], 3, axis=-1))
        sh2, sc2, gt2 = (vec(t) for t in jnp.split(mod[l, 1], 3, axis=-1))
        attn, dnqkv, gate, dz, cv = _inproj(x, norm_mix_g[l].reshape(1, d), sc1, sh1, rc, ra, rb, _pack_w_in(w_in[l]))
        y_attn = _attention(attn, attn_sinks[l].astype(F32))
        y_dn = _deltanet(dnqkv, gate, dz, dn_conv_w[l], dn_a_log[l], dn_dt_bias[l], dn_norm_g[l])
        y_cv = _conformer(cv, cv_dw_w[l], cv_dw_b[l], cv_ln_g[l], cv_ln_b[l])
        t = b * s
        ys = (y_attn.reshape(t, ATTN_QW), y_dn.reshape(t, DN_W), y_cv.reshape(t, CV_C))
        wo = w_out[l].astype(BF16)
        wos = (wo[:ATTN_QW], wo[ATTN_QW:ATTN_QW + DN_W], wo[ATTN_QW + DN_W:])
        mods = (gt1, sc2, sh2, gt2)
        g2 = norm_ffn_g[l].reshape(1, d)
        last = l == depth - 1
        j = l // 2
        if l % 2 == 0:
            x2 = _dense_ffn(x.reshape(t, d), ys, wos, mods, g2, final_g, ffn_w_gate[j].astype(BF16),
                            ffn_w_up[j].astype(BF16), ffn_w_down[j].astype(BF16), s, last)
        else:
            rw = jnp.zeros((d, LANES), F32).at[:, :N_EXPERTS].set(router_w[j])
            rbias = jnp.zeros((1, LANES), F32).at[0, :N_EXPERTS].set(router_b[j])
            x2 = _moe_ffn(x.reshape(t, d), ys, wos, mods, g2, final_g, rw, rbias, moe_w_gate[j].astype(BF16),
                          moe_w_up[j].astype(BF16), moe_w_down[j].astype(BF16), s, last)
        x = x2.reshape(b, s, d)
    return x
```

```python
import functools

import numpy as np
import jax
import jax.numpy as jnp
from jax import lax
from jax.experimental import pallas as pl
from jax.experimental.pallas import tpu as pltpu

F32 = jnp.float32
BF16 = jnp.bfloat16
HIGHEST = lax.Precision.HIGHEST

HEAD_DIM = 64
ATTN_HQ = 8
ATTN_HKV = 2
ATTN_GROUP = ATTN_HQ // ATTN_HKV
ATTN_BLOCK = 128
ROT_DIM = HEAD_DIM // 4
ROPE_THETA = 500000.0
DN_H = 4
DN_D = 64
DN_CONV = 4
DN_CHUNK = 64
CV_C = 256
CV_K = 31
ATTN_QW = ATTN_HQ * HEAD_DIM
ATTN_KVW = ATTN_HKV * HEAD_DIM
ATTN_W = ATTN_QW + 2 * ATTN_KVW
DN_W = DN_H * DN_D
N_EXPERTS = 8
EPS = 1e-6
LANES = 128
SUBLANES = 8
VMEM_LIMIT = 48 * 1024 * 1024

TM_PROJ = 512
TQ_ATTN = 512
R_DN = 512
R_CV = 512
TM_FFN = 512
CV_HALO = 32
DN_HALO = 8


def _dot(a, b, precision=None):
    return jnp.dot(a, b, preferred_element_type=F32, precision=precision)


def _dot_nt(a, b):
    return lax.dot_general(a, b, (((1,), (1,)), ((), ())), preferred_element_type=F32)


def _dot_tn(a, b):
    return lax.dot_general(a, b, (((0,), (0,)), ((), ())), preferred_element_type=F32)


def _sigmoid(x):
    return 1.0 / (1.0 + jnp.exp(-x))


def _silu(x):
    return x * _sigmoid(x)


def _rms_modulate(x, g, sc, sh):
    y = x * lax.rsqrt(jnp.mean(x * x, axis=-1, keepdims=True) + EPS)
    return (y * g) * (1.0 + sc) + sh


def _adaln_kernel(c_ref, w_ref, b_ref, o_ref):
    o_ref[0] = _dot(_silu(c_ref[...]), w_ref[0], HIGHEST) + b_ref[0]


def _adaln(c, ada_w, ada_b):
    depth, _, d, d3 = ada_w.shape
    b = c.shape[0]
    rows = 8
    cp = jnp.zeros((rows, d), F32).at[:b].set(c)
    w = ada_w.reshape(depth * 2, d, d3)
    bias = ada_b.reshape(depth * 2, 1, d3)
    tn = 1024
    out = pl.pallas_call(
        _adaln_kernel,
        grid=(depth * 2, d3 // tn),
        in_specs=[
            pl.BlockSpec((rows, d), lambda i, j: (0, 0)),
            pl.BlockSpec((1, d, tn), lambda i, j: (i, 0, j)),
            pl.BlockSpec((1, 1, tn), lambda i, j: (i, 0, j)),
        ],
        out_specs=pl.BlockSpec((1, rows, tn), lambda i, j: (i, 0, j)),
        out_shape=jax.ShapeDtypeStruct((depth * 2, rows, d3), F32),
        compiler_params=pltpu.CompilerParams(vmem_limit_bytes=VMEM_LIMIT),
        name="adaln",
    )(cp, w, bias)
    return out[:, :b].reshape(depth, 2, b, d3)


def _inproj_kernel(x_ref, g_ref, sc_ref, sh_ref, rc_ref, ra_ref, rb_ref, w_ref,
                   attn_ref, dnqkv_ref, gate_ref, dz_ref, cv_ref):
    h = _rms_modulate(x_ref[0], g_ref[...], sc_ref[0], sh_ref[0]).astype(BF16)
    a = _dot(h, w_ref[:, 0:ATTN_W])
    rc, ra, rb = rc_ref[0], ra_ref[0], rb_ref[0]
    half = ROT_DIM // 2
    for j in range((ATTN_QW + ATTN_KVW) // LANES):
        t = a[:, j * LANES:(j + 1) * LANES]
        r = t * rc + pltpu.roll(t, LANES - half, 1) * ra + pltpu.roll(t, half, 1) * rb
        if j < ATTN_QW // LANES:
            r = r * (HEAD_DIM ** -0.5)
        attn_ref[0, :, j * LANES:(j + 1) * LANES] = r.astype(BF16)
    attn_ref[0, :, ATTN_QW + ATTN_KVW:ATTN_W] = a[:, ATTN_QW + ATTN_KVW:ATTN_W].astype(BF16)
    o = ATTN_W
    dnqkv_ref[0] = _dot(h, w_ref[:, o:o + 3 * DN_W])
    o += 3 * DN_W
    gate_ref[0] = _dot(h, w_ref[:, o:o + LANES])
    o += LANES
    dz_ref[0] = _dot(h, w_ref[:, o:o + DN_W])
    o += DN_W
    cv_ref[0] = _dot(h, w_ref[:, o:o + 2 * CV_C])


def _inproj(x, g, sc, sh, rc, ra, rb, w):
    b, s, d = x.shape
    tm = TM_PROJ
    n = w.shape[1]
    row = lambda width: pl.BlockSpec((1, tm, width), lambda bi, i: (bi, i, 0))
    vec = pl.BlockSpec((1, 1, d), lambda bi, i: (bi, 0, 0))
    widths = (ATTN_W, 3 * DN_W, LANES, DN_W, 2 * CV_C)
    dtypes = (BF16, F32, F32, F32, F32)
    return pl.pallas_call(
        _inproj_kernel,
        grid=(b, s // tm),
        in_specs=[row(d), pl.BlockSpec((1, d), lambda bi, i: (0, 0)), vec, vec,
                  row(LANES), row(LANES), row(LANES),
                  pl.BlockSpec((d, n), lambda bi, i: (0, 0))],
        out_specs=[row(wd) for wd in widths],
        out_shape=[jax.ShapeDtypeStruct((b, s, wd), dt) for wd, dt in zip(widths, dtypes)],
        compiler_params=pltpu.CompilerParams(vmem_limit_bytes=VMEM_LIMIT),
        name="inproj",
    )(x, g, sc, sh, rc, ra, rb, w)


def _attn_kernel(sink_ref, cur_ref, prev_ref, o_ref):
    i = pl.program_id(1)
    blk = ATTN_BLOCK
    cur = cur_ref[0]
    prev = prev_ref[0]
    kv_all = jnp.concatenate([prev[:, ATTN_QW:], cur[:, ATTN_QW:]], axis=0)
    band_w = 2 * blk
    qi = lax.broadcasted_iota(jnp.int32, (blk, 2 * band_w), 0)
    kj = lax.broadcasted_iota(jnp.int32, (blk, 2 * band_w), 1) % band_w
    diff = qi + blk - kj
    band = (diff >= 0) & (diff < blk)
    band_first = band & (kj >= jnp.where(i == 0, blk, 0))
    first_head = lax.broadcasted_iota(jnp.int32, (blk, LANES), 1) < HEAD_DIM
    zeros = jnp.zeros((band_w, HEAD_DIM), BF16)

    def blockdiag(t):
        return jnp.concatenate([jnp.concatenate([t, zeros], axis=1), jnp.concatenate([zeros, t], axis=1)], axis=0)

    units = [(r, j, pr) for r in range(cur.shape[0] // blk) for j in range(ATTN_HKV) for pr in range(ATTN_GROUP // 2)]

    def scores(r, j, pr):
        kb = kv_all[r * blk:(r + 2) * blk, j * HEAD_DIM:(j + 1) * HEAD_DIM]
        slab = (j * (ATTN_GROUP // 2) + pr) * LANES
        return _dot_nt(cur[r * blk:(r + 1) * blk, slab:slab + LANES], blockdiag(kb))

    lane_row = lax.broadcasted_iota(jnp.int32, (1, 2 * band_w), 1)
    not_row0 = lax.broadcasted_iota(jnp.int32, (band_w, HEAD_DIM), 0) > 0
    nxt = scores(*units[0])
    for n, (r, j, pr) in enumerate(units):
        hq = j * ATTN_GROUP + 2 * pr
        fill = jnp.where(lane_row == 0, sink_ref[hq], jnp.where(lane_row == band_w, sink_ref[hq + 1], -jnp.inf))
        sc = jnp.where(band_first if r == 0 else band, nxt, fill)
        if n + 1 < len(units):
            nxt = scores(*units[n + 1])
        ps, rs = [], []
        for t in range(2):
            st = sc[:, t * band_w:(t + 1) * band_w]
            p = jnp.exp(st - jnp.max(st, axis=-1, keepdims=True))
            rs.append(1.0 / jnp.sum(p, axis=-1, keepdims=True))
            ps.append(p.astype(BF16))
        vb = kv_all[r * blk:(r + 2) * blk, ATTN_KVW + j * HEAD_DIM:ATTN_KVW + (j + 1) * HEAD_DIM]
        vb = jnp.where(not_row0, vb, jnp.zeros_like(vb))
        o = _dot(jnp.concatenate(ps, axis=1), blockdiag(vb)) * jnp.where(first_head, rs[0], rs[1])
        slab = (j * (ATTN_GROUP // 2) + pr) * LANES
        o_ref[0, r * blk:(r + 1) * blk, slab:slab + LANES] = o.astype(BF16)


def _attention(attn, sinks):
    b, s, _ = attn.shape
    tq = TQ_ATTN
    per = tq // ATTN_BLOCK
    return pl.pallas_call(
        _attn_kernel,
        grid=(b, s // tq),
        in_specs=[
            pl.BlockSpec(memory_space=pltpu.SMEM),
            pl.BlockSpec((1, tq, ATTN_W), lambda bi, i: (bi, i, 0)),
            pl.BlockSpec((1, ATTN_BLOCK, ATTN_W), lambda bi, i: (bi, jnp.maximum(i * per - 1, 0), 0)),
        ],
        out_specs=pl.BlockSpec((1, tq, ATTN_QW), lambda bi, i: (bi, i, 0)),
        out_shape=jax.ShapeDtypeStruct((b, s, ATTN_QW), BF16),
        compiler_params=pltpu.CompilerParams(vmem_limit_bytes=VMEM_LIMIT),
        name="attention",
    )(sinks, attn, attn)


def _level_masks(n):
    r = lax.broadcasted_iota(jnp.int32, (n, n), 0)
    c = lax.broadcasted_iota(jnp.int32, (n, n), 1)
    masks = []
    b = 1
    while b < n:
        masks.append((r // (2 * b) == c // (2 * b)) & ((r // b) % 2 == 1) & ((c // b) % 2 == 0))
        b *= 2
    return r == c, masks


def _unit_lower_inverse(lower, eye, masks):
    t = eye.astype(F32) - jnp.where(masks[0], lower, 0.0)
    for m in masks[1:]:
        tb = t.astype(BF16)
        t = t - _dot(tb, _dot(jnp.where(m, lower, 0.0).astype(BF16), tb).astype(BF16))
    return t


def _deltanet_kernel(cur_ref, prev_ref, gate_ref, dz_ref, cw_ref, alog_ref, dtb_ref, ng_ref,
                     expand_ref, tril_ref, slow_ref, ones_ref, o_ref,
                     xext, q_s, k_s, v_s, b_s, g_s, o_s, state):
    i = pl.program_id(1)
    rows = cur_ref.shape[1]
    ck = DN_CHUNK

    @pl.when(i == 0)
    def _():
        state[...] = jnp.zeros_like(state)

    xext[0:DN_HALO, :] = jnp.where(i > 0, prev_ref[0], 0.0)
    xext[DN_HALO:DN_HALO + rows, :] = cur_ref[0]
    acc = cw_ref[0:1, :] * xext[pl.ds(DN_HALO - DN_CONV + 1, rows), :]
    for t in range(1, DN_CONV):
        acc = acc + cw_ref[t:t + 1, :] * xext[pl.ds(DN_HALO - DN_CONV + 1 + t, rows), :]
    qkv = _silu(acc)
    ones_blk = ones_ref[...]

    def l2n(t):
        ss = _dot((t * t).astype(BF16), ones_blk)
        return t * lax.rsqrt(ss + EPS)

    q_s[...] = l2n(qkv[:, 0:DN_W]) * (DN_D ** -0.5)
    k_s[...] = l2n(qkv[:, DN_W:2 * DN_W])
    v_s[...] = qkv[:, 2 * DN_W:3 * DN_W]
    ge = _dot(gate_ref[0], expand_ref[...], HIGHEST)
    b_s[...] = _sigmoid(ge[:, 0:DN_W])
    da = ge[:, DN_W:2 * DN_W] + dtb_ref[...]
    softplus = jnp.maximum(da, 0.0) + jnp.log1p(jnp.exp(-jnp.abs(da)))
    g_s[...] = -jnp.exp(alog_ref[...]) * softplus

    tril = tril_ref[...]
    slow = slow_ref[...]
    ri = lax.broadcasted_iota(jnp.int32, (ck, DN_W), 0)
    ci = lax.broadcasted_iota(jnp.int32, (ck, DN_W), 1) % ck
    incl = ri >= ci
    r2 = lax.broadcasted_iota(jnp.int32, (ck, ck), 0)
    c2 = lax.broadcasted_iota(jnp.int32, (ck, ck), 1)
    strict = r2 > c2
    eye, masks = _level_masks(ck)

    nc = rows // ck
    lanes = [slice(h * DN_D, (h + 1) * DN_D) for h in range(DN_H)]
    kbeta16, k16, q16, dec, vbeta, kbg, qg, kg, egl = ([] for _ in range(9))
    for c in range(nc):
        r0 = c * ck
        q = q_s[r0:r0 + ck, :]
        k = k_s[r0:r0 + ck, :]
        beta = b_s[r0:r0 + ck, :]
        g = g_s[r0:r0 + ck, :]
        gc = _dot(tril, g, HIGHEST)
        gdiff = _dot(tril, g * slow, HIGHEST)
        decay = jnp.exp(jnp.where(incl, gdiff, -jnp.inf))
        egc = jnp.exp(gc)
        glast = gc[ck - 1:ck, :]
        kbeta = k * beta
        per_head = ((kbeta16, kbeta.astype(BF16)), (k16, k.astype(BF16)), (q16, q.astype(BF16)), (dec, decay),
                    (vbeta, (v_s[r0:r0 + ck, :] * beta).astype(BF16)), (kbg, (kbeta * egc).astype(BF16)),
                    (qg, q * egc), (kg, (k * jnp.exp(glast - gc)).astype(BF16)), (egl, jnp.exp(glast)))
        for dst, full in per_head:
            dst.extend(full[:, sl] for sl in lanes)
    n_inst = nc * DN_H
    inst = range(n_inst)
    lower = [jnp.where(strict, _dot_nt(kbeta16[n], k16[n]) * dec[n], 0.0) for n in inst]
    a_intra = [(_dot_nt(q16[n], k16[n]) * dec[n]).astype(BF16) for n in inst]
    lower16 = [t.astype(BF16) for t in lower]
    tinv = [eye.astype(F32) - jnp.where(masks[0], lower[n], 0.0) for n in inst]
    for m in masks[1:]:
        t16 = [t.astype(BF16) for t in tinv]
        x16 = [_dot(jnp.where(m, lower16[n], jnp.zeros_like(lower16[n])), t16[n]).astype(BF16) for n in inst]
        tinv = [tinv[n] - _dot(t16[n], x16[n]) for n in inst]
    t16 = [t.astype(BF16) for t in tinv]
    w16 = [_dot(t16[n], kbg[n]).astype(BF16) for n in inst]
    u16 = [_dot(t16[n], vbeta[n]).astype(BF16) for n in inst]
    kw16 = [_dot_tn(kg[n], w16[n]).astype(BF16) for n in inst]
    ku = [_dot_tn(kg[n], u16[n]) for n in inst]
    qeff16 = [(qg[n] - _dot(a_intra[n], w16[n])).astype(BF16) for n in inst]
    au = [_dot(a_intra[n], u16[n]) for n in inst]
    st = [state[h] for h in range(DN_H)]
    for c in range(nc):
        outs = []
        for h in range(DN_H):
            n = c * DN_H + h
            s16 = st[h].astype(BF16)
            outs.append(_dot(qeff16[n], s16) + au[n])
            st[h] = st[h] * egl[n] - _dot(kw16[n], s16) + ku[n]
        o_s[c * ck:(c + 1) * ck, :] = jnp.concatenate(outs, axis=1)
    for h in range(DN_H):
        state[h] = st[h]
    o = o_s[...]
    ms = _dot((o * o).astype(BF16), ones_blk) * (1.0 / DN_D)
    y = o * lax.rsqrt(ms + EPS) * ng_ref[...]
    o_ref[0] = (y * _silu(dz_ref[0])).astype(BF16)


def _deltanet(dnqkv, gate, dz, conv_w, a_log, dt_bias, norm_g):
    b, s, _ = dnqkv.shape
    rows = R_DN
    ck = DN_CHUNK
    lane_head = np.arange(2 * DN_W) // DN_D
    expand = (np.arange(LANES)[:, None] == lane_head[None, :]).astype(np.float32)
    tril = np.tril(np.ones((ck, ck), np.float32))
    slow = (np.arange(ck)[:, None] > (np.arange(DN_W)[None, :] % ck)).astype(np.float32)
    ones_blk = (np.arange(DN_W)[:, None] // DN_D == np.arange(DN_W)[None, :] // DN_D).astype(np.float32)
    rep = lambda t: jnp.repeat(t.astype(F32), DN_D).reshape(1, DN_W)
    const = lambda shape: pl.BlockSpec(shape, lambda bi, i: (0,) * len(shape))
    row = lambda width: pl.BlockSpec((1, rows, width), lambda bi, i: (bi, i, 0))
    per = rows // DN_HALO
    return pl.pallas_call(
        _deltanet_kernel,
        grid=(b, s // rows),
        in_specs=[
            row(3 * DN_W),
            pl.BlockSpec((1, DN_HALO, 3 * DN_W), lambda bi, i: (bi, jnp.maximum(i * per - 1, 0), 0)),
            row(LANES), row(DN_W),
            const((DN_CONV, 3 * DN_W)), const((1, DN_W)), const((1, DN_W)), const((1, DN_W)),
            const((LANES, 2 * DN_W)), const((ck, ck)), const((ck, DN_W)), const((DN_W, DN_W)),
        ],
        out_specs=row(DN_W),
        out_shape=jax.ShapeDtypeStruct((b, s, DN_W), BF16),
        scratch_shapes=[
            pltpu.VMEM((rows + DN_HALO, 3 * DN_W), F32),
            pltpu.VMEM((rows, DN_W), F32), pltpu.VMEM((rows, DN_W), F32), pltpu.VMEM((rows, DN_W), F32),
            pltpu.VMEM((rows, DN_W), F32), pltpu.VMEM((rows, DN_W), F32), pltpu.VMEM((rows, DN_W), F32),
            pltpu.VMEM((DN_H, DN_D, DN_D), F32),
        ],
        compiler_params=pltpu.CompilerParams(vmem_limit_bytes=VMEM_LIMIT),
        name="deltanet",
    )(dnqkv, dnqkv, gate, dz, conv_w, rep(a_log), rep(dt_bias), jnp.tile(norm_g.astype(F32), DN_H).reshape(1, DN_W),
      jnp.asarray(expand), jnp.asarray(tril), jnp.asarray(slow), jnp.asarray(ones_blk, dtype=BF16))


def _conformer_kernel(cur_ref, prev_ref, w_ref, b_ref, lg_ref, lb_ref, o_ref, u_s, sh_s):
    i = pl.program_id(1)
    rows = cur_ref.shape[1]
    prev = jnp.where(i > 0, prev_ref[0], 0.0)
    u_s[0:CV_HALO, :] = prev[:, 0:CV_C] * _sigmoid(prev[:, CV_C:])
    cur = cur_ref[0]
    u_s[CV_HALO:CV_HALO + rows, :] = cur[:, 0:CV_C] * _sigmoid(cur[:, CV_C:])
    span = rows + CV_HALO - SUBLANES
    for ph in range(1, SUBLANES):
        sh_s[ph - 1, 0:span, :] = u_s[pl.ds(ph, span), :]
    base = CV_HALO - CV_K + 1
    acc = b_ref[...]
    for t in range(CV_K):
        blk, ph = divmod(base + t, SUBLANES)
        src = u_s if ph == 0 else sh_s.at[ph - 1]
        acc = acc + w_ref[t:t + 1, :] * src[blk * SUBLANES:blk * SUBLANES + rows, :]
    mu = jnp.mean(acc, axis=-1, keepdims=True)
    xc = acc - mu
    y = xc * lax.rsqrt(jnp.mean(xc * xc, axis=-1, keepdims=True) + EPS)
    o_ref[0] = _silu(y * lg_ref[...] + lb_ref[...]).astype(BF16)


def _conformer(cv, w, bias, ln_g, ln_b):
    b, s, _ = cv.shape
    rows = R_CV
    per = rows // CV_HALO
    const = lambda shape: pl.BlockSpec(shape, lambda bi, i: (0,) * len(shape))
    return pl.pallas_call(
        _conformer_kernel,
        grid=(b, s // rows),
        in_specs=[
            pl.BlockSpec((1, rows, 2 * CV_C), lambda bi, i: (bi, i, 0)),
            pl.BlockSpec((1, CV_HALO, 2 * CV_C), lambda bi, i: (bi, jnp.maximum(i * per - 1, 0), 0)),
            const((CV_K, CV_C)), const((1, CV_C)), const((1, CV_C)), const((1, CV_C)),
        ],
        out_specs=pl.BlockSpec((1, rows, CV_C), lambda bi, i: (bi, i, 0)),
        out_shape=jax.ShapeDtypeStruct((b, s, CV_C), BF16),
        scratch_shapes=[pltpu.VMEM((rows + CV_HALO, CV_C), F32),
                        pltpu.VMEM((SUBLANES - 1, rows + CV_HALO - SUBLANES, CV_C), F32)],
        compiler_params=pltpu.CompilerParams(vmem_limit_bytes=VMEM_LIMIT),
        name="conformer",
    )(cv, cv, w, bias.reshape(1, CV_C), ln_g.reshape(1, CV_C), ln_b.reshape(1, CV_C))


def _mix_residual(x_ref, ya_ref, yd_ref, yc_ref, wa_ref, wd_ref, wc_ref, gt1_ref):
    y = _dot(ya_ref[...], wa_ref[...]) + _dot(yd_ref[...], wd_ref[...]) + _dot(yc_ref[...], wc_ref[...])
    return x_ref[...] + gt1_ref[0] * y


def _finish(x1, acc, gt2_ref, fg_ref, o_ref, final_norm):
    x2 = x1 + gt2_ref[0] * acc
    if final_norm:
        x2 = x2 * lax.rsqrt(jnp.mean(x2 * x2, axis=-1, keepdims=True) + EPS) * fg_ref[...]
    o_ref[...] = x2


def _dense_ffn_kernel(x_ref, ya_ref, yd_ref, yc_ref, wa_ref, wd_ref, wc_ref, gt1_ref, g_ref, sc_ref, sh_ref, gt2_ref,
                      fg_ref, wg_ref, wu_ref, wdn_ref, o_ref, x1_s, h_s, acc_s, *, final_norm):
    f = pl.program_id(1)

    @pl.when(f == 0)
    def _():
        x1 = _mix_residual(x_ref, ya_ref, yd_ref, yc_ref, wa_ref, wd_ref, wc_ref, gt1_ref)
        x1_s[...] = x1
        h_s[...] = _rms_modulate(x1, g_ref[...], sc_ref[0], sh_ref[0]).astype(BF16)
        acc_s[...] = jnp.zeros_like(acc_s)

    h = h_s[...]
    a = (_silu(_dot(h, wg_ref[...])) * _dot(h, wu_ref[...])).astype(BF16)
    acc_s[...] += _dot(a, wdn_ref[...])

    @pl.when(f == pl.num_programs(1) - 1)
    def _():
        _finish(x1_s[...], acc_s[...], gt2_ref, fg_ref, o_ref, final_norm)


def _moe_ffn_kernel(x_ref, ya_ref, yd_ref, yc_ref, wa_ref, wd_ref, wc_ref, gt1_ref, g_ref, sc_ref, sh_ref, gt2_ref,
                    fg_ref, rw_ref, rb_ref, wg_ref, wu_ref, wdn_ref, o_ref, x1_s, h_s, acc_s, comb_s, *, final_norm):
    e = pl.program_id(1)
    lane = lax.broadcasted_iota(jnp.int32, comb_s.shape, 1)

    @pl.when(e == 0)
    def _():
        x1 = _mix_residual(x_ref, ya_ref, yd_ref, yc_ref, wa_ref, wd_ref, wc_ref, gt1_ref)
        x1_s[...] = x1
        h = _rms_modulate(x1, g_ref[...], sc_ref[0], sh_ref[0])
        h_s[...] = h.astype(BF16)
        acc_s[...] = jnp.zeros_like(acc_s)
        logits = jnp.where(lane < N_EXPERTS, _dot(h, rw_ref[...], HIGHEST) + rb_ref[...], -jnp.inf)
        m1 = jnp.max(logits, axis=-1, keepdims=True)
        i1 = jnp.min(jnp.where(logits == m1, lane, LANES), axis=-1, keepdims=True)
        rest = jnp.where(lane == i1, -jnp.inf, logits)
        m2 = jnp.max(rest, axis=-1, keepdims=True)
        i2 = jnp.min(jnp.where(rest == m2, lane, LANES), axis=-1, keepdims=True)
        e2 = jnp.exp(m2 - m1)
        comb_s[...] = jnp.where(lane == i1, 1.0 / (1.0 + e2), 0.0) + jnp.where(lane == i2, e2 / (1.0 + e2), 0.0)

    h = h_s[...]
    a = (_silu(_dot(h, wg_ref[0])) * _dot(h, wu_ref[0])).astype(BF16)
    cw = jnp.sum(jnp.where(lane == e, comb_s[...], 0.0), axis=-1, keepdims=True)
    acc_s[...] += cw * _dot(a, wdn_ref[0])

    @pl.when(e == pl.num_programs(1) - 1)
    def _():
        _finish(x1_s[...], acc_s[...], gt2_ref, fg_ref, o_ref, final_norm)


def _mixer_common_specs(d, tm, s):
    tok = lambda width: pl.BlockSpec((tm, width), lambda i, f: (i, 0))
    const = lambda shape: pl.BlockSpec(shape, lambda i, f: (0,) * len(shape))
    vec = pl.BlockSpec((1, 1, d), lambda i, f: ((i * tm) // s, 0, 0))
    return [tok(d), tok(ATTN_QW), tok(DN_W), tok(CV_C),
            const((ATTN_QW, d)), const((DN_W, d)), const((CV_C, d)),
            vec, const((1, d)), vec, vec, vec, const((1, d))], tok(d)


def _dense_ffn(x, ys, w_out, mods, g, final_g, wg, wu, wdn, s, final_norm):
    t, d = x.shape
    tm = TM_FFN
    ff = wg.shape[1]
    tf = ff // 2
    common, out_spec = _mixer_common_specs(d, tm, s)
    gt1, sc, sh, gt2 = mods
    return pl.pallas_call(
        functools.partial(_dense_ffn_kernel, final_norm=final_norm),
        grid=(t // tm, ff // tf),
        in_specs=common + [
            pl.BlockSpec((d, tf), lambda i, f: (0, f)),
            pl.BlockSpec((d, tf), lambda i, f: (0, f)),
            pl.BlockSpec((tf, d), lambda i, f: (f, 0)),
        ],
        out_specs=out_spec,
        out_shape=jax.ShapeDtypeStruct((t, d), F32),
        scratch_shapes=[pltpu.VMEM((tm, d), F32), pltpu.VMEM((tm, d), BF16), pltpu.VMEM((tm, d), F32)],
        compiler_params=pltpu.CompilerParams(vmem_limit_bytes=VMEM_LIMIT),
        name="dense_ffn",
    )(x, *ys, *w_out, gt1, g, sc, sh, gt2, final_g, wg, wu, wdn)


def _moe_ffn(x, ys, w_out, mods, g, final_g, rw, rb, wg, wu, wdn, s, final_norm):
    t, d = x.shape
    tm = TM_FFN
    n_e, _, ffe = wg.shape
    common, out_spec = _mixer_common_specs(d, tm, s)
    gt1, sc, sh, gt2 = mods
    return pl.pallas_call(
        functools.partial(_moe_ffn_kernel, final_norm=final_norm),
        grid=(t // tm, n_e),
        in_specs=common + [
            pl.BlockSpec((d, LANES), lambda i, e: (0, 0)),
            pl.BlockSpec((1, LANES), lambda i, e: (0, 0)),
            pl.BlockSpec((1, d, ffe), lambda i, e: (e, 0, 0)),
            pl.BlockSpec((1, d, ffe), lambda i, e: (e, 0, 0)),
            pl.BlockSpec((1, ffe, d), lambda i, e: (e, 0, 0)),
        ],
        out_specs=out_spec,
        out_shape=jax.ShapeDtypeStruct((t, d), F32),
        scratch_shapes=[pltpu.VMEM((tm, d), F32), pltpu.VMEM((tm, d), BF16), pltpu.VMEM((tm, d), F32),
                        pltpu.VMEM((tm, LANES), F32)],
        compiler_params=pltpu.CompilerParams(vmem_limit_bytes=VMEM_LIMIT),
        name="moe_ffn",
    )(x, *ys, *w_out, gt1, g, sc, sh, gt2, final_g, rw, rb, wg, wu, wdn)


def _rope_tables(positions):
    half = ROT_DIM // 2
    inv_freq = ROPE_THETA ** (-jnp.arange(0, ROT_DIM, 2, dtype=F32) / ROT_DIM)
    ang = positions.astype(F32)[..., None] * inv_freq
    cos, sin = jnp.cos(ang), jnp.sin(ang)
    pad = lambda t, lo, hi, val: jnp.pad(t, ((0, 0), (0, 0), (lo, hi)), constant_values=val)
    rest = HEAD_DIM - ROT_DIM
    c = pad(jnp.concatenate([cos, cos], -1), 0, rest, 1.0)
    a = pad(-sin, 0, HEAD_DIM - half, 0.0)
    b = pad(sin, half, rest, 0.0)
    two = lambda t: jnp.concatenate([t, t], -1)
    return two(c), two(a), two(b)


def _pack_w_in(w):
    d = w.shape[0]
    o1 = ATTN_W
    o2 = o1 + 3 * DN_W
    o3 = o2 + 2 * DN_H
    o4 = o3 + DN_W
    gates = jnp.zeros((d, LANES), w.dtype).at[:, :2 * DN_H].set(w[:, o2:o3])
    return jnp.concatenate([w[:, :o2], gates, w[:, o3:o4], w[:, o4:]], axis=1).astype(BF16)


def kernel(x, c, positions, ada_w, ada_b, norm_mix_g, norm_ffn_g, w_in, attn_sinks, dn_conv_w, dn_a_log, dn_dt_bias, dn_norm_g, cv_dw_w, cv_dw_b, cv_ln_g, cv_ln_b, w_out, ffn_w_gate, ffn_w_up, ffn_w_down, router_w, router_b, moe_w_gate, moe_w_up, moe_w_down, final_norm_g):
    b, s, d = x.shape
    depth = w_in.shape[0]
    mod = _adaln(c, ada_w, ada_b)
    rc, ra, rb = _rope_tables(positions)
    final_g = final_norm_g.reshape(1, d)
    vec = lambda t: t.reshape(b, 1, d)
    for l in range(depth):
        sh1, sc1, gt1 = (vec(t) for t in jnp.split(mod[l, 0], 3, axis=-1))
        sh2, sc2, gt2 = (vec(t) for t in jnp.split(mod[l, 1], 3, axis=-1))
        attn, dnqkv, gate, dz, cv = _inproj(x, norm_mix_g[l].reshape(1, d), sc1, sh1, rc, ra, rb, _pack_w_in(w_in[l]))
        y_attn = _attention(attn, attn_sinks[l].astype(F32))
        y_dn = _deltanet(dnqkv, gate, dz, dn_conv_w[l], dn_a_log[l], dn_dt_bias[l], dn_norm_g[l])
        y_cv = _conformer(cv, cv_dw_w[l], cv_dw_b[l], cv_ln_g[l], cv_ln_b[l])
        t = b * s
        ys = (y_attn.reshape(t, ATTN_QW), y_dn.reshape(t, DN_W), y_cv.reshape(t, CV_C))
        wo = w_out[l].astype(BF16)
        wos = (wo[:ATTN_QW], wo[ATTN_QW:ATTN_QW + DN_W], wo[ATTN_QW + DN_W:])
        mods = (gt1, sc2, sh2, gt2)
        g2 = norm_ffn_g[l].reshape(1, d)
        last = l == depth - 1
        j = l // 2
        if l % 2 == 0:
            x2 = _dense_ffn(x.reshape(t, d), ys, wos, mods, g2, final_g, ffn_w_gate[j].astype(BF16),
                            ffn_w_up[j].astype(BF16), ffn_w_down[j].astype(BF16), s, last)
        else:
            rw = jnp.zeros((d, LANES), F32).at[:, :N_EXPERTS].set(router_w[j])
            rbias = jnp.zeros((1, LANES), F32).at[0, :N_EXPERTS].set(router_b[j])
            x2 = _moe_ffn(x.reshape(t, d), ys, wos, mods, g2, final_g, rw, rbias, moe_w_gate[j].astype(BF16),
                          moe_w_up[j].astype(BF16), moe_w_down[j].astype(BF16), s, last)
        x = x2.reshape(b, s, d)
    return x
```

```python
import functools

import numpy as np
import jax
import jax.numpy as jnp
from jax import lax
from jax.experimental import pallas as pl
from jax.experimental.pallas import tpu as pltpu

F32 = jnp.float32
BF16 = jnp.bfloat16
HIGHEST = lax.Precision.HIGHEST

HEAD_DIM = 64
ATTN_HQ = 8
ATTN_HKV = 2
ATTN_GROUP = ATTN_HQ // ATTN_HKV
ATTN_BLOCK = 128
ROT_DIM = HEAD_DIM // 4
ROPE_THETA = 500000.0
DN_H = 4
DN_D = 64
DN_CONV = 4
DN_CHUNK = 64
CV_C = 256
CV_K = 31
ATTN_QW = ATTN_HQ * HEAD_DIM
ATTN_KVW = ATTN_HKV * HEAD_DIM
ATTN_W = ATTN_QW + 2 * ATTN_KVW
DN_W = DN_H * DN_D
N_EXPERTS = 8
EPS = 1e-6
LANES = 128
SUBLANES = 8
VMEM_LIMIT = 48 * 1024 * 1024

TM_PROJ = 512
TQ_ATTN = 512
R_DN = 512
R_CV = 512
TM_FFN = 512
CV_HALO = 32
DN_HALO = 8
TS_MOE = 1024
MOE_SUB = 256
MOE_WIN = 128
MOE_CHUNK = 256
MOE_TAIL = 384
N_TOP = 2


def _dot(a, b, precision=None):
    return jnp.dot(a, b, preferred_element_type=F32, precision=precision)


def _dot_nt(a, b):
    return lax.dot_general(a, b, (((1,), (1,)), ((), ())), preferred_element_type=F32)


def _dot_tn(a, b):
    return lax.dot_general(a, b, (((0,), (0,)), ((), ())), preferred_element_type=F32)


def _sigmoid(x):
    return 1.0 / (1.0 + jnp.exp(-x))


def _silu(x):
    return x * _sigmoid(x)


def _rms_modulate(x, g, sc, sh):
    y = x * lax.rsqrt(jnp.mean(x * x, axis=-1, keepdims=True) + EPS)
    return (y * g) * (1.0 + sc) + sh


def _adaln_kernel(c_ref, w_ref, b_ref, o_ref):
    o_ref[0] = _dot(_silu(c_ref[...]), w_ref[0], HIGHEST) + b_ref[0]


def _adaln(c, ada_w, ada_b):
    depth, _, d, d3 = ada_w.shape
    b = c.shape[0]
    rows = 8
    cp = jnp.zeros((rows, d), F32).at[:b].set(c)
    w = ada_w.reshape(depth * 2, d, d3)
    bias = ada_b.reshape(depth * 2, 1, d3)
    tn = 1024
    out = pl.pallas_call(
        _adaln_kernel,
        grid=(depth * 2, d3 // tn),
        in_specs=[
            pl.BlockSpec((rows, d), lambda i, j: (0, 0)),
            pl.BlockSpec((1, d, tn), lambda i, j: (i, 0, j)),
            pl.BlockSpec((1, 1, tn), lambda i, j: (i, 0, j)),
        ],
        out_specs=pl.BlockSpec((1, rows, tn), lambda i, j: (i, 0, j)),
        out_shape=jax.ShapeDtypeStruct((depth * 2, rows, d3), F32),
        compiler_params=pltpu.CompilerParams(vmem_limit_bytes=VMEM_LIMIT),
        name="adaln",
    )(cp, w, bias)
    return out[:, :b].reshape(depth, 2, b, d3)


def _inproj_kernel(x_ref, g_ref, sc_ref, sh_ref, rc_ref, ra_ref, rb_ref, w_ref,
                   attn_ref, dnqkv_ref, gate_ref, dz_ref, cv_ref):
    h = _rms_modulate(x_ref[0], g_ref[...], sc_ref[0], sh_ref[0]).astype(BF16)
    a = _dot(h, w_ref[:, 0:ATTN_W])
    rc, ra, rb = rc_ref[0], ra_ref[0], rb_ref[0]
    half = ROT_DIM // 2
    for j in range((ATTN_QW + ATTN_KVW) // LANES):
        t = a[:, j * LANES:(j + 1) * LANES]
        r = t * rc + pltpu.roll(t, LANES - half, 1) * ra + pltpu.roll(t, half, 1) * rb
        if j < ATTN_QW // LANES:
            r = r * (HEAD_DIM ** -0.5)
        attn_ref[0, :, j * LANES:(j + 1) * LANES] = r.astype(BF16)
    attn_ref[0, :, ATTN_QW + ATTN_KVW:ATTN_W] = a[:, ATTN_QW + ATTN_KVW:ATTN_W].astype(BF16)
    o = ATTN_W
    dnqkv_ref[0] = _dot(h, w_ref[:, o:o + 3 * DN_W])
    o += 3 * DN_W
    gate_ref[0] = _dot(h, w_ref[:, o:o + LANES])
    o += LANES
    dz_ref[0] = _dot(h, w_ref[:, o:o + DN_W])
    o += DN_W
    cv_ref[0] = _dot(h, w_ref[:, o:o + 2 * CV_C])


def _inproj(x, g, sc, sh, rc, ra, rb, w):
    b, s, d = x.shape
    tm = TM_PROJ
    n = w.shape[1]
    row = lambda width: pl.BlockSpec((1, tm, width), lambda bi, i: (bi, i, 0))
    vec = pl.BlockSpec((1, 1, d), lambda bi, i: (bi, 0, 0))
    widths = (ATTN_W, 3 * DN_W, LANES, DN_W, 2 * CV_C)
    dtypes = (BF16, F32, F32, F32, F32)
    return pl.pallas_call(
        _inproj_kernel,
        grid=(b, s // tm),
        in_specs=[row(d), pl.BlockSpec((1, d), lambda bi, i: (0, 0)), vec, vec,
                  row(LANES), row(LANES), row(LANES),
                  pl.BlockSpec((d, n), lambda bi, i: (0, 0))],
        out_specs=[row(wd) for wd in widths],
        out_shape=[jax.ShapeDtypeStruct((b, s, wd), dt) for wd, dt in zip(widths, dtypes)],
        compiler_params=pltpu.CompilerParams(vmem_limit_bytes=VMEM_LIMIT),
        name="inproj",
    )(x, g, sc, sh, rc, ra, rb, w)


def _attn_kernel(sink_ref, cur_ref, prev_ref, o_ref):
    i = pl.program_id(1)
    blk = ATTN_BLOCK
    cur = cur_ref[0]
    prev = prev_ref[0]
    kv_all = jnp.concatenate([prev[:, ATTN_QW:], cur[:, ATTN_QW:]], axis=0)
    band_w = 2 * blk
    qi = lax.broadcasted_iota(jnp.int32, (blk, 2 * band_w), 0)
    kj = lax.broadcasted_iota(jnp.int32, (blk, 2 * band_w), 1) % band_w
    diff = qi + blk - kj
    band = (diff >= 0) & (diff < blk)
    band_first = band & (kj >= jnp.where(i == 0, blk, 0))
    first_head = lax.broadcasted_iota(jnp.int32, (blk, LANES), 1) < HEAD_DIM
    zeros = jnp.zeros((band_w, HEAD_DIM), BF16)

    def blockdiag(t):
        return jnp.concatenate([jnp.concatenate([t, zeros], axis=1), jnp.concatenate([zeros, t], axis=1)], axis=0)

    units = [(r, j, pr) for r in range(cur.shape[0] // blk) for j in range(ATTN_HKV) for pr in range(ATTN_GROUP // 2)]

    def scores(r, j, pr):
        kb = kv_all[r * blk:(r + 2) * blk, j * HEAD_DIM:(j + 1) * HEAD_DIM]
        slab = (j * (ATTN_GROUP // 2) + pr) * LANES
        return _dot_nt(cur[r * blk:(r + 1) * blk, slab:slab + LANES], blockdiag(kb))

    lane_row = lax.broadcasted_iota(jnp.int32, (1, 2 * band_w), 1)
    not_row0 = lax.broadcasted_iota(jnp.int32, (band_w, HEAD_DIM), 0) > 0
    nxt = scores(*units[0])
    for n, (r, j, pr) in enumerate(units):
        hq = j * ATTN_GROUP + 2 * pr
        fill = jnp.where(lane_row == 0, sink_ref[hq], jnp.where(lane_row == band_w, sink_ref[hq + 1], -jnp.inf))
        sc = jnp.where(band_first if r == 0 else band, nxt, fill)
        if n + 1 < len(units):
            nxt = scores(*units[n + 1])
        ps, rs = [], []
        for t in range(2):
            st = sc[:, t * band_w:(t + 1) * band_w]
            p = jnp.exp(st - jnp.max(st, axis=-1, keepdims=True))
            rs.append(1.0 / jnp.sum(p, axis=-1, keepdims=True))
            ps.append(p.astype(BF16))
        vb = kv_all[r * blk:(r + 2) * blk, ATTN_KVW + j * HEAD_DIM:ATTN_KVW + (j + 1) * HEAD_DIM]
        vb = jnp.where(not_row0, vb, jnp.zeros_like(vb))
        o = _dot(jnp.concatenate(ps, axis=1), blockdiag(vb)) * jnp.where(first_head, rs[0], rs[1])
        slab = (j * (ATTN_GROUP // 2) + pr) * LANES
        o_ref[0, r * blk:(r + 1) * blk, slab:slab + LANES] = o.astype(BF16)


def _attention(attn, sinks):
    b, s, _ = attn.shape
    tq = TQ_ATTN
    per = tq // ATTN_BLOCK
    return pl.pallas_call(
        _attn_kernel,
        grid=(b, s // tq),
        in_specs=[
            pl.BlockSpec(memory_space=pltpu.SMEM),
            pl.BlockSpec((1, tq, ATTN_W), lambda bi, i: (bi, i, 0)),
            pl.BlockSpec((1, ATTN_BLOCK, ATTN_W), lambda bi, i: (bi, jnp.maximum(i * per - 1, 0), 0)),
        ],
        out_specs=pl.BlockSpec((1, tq, ATTN_QW), lambda bi, i: (bi, i, 0)),
        out_shape=jax.ShapeDtypeStruct((b, s, ATTN_QW), BF16),
        compiler_params=pltpu.CompilerParams(vmem_limit_bytes=VMEM_LIMIT),
        name="attention",
    )(sinks, attn, attn)


def _level_masks(n):
    r = lax.broadcasted_iota(jnp.int32, (n, n), 0)
    c = lax.broadcasted_iota(jnp.int32, (n, n), 1)
    masks = []
    b = 1
    while b < n:
        masks.append((r // (2 * b) == c // (2 * b)) & ((r // b) % 2 == 1) & ((c // b) % 2 == 0))
        b *= 2
    return r == c, masks


def _unit_lower_inverse(lower, eye, masks):
    t = eye.astype(F32) - jnp.where(masks[0], lower, 0.0)
    for m in masks[1:]:
        tb = t.astype(BF16)
        t = t - _dot(tb, _dot(jnp.where(m, lower, 0.0).astype(BF16), tb).astype(BF16))
    return t


def _deltanet_kernel(cur_ref, prev_ref, gate_ref, dz_ref, cw_ref, alog_ref, dtb_ref, ng_ref,
                     expand_ref, tril_ref, slow_ref, ones_ref, o_ref,
                     xext, q_s, k_s, v_s, b_s, g_s, o_s, state):
    i = pl.program_id(1)
    rows = cur_ref.shape[1]
    ck = DN_CHUNK

    @pl.when(i == 0)
    def _():
        state[...] = jnp.zeros_like(state)

    xext[0:DN_HALO, :] = jnp.where(i > 0, prev_ref[0], 0.0)
    xext[DN_HALO:DN_HALO + rows, :] = cur_ref[0]
    acc = cw_ref[0:1, :] * xext[pl.ds(DN_HALO - DN_CONV + 1, rows), :]
    for t in range(1, DN_CONV):
        acc = acc + cw_ref[t:t + 1, :] * xext[pl.ds(DN_HALO - DN_CONV + 1 + t, rows), :]
    qkv = _silu(acc)
    ones_blk = ones_ref[...]

    def l2n(t):
        ss = _dot((t * t).astype(BF16), ones_blk)
        return t * lax.rsqrt(ss + EPS)

    q_s[...] = l2n(qkv[:, 0:DN_W]) * (DN_D ** -0.5)
    k_s[...] = l2n(qkv[:, DN_W:2 * DN_W])
    v_s[...] = qkv[:, 2 * DN_W:3 * DN_W]
    ge = _dot(gate_ref[0], expand_ref[...], HIGHEST)
    b_s[...] = _sigmoid(ge[:, 0:DN_W])
    da = ge[:, DN_W:2 * DN_W] + dtb_ref[...]
    softplus = jnp.maximum(da, 0.0) + jnp.log1p(jnp.exp(-jnp.abs(da)))
    g_s[...] = -jnp.exp(alog_ref[...]) * softplus

    tril = tril_ref[...]
    slow = slow_ref[...]
    ri = lax.broadcasted_iota(jnp.int32, (ck, DN_W), 0)
    ci = lax.broadcasted_iota(jnp.int32, (ck, DN_W), 1) % ck
    incl = ri >= ci
    r2 = lax.broadcasted_iota(jnp.int32, (ck, ck), 0)
    c2 = lax.broadcasted_iota(jnp.int32, (ck, ck), 1)
    strict = r2 > c2
    eye, masks = _level_masks(ck)

    nc = rows // ck
    lanes = [slice(h * DN_D, (h + 1) * DN_D) for h in range(DN_H)]
    kbeta16, k16, q16, dec, vbeta, kbg, qg, kg, egl = ([] for _ in range(9))
    for c in range(nc):
        r0 = c * ck
        q = q_s[r0:r0 + ck, :]
        k = k_s[r0:r0 + ck, :]
        beta = b_s[r0:r0 + ck, :]
        g = g_s[r0:r0 + ck, :]
        gc = _dot(tril, g, HIGHEST)
        gdiff = _dot(tril, g * slow, HIGHEST)
        decay = jnp.exp(jnp.where(incl, gdiff, -jnp.inf))
        egc = jnp.exp(gc)
        glast = gc[ck - 1:ck, :]
        kbeta = k * beta
        per_head = ((kbeta16, kbeta.astype(BF16)), (k16, k.astype(BF16)), (q16, q.astype(BF16)), (dec, decay),
                    (vbeta, (v_s[r0:r0 + ck, :] * beta).astype(BF16)), (kbg, (kbeta * egc).astype(BF16)),
                    (qg, q * egc), (kg, (k * jnp.exp(glast - gc)).astype(BF16)), (egl, jnp.exp(glast)))
        for dst, full in per_head:
            dst.extend(full[:, sl] for sl in lanes)
    n_inst = nc * DN_H
    inst = range(n_inst)
    lower = [jnp.where(strict, _dot_nt(kbeta16[n], k16[n]) * dec[n], 0.0) for n in inst]
    a_intra = [(_dot_nt(q16[n], k16[n]) * dec[n]).astype(BF16) for n in inst]
    lower16 = [t.astype(BF16) for t in lower]
    tinv = [eye.astype(F32) - jnp.where(masks[0], lower[n], 0.0) for n in inst]
    for m in masks[1:]:
        t16 = [t.astype(BF16) for t in tinv]
        x16 = [_dot(jnp.where(m, lower16[n], jnp.zeros_like(lower16[n])), t16[n]).astype(BF16) for n in inst]
        tinv = [tinv[n] - _dot(t16[n], x16[n]) for n in inst]
    t16 = [t.astype(BF16) for t in tinv]
    w16 = [_dot(t16[n], kbg[n]).astype(BF16) for n in inst]
    u16 = [_dot(t16[n], vbeta[n]).astype(BF16) for n in inst]
    kw16 = [_dot_tn(kg[n], w16[n]).astype(BF16) for n in inst]
    ku = [_dot_tn(kg[n], u16[n]) for n in inst]
    qeff16 = [(qg[n] - _dot(a_intra[n], w16[n])).astype(BF16) for n in inst]
    au = [_dot(a_intra[n], u16[n]) for n in inst]
    st = [state[h] for h in range(DN_H)]
    for c in range(nc):
        outs = []
        for h in range(DN_H):
            n = c * DN_H + h
            s16 = st[h].astype(BF16)
            outs.append(_dot(qeff16[n], s16) + au[n])
            st[h] = st[h] * egl[n] - _dot(kw16[n], s16) + ku[n]
        o_s[c * ck:(c + 1) * ck, :] = jnp.concatenate(outs, axis=1)
    for h in range(DN_H):
        state[h] = st[h]
    o = o_s[...]
    ms = _dot((o * o).astype(BF16), ones_blk) * (1.0 / DN_D)
    y = o * lax.rsqrt(ms + EPS) * ng_ref[...]
    o_ref[0] = (y * _silu(dz_ref[0])).astype(BF16)


def _deltanet(dnqkv, gate, dz, conv_w, a_log, dt_bias, norm_g):
    b, s, _ = dnqkv.shape
    rows = R_DN
    ck = DN_CHUNK
    lane_head = np.arange(2 * DN_W) // DN_D
    expand = (np.arange(LANES)[:, None] == lane_head[None, :]).astype(np.float32)
    tril = np.tril(np.ones((ck, ck), np.float32))
    slow = (np.arange(ck)[:, None] > (np.arange(DN_W)[None, :] % ck)).astype(np.float32)
    ones_blk = (np.arange(DN_W)[:, None] // DN_D == np.arange(DN_W)[None, :] // DN_D).astype(np.float32)
    rep = lambda t: jnp.repeat(t.astype(F32), DN_D).reshape(1, DN_W)
    const = lambda shape: pl.BlockSpec(shape, lambda bi, i: (0,) * len(shape))
    row = lambda width: pl.BlockSpec((1, rows, width), lambda bi, i: (bi, i, 0))
    per = rows // DN_HALO
    return pl.pallas_call(
        _deltanet_kernel,
        grid=(b, s // rows),
        in_specs=[
            row(3 * DN_W),
            pl.BlockSpec((1, DN_HALO, 3 * DN_W), lambda bi, i: (bi, jnp.maximum(i * per - 1, 0), 0)),
            row(LANES), row(DN_W),
            const((DN_CONV, 3 * DN_W)), const((1, DN_W)), const((1, DN_W)), const((1, DN_W)),
            const((LANES, 2 * DN_W)), const((ck, ck)), const((ck, DN_W)), const((DN_W, DN_W)),
        ],
        out_specs=row(DN_W),
        out_shape=jax.ShapeDtypeStruct((b, s, DN_W), BF16),
        scratch_shapes=[
            pltpu.VMEM((rows + DN_HALO, 3 * DN_W), F32),
            pltpu.VMEM((rows, DN_W), F32), pltpu.VMEM((rows, DN_W), F32), pltpu.VMEM((rows, DN_W), F32),
            pltpu.VMEM((rows, DN_W), F32), pltpu.VMEM((rows, DN_W), F32), pltpu.VMEM((rows, DN_W), F32),
            pltpu.VMEM((DN_H, DN_D, DN_D), F32),
        ],
        compiler_params=pltpu.CompilerParams(vmem_limit_bytes=VMEM_LIMIT),
        name="deltanet",
    )(dnqkv, dnqkv, gate, dz, conv_w, rep(a_log), rep(dt_bias), jnp.tile(norm_g.astype(F32), DN_H).reshape(1, DN_W),
      jnp.asarray(expand), jnp.asarray(tril), jnp.asarray(slow), jnp.asarray(ones_blk, dtype=BF16))


def _conformer_kernel(cur_ref, prev_ref, w_ref, b_ref, lg_ref, lb_ref, o_ref, u_s, sh_s):
    i = pl.program_id(1)
    rows = cur_ref.shape[1]
    prev = jnp.where(i > 0, prev_ref[0], 0.0)
    u_s[0:CV_HALO, :] = prev[:, 0:CV_C] * _sigmoid(prev[:, CV_C:])
    cur = cur_ref[0]
    u_s[CV_HALO:CV_HALO + rows, :] = cur[:, 0:CV_C] * _sigmoid(cur[:, CV_C:])
    span = rows + CV_HALO - SUBLANES
    for ph in range(1, SUBLANES):
        sh_s[ph - 1, 0:span, :] = u_s[pl.ds(ph, span), :]
    base = CV_HALO - CV_K + 1
    acc = b_ref[...]
    for t in range(CV_K):
        blk, ph = divmod(base + t, SUBLANES)
        src = u_s if ph == 0 else sh_s.at[ph - 1]
        acc = acc + w_ref[t:t + 1, :] * src[blk * SUBLANES:blk * SUBLANES + rows, :]
    mu = jnp.mean(acc, axis=-1, keepdims=True)
    xc = acc - mu
    y = xc * lax.rsqrt(jnp.mean(xc * xc, axis=-1, keepdims=True) + EPS)
    o_ref[0] = _silu(y * lg_ref[...] + lb_ref[...]).astype(BF16)


def _conformer(cv, w, bias, ln_g, ln_b):
    b, s, _ = cv.shape
    rows = R_CV
    per = rows // CV_HALO
    const = lambda shape: pl.BlockSpec(shape, lambda bi, i: (0,) * len(shape))
    return pl.pallas_call(
        _conformer_kernel,
        grid=(b, s // rows),
        in_specs=[
            pl.BlockSpec((1, rows, 2 * CV_C), lambda bi, i: (bi, i, 0)),
            pl.BlockSpec((1, CV_HALO, 2 * CV_C), lambda bi, i: (bi, jnp.maximum(i * per - 1, 0), 0)),
            const((CV_K, CV_C)), const((1, CV_C)), const((1, CV_C)), const((1, CV_C)),
        ],
        out_specs=pl.BlockSpec((1, rows, CV_C), lambda bi, i: (bi, i, 0)),
        out_shape=jax.ShapeDtypeStruct((b, s, CV_C), BF16),
        scratch_shapes=[pltpu.VMEM((rows + CV_HALO, CV_C), F32),
                        pltpu.VMEM((SUBLANES - 1, rows + CV_HALO - SUBLANES, CV_C), F32)],
        compiler_params=pltpu.CompilerParams(vmem_limit_bytes=VMEM_LIMIT),
        name="conformer",
    )(cv, cv, w, bias.reshape(1, CV_C), ln_g.reshape(1, CV_C), ln_b.reshape(1, CV_C))


def _mix_residual(x_ref, ya_ref, yd_ref, yc_ref, wa_ref, wd_ref, wc_ref, gt1_ref):
    y = _dot(ya_ref[...], wa_ref[...]) + _dot(yd_ref[...], wd_ref[...]) + _dot(yc_ref[...], wc_ref[...])
    return x_ref[...] + gt1_ref[0] * y


def _finish(x1, acc, gt2_ref, fg_ref, o_ref, final_norm):
    x2 = x1 + gt2_ref[0] * acc
    if final_norm:
        x2 = x2 * lax.rsqrt(jnp.mean(x2 * x2, axis=-1, keepdims=True) + EPS) * fg_ref[...]
    o_ref[...] = x2


def _dense_ffn_kernel(x_ref, ya_ref, yd_ref, yc_ref, wa_ref, wd_ref, wc_ref, gt1_ref, g_ref, sc_ref, sh_ref, gt2_ref,
                      fg_ref, wg_ref, wu_ref, wdn_ref, o_ref, x1_s, h_s, acc_s, *, final_norm):
    f = pl.program_id(1)

    @pl.when(f == 0)
    def _():
        x1 = _mix_residual(x_ref, ya_ref, yd_ref, yc_ref, wa_ref, wd_ref, wc_ref, gt1_ref)
        x1_s[...] = x1
        h_s[...] = _rms_modulate(x1, g_ref[...], sc_ref[0], sh_ref[0]).astype(BF16)
        acc_s[...] = jnp.zeros_like(acc_s)

    h = h_s[...]
    a = (_silu(_dot(h, wg_ref[...])) * _dot(h, wu_ref[...])).astype(BF16)
    acc_s[...] += _dot(a, wdn_ref[...])

    @pl.when(f == pl.num_programs(1) - 1)
    def _():
        _finish(x1_s[...], acc_s[...], gt2_ref, fg_ref, o_ref, final_norm)


def _route_kernel(x_ref, ya_ref, yd_ref, yc_ref, wa_ref, wd_ref, wc_ref, gt1_ref, g_ref, sc_ref, sh_ref,
                  rwt_ref, rb_ref, upper_ref, x1_ref, h_ref, comb_ref, pos_ref, cnt_ref):
    x1 = _mix_residual(x_ref, ya_ref, yd_ref, yc_ref, wa_ref, wd_ref, wc_ref, gt1_ref)
    x1_ref[...] = x1
    h = _rms_modulate(x1, g_ref[...], sc_ref[0], sh_ref[0])
    h_ref[...] = h.astype(BF16)
    logits = lax.dot_general(rwt_ref[...], h, (((1,), (1,)), ((), ())), preferred_element_type=F32,
                             precision=HIGHEST) + rb_ref[...]
    row = lax.broadcasted_iota(jnp.int32, logits.shape, 0)
    m1 = jnp.max(logits, axis=0, keepdims=True)
    i1 = jnp.min(jnp.where(logits == m1, row, N_EXPERTS), axis=0, keepdims=True)
    rest = jnp.where(row == i1, -jnp.inf, logits)
    m2 = jnp.max(rest, axis=0, keepdims=True)
    i2 = jnp.min(jnp.where(rest == m2, row, N_EXPERTS), axis=0, keepdims=True)
    e2 = jnp.exp(m2 - m1)
    comb_ref[...] = jnp.where(row == i1, 1.0 / (1.0 + e2), 0.0) + jnp.where(row == i2, e2 / (1.0 + e2), 0.0)
    sel = jnp.where((row == i1) | (row == i2), 1.0, 0.0)
    rank = _dot(sel.astype(BF16), upper_ref[...])
    pos_ref[...] = jnp.where(sel > 0.0, rank, -1.0)
    lane = lax.broadcasted_iota(jnp.int32, (N_EXPERTS, LANES), 1)
    cnt = jnp.zeros((N_EXPERTS, LANES), F32)
    for sub in range(sel.shape[1] // MOE_SUB):
        n = jnp.sum(sel[:, sub * MOE_SUB:(sub + 1) * MOE_SUB], axis=1, keepdims=True)
        cnt = cnt + jnp.where(lane == sub, n, 0.0)
    cnt_ref[0] = cnt.astype(jnp.int32)


def _route(x, ys, w_out, mods, g, rwt, rb, s):
    t, d = x.shape
    tm = TM_FFN
    gt1, sc, sh, _ = mods
    tok = lambda width: pl.BlockSpec((tm, width), lambda i: (i, 0))
    const = lambda shape: pl.BlockSpec(shape, lambda i: (0,) * len(shape))
    vec = pl.BlockSpec((1, 1, d), lambda i: ((i * tm) // s, 0, 0))
    col = pl.BlockSpec((N_EXPERTS, tm), lambda i: (0, i))
    idx = np.arange(tm)
    upper = ((idx[:, None] < idx[None, :]) & (idx[:, None] // MOE_SUB == idx[None, :] // MOE_SUB)).astype(np.float32)
    return pl.pallas_call(
        _route_kernel,
        grid=(t // tm,),
        in_specs=[tok(d), tok(ATTN_QW), tok(DN_W), tok(CV_C), const((ATTN_QW, d)), const((DN_W, d)), const((CV_C, d)),
                  vec, const((1, d)), vec, vec, const((N_EXPERTS, d)), const((N_EXPERTS, 1)), const((tm, tm))],
        out_specs=[tok(d), tok(d), col, col, pl.BlockSpec((1, N_EXPERTS, LANES), lambda i: (i, 0, 0))],
        out_shape=[jax.ShapeDtypeStruct((t, d), F32), jax.ShapeDtypeStruct((t, d), BF16),
                   jax.ShapeDtypeStruct((N_EXPERTS, t), F32), jax.ShapeDtypeStruct((N_EXPERTS, t), F32),
                   jax.ShapeDtypeStruct((t // tm, N_EXPERTS, LANES), jnp.int32)],
        compiler_params=pltpu.CompilerParams(vmem_limit_bytes=VMEM_LIMIT),
        name="route",
    )(x, *ys, *w_out, gt1, g, sc, sh, rwt, rb, jnp.asarray(upper, dtype=BF16))


def _moe_sparse_kernel(cnt_ref, off_ref, tot_ref, base_ref, h_ref, x1_ref, comb_ref, pos_ref, gt2_ref, fg_ref,
                       wg_ref, wu_ref, wdn_ref, o_ref, xg_s, wrow_s, yw_s, *, final_norm):
    i = pl.program_id(0)
    e = pl.program_id(1)
    n_sub = h_ref.shape[0] // MOE_SUB
    d = h_ref.shape[1]
    win_row = lax.broadcasted_iota(jnp.int32, (MOE_WIN, MOE_SUB), 0)

    for sub in range(n_sub):
        k = (i * n_sub + sub) * N_EXPERTS + e
        count = cnt_ref[k]
        start = off_ref[k]
        lanes = slice(sub * MOE_SUB, (sub + 1) * MOE_SUB)
        for p in range(MOE_SUB // MOE_WIN):
            @pl.when(count > p * MOE_WIN)
            def _():
                pos = pos_ref[pl.ds(e, 1), lanes].astype(jnp.int32)
                sel = win_row + p * MOE_WIN == pos
                r0 = pl.multiple_of(start + p * MOE_WIN, 16)
                rows = _dot(jnp.where(sel, 1.0, 0.0).astype(BF16), h_ref[lanes, :])
                xg_s[pl.ds(r0, MOE_WIN), :] = rows.astype(BF16)
                w = jnp.sum(jnp.where(sel, comb_ref[pl.ds(e, 1), lanes], 0.0), axis=1, keepdims=True)
                wrow_s[pl.ds(r0, MOE_WIN), :] = jnp.broadcast_to(w, (MOE_WIN, LANES))

    total = pl.multiple_of(tot_ref[i * N_EXPERTS + e], 16)
    base = pl.multiple_of(base_ref[i * N_EXPERTS + e], 16)
    xg_s[pl.ds(total, MOE_TAIL), :] = jnp.zeros((MOE_TAIL, d), BF16)
    wrow_s[pl.ds(total, MOE_TAIL), :] = jnp.zeros((MOE_TAIL, LANES), F32)

    def ffn(r0, rows):
        xc = xg_s[pl.ds(r0, rows), :]
        a = (_silu(_dot(xc, wg_ref[0])) * _dot(xc, wu_ref[0])).astype(BF16)
        y = _dot(a, wdn_ref[0]) * jnp.tile(wrow_s[pl.ds(r0, rows), :], (1, d // LANES))
        yw_s[pl.ds(base + r0, rows), :] = y.astype(BF16)

    n_full = total // MOE_CHUNK
    rem = total - n_full * MOE_CHUNK

    def full_chunk(c, carry):
        ffn(pl.multiple_of(c * MOE_CHUNK, MOE_CHUNK), MOE_CHUNK)
        return carry

    lax.fori_loop(0, n_full + jnp.where(rem > MOE_CHUNK // 2, 1, 0), full_chunk, 0)

    @pl.when((rem > 0) & (rem <= MOE_CHUNK // 2))
    def _():
        ffn(pl.multiple_of(n_full * MOE_CHUNK, MOE_CHUNK), MOE_CHUNK // 2)

    covered = jnp.where(rem > MOE_CHUNK // 2, MOE_CHUNK, jnp.where(rem > 0, MOE_CHUNK // 2, 0)) + n_full * MOE_CHUNK
    yw_s[pl.ds(pl.multiple_of(base + covered, 16), MOE_TAIL), :] = jnp.zeros((MOE_TAIL, d), BF16)

    @pl.when(e == pl.num_programs(1) - 1)
    def _():
        for sub in range(n_sub):
            lanes = slice(sub * MOE_SUB, (sub + 1) * MOE_SUB)
            first = (i * n_sub + sub) * N_EXPERTS

            def windows(p):
                sels, rows = [], []
                for ex in range(N_EXPERTS):
                    r0 = pl.multiple_of(base_ref[i * N_EXPERTS + ex] + off_ref[first + ex] + p * MOE_WIN, 16)
                    pos = pos_ref[ex:ex + 1, lanes].astype(jnp.int32)
                    sels.append(jnp.where(win_row + p * MOE_WIN == pos, 1.0, 0.0).astype(BF16))
                    rows.append(yw_s[pl.ds(r0, MOE_WIN), :])
                return _dot_tn(jnp.concatenate(sels, axis=0), jnp.concatenate(rows, axis=0))

            o_ref[lanes, :] = windows(0)
            most = cnt_ref[first]
            for ex in range(1, N_EXPERTS):
                most = jnp.maximum(most, cnt_ref[first + ex])
            for p in range(1, MOE_SUB // MOE_WIN):
                @pl.when(most > p * MOE_WIN)
                def _():
                    o_ref[lanes, :] += windows(p)
        _finish(x1_ref[...], o_ref[...], gt2_ref, fg_ref, o_ref, final_norm)


def _mixer_common_specs(d, tm, s):
    tok = lambda width: pl.BlockSpec((tm, width), lambda i, f: (i, 0))
    const = lambda shape: pl.BlockSpec(shape, lambda i, f: (0,) * len(shape))
    vec = pl.BlockSpec((1, 1, d), lambda i, f: ((i * tm) // s, 0, 0))
    return [tok(d), tok(ATTN_QW), tok(DN_W), tok(CV_C),
            const((ATTN_QW, d)), const((DN_W, d)), const((CV_C, d)),
            vec, const((1, d)), vec, vec, vec, const((1, d))], tok(d)


def _dense_ffn(x, ys, w_out, mods, g, final_g, wg, wu, wdn, s, final_norm):
    t, d = x.shape
    tm = TM_FFN
    ff = wg.shape[1]
    tf = ff // 2
    common, out_spec = _mixer_common_specs(d, tm, s)
    gt1, sc, sh, gt2 = mods
    return pl.pallas_call(
        functools.partial(_dense_ffn_kernel, final_norm=final_norm),
        grid=(t // tm, ff // tf),
        in_specs=common + [
            pl.BlockSpec((d, tf), lambda i, f: (0, f)),
            pl.BlockSpec((d, tf), lambda i, f: (0, f)),
            pl.BlockSpec((tf, d), lambda i, f: (f, 0)),
        ],
        out_specs=out_spec,
        out_shape=jax.ShapeDtypeStruct((t, d), F32),
        scratch_shapes=[pltpu.VMEM((tm, d), F32), pltpu.VMEM((tm, d), BF16), pltpu.VMEM((tm, d), F32)],
        compiler_params=pltpu.CompilerParams(vmem_limit_bytes=VMEM_LIMIT),
        name="dense_ffn",
    )(x, *ys, *w_out, gt1, g, sc, sh, gt2, final_g, wg, wu, wdn)


def _moe_sparse(h, x1, comb, pos, cnt, gt2, final_g, wg, wu, wdn, s, final_norm):
    t, d = x1.shape
    ts = TS_MOE
    n_e, _, ffe = wg.shape
    n_sub = ts // MOE_SUB
    counts = cnt[:, :, :TM_FFN // MOE_SUB].transpose(0, 2, 1).reshape(t // ts, n_sub, n_e)
    padded = (counts + 15) // 16 * 16
    offs = jnp.cumsum(padded, axis=1) - padded
    tots = jnp.sum(padded, axis=1)
    bases = jnp.cumsum(tots, axis=1) - tots
    cap_one = -(-(ts + n_sub * 16 + MOE_TAIL) // 16) * 16
    cap_all = -(-(N_TOP * ts + n_e * n_sub * 16 + MOE_CHUNK + MOE_TAIL) // 16) * 16
    tok = lambda width: pl.BlockSpec((ts, width), lambda i, e, *_: (i, 0))
    col = pl.BlockSpec((n_e, ts), lambda i, e, *_: (0, i))
    wspec = lambda a, b_: pl.BlockSpec((1, a, b_), lambda i, e, *_: (e, 0, 0))
    flat = lambda a: a.reshape(-1).astype(jnp.int32)
    grid_spec = pltpu.PrefetchScalarGridSpec(
        num_scalar_prefetch=4,
        grid=(t // ts, n_e),
        in_specs=[tok(d), tok(d), col, col,
                  pl.BlockSpec((1, 1, d), lambda i, e, *_: ((i * ts) // s, 0, 0)),
                  pl.BlockSpec((1, d), lambda i, e, *_: (0, 0)),
                  wspec(d, ffe), wspec(d, ffe), wspec(ffe, d)],
        out_specs=tok(d),
        scratch_shapes=[pltpu.VMEM((cap_one, d), BF16), pltpu.VMEM((cap_one, LANES), F32),
                        pltpu.VMEM((cap_all, d), BF16)],
    )
    return pl.pallas_call(
        functools.partial(_moe_sparse_kernel, final_norm=final_norm),
        grid_spec=grid_spec,
        out_shape=jax.ShapeDtypeStruct((t, d), F32),
        compiler_params=pltpu.CompilerParams(vmem_limit_bytes=VMEM_LIMIT),
        name="moe_sparse",
    )(flat(counts), flat(offs), flat(tots), flat(bases), h, x1, comb, pos, gt2, final_g, wg, wu, wdn)


def _rope_tables(positions):
    half = ROT_DIM // 2
    inv_freq = ROPE_THETA ** (-jnp.arange(0, ROT_DIM, 2, dtype=F32) / ROT_DIM)
    ang = positions.astype(F32)[..., None] * inv_freq
    cos, sin = jnp.cos(ang), jnp.sin(ang)
    pad = lambda t, lo, hi, val: jnp.pad(t, ((0, 0), (0, 0), (lo, hi)), constant_values=val)
    rest = HEAD_DIM - ROT_DIM
    c = pad(jnp.concatenate([cos, cos], -1), 0, rest, 1.0)
    a = pad(-sin, 0, HEAD_DIM - half, 0.0)
    b = pad(sin, half, rest, 0.0)
    two = lambda t: jnp.concatenate([t, t], -1)
    return two(c), two(a), two(b)


def _pack_w_in(w):
    d = w.shape[0]
    o1 = ATTN_W
    o2 = o1 + 3 * DN_W
    o3 = o2 + 2 * DN_H
    o4 = o3 + DN_W
    gates = jnp.zeros((d, LANES), w.dtype).at[:, :2 * DN_H].set(w[:, o2:o3])
    return jnp.concatenate([w[:, :o2], gates, w[:, o3:o4], w[:, o4:]], axis=1).astype(BF16)


def kernel(x, c, positions, ada_w, ada_b, norm_mix_g, norm_ffn_g, w_in, attn_sinks, dn_conv_w, dn_a_log, dn_dt_bias, dn_norm_g, cv_dw_w, cv_dw_b, cv_ln_g, cv_ln_b, w_out, ffn_w_gate, ffn_w_up, ffn_w_down, router_w, router_b, moe_w_gate, moe_w_up, moe_w_down, final_norm_g):
    b, s, d = x.shape
    depth = w_in.shape[0]
    mod = _adaln(c, ada_w, ada_b)
    rc, ra, rb = _rope_tables(positions)
    final_g = final_norm_g.reshape(1, d)
    vec = lambda t: t.reshape(b, 1, d)
    for l in range(depth):
        sh1, sc1, gt1 = (vec(t) for t in jnp.split(mod[l, 0], 3, axis=-1))
        sh2, sc2, gt2 = (vec(t) for t in jnp.split(mod[l, 1], 3, axis=-1))
        attn, dnqkv, gate, dz, cv = _inproj(x, norm_mix_g[l].reshape(1, d), sc1, sh1, rc, ra, rb, _pack_w_in(w_in[l]))
        y_attn = _attention(attn, attn_sinks[l].astype(F32))
        y_dn = _deltanet(dnqkv, gate, dz, dn_conv_w[l], dn_a_log[l], dn_dt_bias[l], dn_norm_g[l])
        y_cv = _conformer(cv, cv_dw_w[l], cv_dw_b[l], cv_ln_g[l], cv_ln_b[l])
        t = b * s
        ys = (y_attn.reshape(t, ATTN_QW), y_dn.reshape(t, DN_W), y_cv.reshape(t, CV_C))
        wo = w_out[l].astype(BF16)
        wos = (wo[:ATTN_QW], wo[ATTN_QW:ATTN_QW + DN_W], wo[ATTN_QW + DN_W:])
        mods = (gt1, sc2, sh2, gt2)
        g2 = norm_ffn_g[l].reshape(1, d)
        last = l == depth - 1
        j = l // 2
        if l % 2 == 0:
            x2 = _dense_ffn(x.reshape(t, d), ys, wos, mods, g2, final_g, ffn_w_gate[j].astype(BF16),
                            ffn_w_up[j].astype(BF16), ffn_w_down[j].astype(BF16), s, last)
        else:
            x1, h2, comb, pos, cnt = _route(x.reshape(t, d), ys, wos, mods, g2, router_w[j].astype(F32).T,
                                            router_b[j].astype(F32).reshape(N_EXPERTS, 1), s)
            x2 = _moe_sparse(h2, x1, comb, pos, cnt, gt2, final_g, moe_w_gate[j].astype(BF16),
                             moe_w_up[j].astype(BF16), moe_w_down[j].astype(BF16), s, last)
        x = x2.reshape(b, s, d)
    return x
```

```python
import functools

import numpy as np
import jax
import jax.numpy as jnp
from jax import lax
from jax.experimental import pallas as pl
from jax.experimental.pallas import tpu as pltpu

F32 = jnp.float32
BF16 = jnp.bfloat16
HIGHEST = lax.Precision.HIGHEST

HEAD_DIM = 64
ATTN_HQ = 8
ATTN_HKV = 2
ATTN_GROUP = ATTN_HQ // ATTN_HKV
ATTN_BLOCK = 128
ROT_DIM = HEAD_DIM // 4
ROPE_THETA = 500000.0
DN_H = 4
DN_D = 64
DN_CONV = 4
DN_CHUNK = 64
CV_C = 256
CV_K = 31
ATTN_QW = ATTN_HQ * HEAD_DIM
ATTN_KVW = ATTN_HKV * HEAD_DIM
ATTN_W = ATTN_QW + 2 * ATTN_KVW
DN_W = DN_H * DN_D
N_EXPERTS = 8
EPS = 1e-6
LANES = 128
SUBLANES = 8
VMEM_LIMIT = 48 * 1024 * 1024

TM_PROJ = 512
TQ_ATTN = 512
R_DN = 512
R_CV = 512
TM_FFN = 512
CV_HALO = 32
DN_HALO = 8
TS_MOE = 1024
MOE_SUB = 256
MOE_WIN = 128
MOE_CHUNK = 256
MOE_TAIL = 384
N_TOP = 2


def _dot(a, b, precision=None):
    return jnp.dot(a, b, preferred_element_type=F32, precision=precision)


def _dot_nt(a, b):
    return lax.dot_general(a, b, (((1,), (1,)), ((), ())), preferred_element_type=F32)


def _dot_tn(a, b):
    return lax.dot_general(a, b, (((0,), (0,)), ((), ())), preferred_element_type=F32)


def _split3(x):
    hi = x.astype(BF16)
    rest = x - hi.astype(F32)
    mid = rest.astype(BF16)
    return hi, mid, (rest - mid.astype(F32)).astype(BF16)


def _sigmoid(x):
    return 1.0 / (1.0 + jnp.exp(-x))


def _silu(x):
    return x * _sigmoid(x)


def _rms_modulate(x, g, sc, sh):
    y = x * lax.rsqrt(jnp.mean(x * x, axis=-1, keepdims=True) + EPS)
    return (y * g) * (1.0 + sc) + sh


def _adaln_kernel(c_ref, w_ref, b_ref, o_ref):
    o_ref[0] = _dot(_silu(c_ref[...]), w_ref[0], HIGHEST) + b_ref[0]


def _adaln(c, ada_w, ada_b):
    depth, _, d, d3 = ada_w.shape
    b = c.shape[0]
    rows = 8
    cp = jnp.zeros((rows, d), F32).at[:b].set(c)
    w = ada_w.reshape(depth * 2, d, d3)
    bias = ada_b.reshape(depth * 2, 1, d3)
    tn = 1024
    out = pl.pallas_call(
        _adaln_kernel,
        grid=(depth * 2, d3 // tn),
        in_specs=[
            pl.BlockSpec((rows, d), lambda i, j: (0, 0)),
            pl.BlockSpec((1, d, tn), lambda i, j: (i, 0, j)),
            pl.BlockSpec((1, 1, tn), lambda i, j: (i, 0, j)),
        ],
        out_specs=pl.BlockSpec((1, rows, tn), lambda i, j: (i, 0, j)),
        out_shape=jax.ShapeDtypeStruct((depth * 2, rows, d3), F32),
        compiler_params=pltpu.CompilerParams(vmem_limit_bytes=VMEM_LIMIT),
        name="adaln",
    )(cp, w, bias)
    return out[:, :b].reshape(depth, 2, b, d3)


def _inproj_kernel(x_ref, g_ref, sc_ref, sh_ref, rc_ref, ra_ref, rb_ref, w_ref,
                   attn_ref, dnqkv_ref, gate_ref, dz_ref, cv_ref):
    h = _rms_modulate(x_ref[0], g_ref[...], sc_ref[0], sh_ref[0]).astype(BF16)
    a = _dot(h, w_ref[:, 0:ATTN_W])
    rc, ra, rb = rc_ref[0], ra_ref[0], rb_ref[0]
    half = ROT_DIM // 2
    for j in range((ATTN_QW + ATTN_KVW) // LANES):
        t = a[:, j * LANES:(j + 1) * LANES]
        r = t * rc + pltpu.roll(t, LANES - half, 1) * ra + pltpu.roll(t, half, 1) * rb
        if j < ATTN_QW // LANES:
            r = r * (HEAD_DIM ** -0.5)
        attn_ref[0, :, j * LANES:(j + 1) * LANES] = r.astype(BF16)
    attn_ref[0, :, ATTN_QW + ATTN_KVW:ATTN_W] = a[:, ATTN_QW + ATTN_KVW:ATTN_W].astype(BF16)
    o = ATTN_W
    dnqkv_ref[0] = _dot(h, w_ref[:, o:o + 3 * DN_W])
    o += 3 * DN_W
    gate_ref[0] = _dot(h, w_ref[:, o:o + LANES])
    o += LANES
    dz_ref[0] = _dot(h, w_ref[:, o:o + DN_W])
    o += DN_W
    cv_ref[0] = _dot(h, w_ref[:, o:o + 2 * CV_C])


def _inproj(x, g, sc, sh, rc, ra, rb, w):
    b, s, d = x.shape
    tm = TM_PROJ
    n = w.shape[1]
    row = lambda width: pl.BlockSpec((1, tm, width), lambda bi, i: (bi, i, 0))
    vec = pl.BlockSpec((1, 1, d), lambda bi, i: (bi, 0, 0))
    widths = (ATTN_W, 3 * DN_W, LANES, DN_W, 2 * CV_C)
    dtypes = (BF16, F32, F32, F32, F32)
    return pl.pallas_call(
        _inproj_kernel,
        grid=(b, s // tm),
        in_specs=[row(d), pl.BlockSpec((1, d), lambda bi, i: (0, 0)), vec, vec,
                  row(LANES), row(LANES), row(LANES),
                  pl.BlockSpec((d, n), lambda bi, i: (0, 0))],
        out_specs=[row(wd) for wd in widths],
        out_shape=[jax.ShapeDtypeStruct((b, s, wd), dt) for wd, dt in zip(widths, dtypes)],
        compiler_params=pltpu.CompilerParams(vmem_limit_bytes=VMEM_LIMIT),
        name="inproj",
    )(x, g, sc, sh, rc, ra, rb, w)


def _attn_kernel(sink_ref, cur_ref, prev_ref, o_ref):
    i = pl.program_id(1)
    blk = ATTN_BLOCK
    cur = cur_ref[0]
    prev = prev_ref[0]
    kv_all = jnp.concatenate([prev[:, ATTN_QW:], cur[:, ATTN_QW:]], axis=0)
    band_w = 2 * blk
    qi = lax.broadcasted_iota(jnp.int32, (blk, 2 * band_w), 0)
    kj = lax.broadcasted_iota(jnp.int32, (blk, 2 * band_w), 1) % band_w
    diff = qi + blk - kj
    band = (diff >= 0) & (diff < blk)
    band_first = band & (kj >= jnp.where(i == 0, blk, 0))
    first_head = lax.broadcasted_iota(jnp.int32, (blk, LANES), 1) < HEAD_DIM
    zeros = jnp.zeros((band_w, HEAD_DIM), BF16)

    def blockdiag(t):
        return jnp.concatenate([jnp.concatenate([t, zeros], axis=1), jnp.concatenate([zeros, t], axis=1)], axis=0)

    units = [(r, j, pr) for r in range(cur.shape[0] // blk) for j in range(ATTN_HKV) for pr in range(ATTN_GROUP // 2)]

    def scores(r, j, pr):
        kb = kv_all[r * blk:(r + 2) * blk, j * HEAD_DIM:(j + 1) * HEAD_DIM]
        slab = (j * (ATTN_GROUP // 2) + pr) * LANES
        return _dot_nt(cur[r * blk:(r + 1) * blk, slab:slab + LANES], blockdiag(kb))

    lane_row = lax.broadcasted_iota(jnp.int32, (1, 2 * band_w), 1)
    not_row0 = lax.broadcasted_iota(jnp.int32, (band_w, HEAD_DIM), 0) > 0
    nxt = scores(*units[0])
    for n, (r, j, pr) in enumerate(units):
        hq = j * ATTN_GROUP + 2 * pr
        fill = jnp.where(lane_row == 0, sink_ref[hq], jnp.where(lane_row == band_w, sink_ref[hq + 1], -jnp.inf))
        sc = jnp.where(band_first if r == 0 else band, nxt, fill)
        if n + 1 < len(units):
            nxt = scores(*units[n + 1])
        ps, rs = [], []
        for t in range(2):
            st = sc[:, t * band_w:(t + 1) * band_w]
            p = jnp.exp(st - jnp.max(st, axis=-1, keepdims=True))
            rs.append(1.0 / jnp.sum(p, axis=-1, keepdims=True))
            ps.append(p.astype(BF16))
        vb = kv_all[r * blk:(r + 2) * blk, ATTN_KVW + j * HEAD_DIM:ATTN_KVW + (j + 1) * HEAD_DIM]
        vb = jnp.where(not_row0, vb, jnp.zeros_like(vb))
        o = _dot(jnp.concatenate(ps, axis=1), blockdiag(vb)) * jnp.where(first_head, rs[0], rs[1])
        slab = (j * (ATTN_GROUP // 2) + pr) * LANES
        o_ref[0, r * blk:(r + 1) * blk, slab:slab + LANES] = o.astype(BF16)


def _attention(attn, sinks):
    b, s, _ = attn.shape
    tq = TQ_ATTN
    per = tq // ATTN_BLOCK
    return pl.pallas_call(
        _attn_kernel,
        grid=(b, s // tq),
        in_specs=[
            pl.BlockSpec(memory_space=pltpu.SMEM),
            pl.BlockSpec((1, tq, ATTN_W), lambda bi, i: (bi, i, 0)),
            pl.BlockSpec((1, ATTN_BLOCK, ATTN_W), lambda bi, i: (bi, jnp.maximum(i * per - 1, 0), 0)),
        ],
        out_specs=pl.BlockSpec((1, tq, ATTN_QW), lambda bi, i: (bi, i, 0)),
        out_shape=jax.ShapeDtypeStruct((b, s, ATTN_QW), BF16),
        compiler_params=pltpu.CompilerParams(vmem_limit_bytes=VMEM_LIMIT),
        name="attention",
    )(sinks, attn, attn)


def _deltanet_kernel(cur_ref, prev_ref, gate_ref, dz_ref, cw_ref, alog_ref, dtb_ref, ng_ref,
                     expand_ref, tril_ref, slow_ref, ones_ref, o_ref,
                     xext, q_s, k_s, v_s, b_s, g_s, o_s, state):
    i = pl.program_id(1)
    rows = cur_ref.shape[1]
    ck = DN_CHUNK

    @pl.when(i == 0)
    def _():
        state[...] = jnp.zeros_like(state)

    xext[0:DN_HALO, :] = jnp.where(i > 0, prev_ref[0], 0.0)
    xext[DN_HALO:DN_HALO + rows, :] = cur_ref[0]
    acc = cw_ref[0:1, :] * xext[pl.ds(DN_HALO - DN_CONV + 1, rows), :]
    for t in range(1, DN_CONV):
        acc = acc + cw_ref[t:t + 1, :] * xext[pl.ds(DN_HALO - DN_CONV + 1 + t, rows), :]
    qkv = _silu(acc)
    ones_blk = ones_ref[...]

    def l2n(t):
        ss = _dot((t * t).astype(BF16), ones_blk)
        return t * lax.rsqrt(ss + EPS)

    q_s[...] = l2n(qkv[:, 0:DN_W]) * (DN_D ** -0.5)
    k_s[...] = l2n(qkv[:, DN_W:2 * DN_W])
    v_s[...] = qkv[:, 2 * DN_W:3 * DN_W]
    ge = _dot(jnp.concatenate(_split3(gate_ref[0]), axis=1), expand_ref[...])
    b_s[...] = _sigmoid(ge[:, 0:DN_W])
    da = ge[:, DN_W:2 * DN_W] + dtb_ref[...]
    softplus = jnp.maximum(da, 0.0) + jnp.log1p(jnp.exp(-jnp.abs(da)))
    g_s[...] = -jnp.exp(alog_ref[...]) * softplus

    tril = tril_ref[...]
    slow = slow_ref[...]
    ri = lax.broadcasted_iota(jnp.int32, (ck, DN_W), 0)
    ci = lax.broadcasted_iota(jnp.int32, (ck, DN_W), 1) % ck
    incl = ri >= ci

    pw = 2 * DN_D
    n_pair = DN_W // pw
    r2 = lax.broadcasted_iota(jnp.int32, (ck, pw), 0)
    l2 = lax.broadcasted_iota(jnp.int32, (ck, pw), 1)
    c2 = l2 % ck
    strict = r2 > c2
    eye = jnp.where(r2 == c2, 1.0, 0.0)
    first = l2 < DN_D
    masks = []
    blk = 1
    while blk < ck:
        masks.append((r2 // (2 * blk) == c2 // (2 * blk)) & ((r2 // blk) % 2 == 1) & ((c2 // blk) % 2 == 0))
        blk *= 2
    same_head = (lax.broadcasted_iota(jnp.int32, (pw, pw), 0) // DN_D) == (lax.broadcasted_iota(jnp.int32, (pw, pw), 1) // DN_D)

    def blockdiag(t):
        z = jnp.zeros_like(t)
        return jnp.concatenate([jnp.where(first, t, z), jnp.where(first, z, t)], axis=0)

    nc = rows // ck
    slabs = [slice(p * pw, (p + 1) * pw) for p in range(n_pair)]
    kbeta16, k16, q16, dec, vbeta, kbg, qg, kg, egl = ([] for _ in range(9))
    for c in range(nc):
        r0 = c * ck
        q = q_s[r0:r0 + ck, :]
        k = k_s[r0:r0 + ck, :]
        beta = b_s[r0:r0 + ck, :]
        g = g_s[r0:r0 + ck, :]
        cums = _dot(tril, jnp.concatenate([jnp.concatenate(_split3(g), axis=0),
                                           jnp.concatenate(_split3(g * slow), axis=0)], axis=1))
        gc = cums[:, 0:DN_W]
        gdiff = cums[:, DN_W:]
        decay = jnp.exp(jnp.where(incl, gdiff, -jnp.inf))
        egc = jnp.exp(gc)
        glast = gc[ck - 1:ck, :]
        kbeta = k * beta
        per_slab = ((kbeta16, kbeta.astype(BF16)), (k16, k.astype(BF16)), (q16, q.astype(BF16)), (dec, decay),
                    (vbeta, (v_s[r0:r0 + ck, :] * beta).astype(BF16)), (kbg, (kbeta * egc).astype(BF16)),
                    (qg, q * egc), (kg, (k * jnp.exp(glast - gc)).astype(BF16)), (egl, jnp.exp(glast)))
        for dst, full in per_slab:
            dst.extend(full[:, sl] for sl in slabs)
    inst = range(nc * n_pair)
    bdk = [blockdiag(k16[n]) for n in inst]
    lower = [jnp.where(strict, _dot_nt(kbeta16[n], bdk[n]) * dec[n], 0.0) for n in inst]
    a_intra = [(_dot_nt(q16[n], bdk[n]) * dec[n]).astype(BF16) for n in inst]
    lower16 = [t.astype(BF16) for t in lower]
    tinv = [eye - jnp.where(masks[0], lower[n], 0.0) for n in inst]
    for m in masks[1:]:
        t16 = [t.astype(BF16) for t in tinv]
        x16 = [_dot(jnp.where(m, lower16[n], jnp.zeros_like(lower16[n])), blockdiag(t16[n])).astype(BF16) for n in inst]
        tinv = [tinv[n] - _dot(t16[n], blockdiag(x16[n])) for n in inst]
    t16 = [t.astype(BF16) for t in tinv]
    w16 = [_dot(t16[n], blockdiag(kbg[n])).astype(BF16) for n in inst]
    u16 = [_dot(t16[n], blockdiag(vbeta[n])).astype(BF16) for n in inst]
    kw16 = [jnp.where(same_head, _dot_tn(kg[n], w16[n]), 0.0).astype(BF16) for n in inst]
    ku = [jnp.where(same_head, _dot_tn(kg[n], u16[n]), 0.0) for n in inst]
    qeff16 = [(qg[n] - _dot(a_intra[n], blockdiag(w16[n]))).astype(BF16) for n in inst]
    au = [_dot(a_intra[n], blockdiag(u16[n])) for n in inst]
    st = [state[p] for p in range(n_pair)]
    for c in range(nc):
        outs = []
        for p in range(n_pair):
            n = c * n_pair + p
            s16 = st[p].astype(BF16)
            outs.append(_dot(qeff16[n], s16) + au[n])
            st[p] = st[p] * egl[n] - _dot(kw16[n], s16) + ku[n]
        o_s[c * ck:(c + 1) * ck, :] = jnp.concatenate(outs, axis=1)
    for p in range(n_pair):
        state[p] = st[p]
    o = o_s[...]
    ms = _dot((o * o).astype(BF16), ones_blk) * (1.0 / DN_D)
    y = o * lax.rsqrt(ms + EPS) * ng_ref[...]
    o_ref[0] = (y * _silu(dz_ref[0])).astype(BF16)


def _deltanet(dnqkv, gate, dz, conv_w, a_log, dt_bias, norm_g):
    b, s, _ = dnqkv.shape
    rows = R_DN
    ck = DN_CHUNK
    lane_head = np.arange(2 * DN_W) // DN_D
    expand = np.tile((np.arange(LANES)[:, None] == lane_head[None, :]).astype(np.float32), (3, 1))
    tril = np.tile(np.tril(np.ones((ck, ck), np.float32)), (1, 3))
    slow = (np.arange(ck)[:, None] > (np.arange(DN_W)[None, :] % ck)).astype(np.float32)
    ones_blk = (np.arange(DN_W)[:, None] // DN_D == np.arange(DN_W)[None, :] // DN_D).astype(np.float32)
    rep = lambda t: jnp.repeat(t.astype(F32), DN_D).reshape(1, DN_W)
    const = lambda shape: pl.BlockSpec(shape, lambda bi, i: (0,) * len(shape))
    row = lambda width: pl.BlockSpec((1, rows, width), lambda bi, i: (bi, i, 0))
    per = rows // DN_HALO
    return pl.pallas_call(
        _deltanet_kernel,
        grid=(b, s // rows),
        in_specs=[
            row(3 * DN_W),
            pl.BlockSpec((1, DN_HALO, 3 * DN_W), lambda bi, i: (bi, jnp.maximum(i * per - 1, 0), 0)),
            row(LANES), row(DN_W),
            const((DN_CONV, 3 * DN_W)), const((1, DN_W)), const((1, DN_W)), const((1, DN_W)),
            const((3 * LANES, 2 * DN_W)), const((ck, 3 * ck)), const((ck, DN_W)), const((DN_W, DN_W)),
        ],
        out_specs=row(DN_W),
        out_shape=jax.ShapeDtypeStruct((b, s, DN_W), BF16),
        scratch_shapes=[
            pltpu.VMEM((rows + DN_HALO, 3 * DN_W), F32),
            pltpu.VMEM((rows, DN_W), F32), pltpu.VMEM((rows, DN_W), F32), pltpu.VMEM((rows, DN_W), F32),
            pltpu.VMEM((rows, DN_W), F32), pltpu.VMEM((rows, DN_W), F32), pltpu.VMEM((rows, DN_W), F32),
            pltpu.VMEM((DN_W // (2 * DN_D), 2 * DN_D, 2 * DN_D), F32),
        ],
        compiler_params=pltpu.CompilerParams(vmem_limit_bytes=VMEM_LIMIT),
        name="deltanet",
    )(dnqkv, dnqkv, gate, dz, conv_w, rep(a_log), rep(dt_bias), jnp.tile(norm_g.astype(F32), DN_H).reshape(1, DN_W),
      jnp.asarray(expand, dtype=BF16), jnp.asarray(tril, dtype=BF16), jnp.asarray(slow), jnp.asarray(ones_blk, dtype=BF16))


def _conformer_kernel(cur_ref, prev_ref, w_ref, b_ref, lg_ref, lb_ref, o_ref, u_s, sh_s):
    i = pl.program_id(1)
    rows = cur_ref.shape[1]
    prev = jnp.where(i > 0, prev_ref[0], 0.0)
    u_s[0:CV_HALO, :] = prev[:, 0:CV_C] * _sigmoid(prev[:, CV_C:])
    cur = cur_ref[0]
    u_s[CV_HALO:CV_HALO + rows, :] = cur[:, 0:CV_C] * _sigmoid(cur[:, CV_C:])
    span = rows + CV_HALO - SUBLANES
    for ph in range(1, SUBLANES):
        sh_s[ph - 1, 0:span, :] = u_s[pl.ds(ph, span), :]
    base = CV_HALO - CV_K + 1
    acc = b_ref[...]
    for t in range(CV_K):
        blk, ph = divmod(base + t, SUBLANES)
        src = u_s if ph == 0 else sh_s.at[ph - 1]
        acc = acc + w_ref[t:t + 1, :] * src[blk * SUBLANES:blk * SUBLANES + rows, :]
    mu = jnp.mean(acc, axis=-1, keepdims=True)
    xc = acc - mu
    y = xc * lax.rsqrt(jnp.mean(xc * xc, axis=-1, keepdims=True) + EPS)
    o_ref[0] = _silu(y * lg_ref[...] + lb_ref[...]).astype(BF16)


def _conformer(cv, w, bias, ln_g, ln_b):
    b, s, _ = cv.shape
    rows = R_CV
    per = rows // CV_HALO
    const = lambda shape: pl.BlockSpec(shape, lambda bi, i: (0,) * len(shape))
    return pl.pallas_call(
        _conformer_kernel,
        grid=(b, s // rows),
        in_specs=[
            pl.BlockSpec((1, rows, 2 * CV_C), lambda bi, i: (bi, i, 0)),
            pl.BlockSpec((1, CV_HALO, 2 * CV_C), lambda bi, i: (bi, jnp.maximum(i * per - 1, 0), 0)),
            const((CV_K, CV_C)), const((1, CV_C)), const((1, CV_C)), const((1, CV_C)),
        ],
        out_specs=pl.BlockSpec((1, rows, CV_C), lambda bi, i: (bi, i, 0)),
        out_shape=jax.ShapeDtypeStruct((b, s, CV_C), BF16),
        scratch_shapes=[pltpu.VMEM((rows + CV_HALO, CV_C), F32),
                        pltpu.VMEM((SUBLANES - 1, rows + CV_HALO - SUBLANES, CV_C), F32)],
        compiler_params=pltpu.CompilerParams(vmem_limit_bytes=VMEM_LIMIT),
        name="conformer",
    )(cv, cv, w, bias.reshape(1, CV_C), ln_g.reshape(1, CV_C), ln_b.reshape(1, CV_C))


def _mix_residual(x_ref, ya_ref, yd_ref, yc_ref, wa_ref, wd_ref, wc_ref, gt1_ref):
    y = _dot(ya_ref[...], wa_ref[...]) + _dot(yd_ref[...], wd_ref[...]) + _dot(yc_ref[...], wc_ref[...])
    return x_ref[...] + gt1_ref[0] * y


def _finish(x1, acc, gt2_ref, fg_ref, o_ref, final_norm):
    x2 = x1 + gt2_ref[0] * acc
    if final_norm:
        x2 = x2 * lax.rsqrt(jnp.mean(x2 * x2, axis=-1, keepdims=True) + EPS) * fg_ref[...]
    o_ref[...] = x2


def _dense_ffn_kernel(x_ref, ya_ref, yd_ref, yc_ref, wa_ref, wd_ref, wc_ref, gt1_ref, g_ref, sc_ref, sh_ref, gt2_ref,
                      fg_ref, wg_ref, wu_ref, wdn_ref, o_ref, x1_s, h_s, acc_s, *, final_norm):
    f = pl.program_id(1)

    @pl.when(f == 0)
    def _():
        x1 = _mix_residual(x_ref, ya_ref, yd_ref, yc_ref, wa_ref, wd_ref, wc_ref, gt1_ref)
        x1_s[...] = x1
        h_s[...] = _rms_modulate(x1, g_ref[...], sc_ref[0], sh_ref[0]).astype(BF16)
        acc_s[...] = jnp.zeros_like(acc_s)

    h = h_s[...]
    a = (_silu(_dot(h, wg_ref[...])) * _dot(h, wu_ref[...])).astype(BF16)
    acc_s[...] += _dot(a, wdn_ref[...])

    @pl.when(f == pl.num_programs(1) - 1)
    def _():
        _finish(x1_s[...], acc_s[...], gt2_ref, fg_ref, o_ref, final_norm)


def _route_kernel(x_ref, ya_ref, yd_ref, yc_ref, wa_ref, wd_ref, wc_ref, gt1_ref, g_ref, sc_ref, sh_ref,
                  rwt_ref, rb_ref, upper_ref, x1_ref, h_ref, comb_ref, pos_ref, cnt_ref):
    x1 = _mix_residual(x_ref, ya_ref, yd_ref, yc_ref, wa_ref, wd_ref, wc_ref, gt1_ref)
    x1_ref[...] = x1
    h = _rms_modulate(x1, g_ref[...], sc_ref[0], sh_ref[0])
    h_ref[...] = h.astype(BF16)
    w_hi, w_mid, _ = _split3(rwt_ref[...])
    h_hi, h_mid, _ = _split3(h)
    logits = _dot_nt(jnp.concatenate([w_hi, w_hi, w_mid], axis=1), jnp.concatenate([h_hi, h_mid, h_hi], axis=1)) + rb_ref[...]
    row = lax.broadcasted_iota(jnp.int32, logits.shape, 0)
    m1 = jnp.max(logits, axis=0, keepdims=True)
    i1 = jnp.min(jnp.where(logits == m1, row, N_EXPERTS), axis=0, keepdims=True)
    rest = jnp.where(row == i1, -jnp.inf, logits)
    m2 = jnp.max(rest, axis=0, keepdims=True)
    i2 = jnp.min(jnp.where(rest == m2, row, N_EXPERTS), axis=0, keepdims=True)
    e2 = jnp.exp(m2 - m1)
    comb_ref[...] = jnp.where(row == i1, 1.0 / (1.0 + e2), 0.0) + jnp.where(row == i2, e2 / (1.0 + e2), 0.0)
    sel = jnp.where((row == i1) | (row == i2), 1.0, 0.0)
    rank = _dot(sel.astype(BF16), upper_ref[...])
    pos_ref[...] = jnp.where(sel > 0.0, rank, -1.0)
    lane = lax.broadcasted_iota(jnp.int32, (N_EXPERTS, LANES), 1)
    cnt = jnp.zeros((N_EXPERTS, LANES), F32)
    for sub in range(sel.shape[1] // MOE_SUB):
        n = jnp.sum(sel[:, sub * MOE_SUB:(sub + 1) * MOE_SUB], axis=1, keepdims=True)
        cnt = cnt + jnp.where(lane == sub, n, 0.0)
    cnt_ref[0] = cnt.astype(jnp.int32)


def _route(x, ys, w_out, mods, g, rwt, rb, s):
    t, d = x.shape
    tm = TM_FFN
    gt1, sc, sh, _ = mods
    tok = lambda width: pl.BlockSpec((tm, width), lambda i: (i, 0))
    const = lambda shape: pl.BlockSpec(shape, lambda i: (0,) * len(shape))
    vec = pl.BlockSpec((1, 1, d), lambda i: ((i * tm) // s, 0, 0))
    col = pl.BlockSpec((N_EXPERTS, tm), lambda i: (0, i))
    idx = np.arange(tm)
    upper = ((idx[:, None] < idx[None, :]) & (idx[:, None] // MOE_SUB == idx[None, :] // MOE_SUB)).astype(np.float32)
    return pl.pallas_call(
        _route_kernel,
        grid=(t // tm,),
        in_specs=[tok(d), tok(ATTN_QW), tok(DN_W), tok(CV_C), const((ATTN_QW, d)), const((DN_W, d)), const((CV_C, d)),
                  vec, const((1, d)), vec, vec, const((N_EXPERTS, d)), const((N_EXPERTS, 1)), const((tm, tm))],
        out_specs=[tok(d), tok(d), col, col, pl.BlockSpec((1, N_EXPERTS, LANES), lambda i: (i, 0, 0))],
        out_shape=[jax.ShapeDtypeStruct((t, d), F32), jax.ShapeDtypeStruct((t, d), BF16),
                   jax.ShapeDtypeStruct((N_EXPERTS, t), F32), jax.ShapeDtypeStruct((N_EXPERTS, t), F32),
                   jax.ShapeDtypeStruct((t // tm, N_EXPERTS, LANES), jnp.int32)],
        compiler_params=pltpu.CompilerParams(vmem_limit_bytes=VMEM_LIMIT),
        name="route",
    )(x, *ys, *w_out, gt1, g, sc, sh, rwt, rb, jnp.asarray(upper, dtype=BF16))


def _moe_sparse_kernel(cnt_ref, off_ref, tot_ref, base_ref, h_ref, x1_ref, comb_ref, pos_ref, gt2_ref, fg_ref,
                       wg_ref, wu_ref, wdn_ref, o_ref, xg_s, wrow_s, yw_s, *, final_norm):
    i = pl.program_id(0)
    e = pl.program_id(1)
    n_sub = h_ref.shape[0] // MOE_SUB
    d = h_ref.shape[1]
    win_row = lax.broadcasted_iota(jnp.int32, (MOE_WIN, MOE_SUB), 0)

    def gather(sub, p):
        k = (i * n_sub + sub) * N_EXPERTS + e
        lanes = slice(sub * MOE_SUB, (sub + 1) * MOE_SUB)
        pos = pos_ref[pl.ds(e, 1), lanes].astype(jnp.int32)
        sel = win_row + p * MOE_WIN == pos
        r0 = pl.multiple_of(off_ref[k] + p * MOE_WIN, 16)
        rows = _dot(jnp.where(sel, 1.0, 0.0).astype(BF16), h_ref[lanes, :])
        xg_s[pl.ds(r0, MOE_WIN), :] = rows.astype(BF16)
        w = jnp.sum(jnp.where(sel, comb_ref[pl.ds(e, 1), lanes], 0.0), axis=1, keepdims=True)
        wrow_s[pl.ds(r0, MOE_WIN), :] = jnp.broadcast_to(w, (MOE_WIN, LANES))

    most = cnt_ref[i * n_sub * N_EXPERTS + e]
    for sub in range(1, n_sub):
        most = jnp.maximum(most, cnt_ref[(i * n_sub + sub) * N_EXPERTS + e])
    for p in range(MOE_SUB // MOE_WIN - 1, 0, -1):
        @pl.when(most > p * MOE_WIN)
        def _():
            for sub in range(n_sub):
                @pl.when(cnt_ref[(i * n_sub + sub) * N_EXPERTS + e] > p * MOE_WIN)
                def _():
                    gather(sub, p)
    for sub in range(n_sub):
        gather(sub, 0)

    total = pl.multiple_of(tot_ref[i * N_EXPERTS + e], 16)
    base = pl.multiple_of(base_ref[i * N_EXPERTS + e], 16)
    xg_s[pl.ds(total, MOE_TAIL), :] = jnp.zeros((MOE_TAIL, d), BF16)
    wrow_s[pl.ds(total, MOE_TAIL), :] = jnp.zeros((MOE_TAIL, LANES), F32)

    def ffn(r0, rows):
        xc = xg_s[pl.ds(r0, rows), :]
        a = (_silu(_dot(xc, wg_ref[0])) * _dot(xc, wu_ref[0])).astype(BF16)
        y = _dot(a, wdn_ref[0]) * jnp.tile(wrow_s[pl.ds(r0, rows), :], (1, d // LANES))
        yw_s[pl.ds(base + r0, rows), :] = y.astype(BF16)

    n_full = total // MOE_CHUNK
    rem = total - n_full * MOE_CHUNK
    n_chunks = n_full + jnp.where(rem > MOE_CHUNK // 2, 1, 0)
    ffn(0, MOE_CHUNK)

    def full_chunk(c, carry):
        ffn(pl.multiple_of(c * MOE_CHUNK, MOE_CHUNK), MOE_CHUNK)
        return carry

    lax.fori_loop(1, n_chunks, full_chunk, 0)
    half = (rem > 0) & (rem <= MOE_CHUNK // 2) & (n_full > 0)

    @pl.when(half)
    def _():
        ffn(pl.multiple_of(n_full * MOE_CHUNK, MOE_CHUNK), MOE_CHUNK // 2)

    covered = jnp.maximum(n_chunks, 1) * MOE_CHUNK + jnp.where(half, MOE_CHUNK // 2, 0)
    yw_s[pl.ds(pl.multiple_of(base + covered, 16), MOE_TAIL), :] = jnp.zeros((MOE_TAIL, d), BF16)

    @pl.when(e == pl.num_programs(1) - 1)
    def _():
        for sub in range(n_sub):
            lanes = slice(sub * MOE_SUB, (sub + 1) * MOE_SUB)
            first = (i * n_sub + sub) * N_EXPERTS

            def windows(p):
                sels, rows = [], []
                for ex in range(N_EXPERTS):
                    r0 = pl.multiple_of(base_ref[i * N_EXPERTS + ex] + off_ref[first + ex] + p * MOE_WIN, 16)
                    pos = pos_ref[ex:ex + 1, lanes].astype(jnp.int32)
                    sels.append(jnp.where(win_row + p * MOE_WIN == pos, 1.0, 0.0).astype(BF16))
                    rows.append(yw_s[pl.ds(r0, MOE_WIN), :])
                return _dot_tn(jnp.concatenate(sels, axis=0), jnp.concatenate(rows, axis=0))

            o_ref[lanes, :] = windows(0)
            most = cnt_ref[first]
            for ex in range(1, N_EXPERTS):
                most = jnp.maximum(most, cnt_ref[first + ex])
            for p in range(1, MOE_SUB // MOE_WIN):
                @pl.when(most > p * MOE_WIN)
                def _():
                    o_ref[lanes, :] += windows(p)
        _finish(x1_ref[...], o_ref[...], gt2_ref, fg_ref, o_ref, final_norm)


def _mixer_common_specs(d, tm, s):
    tok = lambda width: pl.BlockSpec((tm, width), lambda i, f: (i, 0))
    const = lambda shape: pl.BlockSpec(shape, lambda i, f: (0,) * len(shape))
    vec = pl.BlockSpec((1, 1, d), lambda i, f: ((i * tm) // s, 0, 0))
    return [tok(d), tok(ATTN_QW), tok(DN_W), tok(CV_C),
            const((ATTN_QW, d)), const((DN_W, d)), const((CV_C, d)),
            vec, const((1, d)), vec, vec, vec, const((1, d))], tok(d)


def _dense_ffn(x, ys, w_out, mods, g, final_g, wg, wu, wdn, s, final_norm):
    t, d = x.shape
    tm = TM_FFN
    ff = wg.shape[1]
    tf = ff // 2
    common, out_spec = _mixer_common_specs(d, tm, s)
    gt1, sc, sh, gt2 = mods
    return pl.pallas_call(
        functools.partial(_dense_ffn_kernel, final_norm=final_norm),
        grid=(t // tm, ff // tf),
        in_specs=common + [
            pl.BlockSpec((d, tf), lambda i, f: (0, f)),
            pl.BlockSpec((d, tf), lambda i, f: (0, f)),
            pl.BlockSpec((tf, d), lambda i, f: (f, 0)),
        ],
        out_specs=out_spec,
        out_shape=jax.ShapeDtypeStruct((t, d), F32),
        scratch_shapes=[pltpu.VMEM((tm, d), F32), pltpu.VMEM((tm, d), BF16), pltpu.VMEM((tm, d), F32)],
        compiler_params=pltpu.CompilerParams(vmem_limit_bytes=VMEM_LIMIT),
        name="dense_ffn",
    )(x, *ys, *w_out, gt1, g, sc, sh, gt2, final_g, wg, wu, wdn)


def _moe_sparse(h, x1, comb, pos, cnt, gt2, final_g, wg, wu, wdn, s, final_norm):
    t, d = x1.shape
    ts = TS_MOE
    n_e, _, ffe = wg.shape
    n_sub = ts // MOE_SUB
    counts = cnt[:, :, :TM_FFN // MOE_SUB].transpose(0, 2, 1).reshape(t // ts, n_sub, n_e)
    padded = (counts + 15) // 16 * 16
    offs = jnp.cumsum(padded, axis=1) - padded
    tots = jnp.sum(padded, axis=1)
    bases = jnp.cumsum(tots, axis=1) - tots
    cap_one = -(-(ts + n_sub * 16 + MOE_TAIL) // 16) * 16
    cap_all = -(-(N_TOP * ts + n_e * n_sub * 16 + MOE_CHUNK + MOE_TAIL) // 16) * 16
    tok = lambda width: pl.BlockSpec((ts, width), lambda i, e, *_: (i, 0))
    col = pl.BlockSpec((n_e, ts), lambda i, e, *_: (0, i))
    wspec = lambda a, b_: pl.BlockSpec((1, a, b_), lambda i, e, *_: (e, 0, 0))
    flat = lambda a: a.reshape(-1).astype(jnp.int32)
    grid_spec = pltpu.PrefetchScalarGridSpec(
        num_scalar_prefetch=4,
        grid=(t // ts, n_e),
        in_specs=[tok(d), tok(d), col, col,
                  pl.BlockSpec((1, 1, d), lambda i, e, *_: ((i * ts) // s, 0, 0)),
                  pl.BlockSpec((1, d), lambda i, e, *_: (0, 0)),
                  wspec(d, ffe), wspec(d, ffe), wspec(ffe, d)],
        out_specs=tok(d),
        scratch_shapes=[pltpu.VMEM((cap_one, d), BF16), pltpu.VMEM((cap_one, LANES), F32),
                        pltpu.VMEM((cap_all, d), BF16)],
    )
    return pl.pallas_call(
        functools.partial(_moe_sparse_kernel, final_norm=final_norm),
        grid_spec=grid_spec,
        out_shape=jax.ShapeDtypeStruct((t, d), F32),
        compiler_params=pltpu.CompilerParams(vmem_limit_bytes=VMEM_LIMIT),
        name="moe_sparse",
    )(flat(counts), flat(offs), flat(tots), flat(bases), h, x1, comb, pos, gt2, final_g, wg, wu, wdn)


def _rope_tables(positions):
    half = ROT_DIM // 2
    inv_freq = ROPE_THETA ** (-jnp.arange(0, ROT_DIM, 2, dtype=F32) / ROT_DIM)
    ang = positions.astype(F32)[..., None] * inv_freq
    cos, sin = jnp.cos(ang), jnp.sin(ang)
    pad = lambda t, lo, hi, val: jnp.pad(t, ((0, 0), (0, 0), (lo, hi)), constant_values=val)
    rest = HEAD_DIM - ROT_DIM
    c = pad(jnp.concatenate([cos, cos], -1), 0, rest, 1.0)
    a = pad(-sin, 0, HEAD_DIM - half, 0.0)
    b = pad(sin, half, rest, 0.0)
    two = lambda t: jnp.concatenate([t, t], -1)
    return two(c), two(a), two(b)


def _pack_w_in(w):
    d = w.shape[0]
    o1 = ATTN_W
    o2 = o1 + 3 * DN_W
    o3 = o2 + 2 * DN_H
    o4 = o3 + DN_W
    gates = jnp.zeros((d, LANES), w.dtype).at[:, :2 * DN_H].set(w[:, o2:o3])
    return jnp.concatenate([w[:, :o2], gates, w[:, o3:o4], w[:, o4:]], axis=1).astype(BF16)


def kernel(x, c, positions, ada_w, ada_b, norm_mix_g, norm_ffn_g, w_in, attn_sinks, dn_conv_w, dn_a_log, dn_dt_bias, dn_norm_g, cv_dw_w, cv_dw_b, cv_ln_g, cv_ln_b, w_out, ffn_w_gate, ffn_w_up, ffn_w_down, router_w, router_b, moe_w_gate, moe_w_up, moe_w_down, final_norm_g):
    b, s, d = x.shape
    depth = w_in.shape[0]
    mod = _adaln(c, ada_w, ada_b)
    rc, ra, rb = _rope_tables(positions)
    final_g = final_norm_g.reshape(1, d)
    vec = lambda t: t.reshape(b, 1, d)
    for l in range(depth):
        sh1, sc1, gt1 = (vec(t) for t in jnp.split(mod[l, 0], 3, axis=-1))
        sh2, sc2, gt2 = (vec(t) for t in jnp.split(mod[l, 1], 3, axis=-1))
        attn, dnqkv, gate, dz, cv = _inproj(x, norm_mix_g[l].reshape(1, d), sc1, sh1, rc, ra, rb, _pack_w_in(w_in[l]))
        y_attn = _attention(attn, attn_sinks[l].astype(F32))
        y_dn = _deltanet(dnqkv, gate, dz, dn_conv_w[l], dn_a_log[l], dn_dt_bias[l], dn_norm_g[l])
        y_cv = _conformer(cv, cv_dw_w[l], cv_dw_b[l], cv_ln_g[l], cv_ln_b[l])
        t = b * s
        ys = (y_attn.reshape(t, ATTN_QW), y_dn.reshape(t, DN_W), y_cv.reshape(t, CV_C))
        wo = w_out[l].astype(BF16)
        wos = (wo[:ATTN_QW], wo[ATTN_QW:ATTN_QW + DN_W], wo[ATTN_QW + DN_W:])
        mods = (gt1, sc2, sh2, gt2)
        g2 = norm_ffn_g[l].reshape(1, d)
        last = l == depth - 1
        j = l // 2
        if l % 2 == 0:
            x2 = _dense_ffn(x.reshape(t, d), ys, wos, mods, g2, final_g, ffn_w_gate[j].astype(BF16),
                            ffn_w_up[j].astype(BF16), ffn_w_down[j].astype(BF16), s, last)
        else:
            x1, h2, comb, pos, cnt = _route(x.reshape(t, d), ys, wos, mods, g2, router_w[j].astype(F32).T,
                                            router_b[j].astype(F32).reshape(N_EXPERTS, 1), s)
            x2 = _moe_sparse(h2, x1, comb, pos, cnt, gt2, final_g, moe_w_gate[j].astype(BF16),
                             moe_w_up[j].astype(BF16), moe_w_down[j].astype(BF16), s, last)
        x = x2.reshape(b, s, d)
    return x
```

```python
import functools

import numpy as np
import jax
import jax.numpy as jnp
from jax import lax
from jax.experimental import pallas as pl
from jax.experimental.pallas import tpu as pltpu

F32 = jnp.float32
BF16 = jnp.bfloat16
HIGHEST = lax.Precision.HIGHEST

HEAD_DIM = 64
ATTN_HQ = 8
ATTN_HKV = 2
ATTN_GROUP = ATTN_HQ // ATTN_HKV
ATTN_BLOCK = 128
ROT_DIM = HEAD_DIM // 4
ROPE_THETA = 500000.0
DN_H = 4
DN_D = 64
DN_CONV = 4
DN_CHUNK = 64
CV_C = 256
CV_K = 31
ATTN_QW = ATTN_HQ * HEAD_DIM
ATTN_KVW = ATTN_HKV * HEAD_DIM
ATTN_W = ATTN_QW + 2 * ATTN_KVW
DN_W = DN_H * DN_D
N_EXPERTS = 8
EPS = 1e-6
LANES = 128
SUBLANES = 8
VMEM_LIMIT = 48 * 1024 * 1024

TM_PROJ = 512
TQ_ATTN = 512
R_DN = 512
R_CV = 512
TM_FFN = 512
CV_HALO = 32
DN_HALO = 8
TS_MOE = 1024
MOE_SUB = 256
MOE_WIN = 128
MOE_CHUNK = 256
MOE_TAIL = 384
N_TOP = 2


def _dot(a, b, precision=None):
    return jnp.dot(a, b, preferred_element_type=F32, precision=precision)


def _dot_nt(a, b):
    return lax.dot_general(a, b, (((1,), (1,)), ((), ())), preferred_element_type=F32)


def _dot_tn(a, b):
    return lax.dot_general(a, b, (((0,), (0,)), ((), ())), preferred_element_type=F32)


def _split3(x):
    hi = x.astype(BF16)
    rest = x - hi.astype(F32)
    mid = rest.astype(BF16)
    return hi, mid, (rest - mid.astype(F32)).astype(BF16)


def _sigmoid(x):
    return 1.0 / (1.0 + jnp.exp(-x))


def _silu(x):
    return x * _sigmoid(x)


def _rms_modulate(x, g, sc, sh):
    y = x * lax.rsqrt(jnp.mean(x * x, axis=-1, keepdims=True) + EPS)
    return (y * g) * (1.0 + sc) + sh


def _adaln_kernel(c_ref, w_ref, b_ref, o_ref):
    o_ref[0] = _dot(_silu(c_ref[...]), w_ref[0], HIGHEST) + b_ref[0]


def _adaln(c, ada_w, ada_b):
    depth, _, d, d3 = ada_w.shape
    b = c.shape[0]
    rows = 8
    cp = jnp.zeros((rows, d), F32).at[:b].set(c)
    w = ada_w.reshape(depth * 2, d, d3)
    bias = ada_b.reshape(depth * 2, 1, d3)
    tn = 1024
    out = pl.pallas_call(
        _adaln_kernel,
        grid=(depth * 2, d3 // tn),
        in_specs=[
            pl.BlockSpec((rows, d), lambda i, j: (0, 0)),
            pl.BlockSpec((1, d, tn), lambda i, j: (i, 0, j)),
            pl.BlockSpec((1, 1, tn), lambda i, j: (i, 0, j)),
        ],
        out_specs=pl.BlockSpec((1, rows, tn), lambda i, j: (i, 0, j)),
        out_shape=jax.ShapeDtypeStruct((depth * 2, rows, d3), F32),
        compiler_params=pltpu.CompilerParams(vmem_limit_bytes=VMEM_LIMIT),
        name="adaln",
    )(cp, w, bias)
    return out[:, :b].reshape(depth, 2, b, d3)


def _inproj_kernel(x_ref, g_ref, sc_ref, sh_ref, cs_ref, place_ref, ones_ref, w_ref,
                   attn_ref, dnqkv_ref, gate_ref, dz_ref, cv_ref):
    h = _rms_modulate(x_ref[0], g_ref[...], sc_ref[0], sh_ref[0]).astype(BF16)
    a = _dot(h, w_ref[:, 0:ATTN_W])
    tab = _dot_tn(jnp.concatenate(_split3(cs_ref[0]), axis=0), place_ref[...]) + ones_ref[...]
    rc, ra, rb = tab[:, 0:LANES], tab[:, LANES:2 * LANES], tab[:, 2 * LANES:3 * LANES]
    half = ROT_DIM // 2
    for j in range((ATTN_QW + ATTN_KVW) // LANES):
        t = a[:, j * LANES:(j + 1) * LANES]
        r = t * rc + pltpu.roll(t, LANES - half, 1) * ra + pltpu.roll(t, half, 1) * rb
        if j < ATTN_QW // LANES:
            r = r * (HEAD_DIM ** -0.5)
        attn_ref[0, :, j * LANES:(j + 1) * LANES] = r.astype(BF16)
    attn_ref[0, :, ATTN_QW + ATTN_KVW:ATTN_W] = a[:, ATTN_QW + ATTN_KVW:ATTN_W].astype(BF16)
    o = ATTN_W
    dnqkv_ref[0] = _dot(h, w_ref[:, o:o + 3 * DN_W])
    o += 3 * DN_W
    gate_ref[0] = _dot(h, w_ref[:, o:o + LANES])
    o += LANES
    dz_ref[0] = _dot(h, w_ref[:, o:o + DN_W])
    o += DN_W
    cv_ref[0] = _dot(h, w_ref[:, o:o + 2 * CV_C])


def _rope_placement():
    half = ROT_DIM // 2
    place = np.zeros((2 * half, 3 * LANES), np.float32)
    ones = np.zeros((1, 3 * LANES), np.float32)
    for lane in range(LANES):
        dim = lane % HEAD_DIM
        if dim < ROT_DIM:
            place[dim % half, lane] = 1.0
            if dim < half:
                place[half + dim, LANES + lane] = -1.0
            else:
                place[half + dim - half, 2 * LANES + lane] = 1.0
        else:
            ones[0, lane] = 1.0
    return np.tile(place, (3, 1)), ones


def _inproj(x, g, sc, sh, cos_sin, w):
    b, s, d = x.shape
    tm = TM_PROJ
    n = w.shape[1]
    row = lambda width: pl.BlockSpec((1, tm, width), lambda bi, i: (bi, i, 0))
    vec = pl.BlockSpec((1, 1, d), lambda bi, i: (bi, 0, 0))
    const = lambda shape: pl.BlockSpec(shape, lambda bi, i: (0,) * len(shape))
    widths = (ATTN_W, 3 * DN_W, LANES, DN_W, 2 * CV_C)
    dtypes = (BF16, F32, F32, F32, F32)
    place, ones = _rope_placement()
    return pl.pallas_call(
        _inproj_kernel,
        grid=(b, s // tm),
        in_specs=[row(d), const((1, d)), vec, vec,
                  pl.BlockSpec((1, ROT_DIM, tm), lambda bi, i: (bi, 0, i)), const(place.shape), const(ones.shape),
                  const((d, n))],
        out_specs=[row(wd) for wd in widths],
        out_shape=[jax.ShapeDtypeStruct((b, s, wd), dt) for wd, dt in zip(widths, dtypes)],
        compiler_params=pltpu.CompilerParams(vmem_limit_bytes=VMEM_LIMIT),
        name="inproj",
    )(x, g, sc, sh, cos_sin, jnp.asarray(place, dtype=BF16), jnp.asarray(ones), w)


def _attn_kernel(sink_ref, cur_ref, prev_ref, o_ref):
    i = pl.program_id(1)
    blk = ATTN_BLOCK
    cur = cur_ref[0]
    prev = prev_ref[0]
    kv_all = jnp.concatenate([prev[:, ATTN_QW:], cur[:, ATTN_QW:]], axis=0)
    band_w = 2 * blk
    qi = lax.broadcasted_iota(jnp.int32, (blk, 2 * band_w), 0)
    kj = lax.broadcasted_iota(jnp.int32, (blk, 2 * band_w), 1) % band_w
    diff = qi + blk - kj
    band = (diff >= 0) & (diff < blk)
    band_first = band & (kj >= jnp.where(i == 0, blk, 0))
    first_head = lax.broadcasted_iota(jnp.int32, (blk, LANES), 1) < HEAD_DIM
    zeros = jnp.zeros((band_w, HEAD_DIM), BF16)

    def blockdiag(t):
        return jnp.concatenate([jnp.concatenate([t, zeros], axis=1), jnp.concatenate([zeros, t], axis=1)], axis=0)

    units = [(r, j, pr) for r in range(cur.shape[0] // blk) for j in range(ATTN_HKV) for pr in range(ATTN_GROUP // 2)]

    def scores(r, j, pr):
        kb = kv_all[r * blk:(r + 2) * blk, j * HEAD_DIM:(j + 1) * HEAD_DIM]
        slab = (j * (ATTN_GROUP // 2) + pr) * LANES
        return _dot_nt(cur[r * blk:(r + 1) * blk, slab:slab + LANES], blockdiag(kb))

    lane_row = lax.broadcasted_iota(jnp.int32, (1, 2 * band_w), 1)
    not_row0 = lax.broadcasted_iota(jnp.int32, (band_w, HEAD_DIM), 0) > 0
    nxt = scores(*units[0])
    for n, (r, j, pr) in enumerate(units):
        hq = j * ATTN_GROUP + 2 * pr
        fill = jnp.where(lane_row == 0, sink_ref[hq], jnp.where(lane_row == band_w, sink_ref[hq + 1], -jnp.inf))
        sc = jnp.where(band_first if r == 0 else band, nxt, fill)
        if n + 1 < len(units):
            nxt = scores(*units[n + 1])
        ps, rs = [], []
        for t in range(2):
            st = sc[:, t * band_w:(t + 1) * band_w]
            p = jnp.exp(st - jnp.max(st, axis=-1, keepdims=True))
            rs.append(1.0 / jnp.sum(p, axis=-1, keepdims=True))
            ps.append(p.astype(BF16))
        vb = kv_all[r * blk:(r + 2) * blk, ATTN_KVW + j * HEAD_DIM:ATTN_KVW + (j + 1) * HEAD_DIM]
        vb = jnp.where(not_row0, vb, jnp.zeros_like(vb))
        o = _dot(jnp.concatenate(ps, axis=1), blockdiag(vb)) * jnp.where(first_head, rs[0], rs[1])
        slab = (j * (ATTN_GROUP // 2) + pr) * LANES
        o_ref[0, r * blk:(r + 1) * blk, slab:slab + LANES] = o.astype(BF16)


def _attention(attn, sinks):
    b, s, _ = attn.shape
    tq = TQ_ATTN
    per = tq // ATTN_BLOCK
    return pl.pallas_call(
        _attn_kernel,
        grid=(b, s // tq),
        in_specs=[
            pl.BlockSpec(memory_space=pltpu.SMEM),
            pl.BlockSpec((1, tq, ATTN_W), lambda bi, i: (bi, i, 0)),
            pl.BlockSpec((1, ATTN_BLOCK, ATTN_W), lambda bi, i: (bi, jnp.maximum(i * per - 1, 0), 0)),
        ],
        out_specs=pl.BlockSpec((1, tq, ATTN_QW), lambda bi, i: (bi, i, 0)),
        out_shape=jax.ShapeDtypeStruct((b, s, ATTN_QW), BF16),
        compiler_params=pltpu.CompilerParams(vmem_limit_bytes=VMEM_LIMIT),
        name="attention",
    )(sinks, attn, attn)


def _deltanet_kernel(cur_ref, prev_ref, gate_ref, dz_ref, cw_ref, alog_ref, dtb_ref, ng_ref,
                     expand_ref, tril_ref, slow_ref, ones_ref, o_ref,
                     xext, q_s, k_s, v_s, b_s, g_s, o_s, state):
    i = pl.program_id(1)
    rows = cur_ref.shape[1]
    ck = DN_CHUNK

    @pl.when(i == 0)
    def _():
        state[...] = jnp.zeros_like(state)

    xext[0:DN_HALO, :] = jnp.where(i > 0, prev_ref[0], 0.0)
    xext[DN_HALO:DN_HALO + rows, :] = cur_ref[0]
    acc = cw_ref[0:1, :] * xext[pl.ds(DN_HALO - DN_CONV + 1, rows), :]
    for t in range(1, DN_CONV):
        acc = acc + cw_ref[t:t + 1, :] * xext[pl.ds(DN_HALO - DN_CONV + 1 + t, rows), :]
    qkv = _silu(acc)
    ones_blk = ones_ref[...]

    def l2n(t):
        ss = _dot((t * t).astype(BF16), ones_blk)
        return t * lax.rsqrt(ss + EPS)

    q_s[...] = l2n(qkv[:, 0:DN_W]) * (DN_D ** -0.5)
    k_s[...] = l2n(qkv[:, DN_W:2 * DN_W])
    v_s[...] = qkv[:, 2 * DN_W:3 * DN_W]
    ge = _dot(jnp.concatenate(_split3(gate_ref[0]), axis=1), expand_ref[...])
    b_s[...] = _sigmoid(ge[:, 0:DN_W])
    da = ge[:, DN_W:2 * DN_W] + dtb_ref[...]
    softplus = jnp.maximum(da, 0.0) + jnp.log1p(jnp.exp(-jnp.abs(da)))
    g_s[...] = -jnp.exp(alog_ref[...]) * softplus

    tril = tril_ref[...]
    slow = slow_ref[...]
    ri = lax.broadcasted_iota(jnp.int32, (ck, DN_W), 0)
    ci = lax.broadcasted_iota(jnp.int32, (ck, DN_W), 1) % ck
    incl = ri >= ci

    pw = 2 * DN_D
    n_pair = DN_W // pw
    r2 = lax.broadcasted_iota(jnp.int32, (ck, pw), 0)
    l2 = lax.broadcasted_iota(jnp.int32, (ck, pw), 1)
    c2 = l2 % ck
    strict = r2 > c2
    eye = jnp.where(r2 == c2, 1.0, 0.0)
    first = l2 < DN_D
    masks = []
    blk = 1
    while blk < ck:
        masks.append((r2 // (2 * blk) == c2 // (2 * blk)) & ((r2 // blk) % 2 == 1) & ((c2 // blk) % 2 == 0))
        blk *= 2
    same_head = (lax.broadcasted_iota(jnp.int32, (pw, pw), 0) // DN_D) == (lax.broadcasted_iota(jnp.int32, (pw, pw), 1) // DN_D)

    def blockdiag(t):
        z = jnp.zeros_like(t)
        return jnp.concatenate([jnp.where(first, t, z), jnp.where(first, z, t)], axis=0)

    nc = rows // ck
    slabs = [slice(p * pw, (p + 1) * pw) for p in range(n_pair)]
    kbeta16, k16, q16, dec, vbeta, kbg, qg, kg, egl = ([] for _ in range(9))
    for c in range(nc):
        r0 = c * ck
        q = q_s[r0:r0 + ck, :]
        k = k_s[r0:r0 + ck, :]
        beta = b_s[r0:r0 + ck, :]
        g = g_s[r0:r0 + ck, :]
        cums = _dot(tril, jnp.concatenate([jnp.concatenate(_split3(g), axis=0),
                                           jnp.concatenate(_split3(g * slow), axis=0)], axis=1))
        gc = cums[:, 0:DN_W]
        gdiff = cums[:, DN_W:]
        decay = jnp.exp(jnp.where(incl, gdiff, -jnp.inf))
        egc = jnp.exp(gc)
        glast = gc[ck - 1:ck, :]
        kbeta = k * beta
        per_slab = ((kbeta16, kbeta.astype(BF16)), (k16, k.astype(BF16)), (q16, q.astype(BF16)), (dec, decay),
                    (vbeta, (v_s[r0:r0 + ck, :] * beta).astype(BF16)), (kbg, (kbeta * egc).astype(BF16)),
                    (qg, q * egc), (kg, (k * jnp.exp(glast - gc)).astype(BF16)), (egl, jnp.exp(glast)))
        for dst, full in per_slab:
            dst.extend(full[:, sl] for sl in slabs)
    inst = range(nc * n_pair)
    bdk = [blockdiag(k16[n]) for n in inst]
    lower = [jnp.where(strict, _dot_nt(kbeta16[n], bdk[n]) * dec[n], 0.0) for n in inst]
    a_intra = [(_dot_nt(q16[n], bdk[n]) * dec[n]).astype(BF16) for n in inst]
    lower16 = [t.astype(BF16) for t in lower]
    tinv = [eye - jnp.where(masks[0], lower[n], 0.0) for n in inst]
    for m in masks[1:]:
        t16 = [t.astype(BF16) for t in tinv]
        x16 = [_dot(jnp.where(m, lower16[n], jnp.zeros_like(lower16[n])), blockdiag(t16[n])).astype(BF16) for n in inst]
        tinv = [tinv[n] - _dot(t16[n], blockdiag(x16[n])) for n in inst]
    t16 = [t.astype(BF16) for t in tinv]
    w16 = [_dot(t16[n], blockdiag(kbg[n])).astype(BF16) for n in inst]
    u16 = [_dot(t16[n], blockdiag(vbeta[n])).astype(BF16) for n in inst]
    kw16 = [jnp.where(same_head, _dot_tn(kg[n], w16[n]), 0.0).astype(BF16) for n in inst]
    ku = [jnp.where(same_head, _dot_tn(kg[n], u16[n]), 0.0) for n in inst]
    qeff16 = [(qg[n] - _dot(a_intra[n], blockdiag(w16[n]))).astype(BF16) for n in inst]
    au = [_dot(a_intra[n], blockdiag(u16[n])) for n in inst]
    st = [state[p] for p in range(n_pair)]
    for c in range(nc):
        outs = []
        for p in range(n_pair):
            n = c * n_pair + p
            s16 = st[p].astype(BF16)
            outs.append(_dot(qeff16[n], s16) + au[n])
            st[p] = st[p] * egl[n] - _dot(kw16[n], s16) + ku[n]
        o_s[c * ck:(c + 1) * ck, :] = jnp.concatenate(outs, axis=1)
    for p in range(n_pair):
        state[p] = st[p]
    o = o_s[...]
    ms = _dot((o * o).astype(BF16), ones_blk) * (1.0 / DN_D)
    y = o * lax.rsqrt(ms + EPS) * ng_ref[...]
    o_ref[0] = (y * _silu(dz_ref[0])).astype(BF16)


def _deltanet(dnqkv, gate, dz, conv_w, a_log, dt_bias, norm_g):
    b, s, _ = dnqkv.shape
    rows = R_DN
    ck = DN_CHUNK
    lane_head = np.arange(2 * DN_W) // DN_D
    expand = np.tile((np.arange(LANES)[:, None] == lane_head[None, :]).astype(np.float32), (3, 1))
    tril = np.tile(np.tril(np.ones((ck, ck), np.float32)), (1, 3))
    slow = (np.arange(ck)[:, None] > (np.arange(DN_W)[None, :] % ck)).astype(np.float32)
    ones_blk = (np.arange(DN_W)[:, None] // DN_D == np.arange(DN_W)[None, :] // DN_D).astype(np.float32)
    rep = lambda t: jnp.repeat(t.astype(F32), DN_D).reshape(1, DN_W)
    const = lambda shape: pl.BlockSpec(shape, lambda bi, i: (0,) * len(shape))
    row = lambda width: pl.BlockSpec((1, rows, width), lambda bi, i: (bi, i, 0))
    per = rows // DN_HALO
    return pl.pallas_call(
        _deltanet_kernel,
        grid=(b, s // rows),
        in_specs=[
            row(3 * DN_W),
            pl.BlockSpec((1, DN_HALO, 3 * DN_W), lambda bi, i: (bi, jnp.maximum(i * per - 1, 0), 0)),
            row(LANES), row(DN_W),
            const((DN_CONV, 3 * DN_W)), const((1, DN_W)), const((1, DN_W)), const((1, DN_W)),
            const((3 * LANES, 2 * DN_W)), const((ck, 3 * ck)), const((ck, DN_W)), const((DN_W, DN_W)),
        ],
        out_specs=row(DN_W),
        out_shape=jax.ShapeDtypeStruct((b, s, DN_W), BF16),
        scratch_shapes=[
            pltpu.VMEM((rows + DN_HALO, 3 * DN_W), F32),
            pltpu.VMEM((rows, DN_W), F32), pltpu.VMEM((rows, DN_W), F32), pltpu.VMEM((rows, DN_W), F32),
            pltpu.VMEM((rows, DN_W), F32), pltpu.VMEM((rows, DN_W), F32), pltpu.VMEM((rows, DN_W), F32),
            pltpu.VMEM((DN_W // (2 * DN_D), 2 * DN_D, 2 * DN_D), F32),
        ],
        compiler_params=pltpu.CompilerParams(vmem_limit_bytes=VMEM_LIMIT),
        name="deltanet",
    )(dnqkv, dnqkv, gate, dz, conv_w, rep(a_log), rep(dt_bias), jnp.tile(norm_g.astype(F32), DN_H).reshape(1, DN_W),
      jnp.asarray(expand, dtype=BF16), jnp.asarray(tril, dtype=BF16), jnp.asarray(slow), jnp.asarray(ones_blk, dtype=BF16))


def _conformer_kernel(cur_ref, prev_ref, w_ref, b_ref, lg_ref, lb_ref, o_ref, u_s, sh_s):
    i = pl.program_id(1)
    rows = cur_ref.shape[1]
    prev = jnp.where(i > 0, prev_ref[0], 0.0)
    u_s[0:CV_HALO, :] = prev[:, 0:CV_C] * _sigmoid(prev[:, CV_C:])
    cur = cur_ref[0]
    u_s[CV_HALO:CV_HALO + rows, :] = cur[:, 0:CV_C] * _sigmoid(cur[:, CV_C:])
    span = rows + CV_HALO - SUBLANES
    for ph in range(1, SUBLANES):
        sh_s[ph - 1, 0:span, :] = u_s[pl.ds(ph, span), :]
    base = CV_HALO - CV_K + 1
    acc = b_ref[...]
    for t in range(CV_K):
        blk, ph = divmod(base + t, SUBLANES)
        src = u_s if ph == 0 else sh_s.at[ph - 1]
        acc = acc + w_ref[t:t + 1, :] * src[blk * SUBLANES:blk * SUBLANES + rows, :]
    mu = jnp.mean(acc, axis=-1, keepdims=True)
    xc = acc - mu
    y = xc * lax.rsqrt(jnp.mean(xc * xc, axis=-1, keepdims=True) + EPS)
    o_ref[0] = _silu(y * lg_ref[...] + lb_ref[...]).astype(BF16)


def _conformer(cv, w, bias, ln_g, ln_b):
    b, s, _ = cv.shape
    rows = R_CV
    per = rows // CV_HALO
    const = lambda shape: pl.BlockSpec(shape, lambda bi, i: (0,) * len(shape))
    return pl.pallas_call(
        _conformer_kernel,
        grid=(b, s // rows),
        in_specs=[
            pl.BlockSpec((1, rows, 2 * CV_C), lambda bi, i: (bi, i, 0)),
            pl.BlockSpec((1, CV_HALO, 2 * CV_C), lambda bi, i: (bi, jnp.maximum(i * per - 1, 0), 0)),
            const((CV_K, CV_C)), const((1, CV_C)), const((1, CV_C)), const((1, CV_C)),
        ],
        out_specs=pl.BlockSpec((1, rows, CV_C), lambda bi, i: (bi, i, 0)),
        out_shape=jax.ShapeDtypeStruct((b, s, CV_C), BF16),
        scratch_shapes=[pltpu.VMEM((rows + CV_HALO, CV_C), F32),
                        pltpu.VMEM((SUBLANES - 1, rows + CV_HALO - SUBLANES, CV_C), F32)],
        compiler_params=pltpu.CompilerParams(vmem_limit_bytes=VMEM_LIMIT),
        name="conformer",
    )(cv, cv, w, bias.reshape(1, CV_C), ln_g.reshape(1, CV_C), ln_b.reshape(1, CV_C))


def _mix_residual(x_ref, ya_ref, yd_ref, yc_ref, wa_ref, wd_ref, wc_ref, gt1_ref):
    y = _dot(ya_ref[...], wa_ref[...]) + _dot(yd_ref[...], wd_ref[...]) + _dot(yc_ref[...], wc_ref[...])
    return x_ref[...] + gt1_ref[0] * y


def _finish(x1, acc, gt2_ref, fg_ref, o_ref, final_norm):
    x2 = x1 + gt2_ref[0] * acc
    if final_norm:
        x2 = x2 * lax.rsqrt(jnp.mean(x2 * x2, axis=-1, keepdims=True) + EPS) * fg_ref[...]
    o_ref[...] = x2


def _dense_ffn_kernel(x_ref, ya_ref, yd_ref, yc_ref, wa_ref, wd_ref, wc_ref, gt1_ref, g_ref, sc_ref, sh_ref, gt2_ref,
                      fg_ref, wg_ref, wu_ref, wdn_ref, o_ref, x1_s, h_s, acc_s, *, final_norm):
    f = pl.program_id(1)

    @pl.when(f == 0)
    def _():
        x1 = _mix_residual(x_ref, ya_ref, yd_ref, yc_ref, wa_ref, wd_ref, wc_ref, gt1_ref)
        x1_s[...] = x1
        h_s[...] = _rms_modulate(x1, g_ref[...], sc_ref[0], sh_ref[0]).astype(BF16)
        acc_s[...] = jnp.zeros_like(acc_s)

    h = h_s[...]
    a = (_silu(_dot(h, wg_ref[...])) * _dot(h, wu_ref[...])).astype(BF16)
    acc_s[...] += _dot(a, wdn_ref[...])

    @pl.when(f == pl.num_programs(1) - 1)
    def _():
        _finish(x1_s[...], acc_s[...], gt2_ref, fg_ref, o_ref, final_norm)


def _route_kernel(x_ref, ya_ref, yd_ref, yc_ref, wa_ref, wd_ref, wc_ref, gt1_ref, g_ref, sc_ref, sh_ref,
                  rwt_ref, rb_ref, upper_ref, x1_ref, h_ref, comb_ref, pos_ref, cnt_ref):
    x1 = _mix_residual(x_ref, ya_ref, yd_ref, yc_ref, wa_ref, wd_ref, wc_ref, gt1_ref)
    x1_ref[...] = x1
    h = _rms_modulate(x1, g_ref[...], sc_ref[0], sh_ref[0])
    h_ref[...] = h.astype(BF16)
    w_hi, w_mid, _ = _split3(rwt_ref[...])
    h_hi, h_mid, _ = _split3(h)
    logits = _dot_nt(jnp.concatenate([w_hi, w_hi, w_mid], axis=1), jnp.concatenate([h_hi, h_mid, h_hi], axis=1)) + rb_ref[...]
    row = lax.broadcasted_iota(jnp.int32, logits.shape, 0)
    m1 = jnp.max(logits, axis=0, keepdims=True)
    i1 = jnp.min(jnp.where(logits == m1, row, N_EXPERTS), axis=0, keepdims=True)
    rest = jnp.where(row == i1, -jnp.inf, logits)
    m2 = jnp.max(rest, axis=0, keepdims=True)
    i2 = jnp.min(jnp.where(rest == m2, row, N_EXPERTS), axis=0, keepdims=True)
    e2 = jnp.exp(m2 - m1)
    comb_ref[...] = jnp.where(row == i1, 1.0 / (1.0 + e2), 0.0) + jnp.where(row == i2, e2 / (1.0 + e2), 0.0)
    sel = jnp.where((row == i1) | (row == i2), 1.0, 0.0)
    rank = _dot(sel.astype(BF16), upper_ref[...])
    pos_ref[...] = jnp.where(sel > 0.0, rank, -1.0)
    lane = lax.broadcasted_iota(jnp.int32, (N_EXPERTS, LANES), 1)
    cnt = jnp.zeros((N_EXPERTS, LANES), F32)
    for sub in range(sel.shape[1] // MOE_SUB):
        n = jnp.sum(sel[:, sub * MOE_SUB:(sub + 1) * MOE_SUB], axis=1, keepdims=True)
        cnt = cnt + jnp.where(lane == sub, n, 0.0)
    cnt_ref[0] = cnt.astype(jnp.int32)


def _route(x, ys, w_out, mods, g, rwt, rb, s):
    t, d = x.shape
    tm = TM_FFN
    gt1, sc, sh, _ = mods
    tok = lambda width: pl.BlockSpec((tm, width), lambda i: (i, 0))
    const = lambda shape: pl.BlockSpec(shape, lambda i: (0,) * len(shape))
    vec = pl.BlockSpec((1, 1, d), lambda i: ((i * tm) // s, 0, 0))
    col = pl.BlockSpec((N_EXPERTS, tm), lambda i: (0, i))
    idx = np.arange(tm)
    upper = ((idx[:, None] < idx[None, :]) & (idx[:, None] // MOE_SUB == idx[None, :] // MOE_SUB)).astype(np.float32)
    return pl.pallas_call(
        _route_kernel,
        grid=(t // tm,),
        in_specs=[tok(d), tok(ATTN_QW), tok(DN_W), tok(CV_C), const((ATTN_QW, d)), const((DN_W, d)), const((CV_C, d)),
                  vec, const((1, d)), vec, vec, const((N_EXPERTS, d)), const((N_EXPERTS, 1)), const((tm, tm))],
        out_specs=[tok(d), tok(d), col, col, pl.BlockSpec((1, N_EXPERTS, LANES), lambda i: (i, 0, 0))],
        out_shape=[jax.ShapeDtypeStruct((t, d), F32), jax.ShapeDtypeStruct((t, d), BF16),
                   jax.ShapeDtypeStruct((N_EXPERTS, t), F32), jax.ShapeDtypeStruct((N_EXPERTS, t), F32),
                   jax.ShapeDtypeStruct((t // tm, N_EXPERTS, LANES), jnp.int32)],
        compiler_params=pltpu.CompilerParams(vmem_limit_bytes=VMEM_LIMIT),
        name="route",
    )(x, *ys, *w_out, gt1, g, sc, sh, rwt, rb, jnp.asarray(upper, dtype=BF16))


def _moe_sparse_kernel(cnt_ref, off_ref, tot_ref, base_ref, h_ref, x1_ref, comb_ref, pos_ref, gt2_ref, fg_ref,
                       wg_ref, wu_ref, wdn_ref, o_ref, xg_s, wrow_s, yw_s, *, final_norm):
    i = pl.program_id(0)
    e = pl.program_id(1)
    n_sub = h_ref.shape[0] // MOE_SUB
    d = h_ref.shape[1]
    win_row = lax.broadcasted_iota(jnp.int32, (MOE_WIN, MOE_SUB), 0)

    def gather(sub, p):
        k = (i * n_sub + sub) * N_EXPERTS + e
        lanes = slice(sub * MOE_SUB, (sub + 1) * MOE_SUB)
        pos = pos_ref[pl.ds(e, 1), lanes].astype(jnp.int32)
        sel = win_row + p * MOE_WIN == pos
        r0 = pl.multiple_of(off_ref[k] + p * MOE_WIN, 16)
        rows = _dot(jnp.where(sel, 1.0, 0.0).astype(BF16), h_ref[lanes, :])
        xg_s[pl.ds(r0, MOE_WIN), :] = rows.astype(BF16)
        w = jnp.sum(jnp.where(sel, comb_ref[pl.ds(e, 1), lanes], 0.0), axis=1, keepdims=True)
        wrow_s[pl.ds(r0, MOE_WIN), :] = jnp.broadcast_to(w, (MOE_WIN, LANES))

    most = cnt_ref[i * n_sub * N_EXPERTS + e]
    for sub in range(1, n_sub):
        most = jnp.maximum(most, cnt_ref[(i * n_sub + sub) * N_EXPERTS + e])
    for p in range(MOE_SUB // MOE_WIN - 1, 0, -1):
        @pl.when(most > p * MOE_WIN)
        def _():
            for sub in range(n_sub):
                @pl.when(cnt_ref[(i * n_sub + sub) * N_EXPERTS + e] > p * MOE_WIN)
                def _():
                    gather(sub, p)
    for sub in range(n_sub):
        gather(sub, 0)

    total = pl.multiple_of(tot_ref[i * N_EXPERTS + e], 16)
    base = pl.multiple_of(base_ref[i * N_EXPERTS + e], 16)
    xg_s[pl.ds(total, MOE_TAIL), :] = jnp.zeros((MOE_TAIL, d), BF16)
    wrow_s[pl.ds(total, MOE_TAIL), :] = jnp.zeros((MOE_TAIL, LANES), F32)

    def ffn(r0, rows):
        xc = xg_s[pl.ds(r0, rows), :]
        a = (_silu(_dot(xc, wg_ref[0])) * _dot(xc, wu_ref[0])).astype(BF16)
        y = _dot(a, wdn_ref[0]) * jnp.tile(wrow_s[pl.ds(r0, rows), :], (1, d // LANES))
        yw_s[pl.ds(base + r0, rows), :] = y.astype(BF16)

    half_rows = MOE_CHUNK // 2
    ffn(0, half_rows)
    rest = jnp.maximum(total - half_rows, 0)
    n_full = rest // MOE_CHUNK
    rem = rest - n_full * MOE_CHUNK
    n_chunks = n_full + jnp.where(rem > half_rows, 1, 0)

    def full_chunk(c, carry):
        ffn(pl.multiple_of(half_rows + c * MOE_CHUNK, half_rows), MOE_CHUNK)
        return carry

    lax.fori_loop(0, n_chunks, full_chunk, 0)
    half = (rem > 0) & (rem <= half_rows)

    @pl.when(half)
    def _():
        ffn(pl.multiple_of(half_rows + n_full * MOE_CHUNK, half_rows), half_rows)

    covered = half_rows + n_chunks * MOE_CHUNK + jnp.where(half, half_rows, 0)
    yw_s[pl.ds(pl.multiple_of(base + covered, 16), MOE_TAIL), :] = jnp.zeros((MOE_TAIL, d), BF16)

    @pl.when(e == pl.num_programs(1) - 1)
    def _():
        for sub in range(n_sub):
            lanes = slice(sub * MOE_SUB, (sub + 1) * MOE_SUB)
            first = (i * n_sub + sub) * N_EXPERTS

            def windows(p):
                sels, rows = [], []
                for ex in range(N_EXPERTS):
                    r0 = pl.multiple_of(base_ref[i * N_EXPERTS + ex] + off_ref[first + ex] + p * MOE_WIN, 16)
                    pos = pos_ref[ex:ex + 1, lanes].astype(jnp.int32)
                    sels.append(jnp.where(win_row + p * MOE_WIN == pos, 1.0, 0.0).astype(BF16))
                    rows.append(yw_s[pl.ds(r0, MOE_WIN), :])
                return _dot_tn(jnp.concatenate(sels, axis=0), jnp.concatenate(rows, axis=0))

            o_ref[lanes, :] = windows(0)
            most = cnt_ref[first]
            for ex in range(1, N_EXPERTS):
                most = jnp.maximum(most, cnt_ref[first + ex])
            for p in range(1, MOE_SUB // MOE_WIN):
                @pl.when(most > p * MOE_WIN)
                def _():
                    o_ref[lanes, :] += windows(p)
        _finish(x1_ref[...], o_ref[...], gt2_ref, fg_ref, o_ref, final_norm)


def _mixer_common_specs(d, tm, s):
    tok = lambda width: pl.BlockSpec((tm, width), lambda i, f: (i, 0))
    const = lambda shape: pl.BlockSpec(shape, lambda i, f: (0,) * len(shape))
    vec = pl.BlockSpec((1, 1, d), lambda i, f: ((i * tm) // s, 0, 0))
    return [tok(d), tok(ATTN_QW), tok(DN_W), tok(CV_C),
            const((ATTN_QW, d)), const((DN_W, d)), const((CV_C, d)),
            vec, const((1, d)), vec, vec, vec, const((1, d))], tok(d)


def _dense_ffn(x, ys, w_out, mods, g, final_g, wg, wu, wdn, s, final_norm):
    t, d = x.shape
    tm = TM_FFN
    ff = wg.shape[1]
    tf = ff // 2
    common, out_spec = _mixer_common_specs(d, tm, s)
    gt1, sc, sh, gt2 = mods
    return pl.pallas_call(
        functools.partial(_dense_ffn_kernel, final_norm=final_norm),
        grid=(t // tm, ff // tf),
        in_specs=common + [
            pl.BlockSpec((d, tf), lambda i, f: (0, f)),
            pl.BlockSpec((d, tf), lambda i, f: (0, f)),
            pl.BlockSpec((tf, d), lambda i, f: (f, 0)),
        ],
        out_specs=out_spec,
        out_shape=jax.ShapeDtypeStruct((t, d), F32),
        scratch_shapes=[pltpu.VMEM((tm, d), F32), pltpu.VMEM((tm, d), BF16), pltpu.VMEM((tm, d), F32)],
        compiler_params=pltpu.CompilerParams(vmem_limit_bytes=VMEM_LIMIT),
        name="dense_ffn",
    )(x, *ys, *w_out, gt1, g, sc, sh, gt2, final_g, wg, wu, wdn)


def _moe_sparse(h, x1, comb, pos, cnt, gt2, final_g, wg, wu, wdn, s, final_norm):
    t, d = x1.shape
    ts = TS_MOE
    n_e, _, ffe = wg.shape
    n_sub = ts // MOE_SUB
    counts = cnt[:, :, :TM_FFN // MOE_SUB].transpose(0, 2, 1).reshape(t // ts, n_sub, n_e)
    padded = (counts + 15) // 16 * 16
    offs = jnp.cumsum(padded, axis=1) - padded
    tots = jnp.sum(padded, axis=1)
    bases = jnp.cumsum(tots, axis=1) - tots
    cap_one = -(-(ts + n_sub * 16 + MOE_TAIL) // 16) * 16
    cap_all = -(-(N_TOP * ts + n_e * n_sub * 16 + MOE_CHUNK + MOE_TAIL) // 16) * 16
    tok = lambda width: pl.BlockSpec((ts, width), lambda i, e, *_: (i, 0))
    col = pl.BlockSpec((n_e, ts), lambda i, e, *_: (0, i))
    wspec = lambda a, b_: pl.BlockSpec((1, a, b_), lambda i, e, *_: (e, 0, 0))
    flat = lambda a: a.reshape(-1).astype(jnp.int32)
    grid_spec = pltpu.PrefetchScalarGridSpec(
        num_scalar_prefetch=4,
        grid=(t // ts, n_e),
        in_specs=[tok(d), tok(d), col, col,
                  pl.BlockSpec((1, 1, d), lambda i, e, *_: ((i * ts) // s, 0, 0)),
                  pl.BlockSpec((1, d), lambda i, e, *_: (0, 0)),
                  wspec(d, ffe), wspec(d, ffe), wspec(ffe, d)],
        out_specs=tok(d),
        scratch_shapes=[pltpu.VMEM((cap_one, d), BF16), pltpu.VMEM((cap_one, LANES), F32),
                        pltpu.VMEM((cap_all, d), BF16)],
    )
    return pl.pallas_call(
        functools.partial(_moe_sparse_kernel, final_norm=final_norm),
        grid_spec=grid_spec,
        out_shape=jax.ShapeDtypeStruct((t, d), F32),
        compiler_params=pltpu.CompilerParams(vmem_limit_bytes=VMEM_LIMIT),
        name="moe_sparse",
    )(flat(counts), flat(offs), flat(tots), flat(bases), h, x1, comb, pos, gt2, final_g, wg, wu, wdn)


def _rope_cos_sin(positions):
    inv_freq = ROPE_THETA ** (-jnp.arange(0, ROT_DIM, 2, dtype=F32) / ROT_DIM)
    ang = positions.astype(F32)[:, None, :] * inv_freq[None, :, None]
    return jnp.concatenate([jnp.cos(ang), jnp.sin(ang)], axis=1)


def _pack_w_in(w):
    d = w.shape[0]
    o1 = ATTN_W
    o2 = o1 + 3 * DN_W
    o3 = o2 + 2 * DN_H
    o4 = o3 + DN_W
    gates = jnp.zeros((d, LANES), w.dtype).at[:, :2 * DN_H].set(w[:, o2:o3])
    return jnp.concatenate([w[:, :o2], gates, w[:, o3:o4], w[:, o4:]], axis=1).astype(BF16)


def kernel(x, c, positions, ada_w, ada_b, norm_mix_g, norm_ffn_g, w_in, attn_sinks, dn_conv_w, dn_a_log, dn_dt_bias, dn_norm_g, cv_dw_w, cv_dw_b, cv_ln_g, cv_ln_b, w_out, ffn_w_gate, ffn_w_up, ffn_w_down, router_w, router_b, moe_w_gate, moe_w_up, moe_w_down, final_norm_g):
    b, s, d = x.shape
    depth = w_in.shape[0]
    mod = _adaln(c, ada_w, ada_b)
    cos_sin = _rope_cos_sin(positions)
    final_g = final_norm_g.reshape(1, d)
    vec = lambda t: t.reshape(b, 1, d)
    for l in range(depth):
        sh1, sc1, gt1 = (vec(t) for t in jnp.split(mod[l, 0], 3, axis=-1))
        sh2, sc2, gt2 = (vec(t) for t in jnp.split(mod[l, 1], 3, axis=-1))
        attn, dnqkv, gate, dz, cv = _inproj(x, norm_mix_g[l].reshape(1, d), sc1, sh1, cos_sin, _pack_w_in(w_in[l]))
        y_attn = _attention(attn, attn_sinks[l].astype(F32))
        y_dn = _deltanet(dnqkv, gate, dz, dn_conv_w[l], dn_a_log[l], dn_dt_bias[l], dn_norm_g[l])
        y_cv = _conformer(cv, cv_dw_w[l], cv_dw_b[l], cv_ln_g[l], cv_ln_b[l])
        t = b * s
        ys = (y_attn.reshape(t, ATTN_QW), y_dn.reshape(t, DN_W), y_cv.reshape(t, CV_C))
        wo = w_out[l].astype(BF16)
        wos = (wo[:ATTN_QW], wo[ATTN_QW:ATTN_QW + DN_W], wo[ATTN_QW + DN_W:])
        mods = (gt1, sc2, sh2, gt2)
        g2 = norm_ffn_g[l].reshape(1, d)
        last = l == depth - 1
        j = l // 2
        if l % 2 == 0:
            x2 = _dense_ffn(x.reshape(t, d), ys, wos, mods, g2, final_g, ffn_w_gate[j].astype(BF16),
                            ffn_w_up[j].astype(BF16), ffn_w_down[j].astype(BF16), s, last)
        else:
            x1, h2, comb, pos, cnt = _route(x.reshape(t, d), ys, wos, mods, g2, router_w[j].astype(F32).T,
                                            router_b[j].astype(F32).reshape(N_EXPERTS, 1), s)
            x2 = _moe_sparse(h2, x1, comb, pos, cnt, gt2, final_g, moe_w_gate[j].astype(BF16),
                             moe_w_up[j].astype(BF16), moe_w_down[j].astype(BF16), s, last)
        x = x2.reshape(b, s, d)
    return x
```

```python
import functools

import numpy as np
import jax
import jax.numpy as jnp
from jax import lax
from jax.experimental import pallas as pl
from jax.experimental.pallas import tpu as pltpu

F32 = jnp.float32
BF16 = jnp.bfloat16
HIGHEST = lax.Precision.HIGHEST

HEAD_DIM = 64
ATTN_HQ = 8
ATTN_HKV = 2
ATTN_GROUP = ATTN_HQ // ATTN_HKV
ATTN_BLOCK = 128
ROT_DIM = HEAD_DIM // 4
ROPE_THETA = 500000.0
DN_H = 4
DN_D = 64
DN_CONV = 4
DN_CHUNK = 64
CV_C = 256
CV_K = 31
ATTN_QW = ATTN_HQ * HEAD_DIM
ATTN_KVW = ATTN_HKV * HEAD_DIM
ATTN_W = ATTN_QW + 2 * ATTN_KVW
DN_W = DN_H * DN_D
N_EXPERTS = 8
EPS = 1e-6
LANES = 128
SUBLANES = 8
VMEM_LIMIT = 48 * 1024 * 1024

TM_PROJ = 512
TQ_ATTN = 512
ATTN_GROUP_UNITS = 8
R_DN = 512
R_CV = 512
TM_FFN = 512
FF_PIECE = 1536
CV_HALO = 32
DN_HALO = 8
TS_MOE = 1024
MOE_SUB = 256
MOE_WIN = 128
MOE_CHUNK = 256
MOE_TAIL = 384
N_TOP = 2


def _dot(a, b, precision=None):
    return jnp.dot(a, b, preferred_element_type=F32, precision=precision)


def _dot_nt(a, b):
    return lax.dot_general(a, b, (((1,), (1,)), ((), ())), preferred_element_type=F32)


def _dot_tn(a, b):
    return lax.dot_general(a, b, (((0,), (0,)), ((), ())), preferred_element_type=F32)


def _split3(x):
    hi = x.astype(BF16)
    rest = x - hi.astype(F32)
    mid = rest.astype(BF16)
    return hi, mid, (rest - mid.astype(F32)).astype(BF16)


def _sigmoid(x):
    return 1.0 / (1.0 + jnp.exp(-x))


def _silu(x):
    return x * _sigmoid(x)


def _rms_modulate(x, g, sc, sh):
    y = x * lax.rsqrt(jnp.mean(x * x, axis=-1, keepdims=True) + EPS)
    return (y * g) * (1.0 + sc) + sh


def _adaln_kernel(c_ref, w_ref, b_ref, o_ref):
    o_ref[0] = _dot(_silu(c_ref[...]), w_ref[0], HIGHEST) + b_ref[0]


def _adaln(c, ada_w, ada_b):
    depth, _, d, d3 = ada_w.shape
    b = c.shape[0]
    rows = 8
    cp = jnp.zeros((rows, d), F32).at[:b].set(c)
    w = ada_w.reshape(depth * 2, d, d3)
    bias = ada_b.reshape(depth * 2, 1, d3)
    tn = 1024
    out = pl.pallas_call(
        _adaln_kernel,
        grid=(depth * 2, d3 // tn),
        in_specs=[
            pl.BlockSpec((rows, d), lambda i, j: (0, 0)),
            pl.BlockSpec((1, d, tn), lambda i, j: (i, 0, j)),
            pl.BlockSpec((1, 1, tn), lambda i, j: (i, 0, j)),
        ],
        out_specs=pl.BlockSpec((1, rows, tn), lambda i, j: (i, 0, j)),
        out_shape=jax.ShapeDtypeStruct((depth * 2, rows, d3), F32),
        compiler_params=pltpu.CompilerParams(vmem_limit_bytes=VMEM_LIMIT),
        name="adaln",
    )(cp, w, bias)
    return out[:, :b].reshape(depth, 2, b, d3)


def _inproj_kernel(x_ref, g_ref, sc_ref, sh_ref, cs_ref, place_ref, ones_ref, w_ref,
                   attn_ref, dnqkv_ref, gate_ref, dz_ref, cv_ref):
    h = _rms_modulate(x_ref[0], g_ref[...], sc_ref[0], sh_ref[0]).astype(BF16)
    a = _dot(h, w_ref[:, 0:ATTN_W])
    tab = _dot_tn(jnp.concatenate(_split3(cs_ref[0]), axis=0), place_ref[...]) + ones_ref[...]
    rc, ra, rb = tab[:, 0:LANES], tab[:, LANES:2 * LANES], tab[:, 2 * LANES:3 * LANES]
    half = ROT_DIM // 2
    for j in range((ATTN_QW + ATTN_KVW) // LANES):
        t = a[:, j * LANES:(j + 1) * LANES]
        r = t * rc + pltpu.roll(t, LANES - half, 1) * ra + pltpu.roll(t, half, 1) * rb
        if j < ATTN_QW // LANES:
            r = r * (HEAD_DIM ** -0.5)
        attn_ref[0, :, j * LANES:(j + 1) * LANES] = r.astype(BF16)
    attn_ref[0, :, ATTN_QW + ATTN_KVW:ATTN_W] = a[:, ATTN_QW + ATTN_KVW:ATTN_W].astype(BF16)
    o = ATTN_W
    dnqkv_ref[0] = _dot(h, w_ref[:, o:o + 3 * DN_W])
    o += 3 * DN_W
    gate_ref[0] = _dot(h, w_ref[:, o:o + LANES])
    o += LANES
    dz_ref[0] = _dot(h, w_ref[:, o:o + DN_W])
    o += DN_W
    cv_ref[0] = _dot(h, w_ref[:, o:o + 2 * CV_C])


def _rope_placement():
    half = ROT_DIM // 2
    place = np.zeros((2 * half, 3 * LANES), np.float32)
    ones = np.zeros((1, 3 * LANES), np.float32)
    for lane in range(LANES):
        dim = lane % HEAD_DIM
        if dim < ROT_DIM:
            place[dim % half, lane] = 1.0
            if dim < half:
                place[half + dim, LANES + lane] = -1.0
            else:
                place[half + dim - half, 2 * LANES + lane] = 1.0
        else:
            ones[0, lane] = 1.0
    return np.tile(place, (3, 1)), ones


def _inproj(x, g, sc, sh, cos_sin, w):
    b, s, d = x.shape
    tm = TM_PROJ
    n = w.shape[1]
    row = lambda width: pl.BlockSpec((1, tm, width), lambda bi, i: (bi, i, 0))
    vec = pl.BlockSpec((1, 1, d), lambda bi, i: (bi, 0, 0))
    const = lambda shape: pl.BlockSpec(shape, lambda bi, i: (0,) * len(shape))
    widths = (ATTN_W, 3 * DN_W, LANES, DN_W, 2 * CV_C)
    dtypes = (BF16, F32, F32, F32, F32)
    place, ones = _rope_placement()
    return pl.pallas_call(
        _inproj_kernel,
        grid=(b, s // tm),
        in_specs=[row(d), const((1, d)), vec, vec,
                  pl.BlockSpec((1, ROT_DIM, tm), lambda bi, i: (bi, 0, i)), const(place.shape), const(ones.shape),
                  const((d, n))],
        out_specs=[row(wd) for wd in widths],
        out_shape=[jax.ShapeDtypeStruct((b, s, wd), dt) for wd, dt in zip(widths, dtypes)],
        compiler_params=pltpu.CompilerParams(vmem_limit_bytes=VMEM_LIMIT),
        name="inproj",
    )(x, g, sc, sh, cos_sin, jnp.asarray(place, dtype=BF16), jnp.asarray(ones), w)


def _attn_kernel(sink_ref, cur_ref, prev_ref, o_ref):
    i = pl.program_id(1)
    blk = ATTN_BLOCK
    cur = cur_ref[0]
    prev = prev_ref[0]
    kv_all = jnp.concatenate([prev[:, ATTN_QW:], cur[:, ATTN_QW:]], axis=0)
    band_w = 2 * blk
    qi = lax.broadcasted_iota(jnp.int32, (blk, 2 * band_w), 0)
    kj = lax.broadcasted_iota(jnp.int32, (blk, 2 * band_w), 1) % band_w
    diff = qi + blk - kj
    band = (diff >= 0) & (diff < blk)
    band_first = band & (kj >= jnp.where(i == 0, blk, 0))
    first_head = lax.broadcasted_iota(jnp.int32, (blk, LANES), 1) < HEAD_DIM
    zeros = jnp.zeros((band_w, HEAD_DIM), BF16)

    def blockdiag(t):
        return jnp.concatenate([jnp.concatenate([t, zeros], axis=1), jnp.concatenate([zeros, t], axis=1)], axis=0)

    units = [(r, j, pr) for r in range(cur.shape[0] // blk) for j in range(ATTN_HKV) for pr in range(ATTN_GROUP // 2)]

    def scores(r, j, pr):
        kb = kv_all[r * blk:(r + 2) * blk, j * HEAD_DIM:(j + 1) * HEAD_DIM]
        slab = (j * (ATTN_GROUP // 2) + pr) * LANES
        return _dot_nt(cur[r * blk:(r + 1) * blk, slab:slab + LANES], blockdiag(kb))

    lane_row = lax.broadcasted_iota(jnp.int32, (1, 2 * band_w), 1)
    not_row0 = lax.broadcasted_iota(jnp.int32, (band_w, HEAD_DIM), 0) > 0
    def value_rows(r, j):
        vb = kv_all[r * blk:(r + 2) * blk, ATTN_KVW + j * HEAD_DIM:ATTN_KVW + (j + 1) * HEAD_DIM]
        return blockdiag(jnp.where(not_row0, vb, jnp.zeros_like(vb)))

    for g0 in range(0, len(units), ATTN_GROUP_UNITS):
        group = units[g0:g0 + ATTN_GROUP_UNITS]
        raw = [scores(*u) for u in group]
        sc = []
        for (r, j, pr), s_raw in zip(group, raw):
            hq = j * ATTN_GROUP + 2 * pr
            fill = jnp.where(lane_row == 0, sink_ref[hq], jnp.where(lane_row == band_w, sink_ref[hq + 1], -jnp.inf))
            sc.append(jnp.where(band_first if r == 0 else band, s_raw, fill))
        halves = [[s_[:, t * band_w:(t + 1) * band_w] for t in range(2)] for s_ in sc]
        mx = [[jnp.max(h_, axis=-1, keepdims=True) for h_ in hs] for hs in halves]
        p = [[jnp.exp(h_ - m_) for h_, m_ in zip(hs, ms)] for hs, ms in zip(halves, mx)]
        rs = [[1.0 / jnp.sum(p_, axis=-1, keepdims=True) for p_ in ps] for ps in p]
        p16 = [jnp.concatenate([p_.astype(BF16) for p_ in ps], axis=1) for ps in p]
        outs = [_dot(p16[n], value_rows(r, j)) for n, (r, j, pr) in enumerate(group)]
        for n, (r, j, pr) in enumerate(group):
            o = outs[n] * jnp.where(first_head, rs[n][0], rs[n][1])
            slab = (j * (ATTN_GROUP // 2) + pr) * LANES
            o_ref[0, r * blk:(r + 1) * blk, slab:slab + LANES] = o.astype(BF16)


def _attention(attn, sinks):
    b, s, _ = attn.shape
    tq = TQ_ATTN
    per = tq // ATTN_BLOCK
    return pl.pallas_call(
        _attn_kernel,
        grid=(b, s // tq),
        in_specs=[
            pl.BlockSpec(memory_space=pltpu.SMEM),
            pl.BlockSpec((1, tq, ATTN_W), lambda bi, i: (bi, i, 0)),
            pl.BlockSpec((1, ATTN_BLOCK, ATTN_W), lambda bi, i: (bi, jnp.maximum(i * per - 1, 0), 0)),
        ],
        out_specs=pl.BlockSpec((1, tq, ATTN_QW), lambda bi, i: (bi, i, 0)),
        out_shape=jax.ShapeDtypeStruct((b, s, ATTN_QW), BF16),
        compiler_params=pltpu.CompilerParams(vmem_limit_bytes=VMEM_LIMIT),
        name="attention",
    )(sinks, attn, attn)


def _deltanet_kernel(cur_ref, prev_ref, gate_ref, dz_ref, cw_ref, alog_ref, dtb_ref, ng_ref,
                     expand_ref, tril_ref, slow_ref, ones_ref, o_ref,
                     xext, q_s, k_s, v_s, b_s, g_s, o_s, state):
    i = pl.program_id(1)
    rows = cur_ref.shape[1]
    ck = DN_CHUNK

    @pl.when(i == 0)
    def _():
        state[...] = jnp.zeros_like(state)

    xext[0:DN_HALO, :] = jnp.where(i > 0, prev_ref[0], 0.0)
    xext[DN_HALO:DN_HALO + rows, :] = cur_ref[0]
    acc = cw_ref[0:1, :] * xext[pl.ds(DN_HALO - DN_CONV + 1, rows), :]
    for t in range(1, DN_CONV):
        acc = acc + cw_ref[t:t + 1, :] * xext[pl.ds(DN_HALO - DN_CONV + 1 + t, rows), :]
    qkv = _silu(acc)
    ones_blk = ones_ref[...]

    def l2n(t):
        ss = _dot((t * t).astype(BF16), ones_blk)
        return t * lax.rsqrt(ss + EPS)

    q_s[...] = l2n(qkv[:, 0:DN_W]) * (DN_D ** -0.5)
    k_s[...] = l2n(qkv[:, DN_W:2 * DN_W])
    v_s[...] = qkv[:, 2 * DN_W:3 * DN_W]
    ge = _dot(jnp.concatenate(_split3(gate_ref[0]), axis=1), expand_ref[...])
    b_s[...] = _sigmoid(ge[:, 0:DN_W])
    da = ge[:, DN_W:2 * DN_W] + dtb_ref[...]
    softplus = jnp.maximum(da, 0.0) + jnp.log1p(jnp.exp(-jnp.abs(da)))
    g_s[...] = -jnp.exp(alog_ref[...]) * softplus

    tril = tril_ref[...]
    slow = slow_ref[...]
    ri = lax.broadcasted_iota(jnp.int32, (ck, DN_W), 0)
    ci = lax.broadcasted_iota(jnp.int32, (ck, DN_W), 1) % ck
    incl = ri >= ci

    pw = 2 * DN_D
    n_pair = DN_W // pw
    r2 = lax.broadcasted_iota(jnp.int32, (ck, pw), 0)
    l2 = lax.broadcasted_iota(jnp.int32, (ck, pw), 1)
    c2 = l2 % ck
    strict = r2 > c2
    eye = jnp.where(r2 == c2, 1.0, 0.0)
    first = l2 < DN_D
    masks = []
    blk = 1
    while blk < ck:
        masks.append((r2 // (2 * blk) == c2 // (2 * blk)) & ((r2 // blk) % 2 == 1) & ((c2 // blk) % 2 == 0))
        blk *= 2
    same_head = (lax.broadcasted_iota(jnp.int32, (pw, pw), 0) // DN_D) == (lax.broadcasted_iota(jnp.int32, (pw, pw), 1) // DN_D)

    def blockdiag(t):
        z = jnp.zeros_like(t)
        return jnp.concatenate([jnp.where(first, t, z), jnp.where(first, z, t)], axis=0)

    nc = rows // ck
    slabs = [slice(p * pw, (p + 1) * pw) for p in range(n_pair)]
    kbeta16, k16, q16, dec, vbeta, kbg, qg, kg, egl = ([] for _ in range(9))
    for c in range(nc):
        r0 = c * ck
        q = q_s[r0:r0 + ck, :]
        k = k_s[r0:r0 + ck, :]
        beta = b_s[r0:r0 + ck, :]
        g = g_s[r0:r0 + ck, :]
        cums = _dot(tril, jnp.concatenate([jnp.concatenate(_split3(g), axis=0),
                                           jnp.concatenate(_split3(g * slow), axis=0)], axis=1))
        gc = cums[:, 0:DN_W]
        gdiff = cums[:, DN_W:]
        decay = jnp.exp(jnp.where(incl, gdiff, -jnp.inf))
        egc = jnp.exp(gc)
        glast = gc[ck - 1:ck, :]
        kbeta = k * beta
        per_slab = ((kbeta16, kbeta.astype(BF16)), (k16, k.astype(BF16)), (q16, q.astype(BF16)), (dec, decay),
                    (vbeta, (v_s[r0:r0 + ck, :] * beta).astype(BF16)), (kbg, (kbeta * egc).astype(BF16)),
                    (qg, q * egc), (kg, (k * jnp.exp(glast - gc)).astype(BF16)), (egl, jnp.exp(glast)))
        for dst, full in per_slab:
            dst.extend(full[:, sl] for sl in slabs)
    inst = range(nc * n_pair)
    bdk = [blockdiag(k16[n]) for n in inst]
    lower = [jnp.where(strict, _dot_nt(kbeta16[n], bdk[n]) * dec[n], 0.0) for n in inst]
    a_intra = [(_dot_nt(q16[n], bdk[n]) * dec[n]).astype(BF16) for n in inst]
    lower16 = [t.astype(BF16) for t in lower]
    tinv = [eye - jnp.where(masks[0], lower[n], 0.0) for n in inst]
    for m in masks[1:]:
        t16 = [t.astype(BF16) for t in tinv]
        x16 = [_dot(jnp.where(m, lower16[n], jnp.zeros_like(lower16[n])), blockdiag(t16[n])).astype(BF16) for n in inst]
        tinv = [tinv[n] - _dot(t16[n], blockdiag(x16[n])) for n in inst]
    t16 = [t.astype(BF16) for t in tinv]
    w16 = [_dot(t16[n], blockdiag(kbg[n])).astype(BF16) for n in inst]
    u16 = [_dot(t16[n], blockdiag(vbeta[n])).astype(BF16) for n in inst]
    kw16 = [jnp.where(same_head, _dot_tn(kg[n], w16[n]), 0.0).astype(BF16) for n in inst]
    ku = [jnp.where(same_head, _dot_tn(kg[n], u16[n]), 0.0) for n in inst]
    qeff16 = [(qg[n] - _dot(a_intra[n], blockdiag(w16[n]))).astype(BF16) for n in inst]
    au = [_dot(a_intra[n], blockdiag(u16[n])) for n in inst]
    st = [state[p] for p in range(n_pair)]
    for c in range(nc):
        outs = []
        for p in range(n_pair):
            n = c * n_pair + p
            s16 = st[p].astype(BF16)
            outs.append(_dot(qeff16[n], s16) + au[n])
            st[p] = st[p] * egl[n] - _dot(kw16[n], s16) + ku[n]
        o_s[c * ck:(c + 1) * ck, :] = jnp.concatenate(outs, axis=1)
    for p in range(n_pair):
        state[p] = st[p]
    o = o_s[...]
    ms = _dot((o * o).astype(BF16), ones_blk) * (1.0 / DN_D)
    y = o * lax.rsqrt(ms + EPS) * ng_ref[...]
    o_ref[0] = (y * _silu(dz_ref[0])).astype(BF16)


def _deltanet(dnqkv, gate, dz, conv_w, a_log, dt_bias, norm_g):
    b, s, _ = dnqkv.shape
    rows = R_DN
    ck = DN_CHUNK
    lane_head = np.arange(2 * DN_W) // DN_D
    expand = np.tile((np.arange(LANES)[:, None] == lane_head[None, :]).astype(np.float32), (3, 1))
    tril = np.tile(np.tril(np.ones((ck, ck), np.float32)), (1, 3))
    slow = (np.arange(ck)[:, None] > (np.arange(DN_W)[None, :] % ck)).astype(np.float32)
    ones_blk = (np.arange(DN_W)[:, None] // DN_D == np.arange(DN_W)[None, :] // DN_D).astype(np.float32)
    rep = lambda t: jnp.repeat(t.astype(F32), DN_D).reshape(1, DN_W)
    const = lambda shape: pl.BlockSpec(shape, lambda bi, i: (0,) * len(shape))
    row = lambda width: pl.BlockSpec((1, rows, width), lambda bi, i: (bi, i, 0))
    per = rows // DN_HALO
    return pl.pallas_call(
        _deltanet_kernel,
        grid=(b, s // rows),
        in_specs=[
            row(3 * DN_W),
            pl.BlockSpec((1, DN_HALO, 3 * DN_W), lambda bi, i: (bi, jnp.maximum(i * per - 1, 0), 0)),
            row(LANES), row(DN_W),
            const((DN_CONV, 3 * DN_W)), const((1, DN_W)), const((1, DN_W)), const((1, DN_W)),
            const((3 * LANES, 2 * DN_W)), const((ck, 3 * ck)), const((ck, DN_W)), const((DN_W, DN_W)),
        ],
        out_specs=row(DN_W),
        out_shape=jax.ShapeDtypeStruct((b, s, DN_W), BF16),
        scratch_shapes=[
            pltpu.VMEM((rows + DN_HALO, 3 * DN_W), F32),
            pltpu.VMEM((rows, DN_W), F32), pltpu.VMEM((rows, DN_W), F32), pltpu.VMEM((rows, DN_W), F32),
            pltpu.VMEM((rows, DN_W), F32), pltpu.VMEM((rows, DN_W), F32), pltpu.VMEM((rows, DN_W), F32),
            pltpu.VMEM((DN_W // (2 * DN_D), 2 * DN_D, 2 * DN_D), F32),
        ],
        compiler_params=pltpu.CompilerParams(vmem_limit_bytes=VMEM_LIMIT),
        name="deltanet",
    )(dnqkv, dnqkv, gate, dz, conv_w, rep(a_log), rep(dt_bias), jnp.tile(norm_g.astype(F32), DN_H).reshape(1, DN_W),
      jnp.asarray(expand, dtype=BF16), jnp.asarray(tril, dtype=BF16), jnp.asarray(slow), jnp.asarray(ones_blk, dtype=BF16))


def _conformer_kernel(cur_ref, prev_ref, w_ref, b_ref, lg_ref, lb_ref, o_ref, u_s, sh_s):
    i = pl.program_id(1)
    rows = cur_ref.shape[1]
    prev = jnp.where(i > 0, prev_ref[0], 0.0)
    u_s[0:CV_HALO, :] = prev[:, 0:CV_C] * _sigmoid(prev[:, CV_C:])
    cur = cur_ref[0]
    u_s[CV_HALO:CV_HALO + rows, :] = cur[:, 0:CV_C] * _sigmoid(cur[:, CV_C:])
    span = rows + CV_HALO - SUBLANES
    for ph in range(1, SUBLANES):
        sh_s[ph - 1, 0:span, :] = u_s[pl.ds(ph, span), :]
    base = CV_HALO - CV_K + 1
    acc = b_ref[...]
    for t in range(CV_K):
        blk, ph = divmod(base + t, SUBLANES)
        src = u_s if ph == 0 else sh_s.at[ph - 1]
        acc = acc + w_ref[t:t + 1, :] * src[blk * SUBLANES:blk * SUBLANES + rows, :]
    mu = jnp.mean(acc, axis=-1, keepdims=True)
    xc = acc - mu
    y = xc * lax.rsqrt(jnp.mean(xc * xc, axis=-1, keepdims=True) + EPS)
    o_ref[0] = _silu(y * lg_ref[...] + lb_ref[...]).astype(BF16)


def _conformer(cv, w, bias, ln_g, ln_b):
    b, s, _ = cv.shape
    rows = R_CV
    per = rows // CV_HALO
    const = lambda shape: pl.BlockSpec(shape, lambda bi, i: (0,) * len(shape))
    return pl.pallas_call(
        _conformer_kernel,
        grid=(b, s // rows),
        in_specs=[
            pl.BlockSpec((1, rows, 2 * CV_C), lambda bi, i: (bi, i, 0)),
            pl.BlockSpec((1, CV_HALO, 2 * CV_C), lambda bi, i: (bi, jnp.maximum(i * per - 1, 0), 0)),
            const((CV_K, CV_C)), const((1, CV_C)), const((1, CV_C)), const((1, CV_C)),
        ],
        out_specs=pl.BlockSpec((1, rows, CV_C), lambda bi, i: (bi, i, 0)),
        out_shape=jax.ShapeDtypeStruct((b, s, CV_C), BF16),
        scratch_shapes=[pltpu.VMEM((rows + CV_HALO, CV_C), F32),
                        pltpu.VMEM((SUBLANES - 1, rows + CV_HALO - SUBLANES, CV_C), F32)],
        compiler_params=pltpu.CompilerParams(vmem_limit_bytes=VMEM_LIMIT),
        name="conformer",
    )(cv, cv, w, bias.reshape(1, CV_C), ln_g.reshape(1, CV_C), ln_b.reshape(1, CV_C))


def _mix_residual(x_ref, ya_ref, yd_ref, yc_ref, wa_ref, wd_ref, wc_ref, gt1_ref):
    y = _dot(ya_ref[...], wa_ref[...]) + _dot(yd_ref[...], wd_ref[...]) + _dot(yc_ref[...], wc_ref[...])
    return x_ref[...] + gt1_ref[0] * y


def _finish(x1, acc, gt2_ref, fg_ref, o_ref, final_norm):
    x2 = x1 + gt2_ref[0] * acc
    if final_norm:
        x2 = x2 * lax.rsqrt(jnp.mean(x2 * x2, axis=-1, keepdims=True) + EPS) * fg_ref[...]
    o_ref[...] = x2


def _dense_ffn_kernel(x_ref, ya_ref, yd_ref, yc_ref, wa_ref, wd_ref, wc_ref, gt1_ref, g_ref, sc_ref, sh_ref, gt2_ref,
                      fg_ref, wg_ref, wu_ref, wdn_ref, o_ref, *, final_norm):
    x1 = _mix_residual(x_ref, ya_ref, yd_ref, yc_ref, wa_ref, wd_ref, wc_ref, gt1_ref)
    h = _rms_modulate(x1, g_ref[...], sc_ref[0], sh_ref[0]).astype(BF16)
    ff = wg_ref.shape[1]
    acc = None
    for c0 in range(0, ff, FF_PIECE):
        c1 = min(c0 + FF_PIECE, ff)
        a = (_silu(_dot(h, wg_ref[:, c0:c1])) * _dot(h, wu_ref[:, c0:c1])).astype(BF16)
        part = _dot(a, wdn_ref[c0:c1, :])
        acc = part if acc is None else acc + part
    _finish(x1, acc, gt2_ref, fg_ref, o_ref, final_norm)


def _route_kernel(x_ref, ya_ref, yd_ref, yc_ref, wa_ref, wd_ref, wc_ref, gt1_ref, g_ref, sc_ref, sh_ref,
                  rwt_ref, rb_ref, upper_ref, x1_ref, h_ref, comb_ref, pos_ref, cnt_ref):
    subs = [slice(n * MOE_SUB, (n + 1) * MOE_SUB) for n in range(x_ref.shape[0] // MOE_SUB)]
    wa, wd, wc = wa_ref[...], wd_ref[...], wc_ref[...]
    y = [_dot(ya_ref[sl, :], wa) + _dot(yd_ref[sl, :], wd) + _dot(yc_ref[sl, :], wc) for sl in subs]
    x1 = [x_ref[sl, :] + gt1_ref[0] * y_ for sl, y_ in zip(subs, y)]
    h = [_rms_modulate(x_, g_ref[...], sc_ref[0], sh_ref[0]) for x_ in x1]
    for sl, x_, h_ in zip(subs, x1, h):
        x1_ref[sl, :] = x_
        h_ref[sl, :] = h_.astype(BF16)
    w_hi, w_mid, _ = _split3(rwt_ref[...])
    w_cat = jnp.concatenate([w_hi, w_hi, w_mid], axis=1)
    pieces = [_split3(h_) for h_ in h]
    logits = [_dot_nt(w_cat, jnp.concatenate([hi, mid, hi], axis=1)) + rb_ref[...] for hi, mid, _ in pieces]
    row = lax.broadcasted_iota(jnp.int32, logits[0].shape, 0)
    lane = lax.broadcasted_iota(jnp.int32, (N_EXPERTS, LANES), 1)
    cnt = jnp.zeros((N_EXPERTS, LANES), F32)
    for n, (sl, lg) in enumerate(zip(subs, logits)):
        m1 = jnp.max(lg, axis=0, keepdims=True)
        i1 = jnp.min(jnp.where(lg == m1, row, N_EXPERTS), axis=0, keepdims=True)
        rest = jnp.where(row == i1, -jnp.inf, lg)
        m2 = jnp.max(rest, axis=0, keepdims=True)
        i2 = jnp.min(jnp.where(rest == m2, row, N_EXPERTS), axis=0, keepdims=True)
        e2 = jnp.exp(m2 - m1)
        comb_ref[:, sl] = jnp.where(row == i1, 1.0 / (1.0 + e2), 0.0) + jnp.where(row == i2, e2 / (1.0 + e2), 0.0)
        sel = jnp.where((row == i1) | (row == i2), 1.0, 0.0)
        rank = _dot(sel.astype(BF16), upper_ref[...])
        pos_ref[:, sl] = jnp.where(sel > 0.0, rank, -1.0)
        cnt = cnt + jnp.where(lane == n, jnp.sum(sel, axis=1, keepdims=True), 0.0)
    cnt_ref[0] = cnt.astype(jnp.int32)


def _route(x, ys, w_out, mods, g, rwt, rb, s):
    t, d = x.shape
    tm = TM_FFN
    gt1, sc, sh, _ = mods
    tok = lambda width: pl.BlockSpec((tm, width), lambda i: (i, 0))
    const = lambda shape: pl.BlockSpec(shape, lambda i: (0,) * len(shape))
    vec = pl.BlockSpec((1, 1, d), lambda i: ((i * tm) // s, 0, 0))
    col = pl.BlockSpec((N_EXPERTS, tm), lambda i: (0, i))
    upper = np.triu(np.ones((MOE_SUB, MOE_SUB), np.float32), 1)
    return pl.pallas_call(
        _route_kernel,
        grid=(t // tm,),
        in_specs=[tok(d), tok(ATTN_QW), tok(DN_W), tok(CV_C), const((ATTN_QW, d)), const((DN_W, d)), const((CV_C, d)),
                  vec, const((1, d)), vec, vec, const((N_EXPERTS, d)), const((N_EXPERTS, 1)), const((MOE_SUB, MOE_SUB))],
        out_specs=[tok(d), tok(d), col, col, pl.BlockSpec((1, N_EXPERTS, LANES), lambda i: (i, 0, 0))],
        out_shape=[jax.ShapeDtypeStruct((t, d), F32), jax.ShapeDtypeStruct((t, d), BF16),
                   jax.ShapeDtypeStruct((N_EXPERTS, t), F32), jax.ShapeDtypeStruct((N_EXPERTS, t), F32),
                   jax.ShapeDtypeStruct((t // tm, N_EXPERTS, LANES), jnp.int32)],
        compiler_params=pltpu.CompilerParams(vmem_limit_bytes=VMEM_LIMIT),
        name="route",
    )(x, *ys, *w_out, gt1, g, sc, sh, rwt, rb, jnp.asarray(upper, dtype=BF16))


def _moe_sparse_kernel(cnt_ref, off_ref, tot_ref, base_ref, h_ref, x1_ref, comb_ref, pos_ref, gt2_ref, fg_ref,
                       wg_ref, wu_ref, wdn_ref, o_ref, xg_s, wrow_s, yw_s, *, final_norm):
    i = pl.program_id(0)
    e = pl.program_id(1)
    n_sub = h_ref.shape[0] // MOE_SUB
    d = h_ref.shape[1]
    win_row = lax.broadcasted_iota(jnp.int32, (MOE_WIN, MOE_SUB), 0)

    def gather(sub, p):
        k = (i * n_sub + sub) * N_EXPERTS + e
        lanes = slice(sub * MOE_SUB, (sub + 1) * MOE_SUB)
        pos = pos_ref[pl.ds(e, 1), lanes].astype(jnp.int32)
        sel = win_row + p * MOE_WIN == pos
        r0 = pl.multiple_of(off_ref[k] + p * MOE_WIN, 16)
        rows = _dot(jnp.where(sel, 1.0, 0.0).astype(BF16), h_ref[lanes, :])
        xg_s[pl.ds(r0, MOE_WIN), :] = rows.astype(BF16)
        w = jnp.sum(jnp.where(sel, comb_ref[pl.ds(e, 1), lanes], 0.0), axis=1, keepdims=True)
        wrow_s[pl.ds(r0, MOE_WIN), :] = jnp.broadcast_to(w, (MOE_WIN, LANES))

    most = cnt_ref[i * n_sub * N_EXPERTS + e]
    for sub in range(1, n_sub):
        most = jnp.maximum(most, cnt_ref[(i * n_sub + sub) * N_EXPERTS + e])
    for p in range(MOE_SUB // MOE_WIN - 1, 0, -1):
        @pl.when(most > p * MOE_WIN)
        def _():
            for sub in range(n_sub):
                @pl.when(cnt_ref[(i * n_sub + sub) * N_EXPERTS + e] > p * MOE_WIN)
                def _():
                    gather(sub, p)
    for sub in range(n_sub):
        gather(sub, 0)

    total = pl.multiple_of(tot_ref[i * N_EXPERTS + e], 16)
    base = pl.multiple_of(base_ref[i * N_EXPERTS + e], 16)
    xg_s[pl.ds(total, MOE_TAIL), :] = jnp.zeros((MOE_TAIL, d), BF16)
    wrow_s[pl.ds(total, MOE_TAIL), :] = jnp.zeros((MOE_TAIL, LANES), F32)

    def ffn(r0, rows):
        xc = xg_s[pl.ds(r0, rows), :]
        a = (_silu(_dot(xc, wg_ref[0])) * _dot(xc, wu_ref[0])).astype(BF16)
        y = _dot(a, wdn_ref[0]) * jnp.tile(wrow_s[pl.ds(r0, rows), :], (1, d // LANES))
        yw_s[pl.ds(base + r0, rows), :] = y.astype(BF16)

    n_full = total // MOE_CHUNK
    rem = total - n_full * MOE_CHUNK
    n_chunks = n_full + jnp.where(rem > MOE_CHUNK // 2, 1, 0)
    ffn(0, MOE_CHUNK)

    def full_chunk(c, carry):
        ffn(pl.multiple_of(c * MOE_CHUNK, MOE_CHUNK), MOE_CHUNK)
        return carry

    lax.fori_loop(1, n_chunks, full_chunk, 0)
    half = (rem > 0) & (rem <= MOE_CHUNK // 2) & (n_full > 0)

    @pl.when(half)
    def _():
        ffn(pl.multiple_of(n_full * MOE_CHUNK, MOE_CHUNK), MOE_CHUNK // 2)

    covered = jnp.maximum(n_chunks, 1) * MOE_CHUNK + jnp.where(half, MOE_CHUNK // 2, 0)
    yw_s[pl.ds(pl.multiple_of(base + covered, 16), MOE_TAIL), :] = jnp.zeros((MOE_TAIL, d), BF16)

    @pl.when(e == pl.num_programs(1) - 1)
    def _():
        for sub in range(n_sub):
            lanes = slice(sub * MOE_SUB, (sub + 1) * MOE_SUB)
            first = (i * n_sub + sub) * N_EXPERTS

            def windows(p):
                sels, rows = [], []
                for ex in range(N_EXPERTS):
                    r0 = pl.multiple_of(base_ref[i * N_EXPERTS + ex] + off_ref[first + ex] + p * MOE_WIN, 16)
                    pos = pos_ref[ex:ex + 1, lanes].astype(jnp.int32)
                    sels.append(jnp.where(win_row + p * MOE_WIN == pos, 1.0, 0.0).astype(BF16))
                    rows.append(yw_s[pl.ds(r0, MOE_WIN), :])
                return _dot_tn(jnp.concatenate(sels, axis=0), jnp.concatenate(rows, axis=0))

            o_ref[lanes, :] = windows(0)
            most = cnt_ref[first]
            for ex in range(1, N_EXPERTS):
                most = jnp.maximum(most, cnt_ref[first + ex])
            for p in range(1, MOE_SUB // MOE_WIN):
                @pl.when(most > p * MOE_WIN)
                def _():
                    o_ref[lanes, :] += windows(p)
        _finish(x1_ref[...], o_ref[...], gt2_ref, fg_ref, o_ref, final_norm)


def _dense_ffn(x, ys, w_out, mods, g, final_g, wg, wu, wdn, s, final_norm):
    t, d = x.shape
    tm = TM_FFN
    ff = wg.shape[1]
    gt1, sc, sh, gt2 = mods
    tok = lambda width: pl.BlockSpec((tm, width), lambda i: (i, 0))
    const = lambda shape: pl.BlockSpec(shape, lambda i: (0,) * len(shape), pipeline_mode=pl.Buffered(1))
    vec = pl.BlockSpec((1, 1, d), lambda i: ((i * tm) // s, 0, 0))
    return pl.pallas_call(
        functools.partial(_dense_ffn_kernel, final_norm=final_norm),
        grid=(t // tm,),
        in_specs=[tok(d), tok(ATTN_QW), tok(DN_W), tok(CV_C), const((ATTN_QW, d)), const((DN_W, d)), const((CV_C, d)),
                  vec, const((1, d)), vec, vec, vec, const((1, d)),
                  const((d, ff)), const((d, ff)), const((ff, d))],
        out_specs=tok(d),
        out_shape=jax.ShapeDtypeStruct((t, d), F32),
        compiler_params=pltpu.CompilerParams(vmem_limit_bytes=VMEM_LIMIT),
        name="dense_ffn",
    )(x, *ys, *w_out, gt1, g, sc, sh, gt2, final_g, wg, wu, wdn)


def _moe_sparse(h, x1, comb, pos, cnt, gt2, final_g, wg, wu, wdn, s, final_norm):
    t, d = x1.shape
    ts = TS_MOE
    n_e, _, ffe = wg.shape
    n_sub = ts // MOE_SUB
    counts = cnt[:, :, :TM_FFN // MOE_SUB].transpose(0, 2, 1).reshape(t // ts, n_sub, n_e)
    padded = (counts + 15) // 16 * 16
    offs = jnp.cumsum(padded, axis=1) - padded
    tots = jnp.sum(padded, axis=1)
    bases = jnp.cumsum(tots, axis=1) - tots
    cap_one = -(-(ts + n_sub * 16 + MOE_TAIL) // 16) * 16
    cap_all = -(-(N_TOP * ts + n_e * n_sub * 16 + MOE_CHUNK + MOE_TAIL) // 16) * 16
    tok = lambda width: pl.BlockSpec((ts, width), lambda i, e, *_: (i, 0))
    col = pl.BlockSpec((n_e, ts), lambda i, e, *_: (0, i))
    wspec = lambda a, b_: pl.BlockSpec((1, a, b_), lambda i, e, *_: (e, 0, 0))
    flat = lambda a: a.reshape(-1).astype(jnp.int32)
    grid_spec = pltpu.PrefetchScalarGridSpec(
        num_scalar_prefetch=4,
        grid=(t // ts, n_e),
        in_specs=[tok(d), tok(d), col, col,
                  pl.BlockSpec((1, 1, d), lambda i, e, *_: ((i * ts) // s, 0, 0)),
                  pl.BlockSpec((1, d), lambda i, e, *_: (0, 0)),
                  wspec(d, ffe), wspec(d, ffe), wspec(ffe, d)],
        out_specs=tok(d),
        scratch_shapes=[pltpu.VMEM((cap_one, d), BF16), pltpu.VMEM((cap_one, LANES), F32),
                        pltpu.VMEM((cap_all, d), BF16)],
    )
    return pl.pallas_call(
        functools.partial(_moe_sparse_kernel, final_norm=final_norm),
        grid_spec=grid_spec,
        out_shape=jax.ShapeDtypeStruct((t, d), F32),
        compiler_params=pltpu.CompilerParams(vmem_limit_bytes=VMEM_LIMIT),
        name="moe_sparse",
    )(flat(counts), flat(offs), flat(tots), flat(bases), h, x1, comb, pos, gt2, final_g, wg, wu, wdn)


def _rope_cos_sin(positions):
    inv_freq = ROPE_THETA ** (-jnp.arange(0, ROT_DIM, 2, dtype=F32) / ROT_DIM)
    ang = positions.astype(F32)[:, None, :] * inv_freq[None, :, None]
    return jnp.concatenate([jnp.cos(ang), jnp.sin(ang)], axis=1)


def _pack_w_in(w):
    d = w.shape[0]
    o1 = ATTN_W
    o2 = o1 + 3 * DN_W
    o3 = o2 + 2 * DN_H
    o4 = o3 + DN_W
    gates = jnp.zeros((d, LANES), w.dtype).at[:, :2 * DN_H].set(w[:, o2:o3])
    return jnp.concatenate([w[:, :o2], gates, w[:, o3:o4], w[:, o4:]], axis=1).astype(BF16)


def kernel(x, c, positions, ada_w, ada_b, norm_mix_g, norm_ffn_g, w_in, attn_sinks, dn_conv_w, dn_a_log, dn_dt_bias, dn_norm_g, cv_dw_w, cv_dw_b, cv_ln_g, cv_ln_b, w_out, ffn_w_gate, ffn_w_up, ffn_w_down, router_w, router_b, moe_w_gate, moe_w_up, moe_w_down, final_norm_g):
    b, s, d = x.shape
    depth = w_in.shape[0]
    mod = _adaln(c, ada_w, ada_b)
    cos_sin = _rope_cos_sin(positions)
    final_g = final_norm_g.reshape(1, d)
    vec = lambda t: t.reshape(b, 1, d)
    for l in range(depth):
        sh1, sc1, gt1 = (vec(t) for t in jnp.split(mod[l, 0], 3, axis=-1))
        sh2, sc2, gt2 = (vec(t) for t in jnp.split(mod[l, 1], 3, axis=-1))
        attn, dnqkv, gate, dz, cv = _inproj(x, norm_mix_g[l].reshape(1, d), sc1, sh1, cos_sin, _pack_w_in(w_in[l]))
        y_attn = _attention(attn, attn_sinks[l].astype(F32))
        y_dn = _deltanet(dnqkv, gate, dz, dn_conv_w[l], dn_a_log[l], dn_dt_bias[l], dn_norm_g[l])
        y_cv = _conformer(cv, cv_dw_w[l], cv_dw_b[l], cv_ln_g[l], cv_ln_b[l])
        t = b * s
        ys = (y_attn.reshape(t, ATTN_QW), y_dn.reshape(t, DN_W), y_cv.reshape(t, CV_C))
        wo = w_out[l].astype(BF16)
        wos = (wo[:ATTN_QW], wo[ATTN_QW:ATTN_QW + DN_W], wo[ATTN_QW + DN_W:])
        mods = (gt1, sc2, sh2, gt2)
        g2 = norm_ffn_g[l].reshape(1, d)
        last = l == depth - 1
        j = l // 2
        if l % 2 == 0:
            x2 = _dense_ffn(x.reshape(t, d), ys, wos, mods, g2, final_g, ffn_w_gate[j].astype(BF16),
                            ffn_w_up[j].astype(BF16), ffn_w_down[j].astype(BF16), s, last)
        else:
            x1, h2, comb, pos, cnt = _route(x.reshape(t, d), ys, wos, mods, g2, router_w[j].astype(F32).T,
                                            router_b[j].astype(F32).reshape(N_EXPERTS, 1), s)
            x2 = _moe_sparse(h2, x1, comb, pos, cnt, gt2, final_g, moe_w_gate[j].astype(BF16),
                             moe_w_up[j].astype(BF16), moe_w_down[j].astype(BF16), s, last)
        x = x2.reshape(b, s, d)
    return x
```

```python
import functools

import numpy as np
import jax
import jax.numpy as jnp
from jax import lax
from jax.experimental import pallas as pl
from jax.experimental.pallas import tpu as pltpu

F32 = jnp.float32
BF16 = jnp.bfloat16
HIGHEST = lax.Precision.HIGHEST

HEAD_DIM = 64
ATTN_HQ = 8
ATTN_HKV = 2
ATTN_GROUP = ATTN_HQ // ATTN_HKV
ATTN_BLOCK = 128
ROT_DIM = HEAD_DIM // 4
ROPE_THETA = 500000.0
DN_H = 4
DN_D = 64
DN_CONV = 4
DN_CHUNK = 64
CV_C = 256
CV_K = 31
ATTN_QW = ATTN_HQ * HEAD_DIM
ATTN_KVW = ATTN_HKV * HEAD_DIM
ATTN_W = ATTN_QW + 2 * ATTN_KVW
DN_W = DN_H * DN_D
N_EXPERTS = 8
EPS = 1e-6
LANES = 128
SUBLANES = 8
VMEM_LIMIT = 48 * 1024 * 1024
VMEM_LIMIT_MOE = 56 * 1024 * 1024

TM_PROJ = 512
TQ_ATTN = 512
ATTN_GROUP_UNITS = 8
R_DN = 512
R_CV = 512
TM_FFN = 512
FF_PIECE = 1536
CV_HALO = 32
DN_HALO = 8
TS_MOE = 1024
MOE_SUB = 256
MOE_WIN = 128
MOE_CHUNK = 256
MOE_TAIL = 384
N_TOP = 2
W_SLOTS = 3


def _dot(a, b, precision=None):
    return jnp.dot(a, b, preferred_element_type=F32, precision=precision)


def _dot_nt(a, b):
    return lax.dot_general(a, b, (((1,), (1,)), ((), ())), preferred_element_type=F32)


def _dot_tn(a, b):
    return lax.dot_general(a, b, (((0,), (0,)), ((), ())), preferred_element_type=F32)


def _split3(x):
    hi = x.astype(BF16)
    rest = x - hi.astype(F32)
    mid = rest.astype(BF16)
    return hi, mid, (rest - mid.astype(F32)).astype(BF16)


def _sigmoid(x):
    return 1.0 / (1.0 + jnp.exp(-x))


def _silu(x):
    return x * _sigmoid(x)


def _rms_modulate(x, g, sc, sh):
    y = x * lax.rsqrt(jnp.mean(x * x, axis=-1, keepdims=True) + EPS)
    return (y * g) * (1.0 + sc) + sh


def _adaln_kernel(c_ref, w_ref, b_ref, o_ref):
    o_ref[0] = _dot(_silu(c_ref[...]), w_ref[0], HIGHEST) + b_ref[0]


def _adaln(c, ada_w, ada_b):
    depth, _, d, d3 = ada_w.shape
    b = c.shape[0]
    rows = 8
    cp = jnp.zeros((rows, d), F32).at[:b].set(c)
    w = ada_w.reshape(depth * 2, d, d3)
    bias = ada_b.reshape(depth * 2, 1, d3)
    tn = 1024
    out = pl.pallas_call(
        _adaln_kernel,
        grid=(depth * 2, d3 // tn),
        in_specs=[
            pl.BlockSpec((rows, d), lambda i, j: (0, 0)),
            pl.BlockSpec((1, d, tn), lambda i, j: (i, 0, j)),
            pl.BlockSpec((1, 1, tn), lambda i, j: (i, 0, j)),
        ],
        out_specs=pl.BlockSpec((1, rows, tn), lambda i, j: (i, 0, j)),
        out_shape=jax.ShapeDtypeStruct((depth * 2, rows, d3), F32),
        compiler_params=pltpu.CompilerParams(vmem_limit_bytes=VMEM_LIMIT),
        name="adaln",
    )(cp, w, bias)
    return out[:, :b].reshape(depth, 2, b, d3)


def _inproj_kernel(x_ref, g_ref, sc_ref, sh_ref, cs_ref, place_ref, ones_ref, w_ref,
                   attn_ref, dnqkv_ref, gate_ref, dz_ref, cv_ref):
    h = _rms_modulate(x_ref[0], g_ref[...], sc_ref[0], sh_ref[0]).astype(BF16)
    a = _dot(h, w_ref[:, 0:ATTN_W])
    tab = _dot_tn(jnp.concatenate(_split3(cs_ref[0]), axis=0), place_ref[...]) + ones_ref[...]
    rc, ra, rb = tab[:, 0:LANES], tab[:, LANES:2 * LANES], tab[:, 2 * LANES:3 * LANES]
    half = ROT_DIM // 2
    for j in range((ATTN_QW + ATTN_KVW) // LANES):
        t = a[:, j * LANES:(j + 1) * LANES]
        r = t * rc + pltpu.roll(t, LANES - half, 1) * ra + pltpu.roll(t, half, 1) * rb
        if j < ATTN_QW // LANES:
            r = r * (HEAD_DIM ** -0.5)
        attn_ref[0, :, j * LANES:(j + 1) * LANES] = r.astype(BF16)
    attn_ref[0, :, ATTN_QW + ATTN_KVW:ATTN_W] = a[:, ATTN_QW + ATTN_KVW:ATTN_W].astype(BF16)
    o = ATTN_W
    dnqkv_ref[0] = _dot(h, w_ref[:, o:o + 3 * DN_W])
    o += 3 * DN_W
    gate_ref[0] = _dot(h, w_ref[:, o:o + LANES])
    o += LANES
    dz_ref[0] = _dot(h, w_ref[:, o:o + DN_W])
    o += DN_W
    cv_ref[0] = _dot(h, w_ref[:, o:o + 2 * CV_C])


def _rope_placement():
    half = ROT_DIM // 2
    place = np.zeros((2 * half, 3 * LANES), np.float32)
    ones = np.zeros((1, 3 * LANES), np.float32)
    for lane in range(LANES):
        dim = lane % HEAD_DIM
        if dim < ROT_DIM:
            place[dim % half, lane] = 1.0
            if dim < half:
                place[half + dim, LANES + lane] = -1.0
            else:
                place[half + dim - half, 2 * LANES + lane] = 1.0
        else:
            ones[0, lane] = 1.0
    return np.tile(place, (3, 1)), ones


def _inproj(x, g, sc, sh, cos_sin, w):
    b, s, d = x.shape
    tm = TM_PROJ
    n = w.shape[1]
    row = lambda width: pl.BlockSpec((1, tm, width), lambda bi, i: (bi, i, 0))
    vec = pl.BlockSpec((1, 1, d), lambda bi, i: (bi, 0, 0))
    const = lambda shape: pl.BlockSpec(shape, lambda bi, i: (0,) * len(shape))
    widths = (ATTN_W, 3 * DN_W, LANES, DN_W, 2 * CV_C)
    dtypes = (BF16, F32, F32, F32, F32)
    place, ones = _rope_placement()
    return pl.pallas_call(
        _inproj_kernel,
        grid=(b, s // tm),
        in_specs=[row(d), const((1, d)), vec, vec,
                  pl.BlockSpec((1, ROT_DIM, tm), lambda bi, i: (bi, 0, i)), const(place.shape), const(ones.shape),
                  const((d, n))],
        out_specs=[row(wd) for wd in widths],
        out_shape=[jax.ShapeDtypeStruct((b, s, wd), dt) for wd, dt in zip(widths, dtypes)],
        compiler_params=pltpu.CompilerParams(vmem_limit_bytes=VMEM_LIMIT),
        name="inproj",
    )(x, g, sc, sh, cos_sin, jnp.asarray(place, dtype=BF16), jnp.asarray(ones), w)


def _attn_kernel(sink_ref, cur_ref, prev_ref, o_ref):
    i = pl.program_id(1)
    blk = ATTN_BLOCK
    cur = cur_ref[0]
    prev = prev_ref[0]
    kv_all = jnp.concatenate([prev[:, ATTN_QW:], cur[:, ATTN_QW:]], axis=0)
    band_w = 2 * blk
    qi = lax.broadcasted_iota(jnp.int32, (blk, 2 * band_w), 0)
    kj = lax.broadcasted_iota(jnp.int32, (blk, 2 * band_w), 1) % band_w
    diff = qi + blk - kj
    band = (diff >= 0) & (diff < blk)
    band_first = band & (kj >= jnp.where(i == 0, blk, 0))
    first_head = lax.broadcasted_iota(jnp.int32, (blk, LANES), 1) < HEAD_DIM
    zeros = jnp.zeros((band_w, HEAD_DIM), BF16)

    def blockdiag(t):
        return jnp.concatenate([jnp.concatenate([t, zeros], axis=1), jnp.concatenate([zeros, t], axis=1)], axis=0)

    units = [(r, j, pr) for r in range(cur.shape[0] // blk) for j in range(ATTN_HKV) for pr in range(ATTN_GROUP // 2)]

    def scores(r, j, pr):
        kb = kv_all[r * blk:(r + 2) * blk, j * HEAD_DIM:(j + 1) * HEAD_DIM]
        slab = (j * (ATTN_GROUP // 2) + pr) * LANES
        return _dot_nt(cur[r * blk:(r + 1) * blk, slab:slab + LANES], blockdiag(kb))

    lane_row = lax.broadcasted_iota(jnp.int32, (1, 2 * band_w), 1)
    not_row0 = lax.broadcasted_iota(jnp.int32, (band_w, HEAD_DIM), 0) > 0
    def value_rows(r, j):
        vb = kv_all[r * blk:(r + 2) * blk, ATTN_KVW + j * HEAD_DIM:ATTN_KVW + (j + 1) * HEAD_DIM]
        return blockdiag(jnp.where(not_row0, vb, jnp.zeros_like(vb)))

    for g0 in range(0, len(units), ATTN_GROUP_UNITS):
        group = units[g0:g0 + ATTN_GROUP_UNITS]
        raw = [scores(*u) for u in group]
        sc = []
        for (r, j, pr), s_raw in zip(group, raw):
            hq = j * ATTN_GROUP + 2 * pr
            fill = jnp.where(lane_row == 0, sink_ref[hq], jnp.where(lane_row == band_w, sink_ref[hq + 1], -jnp.inf))
            sc.append(jnp.where(band_first if r == 0 else band, s_raw, fill))
        halves = [[s_[:, t * band_w:(t + 1) * band_w] for t in range(2)] for s_ in sc]
        mx = [[jnp.max(h_, axis=-1, keepdims=True) for h_ in hs] for hs in halves]
        p = [[jnp.exp(h_ - m_) for h_, m_ in zip(hs, ms)] for hs, ms in zip(halves, mx)]
        rs = [[1.0 / jnp.sum(p_, axis=-1, keepdims=True) for p_ in ps] for ps in p]
        p16 = [jnp.concatenate([p_.astype(BF16) for p_ in ps], axis=1) for ps in p]
        outs = [_dot(p16[n], value_rows(r, j)) for n, (r, j, pr) in enumerate(group)]
        for n, (r, j, pr) in enumerate(group):
            o = outs[n] * jnp.where(first_head, rs[n][0], rs[n][1])
            slab = (j * (ATTN_GROUP // 2) + pr) * LANES
            o_ref[0, r * blk:(r + 1) * blk, slab:slab + LANES] = o.astype(BF16)


def _attention(attn, sinks):
    b, s, _ = attn.shape
    tq = TQ_ATTN
    per = tq // ATTN_BLOCK
    return pl.pallas_call(
        _attn_kernel,
        grid=(b, s // tq),
        in_specs=[
            pl.BlockSpec(memory_space=pltpu.SMEM),
            pl.BlockSpec((1, tq, ATTN_W), lambda bi, i: (bi, i, 0)),
            pl.BlockSpec((1, ATTN_BLOCK, ATTN_W), lambda bi, i: (bi, jnp.maximum(i * per - 1, 0), 0)),
        ],
        out_specs=pl.BlockSpec((1, tq, ATTN_QW), lambda bi, i: (bi, i, 0)),
        out_shape=jax.ShapeDtypeStruct((b, s, ATTN_QW), BF16),
        compiler_params=pltpu.CompilerParams(vmem_limit_bytes=VMEM_LIMIT),
        name="attention",
    )(sinks, attn, attn)


def _deltanet_kernel(cur_ref, prev_ref, gate_ref, dz_ref, cw_ref, alog_ref, dtb_ref, ng_ref,
                     expand_ref, tril_ref, slow_ref, ones_ref, o_ref,
                     xext, q_s, k_s, v_s, b_s, g_s, o_s, state):
    i = pl.program_id(1)
    rows = cur_ref.shape[1]
    ck = DN_CHUNK

    @pl.when(i == 0)
    def _():
        state[...] = jnp.zeros_like(state)

    xext[0:DN_HALO, :] = jnp.where(i > 0, prev_ref[0], 0.0)
    xext[DN_HALO:DN_HALO + rows, :] = cur_ref[0]
    acc = cw_ref[0:1, :] * xext[pl.ds(DN_HALO - DN_CONV + 1, rows), :]
    for t in range(1, DN_CONV):
        acc = acc + cw_ref[t:t + 1, :] * xext[pl.ds(DN_HALO - DN_CONV + 1 + t, rows), :]
    qkv = _silu(acc)
    ones_blk = ones_ref[...]

    def l2n(t):
        ss = _dot((t * t).astype(BF16), ones_blk)
        return t * lax.rsqrt(ss + EPS)

    q_s[...] = l2n(qkv[:, 0:DN_W]) * (DN_D ** -0.5)
    k_s[...] = l2n(qkv[:, DN_W:2 * DN_W])
    v_s[...] = qkv[:, 2 * DN_W:3 * DN_W]
    ge = _dot(jnp.concatenate(_split3(gate_ref[0]), axis=1), expand_ref[...])
    b_s[...] = _sigmoid(ge[:, 0:DN_W])
    da = ge[:, DN_W:2 * DN_W] + dtb_ref[...]
    softplus = jnp.maximum(da, 0.0) + jnp.log1p(jnp.exp(-jnp.abs(da)))
    g_s[...] = -jnp.exp(alog_ref[...]) * softplus

    tril = tril_ref[...]
    slow = slow_ref[...]
    ri = lax.broadcasted_iota(jnp.int32, (ck, DN_W), 0)
    ci = lax.broadcasted_iota(jnp.int32, (ck, DN_W), 1) % ck
    incl = ri >= ci

    pw = 2 * DN_D
    n_pair = DN_W // pw
    r2 = lax.broadcasted_iota(jnp.int32, (ck, pw), 0)
    l2 = lax.broadcasted_iota(jnp.int32, (ck, pw), 1)
    c2 = l2 % ck
    strict = r2 > c2
    eye = jnp.where(r2 == c2, 1.0, 0.0)
    first = l2 < DN_D
    masks = []
    blk = 1
    while blk < ck:
        masks.append((r2 // (2 * blk) == c2 // (2 * blk)) & ((r2 // blk) % 2 == 1) & ((c2 // blk) % 2 == 0))
        blk *= 2
    same_head = (lax.broadcasted_iota(jnp.int32, (pw, pw), 0) // DN_D) == (lax.broadcasted_iota(jnp.int32, (pw, pw), 1) // DN_D)

    def blockdiag(t):
        z = jnp.zeros_like(t)
        return jnp.concatenate([jnp.where(first, t, z), jnp.where(first, z, t)], axis=0)

    nc = rows // ck
    slabs = [slice(p * pw, (p + 1) * pw) for p in range(n_pair)]
    kbeta16, k16, q16, dec, vbeta, kbg, qg, kg, egl = ([] for _ in range(9))
    for c in range(nc):
        r0 = c * ck
        q = q_s[r0:r0 + ck, :]
        k = k_s[r0:r0 + ck, :]
        beta = b_s[r0:r0 + ck, :]
        g = g_s[r0:r0 + ck, :]
        cums = _dot(tril, jnp.concatenate([jnp.concatenate(_split3(g), axis=0),
                                           jnp.concatenate(_split3(g * slow), axis=0)], axis=1))
        gc = cums[:, 0:DN_W]
        gdiff = cums[:, DN_W:]
        decay = jnp.exp(jnp.where(incl, gdiff, -jnp.inf))
        egc = jnp.exp(gc)
        glast = gc[ck - 1:ck, :]
        kbeta = k * beta
        per_slab = ((kbeta16, kbeta.astype(BF16)), (k16, k.astype(BF16)), (q16, q.astype(BF16)), (dec, decay),
                    (vbeta, (v_s[r0:r0 + ck, :] * beta).astype(BF16)), (kbg, (kbeta * egc).astype(BF16)),
                    (qg, q * egc), (kg, (k * jnp.exp(glast - gc)).astype(BF16)), (egl, jnp.exp(glast)))
        for dst, full in per_slab:
            dst.extend(full[:, sl] for sl in slabs)
    inst = range(nc * n_pair)
    bdk = [blockdiag(k16[n]) for n in inst]
    lower = [jnp.where(strict, _dot_nt(kbeta16[n], bdk[n]) * dec[n], 0.0) for n in inst]
    a_intra = [(_dot_nt(q16[n], bdk[n]) * dec[n]).astype(BF16) for n in inst]
    lower16 = [t.astype(BF16) for t in lower]
    tinv = [eye - jnp.where(masks[0], lower[n], 0.0) for n in inst]
    for m in masks[1:]:
        t16 = [t.astype(BF16) for t in tinv]
        x16 = [_dot(jnp.where(m, lower16[n], jnp.zeros_like(lower16[n])), blockdiag(t16[n])).astype(BF16) for n in inst]
        tinv = [tinv[n] - _dot(t16[n], blockdiag(x16[n])) for n in inst]
    t16 = [t.astype(BF16) for t in tinv]
    w16 = [_dot(t16[n], blockdiag(kbg[n])).astype(BF16) for n in inst]
    u16 = [_dot(t16[n], blockdiag(vbeta[n])).astype(BF16) for n in inst]
    kw16 = [jnp.where(same_head, _dot_tn(kg[n], w16[n]), 0.0).astype(BF16) for n in inst]
    ku = [jnp.where(same_head, _dot_tn(kg[n], u16[n]), 0.0) for n in inst]
    qeff16 = [(qg[n] - _dot(a_intra[n], blockdiag(w16[n]))).astype(BF16) for n in inst]
    au = [_dot(a_intra[n], blockdiag(u16[n])) for n in inst]
    st = [state[p] for p in range(n_pair)]
    for c in range(nc):
        outs = []
        for p in range(n_pair):
            n = c * n_pair + p
            s16 = st[p].astype(BF16)
            outs.append(_dot(qeff16[n], s16) + au[n])
            st[p] = st[p] * egl[n] - _dot(kw16[n], s16) + ku[n]
        o_s[c * ck:(c + 1) * ck, :] = jnp.concatenate(outs, axis=1)
    for p in range(n_pair):
        state[p] = st[p]
    o = o_s[...]
    ms = _dot((o * o).astype(BF16), ones_blk) * (1.0 / DN_D)
    y = o * lax.rsqrt(ms + EPS) * ng_ref[...]
    o_ref[0] = (y * _silu(dz_ref[0])).astype(BF16)


def _deltanet(dnqkv, gate, dz, conv_w, a_log, dt_bias, norm_g):
    b, s, _ = dnqkv.shape
    rows = R_DN
    ck = DN_CHUNK
    lane_head = np.arange(2 * DN_W) // DN_D
    expand = np.tile((np.arange(LANES)[:, None] == lane_head[None, :]).astype(np.float32), (3, 1))
    tril = np.tile(np.tril(np.ones((ck, ck), np.float32)), (1, 3))
    slow = (np.arange(ck)[:, None] > (np.arange(DN_W)[None, :] % ck)).astype(np.float32)
    ones_blk = (np.arange(DN_W)[:, None] // DN_D == np.arange(DN_W)[None, :] // DN_D).astype(np.float32)
    rep = lambda t: jnp.repeat(t.astype(F32), DN_D).reshape(1, DN_W)
    const = lambda shape: pl.BlockSpec(shape, lambda bi, i: (0,) * len(shape))
    row = lambda width: pl.BlockSpec((1, rows, width), lambda bi, i: (bi, i, 0))
    per = rows // DN_HALO
    return pl.pallas_call(
        _deltanet_kernel,
        grid=(b, s // rows),
        in_specs=[
            row(3 * DN_W),
            pl.BlockSpec((1, DN_HALO, 3 * DN_W), lambda bi, i: (bi, jnp.maximum(i * per - 1, 0), 0)),
            row(LANES), row(DN_W),
            const((DN_CONV, 3 * DN_W)), const((1, DN_W)), const((1, DN_W)), const((1, DN_W)),
            const((3 * LANES, 2 * DN_W)), const((ck, 3 * ck)), const((ck, DN_W)), const((DN_W, DN_W)),
        ],
        out_specs=row(DN_W),
        out_shape=jax.ShapeDtypeStruct((b, s, DN_W), BF16),
        scratch_shapes=[
            pltpu.VMEM((rows + DN_HALO, 3 * DN_W), F32),
            pltpu.VMEM((rows, DN_W), F32), pltpu.VMEM((rows, DN_W), F32), pltpu.VMEM((rows, DN_W), F32),
            pltpu.VMEM((rows, DN_W), F32), pltpu.VMEM((rows, DN_W), F32), pltpu.VMEM((rows, DN_W), F32),
            pltpu.VMEM((DN_W // (2 * DN_D), 2 * DN_D, 2 * DN_D), F32),
        ],
        compiler_params=pltpu.CompilerParams(vmem_limit_bytes=VMEM_LIMIT),
        name="deltanet",
    )(dnqkv, dnqkv, gate, dz, conv_w, rep(a_log), rep(dt_bias), jnp.tile(norm_g.astype(F32), DN_H).reshape(1, DN_W),
      jnp.asarray(expand, dtype=BF16), jnp.asarray(tril, dtype=BF16), jnp.asarray(slow), jnp.asarray(ones_blk, dtype=BF16))


def _conformer_kernel(cur_ref, prev_ref, w_ref, b_ref, lg_ref, lb_ref, o_ref, u_s, sh_s):
    i = pl.program_id(1)
    rows = cur_ref.shape[1]
    prev = jnp.where(i > 0, prev_ref[0], 0.0)
    u_s[0:CV_HALO, :] = prev[:, 0:CV_C] * _sigmoid(prev[:, CV_C:])
    cur = cur_ref[0]
    u_s[CV_HALO:CV_HALO + rows, :] = cur[:, 0:CV_C] * _sigmoid(cur[:, CV_C:])
    span = rows + CV_HALO - SUBLANES
    for ph in range(1, SUBLANES):
        sh_s[ph - 1, 0:span, :] = u_s[pl.ds(ph, span), :]
    base = CV_HALO - CV_K + 1
    acc = b_ref[...]
    for t in range(CV_K):
        blk, ph = divmod(base + t, SUBLANES)
        src = u_s if ph == 0 else sh_s.at[ph - 1]
        acc = acc + w_ref[t:t + 1, :] * src[blk * SUBLANES:blk * SUBLANES + rows, :]
    mu = jnp.mean(acc, axis=-1, keepdims=True)
    xc = acc - mu
    y = xc * lax.rsqrt(jnp.mean(xc * xc, axis=-1, keepdims=True) + EPS)
    o_ref[0] = _silu(y * lg_ref[...] + lb_ref[...]).astype(BF16)


def _conformer(cv, w, bias, ln_g, ln_b):
    b, s, _ = cv.shape
    rows = R_CV
    per = rows // CV_HALO
    const = lambda shape: pl.BlockSpec(shape, lambda bi, i: (0,) * len(shape))
    return pl.pallas_call(
        _conformer_kernel,
        grid=(b, s // rows),
        in_specs=[
            pl.BlockSpec((1, rows, 2 * CV_C), lambda bi, i: (bi, i, 0)),
            pl.BlockSpec((1, CV_HALO, 2 * CV_C), lambda bi, i: (bi, jnp.maximum(i * per - 1, 0), 0)),
            const((CV_K, CV_C)), const((1, CV_C)), const((1, CV_C)), const((1, CV_C)),
        ],
        out_specs=pl.BlockSpec((1, rows, CV_C), lambda bi, i: (bi, i, 0)),
        out_shape=jax.ShapeDtypeStruct((b, s, CV_C), BF16),
        scratch_shapes=[pltpu.VMEM((rows + CV_HALO, CV_C), F32),
                        pltpu.VMEM((SUBLANES - 1, rows + CV_HALO - SUBLANES, CV_C), F32)],
        compiler_params=pltpu.CompilerParams(vmem_limit_bytes=VMEM_LIMIT),
        name="conformer",
    )(cv, cv, w, bias.reshape(1, CV_C), ln_g.reshape(1, CV_C), ln_b.reshape(1, CV_C))


def _mix_residual(x_ref, ya_ref, yd_ref, yc_ref, wa_ref, wd_ref, wc_ref, gt1_ref):
    y = _dot(ya_ref[...], wa_ref[...]) + _dot(yd_ref[...], wd_ref[...]) + _dot(yc_ref[...], wc_ref[...])
    return x_ref[...] + gt1_ref[0] * y


def _finish(x1, acc, gt2_ref, fg_ref, o_ref, final_norm):
    x2 = x1 + gt2_ref[0] * acc
    if final_norm:
        x2 = x2 * lax.rsqrt(jnp.mean(x2 * x2, axis=-1, keepdims=True) + EPS) * fg_ref[...]
    o_ref[...] = x2


def _dense_ffn_kernel(x_ref, ya_ref, yd_ref, yc_ref, wa_ref, wd_ref, wc_ref, gt1_ref, g_ref, sc_ref, sh_ref, gt2_ref,
                      fg_ref, wg_ref, wu_ref, wdn_ref, o_ref, *, final_norm):
    x1 = _mix_residual(x_ref, ya_ref, yd_ref, yc_ref, wa_ref, wd_ref, wc_ref, gt1_ref)
    h = _rms_modulate(x1, g_ref[...], sc_ref[0], sh_ref[0]).astype(BF16)
    ff = wg_ref.shape[1]
    acc = None
    for c0 in range(0, ff, FF_PIECE):
        c1 = min(c0 + FF_PIECE, ff)
        a = (_silu(_dot(h, wg_ref[:, c0:c1])) * _dot(h, wu_ref[:, c0:c1])).astype(BF16)
        part = _dot(a, wdn_ref[c0:c1, :])
        acc = part if acc is None else acc + part
    _finish(x1, acc, gt2_ref, fg_ref, o_ref, final_norm)


def _route_kernel(x_ref, ya_ref, yd_ref, yc_ref, wa_ref, wd_ref, wc_ref, gt1_ref, g_ref, sc_ref, sh_ref,
                  rwt_ref, rb_ref, upper_ref, x1_ref, h_ref, comb_ref, pos_ref, cnt_ref):
    subs = [slice(n * MOE_SUB, (n + 1) * MOE_SUB) for n in range(x_ref.shape[0] // MOE_SUB)]
    wa, wd, wc = wa_ref[...], wd_ref[...], wc_ref[...]
    y = [_dot(ya_ref[sl, :], wa) + _dot(yd_ref[sl, :], wd) + _dot(yc_ref[sl, :], wc) for sl in subs]
    x1 = [x_ref[sl, :] + gt1_ref[0] * y_ for sl, y_ in zip(subs, y)]
    h = [_rms_modulate(x_, g_ref[...], sc_ref[0], sh_ref[0]) for x_ in x1]
    for sl, x_, h_ in zip(subs, x1, h):
        x1_ref[sl, :] = x_
        h_ref[sl, :] = h_.astype(BF16)
    w_hi, w_mid, _ = _split3(rwt_ref[...])
    w_cat = jnp.concatenate([w_hi, w_hi, w_mid], axis=1)
    pieces = [_split3(h_) for h_ in h]
    logits = [_dot_nt(w_cat, jnp.concatenate([hi, mid, hi], axis=1)) + rb_ref[...] for hi, mid, _ in pieces]
    row = lax.broadcasted_iota(jnp.int32, logits[0].shape, 0)
    lane = lax.broadcasted_iota(jnp.int32, (N_EXPERTS, LANES), 1)
    cnt = jnp.zeros((N_EXPERTS, LANES), F32)
    for n, (sl, lg) in enumerate(zip(subs, logits)):
        m1 = jnp.max(lg, axis=0, keepdims=True)
        i1 = jnp.min(jnp.where(lg == m1, row, N_EXPERTS), axis=0, keepdims=True)
        rest = jnp.where(row == i1, -jnp.inf, lg)
        m2 = jnp.max(rest, axis=0, keepdims=True)
        i2 = jnp.min(jnp.where(rest == m2, row, N_EXPERTS), axis=0, keepdims=True)
        e2 = jnp.exp(m2 - m1)
        comb_ref[:, sl] = jnp.where(row == i1, 1.0 / (1.0 + e2), 0.0) + jnp.where(row == i2, e2 / (1.0 + e2), 0.0)
        sel = jnp.where((row == i1) | (row == i2), 1.0, 0.0)
        rank = _dot(sel.astype(BF16), upper_ref[...])
        pos_ref[:, sl] = jnp.where(sel > 0.0, rank, -1.0)
        cnt = cnt + jnp.where(lane == n, jnp.sum(sel, axis=1, keepdims=True), 0.0)
    cnt_ref[0] = cnt.astype(jnp.int32)


def _route(x, ys, w_out, mods, g, rwt, rb, s):
    t, d = x.shape
    tm = TM_FFN
    gt1, sc, sh, _ = mods
    tok = lambda width: pl.BlockSpec((tm, width), lambda i: (i, 0))
    const = lambda shape: pl.BlockSpec(shape, lambda i: (0,) * len(shape))
    vec = pl.BlockSpec((1, 1, d), lambda i: ((i * tm) // s, 0, 0))
    col = pl.BlockSpec((N_EXPERTS, tm), lambda i: (0, i))
    upper = np.triu(np.ones((MOE_SUB, MOE_SUB), np.float32), 1)
    return pl.pallas_call(
        _route_kernel,
        grid=(t // tm,),
        in_specs=[tok(d), tok(ATTN_QW), tok(DN_W), tok(CV_C), const((ATTN_QW, d)), const((DN_W, d)), const((CV_C, d)),
                  vec, const((1, d)), vec, vec, const((N_EXPERTS, d)), const((N_EXPERTS, 1)), const((MOE_SUB, MOE_SUB))],
        out_specs=[tok(d), tok(d), col, col, pl.BlockSpec((1, N_EXPERTS, LANES), lambda i: (i, 0, 0))],
        out_shape=[jax.ShapeDtypeStruct((t, d), F32), jax.ShapeDtypeStruct((t, d), BF16),
                   jax.ShapeDtypeStruct((N_EXPERTS, t), F32), jax.ShapeDtypeStruct((N_EXPERTS, t), F32),
                   jax.ShapeDtypeStruct((t // tm, N_EXPERTS, LANES), jnp.int32)],
        compiler_params=pltpu.CompilerParams(vmem_limit_bytes=VMEM_LIMIT),
        name="route",
    )(x, *ys, *w_out, gt1, g, sc, sh, rwt, rb, jnp.asarray(upper, dtype=BF16))


def _moe_sparse_kernel(cnt_ref, off_ref, tot_ref, base_ref, h_ref, x1_ref, comb_ref, pos_ref, gt2_ref, fg_ref,
                       wg_hbm, wu_hbm, wdn_hbm, o_ref, xg_s, wrow_s, yw_s, w_s, w_sem, *, final_norm):
    i = pl.program_id(0)
    e = pl.program_id(1)
    n_e = pl.num_programs(1)
    n_sub = h_ref.shape[0] // MOE_SUB
    d = h_ref.shape[1]
    win_row = lax.broadcasted_iota(jnp.int32, (MOE_WIN, MOE_SUB), 0)

    step = i * n_e + e
    n_steps = pl.num_programs(0) * n_e

    def weight_copies(expert, slot):
        return [pltpu.make_async_copy(src.at[expert], w_s.at[slot, k], w_sem.at[slot, k])
                for k, src in enumerate((wg_hbm, wu_hbm, wdn_hbm))]

    @pl.when(step == 0)
    def _():
        for ahead in range(W_SLOTS - 1):
            for cp in weight_copies(ahead % N_EXPERTS, ahead):
                cp.start()

    @pl.when(step + (W_SLOTS - 1) < n_steps)
    def _():
        for cp in weight_copies(lax.rem(e + (W_SLOTS - 1), n_e), lax.rem(step + (W_SLOTS - 1), W_SLOTS)):
            cp.start()

    slot = lax.rem(step, W_SLOTS)
    for cp in weight_copies(e, slot):
        cp.wait()
    wg_ref, wu_ref, wdn_ref = (w_s.at[slot, k] for k in range(3))

    def gather(sub, p):
        k = (i * n_sub + sub) * N_EXPERTS + e
        lanes = slice(sub * MOE_SUB, (sub + 1) * MOE_SUB)
        pos = pos_ref[pl.ds(e, 1), lanes].astype(jnp.int32)
        sel = win_row + p * MOE_WIN == pos
        r0 = pl.multiple_of(off_ref[k] + p * MOE_WIN, 16)
        rows = _dot(jnp.where(sel, 1.0, 0.0).astype(BF16), h_ref[lanes, :])
        xg_s[pl.ds(r0, MOE_WIN), :] = rows.astype(BF16)
        w = jnp.sum(jnp.where(sel, comb_ref[pl.ds(e, 1), lanes], 0.0), axis=1, keepdims=True)
        wrow_s[pl.ds(r0, MOE_WIN), :] = jnp.broadcast_to(w, (MOE_WIN, LANES))

    most = cnt_ref[i * n_sub * N_EXPERTS + e]
    for sub in range(1, n_sub):
        most = jnp.maximum(most, cnt_ref[(i * n_sub + sub) * N_EXPERTS + e])
    for p in range(MOE_SUB // MOE_WIN - 1, 0, -1):
        @pl.when(most > p * MOE_WIN)
        def _():
            for sub in range(n_sub):
                @pl.when(cnt_ref[(i * n_sub + sub) * N_EXPERTS + e] > p * MOE_WIN)
                def _():
                    gather(sub, p)
    for sub in range(n_sub):
        gather(sub, 0)

    total = pl.multiple_of(tot_ref[i * N_EXPERTS + e], 16)
    base = pl.multiple_of(base_ref[i * N_EXPERTS + e], 16)
    xg_s[pl.ds(total, MOE_TAIL), :] = jnp.zeros((MOE_TAIL, d), BF16)
    wrow_s[pl.ds(total, MOE_TAIL), :] = jnp.zeros((MOE_TAIL, LANES), F32)

    def ffn(r0, rows):
        xc = xg_s[pl.ds(r0, rows), :]
        a = (_silu(_dot(xc, wg_ref[...])) * _dot(xc, wu_ref[...])).astype(BF16)
        y = _dot(a, wdn_ref[...]) * jnp.tile(wrow_s[pl.ds(r0, rows), :], (1, d // LANES))
        yw_s[pl.ds(base + r0, rows), :] = y.astype(BF16)

    n_full = total // MOE_CHUNK
    rem = total - n_full * MOE_CHUNK
    n_chunks = n_full + jnp.where(rem > MOE_CHUNK // 2, 1, 0)
    ffn(0, MOE_CHUNK)

    def full_chunk(c, carry):
        ffn(pl.multiple_of(c * MOE_CHUNK, MOE_CHUNK), MOE_CHUNK)
        return carry

    lax.fori_loop(1, n_chunks, full_chunk, 0)
    half = (rem > 0) & (rem <= MOE_CHUNK // 2) & (n_full > 0)

    @pl.when(half)
    def _():
        ffn(pl.multiple_of(n_full * MOE_CHUNK, MOE_CHUNK), MOE_CHUNK // 2)

    covered = jnp.maximum(n_chunks, 1) * MOE_CHUNK + jnp.where(half, MOE_CHUNK // 2, 0)
    yw_s[pl.ds(pl.multiple_of(base + covered, 16), MOE_TAIL), :] = jnp.zeros((MOE_TAIL, d), BF16)

    @pl.when(e == pl.num_programs(1) - 1)
    def _():
        for sub in range(n_sub):
            lanes = slice(sub * MOE_SUB, (sub + 1) * MOE_SUB)
            first = (i * n_sub + sub) * N_EXPERTS

            def windows(p):
                sels, rows = [], []
                for ex in range(N_EXPERTS):
                    r0 = pl.multiple_of(base_ref[i * N_EXPERTS + ex] + off_ref[first + ex] + p * MOE_WIN, 16)
                    pos = pos_ref[ex:ex + 1, lanes].astype(jnp.int32)
                    sels.append(jnp.where(win_row + p * MOE_WIN == pos, 1.0, 0.0).astype(BF16))
                    rows.append(yw_s[pl.ds(r0, MOE_WIN), :])
                return _dot_tn(jnp.concatenate(sels, axis=0), jnp.concatenate(rows, axis=0))

            o_ref[lanes, :] = windows(0)
            most = cnt_ref[first]
            for ex in range(1, N_EXPERTS):
                most = jnp.maximum(most, cnt_ref[first + ex])
            for p in range(1, MOE_SUB // MOE_WIN):
                @pl.when(most > p * MOE_WIN)
                def _():
                    o_ref[lanes, :] += windows(p)
        _finish(x1_ref[...], o_ref[...], gt2_ref, fg_ref, o_ref, final_norm)


def _dense_ffn(x, ys, w_out, mods, g, final_g, wg, wu, wdn, s, final_norm):
    t, d = x.shape
    tm = TM_FFN
    ff = wg.shape[1]
    gt1, sc, sh, gt2 = mods
    tok = lambda width: pl.BlockSpec((tm, width), lambda i: (i, 0))
    const = lambda shape: pl.BlockSpec(shape, lambda i: (0,) * len(shape), pipeline_mode=pl.Buffered(1))
    vec = pl.BlockSpec((1, 1, d), lambda i: ((i * tm) // s, 0, 0))
    return pl.pallas_call(
        functools.partial(_dense_ffn_kernel, final_norm=final_norm),
        grid=(t // tm,),
        in_specs=[tok(d), tok(ATTN_QW), tok(DN_W), tok(CV_C), const((ATTN_QW, d)), const((DN_W, d)), const((CV_C, d)),
                  vec, const((1, d)), vec, vec, vec, const((1, d)),
                  const((d, ff)), const((d, ff)), const((ff, d))],
        out_specs=tok(d),
        out_shape=jax.ShapeDtypeStruct((t, d), F32),
        compiler_params=pltpu.CompilerParams(vmem_limit_bytes=VMEM_LIMIT),
        name="dense_ffn",
    )(x, *ys, *w_out, gt1, g, sc, sh, gt2, final_g, wg, wu, wdn)


def _moe_sparse(h, x1, comb, pos, cnt, gt2, final_g, wg, wu, wdn, s, final_norm):
    t, d = x1.shape
    ts = TS_MOE
    n_e, _, ffe = wg.shape
    n_sub = ts // MOE_SUB
    counts = cnt[:, :, :TM_FFN // MOE_SUB].transpose(0, 2, 1).reshape(t // ts, n_sub, n_e)
    padded = (counts + 15) // 16 * 16
    offs = jnp.cumsum(padded, axis=1) - padded
    tots = jnp.sum(padded, axis=1)
    bases = jnp.cumsum(tots, axis=1) - tots
    cap_one = -(-(ts + n_sub * 16 + MOE_TAIL) // 16) * 16
    cap_all = -(-(N_TOP * ts + n_e * n_sub * 16 + MOE_CHUNK + MOE_TAIL) // 16) * 16
    tok = lambda width: pl.BlockSpec((ts, width), lambda i, e, *_: (i, 0))
    col = pl.BlockSpec((n_e, ts), lambda i, e, *_: (0, i))
    hbm = pl.BlockSpec(memory_space=pl.ANY)
    assert ffe == d, "the weight ring holds the three expert matrices in one (3, d, ffe) slot"
    flat = lambda a: a.reshape(-1).astype(jnp.int32)
    grid_spec = pltpu.PrefetchScalarGridSpec(
        num_scalar_prefetch=4,
        grid=(t // ts, n_e),
        in_specs=[tok(d), tok(d), col, col,
                  pl.BlockSpec((1, 1, d), lambda i, e, *_: ((i * ts) // s, 0, 0)),
                  pl.BlockSpec((1, d), lambda i, e, *_: (0, 0)),
                  hbm, hbm, hbm],
        out_specs=tok(d),
        scratch_shapes=[pltpu.VMEM((cap_one, d), BF16), pltpu.VMEM((cap_one, LANES), F32),
                        pltpu.VMEM((cap_all, d), BF16),
                        pltpu.VMEM((W_SLOTS, 3, d, ffe), BF16), pltpu.SemaphoreType.DMA((W_SLOTS, 3))],
    )
    return pl.pallas_call(
        functools.partial(_moe_sparse_kernel, final_norm=final_norm),
        grid_spec=grid_spec,
        out_shape=jax.ShapeDtypeStruct((t, d), F32),
        compiler_params=pltpu.CompilerParams(vmem_limit_bytes=VMEM_LIMIT_MOE,
                                             dimension_semantics=("arbitrary", "arbitrary")),
        name="moe_sparse",
    )(flat(counts), flat(offs), flat(tots), flat(bases), h, x1, comb, pos, gt2, final_g, wg, wu, wdn)


def _rope_cos_sin(positions):
    inv_freq = ROPE_THETA ** (-jnp.arange(0, ROT_DIM, 2, dtype=F32) / ROT_DIM)
    ang = positions.astype(F32)[:, None, :] * inv_freq[None, :, None]
    return jnp.concatenate([jnp.cos(ang), jnp.sin(ang)], axis=1)


def _pack_w_in(w):
    d = w.shape[0]
    o1 = ATTN_W
    o2 = o1 + 3 * DN_W
    o3 = o2 + 2 * DN_H
    o4 = o3 + DN_W
    gates = jnp.zeros((d, LANES), w.dtype).at[:, :2 * DN_H].set(w[:, o2:o3])
    return jnp.concatenate([w[:, :o2], gates, w[:, o3:o4], w[:, o4:]], axis=1).astype(BF16)


def kernel(x, c, positions, ada_w, ada_b, norm_mix_g, norm_ffn_g, w_in, attn_sinks, dn_conv_w, dn_a_log, dn_dt_bias, dn_norm_g, cv_dw_w, cv_dw_b, cv_ln_g, cv_ln_b, w_out, ffn_w_gate, ffn_w_up, ffn_w_down, router_w, router_b, moe_w_gate, moe_w_up, moe_w_down, final_norm_g):
    b, s, d = x.shape
    depth = w_in.shape[0]
    mod = _adaln(c, ada_w, ada_b)
    cos_sin = _rope_cos_sin(positions)
    final_g = final_norm_g.reshape(1, d)
    vec = lambda t: t.reshape(b, 1, d)
    for l in range(depth):
        sh1, sc1, gt1 = (vec(t) for t in jnp.split(mod[l, 0], 3, axis=-1))
        sh2, sc2, gt2 = (vec(t) for t in jnp.split(mod[l, 1], 3, axis=-1))
        attn, dnqkv, gate, dz, cv = _inproj(x, norm_mix_g[l].reshape(1, d), sc1, sh1, cos_sin, _pack_w_in(w_in[l]))
        y_attn = _attention(attn, attn_sinks[l].astype(F32))
        y_dn = _deltanet(dnqkv, gate, dz, dn_conv_w[l], dn_a_log[l], dn_dt_bias[l], dn_norm_g[l])
        y_cv = _conformer(cv, cv_dw_w[l], cv_dw_b[l], cv_ln_g[l], cv_ln_b[l])
        t = b * s
        ys = (y_attn.reshape(t, ATTN_QW), y_dn.reshape(t, DN_W), y_cv.reshape(t, CV_C))
        wo = w_out[l].astype(BF16)
        wos = (wo[:ATTN_QW], wo[ATTN_QW:ATTN_QW + DN_W], wo[ATTN_QW + DN_W:])
        mods = (gt1, sc2, sh2, gt2)
        g2 = norm_ffn_g[l].reshape(1, d)
        last = l == depth - 1
        j = l // 2
        if l % 2 == 0:
            x2 = _dense_ffn(x.reshape(t, d), ys, wos, mods, g2, final_g, ffn_w_gate[j].astype(BF16),
                            ffn_w_up[j].astype(BF16), ffn_w_down[j].astype(BF16), s, last)
        else:
            x1, h2, comb, pos, cnt = _route(x.reshape(t, d), ys, wos, mods, g2, router_w[j].astype(F32).T,
                                            router_b[j].astype(F32).reshape(N_EXPERTS, 1), s)
            x2 = _moe_sparse(h2, x1, comb, pos, cnt, gt2, final_g, moe_w_gate[j].astype(BF16),
                             moe_w_up[j].astype(BF16), moe_w_down[j].astype(BF16), s, last)
        x = x2.reshape(b, s, d)
    return x
```

```python
import functools

import numpy as np
import jax
import jax.numpy as jnp
from jax import lax
from jax.experimental import pallas as pl
from jax.experimental.pallas import tpu as pltpu

F32 = jnp.float32
BF16 = jnp.bfloat16
HIGHEST = lax.Precision.HIGHEST

HEAD_DIM = 64
ATTN_HQ = 8
ATTN_HKV = 2
ATTN_GROUP = ATTN_HQ // ATTN_HKV
ATTN_BLOCK = 128
ROT_DIM = HEAD_DIM // 4
ROPE_THETA = 500000.0
DN_H = 4
DN_D = 64
DN_CONV = 4
DN_CHUNK = 64
CV_C = 256
CV_K = 31
ATTN_QW = ATTN_HQ * HEAD_DIM
ATTN_KVW = ATTN_HKV * HEAD_DIM
ATTN_W = ATTN_QW + 2 * ATTN_KVW
DN_W = DN_H * DN_D
N_EXPERTS = 8
EPS = 1e-6
LANES = 128
SUBLANES = 8
VMEM_LIMIT = 48 * 1024 * 1024
VMEM_LIMIT_MOE = 56 * 1024 * 1024

TM_PROJ = 512
TQ_ATTN = 512
ATTN_GROUP_UNITS = 8
R_DN = 512
R_CV = 512
TM_FFN = 512
FF_PIECE = 1536
CV_HALO = 32
DN_HALO = 8
TS_MOE = 1024
MOE_SUB = 256
MOE_WIN = 128
MOE_CHUNK = 256
MOE_TAIL = 384
N_TOP = 2
W_SLOTS = 3


def _dot(a, b, precision=None):
    return jnp.dot(a, b, preferred_element_type=F32, precision=precision)


def _dot_nt(a, b):
    return lax.dot_general(a, b, (((1,), (1,)), ((), ())), preferred_element_type=F32)


def _dot_tn(a, b):
    return lax.dot_general(a, b, (((0,), (0,)), ((), ())), preferred_element_type=F32)


def _split3(x):
    hi = x.astype(BF16)
    rest = x - hi.astype(F32)
    mid = rest.astype(BF16)
    return hi, mid, (rest - mid.astype(F32)).astype(BF16)


def _sigmoid(x):
    return 1.0 / (1.0 + jnp.exp(-x))


def _silu(x):
    return x * _sigmoid(x)


def _rms_modulate(x, g, sc, sh):
    y = x * lax.rsqrt(jnp.mean(x * x, axis=-1, keepdims=True) + EPS)
    return (y * g) * (1.0 + sc) + sh


def _adaln_kernel(c_ref, w_ref, b_ref, o_ref):
    o_ref[0] = _dot(_silu(c_ref[...]), w_ref[0], HIGHEST) + b_ref[0]


def _adaln(c, ada_w, ada_b):
    depth, _, d, d3 = ada_w.shape
    b = c.shape[0]
    rows = 8
    cp = jnp.zeros((rows, d), F32).at[:b].set(c)
    w = ada_w.reshape(depth * 2, d, d3)
    bias = ada_b.reshape(depth * 2, 1, d3)
    tn = 1024
    out = pl.pallas_call(
        _adaln_kernel,
        grid=(depth * 2, d3 // tn),
        in_specs=[
            pl.BlockSpec((rows, d), lambda i, j: (0, 0)),
            pl.BlockSpec((1, d, tn), lambda i, j: (i, 0, j)),
            pl.BlockSpec((1, 1, tn), lambda i, j: (i, 0, j)),
        ],
        out_specs=pl.BlockSpec((1, rows, tn), lambda i, j: (i, 0, j)),
        out_shape=jax.ShapeDtypeStruct((depth * 2, rows, d3), F32),
        compiler_params=pltpu.CompilerParams(vmem_limit_bytes=VMEM_LIMIT),
        name="adaln",
    )(cp, w, bias)
    return out[:, :b].reshape(depth, 2, b, d3)


def _inproj_kernel(x_ref, g_ref, sc_ref, sh_ref, cs_ref, place_ref, ones_ref, w_ref,
                   attn_ref, dnqkv_ref, gate_ref, dz_ref, cv_ref):
    h = _rms_modulate(x_ref[0], g_ref[...], sc_ref[0], sh_ref[0]).astype(BF16)
    a = _dot(h, w_ref[:, 0:ATTN_W])
    tab = _dot_tn(jnp.concatenate(_split3(cs_ref[0]), axis=0), place_ref[...]) + ones_ref[...]
    rc, ra, rb = tab[:, 0:LANES], tab[:, LANES:2 * LANES], tab[:, 2 * LANES:3 * LANES]
    half = ROT_DIM // 2
    for j in range((ATTN_QW + ATTN_KVW) // LANES):
        t = a[:, j * LANES:(j + 1) * LANES]
        r = t * rc + pltpu.roll(t, LANES - half, 1) * ra + pltpu.roll(t, half, 1) * rb
        if j < ATTN_QW // LANES:
            r = r * (HEAD_DIM ** -0.5)
        attn_ref[0, :, j * LANES:(j + 1) * LANES] = r.astype(BF16)
    attn_ref[0, :, ATTN_QW + ATTN_KVW:ATTN_W] = a[:, ATTN_QW + ATTN_KVW:ATTN_W].astype(BF16)
    o = ATTN_W
    dnqkv_ref[0] = _dot(h, w_ref[:, o:o + 3 * DN_W])
    o += 3 * DN_W
    gate_ref[0] = _dot(h, w_ref[:, o:o + LANES])
    o += LANES
    dz_ref[0] = _dot(h, w_ref[:, o:o + DN_W])
    o += DN_W
    cv_ref[0] = _dot(h, w_ref[:, o:o + 2 * CV_C])


def _rope_placement():
    half = ROT_DIM // 2
    place = np.zeros((2 * half, 3 * LANES), np.float32)
    ones = np.zeros((1, 3 * LANES), np.float32)
    for lane in range(LANES):
        dim = lane % HEAD_DIM
        if dim < ROT_DIM:
            place[dim % half, lane] = 1.0
            if dim < half:
                place[half + dim, LANES + lane] = -1.0
            else:
                place[half + dim - half, 2 * LANES + lane] = 1.0
        else:
            ones[0, lane] = 1.0
    return np.tile(place, (3, 1)), ones


def _inproj(x, g, sc, sh, cos_sin, w):
    b, s, d = x.shape
    tm = TM_PROJ
    n = w.shape[1]
    row = lambda width: pl.BlockSpec((1, tm, width), lambda bi, i: (bi, i, 0))
    vec = pl.BlockSpec((1, 1, d), lambda bi, i: (bi, 0, 0))
    const = lambda shape: pl.BlockSpec(shape, lambda bi, i: (0,) * len(shape))
    widths = (ATTN_W, 3 * DN_W, LANES, DN_W, 2 * CV_C)
    dtypes = (BF16, F32, F32, F32, F32)
    place, ones = _rope_placement()
    return pl.pallas_call(
        _inproj_kernel,
        grid=(b, s // tm),
        in_specs=[row(d), const((1, d)), vec, vec,
                  pl.BlockSpec((1, ROT_DIM, tm), lambda bi, i: (bi, 0, i)), const(place.shape), const(ones.shape),
                  const((d, n))],
        out_specs=[row(wd) for wd in widths],
        out_shape=[jax.ShapeDtypeStruct((b, s, wd), dt) for wd, dt in zip(widths, dtypes)],
        compiler_params=pltpu.CompilerParams(vmem_limit_bytes=VMEM_LIMIT),
        name="inproj",
    )(x, g, sc, sh, cos_sin, jnp.asarray(place, dtype=BF16), jnp.asarray(ones), w)


def _attn_kernel(sink_ref, cur_ref, prev_ref, o_ref):
    i = pl.program_id(1)
    blk = ATTN_BLOCK
    cur = cur_ref[0]
    prev = prev_ref[0]
    kv_all = jnp.concatenate([prev[:, ATTN_QW:], cur[:, ATTN_QW:]], axis=0)
    band_w = 2 * blk
    qi = lax.broadcasted_iota(jnp.int32, (blk, 2 * band_w), 0)
    kj = lax.broadcasted_iota(jnp.int32, (blk, 2 * band_w), 1) % band_w
    diff = qi + blk - kj
    band = (diff >= 0) & (diff < blk)
    band_first = band & (kj >= jnp.where(i == 0, blk, 0))
    first_head = lax.broadcasted_iota(jnp.int32, (blk, LANES), 1) < HEAD_DIM
    zeros = jnp.zeros((band_w, HEAD_DIM), BF16)

    def blockdiag(t):
        return jnp.concatenate([jnp.concatenate([t, zeros], axis=1), jnp.concatenate([zeros, t], axis=1)], axis=0)

    units = [(r, j, pr) for r in range(cur.shape[0] // blk) for j in range(ATTN_HKV) for pr in range(ATTN_GROUP // 2)]

    def scores(r, j, pr):
        kb = kv_all[r * blk:(r + 2) * blk, j * HEAD_DIM:(j + 1) * HEAD_DIM]
        slab = (j * (ATTN_GROUP // 2) + pr) * LANES
        return _dot_nt(cur[r * blk:(r + 1) * blk, slab:slab + LANES], blockdiag(kb))

    lane_row = lax.broadcasted_iota(jnp.int32, (1, 2 * band_w), 1)
    not_row0 = lax.broadcasted_iota(jnp.int32, (band_w, HEAD_DIM), 0) > 0
    def value_rows(r, j):
        vb = kv_all[r * blk:(r + 2) * blk, ATTN_KVW + j * HEAD_DIM:ATTN_KVW + (j + 1) * HEAD_DIM]
        return blockdiag(jnp.where(not_row0, vb, jnp.zeros_like(vb)))

    for g0 in range(0, len(units), ATTN_GROUP_UNITS):
        group = units[g0:g0 + ATTN_GROUP_UNITS]
        raw = [scores(*u) for u in group]
        sc = []
        for (r, j, pr), s_raw in zip(group, raw):
            hq = j * ATTN_GROUP + 2 * pr
            fill = jnp.where(lane_row == 0, sink_ref[hq], jnp.where(lane_row == band_w, sink_ref[hq + 1], -jnp.inf))
            sc.append(jnp.where(band_first if r == 0 else band, s_raw, fill))
        halves = [[s_[:, t * band_w:(t + 1) * band_w] for t in range(2)] for s_ in sc]
        mx = [[jnp.max(h_, axis=-1, keepdims=True) for h_ in hs] for hs in halves]
        p = [[jnp.exp(h_ - m_) for h_, m_ in zip(hs, ms)] for hs, ms in zip(halves, mx)]
        rs = [[1.0 / jnp.sum(p_, axis=-1, keepdims=True) for p_ in ps] for ps in p]
        p16 = [jnp.concatenate([p_.astype(BF16) for p_ in ps], axis=1) for ps in p]
        outs = [_dot(p16[n], value_rows(r, j)) for n, (r, j, pr) in enumerate(group)]
        for n, (r, j, pr) in enumerate(group):
            o = outs[n] * jnp.where(first_head, rs[n][0], rs[n][1])
            slab = (j * (ATTN_GROUP // 2) + pr) * LANES
            o_ref[0, r * blk:(r + 1) * blk, slab:slab + LANES] = o.astype(BF16)


def _attention(attn, sinks):
    b, s, _ = attn.shape
    tq = TQ_ATTN
    per = tq // ATTN_BLOCK
    return pl.pallas_call(
        _attn_kernel,
        grid=(b, s // tq),
        in_specs=[
            pl.BlockSpec(memory_space=pltpu.SMEM),
            pl.BlockSpec((1, tq, ATTN_W), lambda bi, i: (bi, i, 0)),
            pl.BlockSpec((1, ATTN_BLOCK, ATTN_W), lambda bi, i: (bi, jnp.maximum(i * per - 1, 0), 0)),
        ],
        out_specs=pl.BlockSpec((1, tq, ATTN_QW), lambda bi, i: (bi, i, 0)),
        out_shape=jax.ShapeDtypeStruct((b, s, ATTN_QW), BF16),
        compiler_params=pltpu.CompilerParams(vmem_limit_bytes=VMEM_LIMIT),
        name="attention",
    )(sinks, attn, attn)


def _deltanet_kernel(cur_ref, prev_ref, gate_ref, dz_ref, cw_ref, alog_ref, dtb_ref, ng_ref,
                     expand_ref, tril_ref, slow_ref, ones_ref, o_ref,
                     xext, q_s, k_s, v_s, b_s, g_s, o_s, state):
    i = pl.program_id(0)
    n_seq, rows = cur_ref.shape[0], cur_ref.shape[1]
    ck = DN_CHUNK

    @pl.when(i == 0)
    def _():
        state[...] = jnp.zeros_like(state)

    ones_blk = ones_ref[...]

    def l2n(t):
        ss = _dot((t * t).astype(BF16), ones_blk)
        return t * lax.rsqrt(ss + EPS)

    for bi in range(n_seq):
        seq = slice(bi * rows, (bi + 1) * rows)
        xext[bi, 0:DN_HALO, :] = jnp.where(i > 0, prev_ref[bi], 0.0)
        xext[bi, DN_HALO:DN_HALO + rows, :] = cur_ref[bi]
        acc = cw_ref[0:1, :] * xext[bi, pl.ds(DN_HALO - DN_CONV + 1, rows), :]
        for t in range(1, DN_CONV):
            acc = acc + cw_ref[t:t + 1, :] * xext[bi, pl.ds(DN_HALO - DN_CONV + 1 + t, rows), :]
        qkv = _silu(acc)
        q_s[seq, :] = l2n(qkv[:, 0:DN_W]) * (DN_D ** -0.5)
        k_s[seq, :] = l2n(qkv[:, DN_W:2 * DN_W])
        v_s[seq, :] = qkv[:, 2 * DN_W:3 * DN_W]
        ge = _dot(jnp.concatenate(_split3(gate_ref[bi]), axis=1), expand_ref[...])
        b_s[seq, :] = _sigmoid(ge[:, 0:DN_W])
        da = ge[:, DN_W:2 * DN_W] + dtb_ref[...]
        softplus = jnp.maximum(da, 0.0) + jnp.log1p(jnp.exp(-jnp.abs(da)))
        g_s[seq, :] = -jnp.exp(alog_ref[...]) * softplus

    tril = tril_ref[...]
    slow = slow_ref[...]
    ri = lax.broadcasted_iota(jnp.int32, (ck, DN_W), 0)
    ci = lax.broadcasted_iota(jnp.int32, (ck, DN_W), 1) % ck
    incl = ri >= ci

    pw = 2 * DN_D
    n_pair = DN_W // pw
    r2 = lax.broadcasted_iota(jnp.int32, (ck, pw), 0)
    l2 = lax.broadcasted_iota(jnp.int32, (ck, pw), 1)
    c2 = l2 % ck
    strict = r2 > c2
    eye = jnp.where(r2 == c2, 1.0, 0.0)
    first = l2 < DN_D
    masks = []
    blk = 1
    while blk < ck:
        masks.append((r2 // (2 * blk) == c2 // (2 * blk)) & ((r2 // blk) % 2 == 1) & ((c2 // blk) % 2 == 0))
        blk *= 2
    same_head = (lax.broadcasted_iota(jnp.int32, (pw, pw), 0) // DN_D) == (lax.broadcasted_iota(jnp.int32, (pw, pw), 1) // DN_D)

    def blockdiag(t):
        z = jnp.zeros_like(t)
        return jnp.concatenate([jnp.where(first, t, z), jnp.where(first, z, t)], axis=0)

    nc = n_seq * rows // ck
    per_seq = rows // ck
    slabs = [slice(p * pw, (p + 1) * pw) for p in range(n_pair)]
    kbeta16, k16, q16, dec, vbeta, kbg, qg, kg, egl = ([] for _ in range(9))
    for c in range(nc):
        r0 = c * ck
        q = q_s[r0:r0 + ck, :]
        k = k_s[r0:r0 + ck, :]
        beta = b_s[r0:r0 + ck, :]
        g = g_s[r0:r0 + ck, :]
        cums = _dot(tril, jnp.concatenate([jnp.concatenate(_split3(g), axis=0),
                                           jnp.concatenate(_split3(g * slow), axis=0)], axis=1))
        gc = cums[:, 0:DN_W]
        gdiff = cums[:, DN_W:]
        decay = jnp.exp(jnp.where(incl, gdiff, -jnp.inf))
        egc = jnp.exp(gc)
        glast = gc[ck - 1:ck, :]
        kbeta = k * beta
        per_slab = ((kbeta16, kbeta.astype(BF16)), (k16, k.astype(BF16)), (q16, q.astype(BF16)), (dec, decay),
                    (vbeta, (v_s[r0:r0 + ck, :] * beta).astype(BF16)), (kbg, (kbeta * egc).astype(BF16)),
                    (qg, q * egc), (kg, (k * jnp.exp(glast - gc)).astype(BF16)), (egl, jnp.exp(glast)))
        for dst, full in per_slab:
            dst.extend(full[:, sl] for sl in slabs)
    inst = range(nc * n_pair)
    bdk = [blockdiag(k16[n]) for n in inst]
    lower = [jnp.where(strict, _dot_nt(kbeta16[n], bdk[n]) * dec[n], 0.0) for n in inst]
    a_intra = [(_dot_nt(q16[n], bdk[n]) * dec[n]).astype(BF16) for n in inst]
    lower16 = [t.astype(BF16) for t in lower]
    tinv = [eye - jnp.where(masks[0], lower[n], 0.0) for n in inst]
    for m in masks[1:]:
        t16 = [t.astype(BF16) for t in tinv]
        x16 = [_dot(jnp.where(m, lower16[n], jnp.zeros_like(lower16[n])), blockdiag(t16[n])).astype(BF16) for n in inst]
        tinv = [tinv[n] - _dot(t16[n], blockdiag(x16[n])) for n in inst]
    t16 = [t.astype(BF16) for t in tinv]
    w16 = [_dot(t16[n], blockdiag(kbg[n])).astype(BF16) for n in inst]
    u16 = [_dot(t16[n], blockdiag(vbeta[n])).astype(BF16) for n in inst]
    kw16 = [jnp.where(same_head, _dot_tn(kg[n], w16[n]), 0.0).astype(BF16) for n in inst]
    ku = [jnp.where(same_head, _dot_tn(kg[n], u16[n]), 0.0) for n in inst]
    qeff16 = [(qg[n] - _dot(a_intra[n], blockdiag(w16[n]))).astype(BF16) for n in inst]
    au = [_dot(a_intra[n], blockdiag(u16[n])) for n in inst]
    st = [[state[bi, p] for p in range(n_pair)] for bi in range(n_seq)]
    for c in range(per_seq):
        for bi in range(n_seq):
            outs = []
            for p in range(n_pair):
                n = (bi * per_seq + c) * n_pair + p
                s16 = st[bi][p].astype(BF16)
                outs.append(_dot(qeff16[n], s16) + au[n])
                st[bi][p] = st[bi][p] * egl[n] - _dot(kw16[n], s16) + ku[n]
            r0 = (bi * per_seq + c) * ck
            o_s[r0:r0 + ck, :] = jnp.concatenate(outs, axis=1)
    for bi in range(n_seq):
        for p in range(n_pair):
            state[bi, p] = st[bi][p]
    for bi in range(n_seq):
        o = o_s[bi * rows:(bi + 1) * rows, :]
        ms = _dot((o * o).astype(BF16), ones_blk) * (1.0 / DN_D)
        y = o * lax.rsqrt(ms + EPS) * ng_ref[...]
        o_ref[bi] = (y * _silu(dz_ref[bi])).astype(BF16)


def _deltanet(dnqkv, gate, dz, conv_w, a_log, dt_bias, norm_g):
    b, s, _ = dnqkv.shape
    rows = R_DN
    ck = DN_CHUNK
    lane_head = np.arange(2 * DN_W) // DN_D
    expand = np.tile((np.arange(LANES)[:, None] == lane_head[None, :]).astype(np.float32), (3, 1))
    tril = np.tile(np.tril(np.ones((ck, ck), np.float32)), (1, 3))
    slow = (np.arange(ck)[:, None] > (np.arange(DN_W)[None, :] % ck)).astype(np.float32)
    ones_blk = (np.arange(DN_W)[:, None] // DN_D == np.arange(DN_W)[None, :] // DN_D).astype(np.float32)
    rep = lambda t: jnp.repeat(t.astype(F32), DN_D).reshape(1, DN_W)
    const = lambda shape: pl.BlockSpec(shape, lambda i: (0,) * len(shape))
    row = lambda width: pl.BlockSpec((b, rows, width), lambda i: (0, i, 0))
    per = rows // DN_HALO
    return pl.pallas_call(
        _deltanet_kernel,
        grid=(s // rows,),
        in_specs=[
            row(3 * DN_W),
            pl.BlockSpec((b, DN_HALO, 3 * DN_W), lambda i: (0, jnp.maximum(i * per - 1, 0), 0)),
            row(LANES), row(DN_W),
            const((DN_CONV, 3 * DN_W)), const((1, DN_W)), const((1, DN_W)), const((1, DN_W)),
            const((3 * LANES, 2 * DN_W)), const((ck, 3 * ck)), const((ck, DN_W)), const((DN_W, DN_W)),
        ],
        out_specs=row(DN_W),
        out_shape=jax.ShapeDtypeStruct((b, s, DN_W), BF16),
        scratch_shapes=[
            pltpu.VMEM((b, rows + DN_HALO, 3 * DN_W), F32),
            *[pltpu.VMEM((b * rows, DN_W), F32) for _ in range(6)],
            pltpu.VMEM((b, DN_W // (2 * DN_D), 2 * DN_D, 2 * DN_D), F32),
        ],
        compiler_params=pltpu.CompilerParams(vmem_limit_bytes=VMEM_LIMIT, dimension_semantics=("arbitrary",)),
        name="deltanet",
    )(dnqkv, dnqkv, gate, dz, conv_w, rep(a_log), rep(dt_bias), jnp.tile(norm_g.astype(F32), DN_H).reshape(1, DN_W),
      jnp.asarray(expand, dtype=BF16), jnp.asarray(tril, dtype=BF16), jnp.asarray(slow), jnp.asarray(ones_blk, dtype=BF16))


def _conformer_kernel(cur_ref, prev_ref, w_ref, b_ref, lg_ref, lb_ref, o_ref, u_s, sh_s):
    i = pl.program_id(1)
    rows = cur_ref.shape[1]
    prev = jnp.where(i > 0, prev_ref[0], 0.0)
    u_s[0:CV_HALO, :] = prev[:, 0:CV_C] * _sigmoid(prev[:, CV_C:])
    cur = cur_ref[0]
    u_s[CV_HALO:CV_HALO + rows, :] = cur[:, 0:CV_C] * _sigmoid(cur[:, CV_C:])
    span = rows + CV_HALO - SUBLANES
    for ph in range(1, SUBLANES):
        sh_s[ph - 1, 0:span, :] = u_s[pl.ds(ph, span), :]
    base = CV_HALO - CV_K + 1
    acc = b_ref[...]
    for t in range(CV_K):
        blk, ph = divmod(base + t, SUBLANES)
        src = u_s if ph == 0 else sh_s.at[ph - 1]
        acc = acc + w_ref[t:t + 1, :] * src[blk * SUBLANES:blk * SUBLANES + rows, :]
    mu = jnp.mean(acc, axis=-1, keepdims=True)
    xc = acc - mu
    y = xc * lax.rsqrt(jnp.mean(xc * xc, axis=-1, keepdims=True) + EPS)
    o_ref[0] = _silu(y * lg_ref[...] + lb_ref[...]).astype(BF16)


def _conformer(cv, w, bias, ln_g, ln_b):
    b, s, _ = cv.shape
    rows = R_CV
    per = rows // CV_HALO
    const = lambda shape: pl.BlockSpec(shape, lambda bi, i: (0,) * len(shape))
    return pl.pallas_call(
        _conformer_kernel,
        grid=(b, s // rows),
        in_specs=[
            pl.BlockSpec((1, rows, 2 * CV_C), lambda bi, i: (bi, i, 0)),
            pl.BlockSpec((1, CV_HALO, 2 * CV_C), lambda bi, i: (bi, jnp.maximum(i * per - 1, 0), 0)),
            const((CV_K, CV_C)), const((1, CV_C)), const((1, CV_C)), const((1, CV_C)),
        ],
        out_specs=pl.BlockSpec((1, rows, CV_C), lambda bi, i: (bi, i, 0)),
        out_shape=jax.ShapeDtypeStruct((b, s, CV_C), BF16),
        scratch_shapes=[pltpu.VMEM((rows + CV_HALO, CV_C), F32),
                        pltpu.VMEM((SUBLANES - 1, rows + CV_HALO - SUBLANES, CV_C), F32)],
        compiler_params=pltpu.CompilerParams(vmem_limit_bytes=VMEM_LIMIT),
        name="conformer",
    )(cv, cv, w, bias.reshape(1, CV_C), ln_g.reshape(1, CV_C), ln_b.reshape(1, CV_C))


def _mix_residual(x_ref, ya_ref, yd_ref, yc_ref, wa_ref, wd_ref, wc_ref, gt1_ref):
    y = _dot(ya_ref[...], wa_ref[...]) + _dot(yd_ref[...], wd_ref[...]) + _dot(yc_ref[...], wc_ref[...])
    return x_ref[...] + gt1_ref[0] * y


def _finish(x1, acc, gt2_ref, fg_ref, o_ref, final_norm):
    x2 = x1 + gt2_ref[0] * acc
    if final_norm:
        x2 = x2 * lax.rsqrt(jnp.mean(x2 * x2, axis=-1, keepdims=True) + EPS) * fg_ref[...]
    o_ref[...] = x2


def _dense_ffn_kernel(x_ref, ya_ref, yd_ref, yc_ref, wa_ref, wd_ref, wc_ref, gt1_ref, g_ref, sc_ref, sh_ref, gt2_ref,
                      fg_ref, wg_ref, wu_ref, wdn_ref, o_ref, *, final_norm):
    x1 = _mix_residual(x_ref, ya_ref, yd_ref, yc_ref, wa_ref, wd_ref, wc_ref, gt1_ref)
    h = _rms_modulate(x1, g_ref[...], sc_ref[0], sh_ref[0]).astype(BF16)
    ff = wg_ref.shape[1]
    acc = None
    for c0 in range(0, ff, FF_PIECE):
        c1 = min(c0 + FF_PIECE, ff)
        a = (_silu(_dot(h, wg_ref[:, c0:c1])) * _dot(h, wu_ref[:, c0:c1])).astype(BF16)
        part = _dot(a, wdn_ref[c0:c1, :])
        acc = part if acc is None else acc + part
    _finish(x1, acc, gt2_ref, fg_ref, o_ref, final_norm)


def _route_kernel(x_ref, ya_ref, yd_ref, yc_ref, wa_ref, wd_ref, wc_ref, gt1_ref, g_ref, sc_ref, sh_ref,
                  rwt_ref, rb_ref, upper_ref, x1_ref, h_ref, comb_ref, pos_ref, cnt_ref):
    subs = [slice(n * MOE_SUB, (n + 1) * MOE_SUB) for n in range(x_ref.shape[0] // MOE_SUB)]
    wa, wd, wc = wa_ref[...], wd_ref[...], wc_ref[...]
    y = [_dot(ya_ref[sl, :], wa) + _dot(yd_ref[sl, :], wd) + _dot(yc_ref[sl, :], wc) for sl in subs]
    x1 = [x_ref[sl, :] + gt1_ref[0] * y_ for sl, y_ in zip(subs, y)]
    h = [_rms_modulate(x_, g_ref[...], sc_ref[0], sh_ref[0]) for x_ in x1]
    for sl, x_, h_ in zip(subs, x1, h):
        x1_ref[sl, :] = x_
        h_ref[sl, :] = h_.astype(BF16)
    w_hi, w_mid, _ = _split3(rwt_ref[...])
    w_cat = jnp.concatenate([w_hi, w_hi, w_mid], axis=1)
    pieces = [_split3(h_) for h_ in h]
    logits = [_dot_nt(w_cat, jnp.concatenate([hi, mid, hi], axis=1)) + rb_ref[...] for hi, mid, _ in pieces]
    row = lax.broadcasted_iota(jnp.int32, logits[0].shape, 0)
    lane = lax.broadcasted_iota(jnp.int32, (N_EXPERTS, LANES), 1)
    cnt = jnp.zeros((N_EXPERTS, LANES), F32)
    for n, (sl, lg) in enumerate(zip(subs, logits)):
        m1 = jnp.max(lg, axis=0, keepdims=True)
        i1 = jnp.min(jnp.where(lg == m1, row, N_EXPERTS), axis=0, keepdims=True)
        rest = jnp.where(row == i1, -jnp.inf, lg)
        m2 = jnp.max(rest, axis=0, keepdims=True)
        i2 = jnp.min(jnp.where(rest == m2, row, N_EXPERTS), axis=0, keepdims=True)
        e2 = jnp.exp(m2 - m1)
        comb_ref[:, sl] = jnp.where(row == i1, 1.0 / (1.0 + e2), 0.0) + jnp.where(row == i2, e2 / (1.0 + e2), 0.0)
        sel = jnp.where((row == i1) | (row == i2), 1.0, 0.0)
        rank = _dot(sel.astype(BF16), upper_ref[...])
        pos_ref[:, sl] = jnp.where(sel > 0.0, rank, -1.0)
        cnt = cnt + jnp.where(lane == n, jnp.sum(sel, axis=1, keepdims=True), 0.0)
    cnt_ref[0] = cnt.astype(jnp.int32)


def _route(x, ys, w_out, mods, g, rwt, rb, s):
    t, d = x.shape
    tm = TM_FFN
    gt1, sc, sh, _ = mods
    tok = lambda width: pl.BlockSpec((tm, width), lambda i: (i, 0))
    const = lambda shape: pl.BlockSpec(shape, lambda i: (0,) * len(shape))
    vec = pl.BlockSpec((1, 1, d), lambda i: ((i * tm) // s, 0, 0))
    col = pl.BlockSpec((N_EXPERTS, tm), lambda i: (0, i))
    upper = np.triu(np.ones((MOE_SUB, MOE_SUB), np.float32), 1)
    return pl.pallas_call(
        _route_kernel,
        grid=(t // tm,),
        in_specs=[tok(d), tok(ATTN_QW), tok(DN_W), tok(CV_C), const((ATTN_QW, d)), const((DN_W, d)), const((CV_C, d)),
                  vec, const((1, d)), vec, vec, const((N_EXPERTS, d)), const((N_EXPERTS, 1)), const((MOE_SUB, MOE_SUB))],
        out_specs=[tok(d), tok(d), col, col, pl.BlockSpec((1, N_EXPERTS, LANES), lambda i: (i, 0, 0))],
        out_shape=[jax.ShapeDtypeStruct((t, d), F32), jax.ShapeDtypeStruct((t, d), BF16),
                   jax.ShapeDtypeStruct((N_EXPERTS, t), F32), jax.ShapeDtypeStruct((N_EXPERTS, t), F32),
                   jax.ShapeDtypeStruct((t // tm, N_EXPERTS, LANES), jnp.int32)],
        compiler_params=pltpu.CompilerParams(vmem_limit_bytes=VMEM_LIMIT),
        name="route",
    )(x, *ys, *w_out, gt1, g, sc, sh, rwt, rb, jnp.asarray(upper, dtype=BF16))


def _moe_sparse_kernel(cnt_ref, off_ref, tot_ref, base_ref, h_ref, x1_ref, comb_ref, pos_ref, gt2_ref, fg_ref,
                       wg_hbm, wu_hbm, wdn_hbm, o_ref, xg_s, wrow_s, yw_s, w_s, w_sem, *, final_norm):
    i = pl.program_id(0)
    e = pl.program_id(1)
    n_e = pl.num_programs(1)
    n_sub = h_ref.shape[0] // MOE_SUB
    d = h_ref.shape[1]
    win_row = lax.broadcasted_iota(jnp.int32, (MOE_WIN, MOE_SUB), 0)

    step = i * n_e + e
    n_steps = pl.num_programs(0) * n_e

    def weight_copies(expert, slot):
        return [pltpu.make_async_copy(src.at[expert], w_s.at[slot, k], w_sem.at[slot, k])
                for k, src in enumerate((wg_hbm, wu_hbm, wdn_hbm))]

    @pl.when(step == 0)
    def _():
        for ahead in range(W_SLOTS - 1):
            for cp in weight_copies(ahead % N_EXPERTS, ahead):
                cp.start()

    @pl.when(step + (W_SLOTS - 1) < n_steps)
    def _():
        for cp in weight_copies(lax.rem(e + (W_SLOTS - 1), n_e), lax.rem(step + (W_SLOTS - 1), W_SLOTS)):
            cp.start()

    slot = lax.rem(step, W_SLOTS)
    for cp in weight_copies(e, slot):
        cp.wait()
    wg_ref, wu_ref, wdn_ref = (w_s.at[slot, k] for k in range(3))

    def gather(sub, p):
        k = (i * n_sub + sub) * N_EXPERTS + e
        lanes = slice(sub * MOE_SUB, (sub + 1) * MOE_SUB)
        pos = pos_ref[pl.ds(e, 1), lanes].astype(jnp.int32)
        sel = win_row + p * MOE_WIN == pos
        r0 = pl.multiple_of(off_ref[k] + p * MOE_WIN, 16)
        rows = _dot(jnp.where(sel, 1.0, 0.0).astype(BF16), h_ref[lanes, :])
        xg_s[pl.ds(r0, MOE_WIN), :] = rows.astype(BF16)
        w = jnp.sum(jnp.where(sel, comb_ref[pl.ds(e, 1), lanes], 0.0), axis=1, keepdims=True)
        wrow_s[pl.ds(r0, MOE_WIN), :] = jnp.broadcast_to(w, (MOE_WIN, LANES))

    most = cnt_ref[i * n_sub * N_EXPERTS + e]
    for sub in range(1, n_sub):
        most = jnp.maximum(most, cnt_ref[(i * n_sub + sub) * N_EXPERTS + e])
    for p in range(MOE_SUB // MOE_WIN - 1, 0, -1):
        @pl.when(most > p * MOE_WIN)
        def _():
            for sub in range(n_sub):
                @pl.when(cnt_ref[(i * n_sub + sub) * N_EXPERTS + e] > p * MOE_WIN)
                def _():
                    gather(sub, p)
    for sub in range(n_sub):
        gather(sub, 0)

    total = pl.multiple_of(tot_ref[i * N_EXPERTS + e], 16)
    base = pl.multiple_of(base_ref[i * N_EXPERTS + e], 16)
    xg_s[pl.ds(total, MOE_TAIL), :] = jnp.zeros((MOE_TAIL, d), BF16)
    wrow_s[pl.ds(total, MOE_TAIL), :] = jnp.zeros((MOE_TAIL, LANES), F32)

    def ffn(r0, rows):
        xc = xg_s[pl.ds(r0, rows), :]
        a = (_silu(_dot(xc, wg_ref[...])) * _dot(xc, wu_ref[...])).astype(BF16)
        y = _dot(a, wdn_ref[...]) * jnp.tile(wrow_s[pl.ds(r0, rows), :], (1, d // LANES))
        yw_s[pl.ds(base + r0, rows), :] = y.astype(BF16)

    n_full = total // MOE_CHUNK
    rem = total - n_full * MOE_CHUNK
    n_chunks = n_full + jnp.where(rem > MOE_CHUNK // 2, 1, 0)
    ffn(0, MOE_CHUNK)

    def full_chunk(c, carry):
        ffn(pl.multiple_of(c * MOE_CHUNK, MOE_CHUNK), MOE_CHUNK)
        return carry

    lax.fori_loop(1, n_chunks, full_chunk, 0)
    half = (rem > 0) & (rem <= MOE_CHUNK // 2) & (n_full > 0)

    @pl.when(half)
    def _():
        ffn(pl.multiple_of(n_full * MOE_CHUNK, MOE_CHUNK), MOE_CHUNK // 2)

    covered = jnp.maximum(n_chunks, 1) * MOE_CHUNK + jnp.where(half, MOE_CHUNK // 2, 0)
    yw_s[pl.ds(pl.multiple_of(base + covered, 16), MOE_TAIL), :] = jnp.zeros((MOE_TAIL, d), BF16)

    @pl.when(e == pl.num_programs(1) - 1)
    def _():
        for sub in range(n_sub):
            lanes = slice(sub * MOE_SUB, (sub + 1) * MOE_SUB)
            first = (i * n_sub + sub) * N_EXPERTS

            def windows(p):
                sels, rows = [], []
                for ex in range(N_EXPERTS):
                    r0 = pl.multiple_of(base_ref[i * N_EXPERTS + ex] + off_ref[first + ex] + p * MOE_WIN, 16)
                    pos = pos_ref[ex:ex + 1, lanes].astype(jnp.int32)
                    sels.append(jnp.where(win_row + p * MOE_WIN == pos, 1.0, 0.0).astype(BF16))
                    rows.append(yw_s[pl.ds(r0, MOE_WIN), :])
                return _dot_tn(jnp.concatenate(sels, axis=0), jnp.concatenate(rows, axis=0))

            o_ref[lanes, :] = windows(0)
            most = cnt_ref[first]
            for ex in range(1, N_EXPERTS):
                most = jnp.maximum(most, cnt_ref[first + ex])
            for p in range(1, MOE_SUB // MOE_WIN):
                @pl.when(most > p * MOE_WIN)
                def _():
                    o_ref[lanes, :] += windows(p)
        _finish(x1_ref[...], o_ref[...], gt2_ref, fg_ref, o_ref, final_norm)


def _dense_ffn(x, ys, w_out, mods, g, final_g, wg, wu, wdn, s, final_norm):
    t, d = x.shape
    tm = TM_FFN
    ff = wg.shape[1]
    gt1, sc, sh, gt2 = mods
    tok = lambda width: pl.BlockSpec((tm, width), lambda i: (i, 0))
    const = lambda shape: pl.BlockSpec(shape, lambda i: (0,) * len(shape), pipeline_mode=pl.Buffered(1))
    vec = pl.BlockSpec((1, 1, d), lambda i: ((i * tm) // s, 0, 0))
    return pl.pallas_call(
        functools.partial(_dense_ffn_kernel, final_norm=final_norm),
        grid=(t // tm,),
        in_specs=[tok(d), tok(ATTN_QW), tok(DN_W), tok(CV_C), const((ATTN_QW, d)), const((DN_W, d)), const((CV_C, d)),
                  vec, const((1, d)), vec, vec, vec, const((1, d)),
                  const((d, ff)), const((d, ff)), const((ff, d))],
        out_specs=tok(d),
        out_shape=jax.ShapeDtypeStruct((t, d), F32),
        compiler_params=pltpu.CompilerParams(vmem_limit_bytes=VMEM_LIMIT),
        name="dense_ffn",
    )(x, *ys, *w_out, gt1, g, sc, sh, gt2, final_g, wg, wu, wdn)


def _moe_sparse(h, x1, comb, pos, cnt, gt2, final_g, wg, wu, wdn, s, final_norm):
    t, d = x1.shape
    ts = TS_MOE
    n_e, _, ffe = wg.shape
    n_sub = ts // MOE_SUB
    counts = cnt[:, :, :TM_FFN // MOE_SUB].transpose(0, 2, 1).reshape(t // ts, n_sub, n_e)
    padded = (counts + 15) // 16 * 16
    offs = jnp.cumsum(padded, axis=1) - padded
    tots = jnp.sum(padded, axis=1)
    bases = jnp.cumsum(tots, axis=1) - tots
    cap_one = -(-(ts + n_sub * 16 + MOE_TAIL) // 16) * 16
    cap_all = -(-(N_TOP * ts + n_e * n_sub * 16 + MOE_CHUNK + MOE_TAIL) // 16) * 16
    tok = lambda width: pl.BlockSpec((ts, width), lambda i, e, *_: (i, 0))
    col = pl.BlockSpec((n_e, ts), lambda i, e, *_: (0, i))
    hbm = pl.BlockSpec(memory_space=pl.ANY)
    assert ffe == d, "the weight ring holds the three expert matrices in one (3, d, ffe) slot"
    flat = lambda a: a.reshape(-1).astype(jnp.int32)
    grid_spec = pltpu.PrefetchScalarGridSpec(
        num_scalar_prefetch=4,
        grid=(t // ts, n_e),
        in_specs=[tok(d), tok(d), col, col,
                  pl.BlockSpec((1, 1, d), lambda i, e, *_: ((i * ts) // s, 0, 0)),
                  pl.BlockSpec((1, d), lambda i, e, *_: (0, 0)),
                  hbm, hbm, hbm],
        out_specs=tok(d),
        scratch_shapes=[pltpu.VMEM((cap_one, d), BF16), pltpu.VMEM((cap_one, LANES), F32),
                        pltpu.VMEM((cap_all, d), BF16),
                        pltpu.VMEM((W_SLOTS, 3, d, ffe), BF16), pltpu.SemaphoreType.DMA((W_SLOTS, 3))],
    )
    return pl.pallas_call(
        functools.partial(_moe_sparse_kernel, final_norm=final_norm),
        grid_spec=grid_spec,
        out_shape=jax.ShapeDtypeStruct((t, d), F32),
        compiler_params=pltpu.CompilerParams(vmem_limit_bytes=VMEM_LIMIT_MOE,
                                             dimension_semantics=("arbitrary", "arbitrary")),
        name="moe_sparse",
    )(flat(counts), flat(offs), flat(tots), flat(bases), h, x1, comb, pos, gt2, final_g, wg, wu, wdn)


def _rope_cos_sin(positions):
    inv_freq = ROPE_THETA ** (-jnp.arange(0, ROT_DIM, 2, dtype=F32) / ROT_DIM)
    ang = positions.astype(F32)[:, None, :] * inv_freq[None, :, None]
    return jnp.concatenate([jnp.cos(ang), jnp.sin(ang)], axis=1)


def _pack_w_in(w):
    d = w.shape[0]
    o1 = ATTN_W
    o2 = o1 + 3 * DN_W
    o3 = o2 + 2 * DN_H
    o4 = o3 + DN_W
    gates = jnp.zeros((d, LANES), w.dtype).at[:, :2 * DN_H].set(w[:, o2:o3])
    return jnp.concatenate([w[:, :o2], gates, w[:, o3:o4], w[:, o4:]], axis=1).astype(BF16)


def kernel(x, c, positions, ada_w, ada_b, norm_mix_g, norm_ffn_g, w_in, attn_sinks, dn_conv_w, dn_a_log, dn_dt_bias, dn_norm_g, cv_dw_w, cv_dw_b, cv_ln_g, cv_ln_b, w_out, ffn_w_gate, ffn_w_up, ffn_w_down, router_w, router_b, moe_w_gate, moe_w_up, moe_w_down, final_norm_g):
    b, s, d = x.shape
    depth = w_in.shape[0]
    mod = _adaln(c, ada_w, ada_b)
    cos_sin = _rope_cos_sin(positions)
    final_g = final_norm_g.reshape(1, d)
    vec = lambda t: t.reshape(b, 1, d)
    for l in range(depth):
        sh1, sc1, gt1 = (vec(t) for t in jnp.split(mod[l, 0], 3, axis=-1))
        sh2, sc2, gt2 = (vec(t) for t in jnp.split(mod[l, 1], 3, axis=-1))
        attn, dnqkv, gate, dz, cv = _inproj(x, norm_mix_g[l].reshape(1, d), sc1, sh1, cos_sin, _pack_w_in(w_in[l]))
        y_attn = _attention(attn, attn_sinks[l].astype(F32))
        y_dn = _deltanet(dnqkv, gate, dz, dn_conv_w[l], dn_a_log[l], dn_dt_bias[l], dn_norm_g[l])
        y_cv = _conformer(cv, cv_dw_w[l], cv_dw_b[l], cv_ln_g[l], cv_ln_b[l])
        t = b * s
        ys = (y_attn.reshape(t, ATTN_QW), y_dn.reshape(t, DN_W), y_cv.reshape(t, CV_C))
        wo = w_out[l].astype(BF16)
        wos = (wo[:ATTN_QW], wo[ATTN_QW:ATTN_QW + DN_W], wo[ATTN_QW + DN_W:])
        mods = (gt1, sc2, sh2, gt2)
        g2 = norm_ffn_g[l].reshape(1, d)
        last = l == depth - 1
        j = l // 2
        if l % 2 == 0:
            x2 = _dense_ffn(x.reshape(t, d), ys, wos, mods, g2, final_g, ffn_w_gate[j].astype(BF16),
                            ffn_w_up[j].astype(BF16), ffn_w_down[j].astype(BF16), s, last)
        else:
            x1, h2, comb, pos, cnt = _route(x.reshape(t, d), ys, wos, mods, g2, router_w[j].astype(F32).T,
                                            router_b[j].astype(F32).reshape(N_EXPERTS, 1), s)
            x2 = _moe_sparse(h2, x1, comb, pos, cnt, gt2, final_g, moe_w_gate[j].astype(BF16),
                             moe_w_up[j].astype(BF16), moe_w_down[j].astype(BF16), s, last)
        x = x2.reshape(b, s, d)
    return x
```

```python
import functools

import numpy as np
import jax
import jax.numpy as jnp
from jax import lax
from jax.experimental import pallas as pl
from jax.experimental.pallas import tpu as pltpu

F32 = jnp.float32
BF16 = jnp.bfloat16
HIGHEST = lax.Precision.HIGHEST

HEAD_DIM = 64
ATTN_HQ = 8
ATTN_HKV = 2
ATTN_GROUP = ATTN_HQ // ATTN_HKV
ATTN_BLOCK = 128
ROT_DIM = HEAD_DIM // 4
ROPE_THETA = 500000.0
DN_H = 4
DN_D = 64
DN_CONV = 4
DN_CHUNK = 64
CV_C = 256
CV_K = 31
ATTN_QW = ATTN_HQ * HEAD_DIM
ATTN_KVW = ATTN_HKV * HEAD_DIM
ATTN_W = ATTN_QW + 2 * ATTN_KVW
DN_W = DN_H * DN_D
N_EXPERTS = 8
EPS = 1e-6
LANES = 128
SUBLANES = 8
PACK_ROWS = 16
VMEM_LIMIT = 48 * 1024 * 1024
VMEM_LIMIT_MOE = 56 * 1024 * 1024

TM_PROJ = 512
TQ_ATTN = 512
ATTN_GROUP_UNITS = 8
R_DN = 512
R_CV = 512
TM_FFN = 512
FF_PIECE = 1536
CV_HALO = 32
DN_HALO = 8
TS_MOE = 2048
MOE_SUB = 256
MOE_WIN = 128
MOE_CHUNK = 256
MOE_TAIL = 384
N_TOP = 2
W_SLOTS = 3


def _dot(a, b, precision=None):
    return jnp.dot(a, b, preferred_element_type=F32, precision=precision)


def _dot_nt(a, b):
    return lax.dot_general(a, b, (((1,), (1,)), ((), ())), preferred_element_type=F32)


def _dot_tn(a, b):
    return lax.dot_general(a, b, (((0,), (0,)), ((), ())), preferred_element_type=F32)


def _split3(x):
    hi = x.astype(BF16)
    rest = x - hi.astype(F32)
    mid = rest.astype(BF16)
    return hi, mid, (rest - mid.astype(F32)).astype(BF16)


def _sigmoid(x):
    return 1.0 / (1.0 + jnp.exp(-x))


def _silu(x):
    return x * _sigmoid(x)


def _rms_modulate(x, g, sc, sh):
    y = x * lax.rsqrt(jnp.mean(x * x, axis=-1, keepdims=True) + EPS)
    return (y * g) * (1.0 + sc) + sh


def _adaln_kernel(c_ref, w_ref, b_ref, o_ref):
    o_ref[0] = _dot(_silu(c_ref[...]), w_ref[0], HIGHEST) + b_ref[0]


def _adaln(c, ada_w, ada_b):
    depth, _, d, d3 = ada_w.shape
    b = c.shape[0]
    rows = 8
    cp = jnp.zeros((rows, d), F32).at[:b].set(c)
    w = ada_w.reshape(depth * 2, d, d3)
    bias = ada_b.reshape(depth * 2, 1, d3)
    tn = 1024
    out = pl.pallas_call(
        _adaln_kernel,
        grid=(depth * 2, d3 // tn),
        in_specs=[
            pl.BlockSpec((rows, d), lambda i, j: (0, 0)),
            pl.BlockSpec((1, d, tn), lambda i, j: (i, 0, j)),
            pl.BlockSpec((1, 1, tn), lambda i, j: (i, 0, j)),
        ],
        out_specs=pl.BlockSpec((1, rows, tn), lambda i, j: (i, 0, j)),
        out_shape=jax.ShapeDtypeStruct((depth * 2, rows, d3), F32),
        compiler_params=pltpu.CompilerParams(vmem_limit_bytes=VMEM_LIMIT),
        name="adaln",
    )(cp, w, bias)
    return out[:, :b].reshape(depth, 2, b, d3)


def _inproj_kernel(x_ref, g_ref, sc_ref, sh_ref, cs_ref, place_ref, ones_ref, w_ref,
                   attn_ref, dnqkv_ref, gate_ref, dz_ref, cv_ref):
    h = _rms_modulate(x_ref[0], g_ref[...], sc_ref[0], sh_ref[0]).astype(BF16)
    a = _dot(h, w_ref[:, 0:ATTN_W])
    tab = _dot_tn(jnp.concatenate(_split3(cs_ref[0]), axis=0), place_ref[...]) + ones_ref[...]
    rc, ra, rb = tab[:, 0:LANES], tab[:, LANES:2 * LANES], tab[:, 2 * LANES:3 * LANES]
    half = ROT_DIM // 2
    for j in range((ATTN_QW + ATTN_KVW) // LANES):
        t = a[:, j * LANES:(j + 1) * LANES]
        r = t * rc + pltpu.roll(t, LANES - half, 1) * ra + pltpu.roll(t, half, 1) * rb
        if j < ATTN_QW // LANES:
            r = r * (HEAD_DIM ** -0.5)
        attn_ref[0, :, j * LANES:(j + 1) * LANES] = r.astype(BF16)
    attn_ref[0, :, ATTN_QW + ATTN_KVW:ATTN_W] = a[:, ATTN_QW + ATTN_KVW:ATTN_W].astype(BF16)
    o = ATTN_W
    dnqkv_ref[0] = _dot(h, w_ref[:, o:o + 3 * DN_W])
    o += 3 * DN_W
    gate_ref[0] = _dot(h, w_ref[:, o:o + LANES])
    o += LANES
    dz_ref[0] = _dot(h, w_ref[:, o:o + DN_W])
    o += DN_W
    cv_ref[0] = _dot(h, w_ref[:, o:o + 2 * CV_C])


def _rope_placement():
    half = ROT_DIM // 2
    place = np.zeros((2 * half, 3 * LANES), np.float32)
    ones = np.zeros((1, 3 * LANES), np.float32)
    for lane in range(LANES):
        dim = lane % HEAD_DIM
        if dim < ROT_DIM:
            place[dim % half, lane] = 1.0
            if dim < half:
                place[half + dim, LANES + lane] = -1.0
            else:
                place[half + dim - half, 2 * LANES + lane] = 1.0
        else:
            ones[0, lane] = 1.0
    return np.tile(place, (3, 1)), ones


def _inproj(x, g, sc, sh, cos_sin, w):
    b, s, d = x.shape
    tm = TM_PROJ
    n = w.shape[1]
    row = lambda width: pl.BlockSpec((1, tm, width), lambda bi, i: (bi, i, 0))
    vec = pl.BlockSpec((1, 1, d), lambda bi, i: (bi, 0, 0))
    const = lambda shape: pl.BlockSpec(shape, lambda bi, i: (0,) * len(shape))
    widths = (ATTN_W, 3 * DN_W, LANES, DN_W, 2 * CV_C)
    dtypes = (BF16, F32, F32, F32, F32)
    place, ones = _rope_placement()
    return pl.pallas_call(
        _inproj_kernel,
        grid=(b, s // tm),
        in_specs=[row(d), const((1, d)), vec, vec,
                  pl.BlockSpec((1, ROT_DIM, tm), lambda bi, i: (bi, 0, i)), const(place.shape), const(ones.shape),
                  const((d, n))],
        out_specs=[row(wd) for wd in widths],
        out_shape=[jax.ShapeDtypeStruct((b, s, wd), dt) for wd, dt in zip(widths, dtypes)],
        compiler_params=pltpu.CompilerParams(vmem_limit_bytes=VMEM_LIMIT),
        name="inproj",
    )(x, g, sc, sh, cos_sin, jnp.asarray(place, dtype=BF16), jnp.asarray(ones), w)


def _attn_kernel(sink_ref, cur_ref, prev_ref, o_ref):
    i = pl.program_id(1)
    blk = ATTN_BLOCK
    cur = cur_ref[0]
    prev = prev_ref[0]
    kv_all = jnp.concatenate([prev[:, ATTN_QW:], cur[:, ATTN_QW:]], axis=0)
    band_w = 2 * blk
    qi = lax.broadcasted_iota(jnp.int32, (blk, 2 * band_w), 0)
    kj = lax.broadcasted_iota(jnp.int32, (blk, 2 * band_w), 1) % band_w
    diff = qi + blk - kj
    band = (diff >= 0) & (diff < blk)
    band_first = band & (kj >= jnp.where(i == 0, blk, 0))
    first_head = lax.broadcasted_iota(jnp.int32, (blk, LANES), 1) < HEAD_DIM
    zeros = jnp.zeros((band_w, HEAD_DIM), BF16)

    def blockdiag(t):
        return jnp.concatenate([jnp.concatenate([t, zeros], axis=1), jnp.concatenate([zeros, t], axis=1)], axis=0)

    units = [(r, j, pr) for r in range(cur.shape[0] // blk) for j in range(ATTN_HKV) for pr in range(ATTN_GROUP // 2)]

    def scores(r, j, pr):
        kb = kv_all[r * blk:(r + 2) * blk, j * HEAD_DIM:(j + 1) * HEAD_DIM]
        slab = (j * (ATTN_GROUP // 2) + pr) * LANES
        return _dot_nt(cur[r * blk:(r + 1) * blk, slab:slab + LANES], blockdiag(kb))

    lane_row = lax.broadcasted_iota(jnp.int32, (1, 2 * band_w), 1)
    not_row0 = lax.broadcasted_iota(jnp.int32, (band_w, HEAD_DIM), 0) > 0
    def value_rows(r, j):
        vb = kv_all[r * blk:(r + 2) * blk, ATTN_KVW + j * HEAD_DIM:ATTN_KVW + (j + 1) * HEAD_DIM]
        return blockdiag(jnp.where(not_row0, vb, jnp.zeros_like(vb)))

    for g0 in range(0, len(units), ATTN_GROUP_UNITS):
        group = units[g0:g0 + ATTN_GROUP_UNITS]
        raw = [scores(*u) for u in group]
        sc = []
        for (r, j, pr), s_raw in zip(group, raw):
            hq = j * ATTN_GROUP + 2 * pr
            fill = jnp.where(lane_row == 0, sink_ref[hq], jnp.where(lane_row == band_w, sink_ref[hq + 1], -jnp.inf))
            sc.append(jnp.where(band_first if r == 0 else band, s_raw, fill))
        halves = [[s_[:, t * band_w:(t + 1) * band_w] for t in range(2)] for s_ in sc]
        mx = [[jnp.max(h_, axis=-1, keepdims=True) for h_ in hs] for hs in halves]
        p = [[jnp.exp(h_ - m_) for h_, m_ in zip(hs, ms)] for hs, ms in zip(halves, mx)]
        rs = [[1.0 / jnp.sum(p_, axis=-1, keepdims=True) for p_ in ps] for ps in p]
        p16 = [jnp.concatenate([p_.astype(BF16) for p_ in ps], axis=1) for ps in p]
        outs = [_dot(p16[n], value_rows(r, j)) for n, (r, j, pr) in enumerate(group)]
        for n, (r, j, pr) in enumerate(group):
            o = outs[n] * jnp.where(first_head, rs[n][0], rs[n][1])
            slab = (j * (ATTN_GROUP // 2) + pr) * LANES
            o_ref[0, r * blk:(r + 1) * blk, slab:slab + LANES] = o.astype(BF16)


def _attention(attn, sinks):
    b, s, _ = attn.shape
    tq = TQ_ATTN
    per = tq // ATTN_BLOCK
    return pl.pallas_call(
        _attn_kernel,
        grid=(b, s // tq),
        in_specs=[
            pl.BlockSpec(memory_space=pltpu.SMEM),
            pl.BlockSpec((1, tq, ATTN_W), lambda bi, i: (bi, i, 0)),
            pl.BlockSpec((1, ATTN_BLOCK, ATTN_W), lambda bi, i: (bi, jnp.maximum(i * per - 1, 0), 0)),
        ],
        out_specs=pl.BlockSpec((1, tq, ATTN_QW), lambda bi, i: (bi, i, 0)),
        out_shape=jax.ShapeDtypeStruct((b, s, ATTN_QW), BF16),
        compiler_params=pltpu.CompilerParams(vmem_limit_bytes=VMEM_LIMIT),
        name="attention",
    )(sinks, attn, attn)


def _deltanet_kernel(cur_ref, prev_ref, gate_ref, dz_ref, cw_ref, alog_ref, dtb_ref, ng_ref,
                     expand_ref, tril_ref, slow_ref, ones_ref, o_ref,
                     xext, q_s, k_s, v_s, b_s, g_s, o_s, state):
    i = pl.program_id(0)
    n_seq, rows = cur_ref.shape[0], cur_ref.shape[1]
    ck = DN_CHUNK

    @pl.when(i == 0)
    def _():
        state[...] = jnp.zeros_like(state)

    ones_blk = ones_ref[...]

    def l2n(t):
        ss = _dot((t * t).astype(BF16), ones_blk)
        return t * lax.rsqrt(ss + EPS)

    for bi in range(n_seq):
        seq = slice(bi * rows, (bi + 1) * rows)
        xext[bi, 0:DN_HALO, :] = jnp.where(i > 0, prev_ref[bi], 0.0)
        xext[bi, DN_HALO:DN_HALO + rows, :] = cur_ref[bi]
        acc = cw_ref[0:1, :] * xext[bi, pl.ds(DN_HALO - DN_CONV + 1, rows), :]
        for t in range(1, DN_CONV):
            acc = acc + cw_ref[t:t + 1, :] * xext[bi, pl.ds(DN_HALO - DN_CONV + 1 + t, rows), :]
        qkv = _silu(acc)
        q_s[seq, :] = l2n(qkv[:, 0:DN_W]) * (DN_D ** -0.5)
        k_s[seq, :] = l2n(qkv[:, DN_W:2 * DN_W])
        v_s[seq, :] = qkv[:, 2 * DN_W:3 * DN_W]
        ge = _dot(jnp.concatenate(_split3(gate_ref[bi]), axis=1), expand_ref[...])
        b_s[seq, :] = _sigmoid(ge[:, 0:DN_W])
        da = ge[:, DN_W:2 * DN_W] + dtb_ref[...]
        softplus = jnp.maximum(da, 0.0) + jnp.log1p(jnp.exp(-jnp.abs(da)))
        g_s[seq, :] = -jnp.exp(alog_ref[...]) * softplus

    tril = tril_ref[...]
    slow = slow_ref[...]
    ri = lax.broadcasted_iota(jnp.int32, (ck, DN_W), 0)
    ci = lax.broadcasted_iota(jnp.int32, (ck, DN_W), 1) % ck
    incl = ri >= ci

    pw = 2 * DN_D
    n_pair = DN_W // pw
    r2 = lax.broadcasted_iota(jnp.int32, (ck, pw), 0)
    l2 = lax.broadcasted_iota(jnp.int32, (ck, pw), 1)
    c2 = l2 % ck
    strict = r2 > c2
    eye = jnp.where(r2 == c2, 1.0, 0.0)
    first = l2 < DN_D
    masks = []
    blk = 1
    while blk < ck:
        masks.append((r2 // (2 * blk) == c2 // (2 * blk)) & ((r2 // blk) % 2 == 1) & ((c2 // blk) % 2 == 0))
        blk *= 2
    same_head = (lax.broadcasted_iota(jnp.int32, (pw, pw), 0) // DN_D) == (lax.broadcasted_iota(jnp.int32, (pw, pw), 1) // DN_D)

    def blockdiag(t):
        z = jnp.zeros_like(t)
        return jnp.concatenate([jnp.where(first, t, z), jnp.where(first, z, t)], axis=0)

    nc = n_seq * rows // ck
    per_seq = rows // ck
    slabs = [slice(p * pw, (p + 1) * pw) for p in range(n_pair)]
    kbeta16, k16, q16, dec, vbeta, kbg, qg, kg, egl = ([] for _ in range(9))
    for c in range(nc):
        r0 = c * ck
        q = q_s[r0:r0 + ck, :]
        k = k_s[r0:r0 + ck, :]
        beta = b_s[r0:r0 + ck, :]
        g = g_s[r0:r0 + ck, :]
        cums = _dot(tril, jnp.concatenate([jnp.concatenate(_split3(g), axis=0),
                                           jnp.concatenate(_split3(g * slow), axis=0)], axis=1))
        gc = cums[:, 0:DN_W]
        gdiff = cums[:, DN_W:]
        decay = jnp.exp(jnp.where(incl, gdiff, -jnp.inf))
        egc = jnp.exp(gc)
        glast = gc[ck - 1:ck, :]
        kbeta = k * beta
        per_slab = ((kbeta16, kbeta.astype(BF16)), (k16, k.astype(BF16)), (q16, q.astype(BF16)), (dec, decay),
                    (vbeta, (v_s[r0:r0 + ck, :] * beta).astype(BF16)), (kbg, (kbeta * egc).astype(BF16)),
                    (qg, q * egc), (kg, (k * jnp.exp(glast - gc)).astype(BF16)), (egl, jnp.exp(glast)))
        for dst, full in per_slab:
            dst.extend(full[:, sl] for sl in slabs)
    inst = range(nc * n_pair)
    bdk = [blockdiag(k16[n]) for n in inst]
    lower = [jnp.where(strict, _dot_nt(kbeta16[n], bdk[n]) * dec[n], 0.0) for n in inst]
    a_intra = [(_dot_nt(q16[n], bdk[n]) * dec[n]).astype(BF16) for n in inst]
    lower16 = [t.astype(BF16) for t in lower]
    tinv = [eye - jnp.where(masks[0], lower[n], 0.0) for n in inst]
    for m in masks[1:]:
        t16 = [t.astype(BF16) for t in tinv]
        x16 = [_dot(jnp.where(m, lower16[n], jnp.zeros_like(lower16[n])), blockdiag(t16[n])).astype(BF16) for n in inst]
        tinv = [tinv[n] - _dot(t16[n], blockdiag(x16[n])) for n in inst]
    t16 = [t.astype(BF16) for t in tinv]
    w16 = [_dot(t16[n], blockdiag(kbg[n])).astype(BF16) for n in inst]
    u16 = [_dot(t16[n], blockdiag(vbeta[n])).astype(BF16) for n in inst]
    kw16 = [jnp.where(same_head, _dot_tn(kg[n], w16[n]), 0.0).astype(BF16) for n in inst]
    ku = [jnp.where(same_head, _dot_tn(kg[n], u16[n]), 0.0) for n in inst]
    qeff16 = [(qg[n] - _dot(a_intra[n], blockdiag(w16[n]))).astype(BF16) for n in inst]
    au = [_dot(a_intra[n], blockdiag(u16[n])) for n in inst]
    st = [[state[bi, p] for p in range(n_pair)] for bi in range(n_seq)]
    for c in range(per_seq):
        for bi in range(n_seq):
            outs = []
            for p in range(n_pair):
                n = (bi * per_seq + c) * n_pair + p
                s16 = st[bi][p].astype(BF16)
                outs.append(_dot(qeff16[n], s16) + au[n])
                st[bi][p] = st[bi][p] * egl[n] - _dot(kw16[n], s16) + ku[n]
            r0 = (bi * per_seq + c) * ck
            o_s[r0:r0 + ck, :] = jnp.concatenate(outs, axis=1)
    for bi in range(n_seq):
        for p in range(n_pair):
            state[bi, p] = st[bi][p]
    for bi in range(n_seq):
        o = o_s[bi * rows:(bi + 1) * rows, :]
        ms = _dot((o * o).astype(BF16), ones_blk) * (1.0 / DN_D)
        y = o * lax.rsqrt(ms + EPS) * ng_ref[...]
        o_ref[bi] = (y * _silu(dz_ref[bi])).astype(BF16)


def _deltanet(dnqkv, gate, dz, conv_w, a_log, dt_bias, norm_g):
    b, s, _ = dnqkv.shape
    rows = R_DN
    ck = DN_CHUNK
    lane_head = np.arange(2 * DN_W) // DN_D
    expand = np.tile((np.arange(LANES)[:, None] == lane_head[None, :]).astype(np.float32), (3, 1))
    tril = np.tile(np.tril(np.ones((ck, ck), np.float32)), (1, 3))
    slow = (np.arange(ck)[:, None] > (np.arange(DN_W)[None, :] % ck)).astype(np.float32)
    ones_blk = (np.arange(DN_W)[:, None] // DN_D == np.arange(DN_W)[None, :] // DN_D).astype(np.float32)
    rep = lambda t: jnp.repeat(t.astype(F32), DN_D).reshape(1, DN_W)
    const = lambda shape: pl.BlockSpec(shape, lambda i: (0,) * len(shape))
    row = lambda width: pl.BlockSpec((b, rows, width), lambda i: (0, i, 0))
    per = rows // DN_HALO
    return pl.pallas_call(
        _deltanet_kernel,
        grid=(s // rows,),
        in_specs=[
            row(3 * DN_W),
            pl.BlockSpec((b, DN_HALO, 3 * DN_W), lambda i: (0, jnp.maximum(i * per - 1, 0), 0)),
            row(LANES), row(DN_W),
            const((DN_CONV, 3 * DN_W)), const((1, DN_W)), const((1, DN_W)), const((1, DN_W)),
            const((3 * LANES, 2 * DN_W)), const((ck, 3 * ck)), const((ck, DN_W)), const((DN_W, DN_W)),
        ],
        out_specs=row(DN_W),
        out_shape=jax.ShapeDtypeStruct((b, s, DN_W), BF16),
        scratch_shapes=[
            pltpu.VMEM((b, rows + DN_HALO, 3 * DN_W), F32),
            *[pltpu.VMEM((b * rows, DN_W), F32) for _ in range(6)],
            pltpu.VMEM((b, DN_W // (2 * DN_D), 2 * DN_D, 2 * DN_D), F32),
        ],
        compiler_params=pltpu.CompilerParams(vmem_limit_bytes=VMEM_LIMIT, dimension_semantics=("arbitrary",)),
        name="deltanet",
    )(dnqkv, dnqkv, gate, dz, conv_w, rep(a_log), rep(dt_bias), jnp.tile(norm_g.astype(F32), DN_H).reshape(1, DN_W),
      jnp.asarray(expand, dtype=BF16), jnp.asarray(tril, dtype=BF16), jnp.asarray(slow), jnp.asarray(ones_blk, dtype=BF16))


def _conformer_kernel(cur_ref, prev_ref, w_ref, b_ref, lg_ref, lb_ref, o_ref, u_s, sh_s):
    i = pl.program_id(1)
    rows = cur_ref.shape[1]
    prev = jnp.where(i > 0, prev_ref[0], 0.0)
    u_s[0:CV_HALO, :] = prev[:, 0:CV_C] * _sigmoid(prev[:, CV_C:])
    cur = cur_ref[0]
    u_s[CV_HALO:CV_HALO + rows, :] = cur[:, 0:CV_C] * _sigmoid(cur[:, CV_C:])
    span = rows + CV_HALO - SUBLANES
    for ph in range(1, SUBLANES):
        sh_s[ph - 1, 0:span, :] = u_s[pl.ds(ph, span), :]
    base = CV_HALO - CV_K + 1
    acc = b_ref[...]
    for t in range(CV_K):
        blk, ph = divmod(base + t, SUBLANES)
        src = u_s if ph == 0 else sh_s.at[ph - 1]
        acc = acc + w_ref[t:t + 1, :] * src[blk * SUBLANES:blk * SUBLANES + rows, :]
    mu = jnp.mean(acc, axis=-1, keepdims=True)
    xc = acc - mu
    y = xc * lax.rsqrt(jnp.mean(xc * xc, axis=-1, keepdims=True) + EPS)
    o_ref[0] = _silu(y * lg_ref[...] + lb_ref[...]).astype(BF16)


def _conformer(cv, w, bias, ln_g, ln_b):
    b, s, _ = cv.shape
    rows = R_CV
    per = rows // CV_HALO
    const = lambda shape: pl.BlockSpec(shape, lambda bi, i: (0,) * len(shape))
    return pl.pallas_call(
        _conformer_kernel,
        grid=(b, s // rows),
        in_specs=[
            pl.BlockSpec((1, rows, 2 * CV_C), lambda bi, i: (bi, i, 0)),
            pl.BlockSpec((1, CV_HALO, 2 * CV_C), lambda bi, i: (bi, jnp.maximum(i * per - 1, 0), 0)),
            const((CV_K, CV_C)), const((1, CV_C)), const((1, CV_C)), const((1, CV_C)),
        ],
        out_specs=pl.BlockSpec((1, rows, CV_C), lambda bi, i: (bi, i, 0)),
        out_shape=jax.ShapeDtypeStruct((b, s, CV_C), BF16),
        scratch_shapes=[pltpu.VMEM((rows + CV_HALO, CV_C), F32),
                        pltpu.VMEM((SUBLANES - 1, rows + CV_HALO - SUBLANES, CV_C), F32)],
        compiler_params=pltpu.CompilerParams(vmem_limit_bytes=VMEM_LIMIT),
        name="conformer",
    )(cv, cv, w, bias.reshape(1, CV_C), ln_g.reshape(1, CV_C), ln_b.reshape(1, CV_C))


def _mix_residual(x_ref, ya_ref, yd_ref, yc_ref, wa_ref, wd_ref, wc_ref, gt1_ref):
    y = _dot(ya_ref[...], wa_ref[...]) + _dot(yd_ref[...], wd_ref[...]) + _dot(yc_ref[...], wc_ref[...])
    return x_ref[...] + gt1_ref[0] * y


def _finish(x1, acc, gt2_ref, fg_ref, o_ref, final_norm):
    x2 = x1 + gt2_ref[0] * acc
    if final_norm:
        x2 = x2 * lax.rsqrt(jnp.mean(x2 * x2, axis=-1, keepdims=True) + EPS) * fg_ref[...]
    o_ref[...] = x2


def _dense_ffn_kernel(x_ref, ya_ref, yd_ref, yc_ref, wa_ref, wd_ref, wc_ref, gt1_ref, g_ref, sc_ref, sh_ref, gt2_ref,
                      fg_ref, wg_ref, wu_ref, wdn_ref, o_ref, *, final_norm):
    x1 = _mix_residual(x_ref, ya_ref, yd_ref, yc_ref, wa_ref, wd_ref, wc_ref, gt1_ref)
    h = _rms_modulate(x1, g_ref[...], sc_ref[0], sh_ref[0]).astype(BF16)
    ff = wg_ref.shape[1]
    acc = None
    for c0 in range(0, ff, FF_PIECE):
        c1 = min(c0 + FF_PIECE, ff)
        a = (_silu(_dot(h, wg_ref[:, c0:c1])) * _dot(h, wu_ref[:, c0:c1])).astype(BF16)
        part = _dot(a, wdn_ref[c0:c1, :])
        acc = part if acc is None else acc + part
    _finish(x1, acc, gt2_ref, fg_ref, o_ref, final_norm)


def _route_kernel(x_ref, ya_ref, yd_ref, yc_ref, wa_ref, wd_ref, wc_ref, gt1_ref, g_ref, sc_ref, sh_ref,
                  rwt_ref, rb_ref, upper_ref, x1_ref, h_ref, comb_ref, pos_ref, cnt_ref):
    subs = [slice(n * MOE_SUB, (n + 1) * MOE_SUB) for n in range(x_ref.shape[0] // MOE_SUB)]
    wa, wd, wc = wa_ref[...], wd_ref[...], wc_ref[...]
    y = [_dot(ya_ref[sl, :], wa) + _dot(yd_ref[sl, :], wd) + _dot(yc_ref[sl, :], wc) for sl in subs]
    x1 = [x_ref[sl, :] + gt1_ref[0] * y_ for sl, y_ in zip(subs, y)]
    h = [_rms_modulate(x_, g_ref[...], sc_ref[0], sh_ref[0]) for x_ in x1]
    for sl, x_, h_ in zip(subs, x1, h):
        x1_ref[sl, :] = x_
        h_ref[sl, :] = h_.astype(BF16)
    w_hi, w_mid, _ = _split3(rwt_ref[...])
    w_cat = jnp.concatenate([w_hi, w_hi, w_mid], axis=1)
    pieces = [_split3(h_) for h_ in h]
    logits = [_dot_nt(w_cat, jnp.concatenate([hi, mid, hi], axis=1)) + rb_ref[...] for hi, mid, _ in pieces]
    row = lax.broadcasted_iota(jnp.int32, logits[0].shape, 0)
    lane = lax.broadcasted_iota(jnp.int32, (N_EXPERTS, LANES), 1)
    cnt = jnp.zeros((N_EXPERTS, LANES), F32)
    for n, (sl, lg) in enumerate(zip(subs, logits)):
        m1 = jnp.max(lg, axis=0, keepdims=True)
        i1 = jnp.min(jnp.where(lg == m1, row, N_EXPERTS), axis=0, keepdims=True)
        rest = jnp.where(row == i1, -jnp.inf, lg)
        m2 = jnp.max(rest, axis=0, keepdims=True)
        i2 = jnp.min(jnp.where(rest == m2, row, N_EXPERTS), axis=0, keepdims=True)
        e2 = jnp.exp(m2 - m1)
        comb_ref[:, sl] = jnp.where(row == i1, 1.0 / (1.0 + e2), 0.0) + jnp.where(row == i2, e2 / (1.0 + e2), 0.0)
        sel = jnp.where((row == i1) | (row == i2), 1.0, 0.0)
        rank = _dot(sel.astype(BF16), upper_ref[...])
        pos_ref[:, sl] = jnp.where(sel > 0.0, rank, -1.0)
        cnt = cnt + jnp.where(lane == n, jnp.sum(sel, axis=1, keepdims=True), 0.0)
    cnt_ref[0] = cnt.astype(jnp.int32)


def _route(x, ys, w_out, mods, g, rwt, rb, s):
    t, d = x.shape
    tm = TM_FFN
    gt1, sc, sh, _ = mods
    tok = lambda width: pl.BlockSpec((tm, width), lambda i: (i, 0))
    const = lambda shape: pl.BlockSpec(shape, lambda i: (0,) * len(shape))
    vec = pl.BlockSpec((1, 1, d), lambda i: ((i * tm) // s, 0, 0))
    col = pl.BlockSpec((N_EXPERTS, tm), lambda i: (0, i))
    upper = np.triu(np.ones((MOE_SUB, MOE_SUB), np.float32), 1)
    return pl.pallas_call(
        _route_kernel,
        grid=(t // tm,),
        in_specs=[tok(d), tok(ATTN_QW), tok(DN_W), tok(CV_C), const((ATTN_QW, d)), const((DN_W, d)), const((CV_C, d)),
                  vec, const((1, d)), vec, vec, const((N_EXPERTS, d)), const((N_EXPERTS, 1)), const((MOE_SUB, MOE_SUB))],
        out_specs=[tok(d), tok(d), col, col, pl.BlockSpec((1, N_EXPERTS, LANES), lambda i: (i, 0, 0))],
        out_shape=[jax.ShapeDtypeStruct((t, d), F32), jax.ShapeDtypeStruct((t, d), BF16),
                   jax.ShapeDtypeStruct((N_EXPERTS, t), F32), jax.ShapeDtypeStruct((N_EXPERTS, t), F32),
                   jax.ShapeDtypeStruct((t // tm, N_EXPERTS, LANES), jnp.int32)],
        compiler_params=pltpu.CompilerParams(vmem_limit_bytes=VMEM_LIMIT),
        name="route",
    )(x, *ys, *w_out, gt1, g, sc, sh, rwt, rb, jnp.asarray(upper, dtype=BF16))


def _moe_sparse_kernel(cnt_ref, off_ref, tot_ref, base_ref, h_ref, x1_hbm, comb_ref, pos_ref, gt2_ref, fg_ref,
                       wg_hbm, wu_hbm, wdn_hbm, o_hbm, xg_s, wrow_s, yw_s, w_s, w_sem, acc_s, x1_s, out_s, io_sem,
                       *, final_norm):
    i = pl.program_id(0)
    e = pl.program_id(1)
    n_e = pl.num_programs(1)
    n_sub = h_ref.shape[0] // MOE_SUB
    d = h_ref.shape[1]
    win_row = lax.broadcasted_iota(jnp.int32, (MOE_WIN, MOE_SUB), 0)

    step = i * n_e + e
    n_steps = pl.num_programs(0) * n_e

    def weight_copies(expert, slot):
        return [pltpu.make_async_copy(src.at[expert], w_s.at[slot, k], w_sem.at[slot, k])
                for k, src in enumerate((wg_hbm, wu_hbm, wdn_hbm))]

    @pl.when(step == 0)
    def _():
        for ahead in range(W_SLOTS - 1):
            for cp in weight_copies(ahead % N_EXPERTS, ahead):
                cp.start()

    @pl.when(step + (W_SLOTS - 1) < n_steps)
    def _():
        for cp in weight_copies(lax.rem(e + (W_SLOTS - 1), n_e), lax.rem(step + (W_SLOTS - 1), W_SLOTS)):
            cp.start()

    slot = lax.rem(step, W_SLOTS)
    for cp in weight_copies(e, slot):
        cp.wait()
    wg_ref, wu_ref, wdn_ref = (w_s.at[slot, k] for k in range(3))

    def gather(sub, p):
        k = (i * n_sub + sub) * N_EXPERTS + e
        lanes = slice(sub * MOE_SUB, (sub + 1) * MOE_SUB)
        pos = pos_ref[pl.ds(e, 1), lanes].astype(jnp.int32)
        sel = win_row + p * MOE_WIN == pos
        r0 = pl.multiple_of(off_ref[k] + p * MOE_WIN, PACK_ROWS)
        rows = _dot(jnp.where(sel, 1.0, 0.0).astype(BF16), h_ref[lanes, :])
        xg_s[pl.ds(r0, MOE_WIN), :] = rows.astype(BF16)
        w = jnp.sum(jnp.where(sel, comb_ref[pl.ds(e, 1), lanes], 0.0), axis=1, keepdims=True)
        wrow_s[pl.ds(r0, MOE_WIN), :] = jnp.broadcast_to(w, (MOE_WIN, LANES))

    most = cnt_ref[i * n_sub * N_EXPERTS + e]
    for sub in range(1, n_sub):
        most = jnp.maximum(most, cnt_ref[(i * n_sub + sub) * N_EXPERTS + e])
    for p in range(MOE_SUB // MOE_WIN - 1, 0, -1):
        @pl.when(most > p * MOE_WIN)
        def _():
            for sub in range(n_sub):
                @pl.when(cnt_ref[(i * n_sub + sub) * N_EXPERTS + e] > p * MOE_WIN)
                def _():
                    gather(sub, p)
    for sub in range(n_sub):
        gather(sub, 0)

    total = pl.multiple_of(tot_ref[i * N_EXPERTS + e], PACK_ROWS)
    base = pl.multiple_of(base_ref[i * N_EXPERTS + e], PACK_ROWS)
    xg_s[pl.ds(total, MOE_TAIL), :] = jnp.zeros((MOE_TAIL, d), BF16)
    wrow_s[pl.ds(total, MOE_TAIL), :] = jnp.zeros((MOE_TAIL, LANES), F32)

    def ffn(r0, rows):
        xc = xg_s[pl.ds(r0, rows), :]
        a = (_silu(_dot(xc, wg_ref[...])) * _dot(xc, wu_ref[...])).astype(BF16)
        y = _dot(a, wdn_ref[...]) * jnp.tile(wrow_s[pl.ds(r0, rows), :], (1, d // LANES))
        yw_s[pl.ds(base + r0, rows), :] = y.astype(BF16)

    n_full = total // MOE_CHUNK
    rem = total - n_full * MOE_CHUNK
    n_chunks = n_full + jnp.where(rem > MOE_CHUNK // 2, 1, 0)
    ffn(0, MOE_CHUNK)

    def full_chunk(c, carry):
        ffn(pl.multiple_of(c * MOE_CHUNK, MOE_CHUNK), MOE_CHUNK)
        return carry

    lax.fori_loop(1, n_chunks, full_chunk, 0)
    half = (rem > 0) & (rem <= MOE_CHUNK // 2) & (n_full > 0)

    @pl.when(half)
    def _():
        ffn(pl.multiple_of(n_full * MOE_CHUNK, MOE_CHUNK), MOE_CHUNK // 2)

    covered = jnp.maximum(n_chunks, 1) * MOE_CHUNK + jnp.where(half, MOE_CHUNK // 2, 0)
    yw_s[pl.ds(pl.multiple_of(base + covered, PACK_ROWS), MOE_TAIL), :] = jnp.zeros((MOE_TAIL, d), BF16)

    @pl.when(e == n_e - 1)
    def _():
        row0 = i * (n_sub * MOE_SUB)

        def rows_of(ref, sub):
            return ref.at[pl.ds(pl.multiple_of(row0 + sub * MOE_SUB, MOE_SUB), MOE_SUB), :]

        def x1_copy(sub):
            return pltpu.make_async_copy(rows_of(x1_hbm, sub), x1_s.at[sub % 2], io_sem.at[0, sub % 2])

        def out_copy(sub):
            return pltpu.make_async_copy(out_s.at[sub % 2], rows_of(o_hbm, sub), io_sem.at[1, sub % 2])

        x1_copy(0).start()
        for sub in range(n_sub):
            if sub + 1 < n_sub:
                x1_copy(sub + 1).start()
            lanes = slice(sub * MOE_SUB, (sub + 1) * MOE_SUB)
            first = (i * n_sub + sub) * N_EXPERTS

            def windows(p):
                sels, rows = [], []
                for ex in range(N_EXPERTS):
                    r0 = pl.multiple_of(base_ref[i * N_EXPERTS + ex] + off_ref[first + ex] + p * MOE_WIN, PACK_ROWS)
                    pos = pos_ref[ex:ex + 1, lanes].astype(jnp.int32)
                    sels.append(jnp.where(win_row + p * MOE_WIN == pos, 1.0, 0.0).astype(BF16))
                    rows.append(yw_s[pl.ds(r0, MOE_WIN), :])
                return _dot_tn(jnp.concatenate(sels, axis=0), jnp.concatenate(rows, axis=0))

            acc_s[...] = windows(0)
            most = cnt_ref[first]
            for ex in range(1, N_EXPERTS):
                most = jnp.maximum(most, cnt_ref[first + ex])
            for p in range(1, MOE_SUB // MOE_WIN):
                @pl.when(most > p * MOE_WIN)
                def _():
                    acc_s[...] += windows(p)
            x1_copy(sub).wait()
            x2 = x1_s[sub % 2] + gt2_ref[0] * acc_s[...]
            if final_norm:
                x2 = x2 * lax.rsqrt(jnp.mean(x2 * x2, axis=-1, keepdims=True) + EPS) * fg_ref[...]
            if sub >= 2:
                out_copy(sub - 2).wait()
            out_s[sub % 2] = x2
            out_copy(sub).start()
        for sub in range(max(n_sub - 2, 0), n_sub):
            out_copy(sub).wait()


def _dense_ffn(x, ys, w_out, mods, g, final_g, wg, wu, wdn, s, final_norm):
    t, d = x.shape
    tm = TM_FFN
    ff = wg.shape[1]
    gt1, sc, sh, gt2 = mods
    tok = lambda width: pl.BlockSpec((tm, width), lambda i: (i, 0))
    const = lambda shape: pl.BlockSpec(shape, lambda i: (0,) * len(shape), pipeline_mode=pl.Buffered(1))
    vec = pl.BlockSpec((1, 1, d), lambda i: ((i * tm) // s, 0, 0))
    return pl.pallas_call(
        functools.partial(_dense_ffn_kernel, final_norm=final_norm),
        grid=(t // tm,),
        in_specs=[tok(d), tok(ATTN_QW), tok(DN_W), tok(CV_C), const((ATTN_QW, d)), const((DN_W, d)), const((CV_C, d)),
                  vec, const((1, d)), vec, vec, vec, const((1, d)),
                  const((d, ff)), const((d, ff)), const((ff, d))],
        out_specs=tok(d),
        out_shape=jax.ShapeDtypeStruct((t, d), F32),
        compiler_params=pltpu.CompilerParams(vmem_limit_bytes=VMEM_LIMIT),
        name="dense_ffn",
    )(x, *ys, *w_out, gt1, g, sc, sh, gt2, final_g, wg, wu, wdn)


def _moe_sparse(h, x1, comb, pos, cnt, gt2, final_g, wg, wu, wdn, s, final_norm):
    t, d = x1.shape
    ts = TS_MOE
    n_e, _, ffe = wg.shape
    n_sub = ts // MOE_SUB
    counts = cnt[:, :, :TM_FFN // MOE_SUB].transpose(0, 2, 1).reshape(t // ts, n_sub, n_e)
    padded = -(-counts // PACK_ROWS) * PACK_ROWS
    offs = jnp.cumsum(padded, axis=1) - padded
    tots = jnp.sum(padded, axis=1)
    bases = jnp.cumsum(tots, axis=1) - tots
    cap_one = -(-(ts + n_sub * PACK_ROWS + MOE_TAIL) // PACK_ROWS) * PACK_ROWS
    cap_all = -(-(N_TOP * ts + n_e * n_sub * PACK_ROWS + MOE_CHUNK + MOE_TAIL) // PACK_ROWS) * PACK_ROWS
    tok = lambda width: pl.BlockSpec((ts, width), lambda i, e, *_: (i, 0))
    col = pl.BlockSpec((n_e, ts), lambda i, e, *_: (0, i))
    hbm = pl.BlockSpec(memory_space=pl.ANY)
    assert ffe == d, "the weight ring holds the three expert matrices in one (3, d, ffe) slot"
    flat = lambda a: a.reshape(-1).astype(jnp.int32)
    grid_spec = pltpu.PrefetchScalarGridSpec(
        num_scalar_prefetch=4,
        grid=(t // ts, n_e),
        in_specs=[tok(d), hbm, col, col,
                  pl.BlockSpec((1, 1, d), lambda i, e, *_: ((i * ts) // s, 0, 0)),
                  pl.BlockSpec((1, d), lambda i, e, *_: (0, 0)),
                  hbm, hbm, hbm],
        out_specs=hbm,
        scratch_shapes=[pltpu.VMEM((cap_one, d), BF16), pltpu.VMEM((cap_one, LANES), F32),
                        pltpu.VMEM((cap_all, d), BF16),
                        pltpu.VMEM((W_SLOTS, 3, d, ffe), BF16), pltpu.SemaphoreType.DMA((W_SLOTS, 3)),
                        pltpu.VMEM((MOE_SUB, d), F32), pltpu.VMEM((2, MOE_SUB, d), F32), pltpu.VMEM((2, MOE_SUB, d), F32),
                        pltpu.SemaphoreType.DMA((2, 2))],
    )
    return pl.pallas_call(
        functools.partial(_moe_sparse_kernel, final_norm=final_norm),
        grid_spec=grid_spec,
        out_shape=jax.ShapeDtypeStruct((t, d), F32),
        compiler_params=pltpu.CompilerParams(vmem_limit_bytes=VMEM_LIMIT_MOE,
                                             dimension_semantics=("arbitrary", "arbitrary")),
        name="moe_sparse",
    )(flat(counts), flat(offs), flat(tots), flat(bases), h, x1, comb, pos, gt2, final_g, wg, wu, wdn)


def _rope_cos_sin(positions):
    inv_freq = ROPE_THETA ** (-jnp.arange(0, ROT_DIM, 2, dtype=F32) / ROT_DIM)
    ang = positions.astype(F32)[:, None, :] * inv_freq[None, :, None]
    return jnp.concatenate([jnp.cos(ang), jnp.sin(ang)], axis=1)


def _pack_w_in(w):
    d = w.shape[0]
    o1 = ATTN_W
    o2 = o1 + 3 * DN_W
    o3 = o2 + 2 * DN_H
    o4 = o3 + DN_W
    gates = jnp.zeros((d, LANES), w.dtype).at[:, :2 * DN_H].set(w[:, o2:o3])
    return jnp.concatenate([w[:, :o2], gates, w[:, o3:o4], w[:, o4:]], axis=1).astype(BF16)


def kernel(x, c, positions, ada_w, ada_b, norm_mix_g, norm_ffn_g, w_in, attn_sinks, dn_conv_w, dn_a_log, dn_dt_bias, dn_norm_g, cv_dw_w, cv_dw_b, cv_ln_g, cv_ln_b, w_out, ffn_w_gate, ffn_w_up, ffn_w_down, router_w, router_b, moe_w_gate, moe_w_up, moe_w_down, final_norm_g):
    b, s, d = x.shape
    depth = w_in.shape[0]
    mod = _adaln(c, ada_w, ada_b)
    cos_sin = _rope_cos_sin(positions)
    final_g = final_norm_g.reshape(1, d)
    vec = lambda t: t.reshape(b, 1, d)
    for l in range(depth):
        sh1, sc1, gt1 = (vec(t) for t in jnp.split(mod[l, 0], 3, axis=-1))
        sh2, sc2, gt2 = (vec(t) for t in jnp.split(mod[l, 1], 3, axis=-1))
        attn, dnqkv, gate, dz, cv = _inproj(x, norm_mix_g[l].reshape(1, d), sc1, sh1, cos_sin, _pack_w_in(w_in[l]))
        y_attn = _attention(attn, attn_sinks[l].astype(F32))
        y_dn = _deltanet(dnqkv, gate, dz, dn_conv_w[l], dn_a_log[l], dn_dt_bias[l], dn_norm_g[l])
        y_cv = _conformer(cv, cv_dw_w[l], cv_dw_b[l], cv_ln_g[l], cv_ln_b[l])
        t = b * s
        ys = (y_attn.reshape(t, ATTN_QW), y_dn.reshape(t, DN_W), y_cv.reshape(t, CV_C))
        wo = w_out[l].astype(BF16)
        wos = (wo[:ATTN_QW], wo[ATTN_QW:ATTN_QW + DN_W], wo[ATTN_QW + DN_W:])
        mods = (gt1, sc2, sh2, gt2)
        g2 = norm_ffn_g[l].reshape(1, d)
        last = l == depth - 1
        j = l // 2
        if l % 2 == 0:
            x2 = _dense_ffn(x.reshape(t, d), ys, wos, mods, g2, final_g, ffn_w_gate[j].astype(BF16),
                            ffn_w_up[j].astype(BF16), ffn_w_down[j].astype(BF16), s, last)
        else:
            x1, h2, comb, pos, cnt = _route(x.reshape(t, d), ys, wos, mods, g2, router_w[j].astype(F32).T,
                                            router_b[j].astype(F32).reshape(N_EXPERTS, 1), s)
            x2 = _moe_sparse(h2, x1, comb, pos, cnt, gt2, final_g, moe_w_gate[j].astype(BF16),
                             moe_w_up[j].astype(BF16), moe_w_down[j].astype(BF16), s, last)
        x = x2.reshape(b, s, d)
    return x
```

```python
import functools

import numpy as np
import jax
import jax.numpy as jnp
from jax import lax
from jax.experimental import pallas as pl
from jax.experimental.pallas import tpu as pltpu

F32 = jnp.float32
BF16 = jnp.bfloat16
HIGHEST = lax.Precision.HIGHEST

HEAD_DIM = 64
ATTN_HQ = 8
ATTN_HKV = 2
ATTN_GROUP = ATTN_HQ // ATTN_HKV
ATTN_BLOCK = 128
ROT_DIM = HEAD_DIM // 4
ROPE_THETA = 500000.0
DN_H = 4
DN_D = 64
DN_CONV = 4
DN_CHUNK = 64
CV_C = 256
CV_K = 31
ATTN_QW = ATTN_HQ * HEAD_DIM
ATTN_KVW = ATTN_HKV * HEAD_DIM
ATTN_W = ATTN_QW + 2 * ATTN_KVW
DN_W = DN_H * DN_D
N_EXPERTS = 8
EPS = 1e-6
LANES = 128
SUBLANES = 8
PACK_ROWS = 16
VMEM_LIMIT = 48 * 1024 * 1024
VMEM_LIMIT_MOE = 56 * 1024 * 1024

TM_PROJ = 1024
TQ_ATTN = 1024
ATTN_GROUP_UNITS = 8
R_DN = 512
R_CV = 1024
TM_FFN = 512
FF_PIECE = 1536
CV_HALO = 32
DN_HALO = 8
TS_MOE = 2048
MOE_SUB = 256
MOE_WIN = 128
MOE_CHUNK = 256
MOE_TAIL = 384
N_TOP = 2
W_SLOTS = 3


def _dot(a, b, precision=None):
    return jnp.dot(a, b, preferred_element_type=F32, precision=precision)


def _dot_nt(a, b):
    return lax.dot_general(a, b, (((1,), (1,)), ((), ())), preferred_element_type=F32)


def _dot_tn(a, b):
    return lax.dot_general(a, b, (((0,), (0,)), ((), ())), preferred_element_type=F32)


def _split3(x):
    hi = x.astype(BF16)
    rest = x - hi.astype(F32)
    mid = rest.astype(BF16)
    return hi, mid, (rest - mid.astype(F32)).astype(BF16)


def _sigmoid(x):
    return 1.0 / (1.0 + jnp.exp(-x))


def _silu(x):
    return x * _sigmoid(x)


def _rms_modulate(x, g, sc, sh):
    y = x * lax.rsqrt(jnp.mean(x * x, axis=-1, keepdims=True) + EPS)
    return (y * g) * (1.0 + sc) + sh


def _adaln_kernel(c_ref, w_ref, b_ref, o_ref):
    o_ref[0] = _dot(_silu(c_ref[...]), w_ref[0], HIGHEST) + b_ref[0]


def _adaln(c, ada_w, ada_b):
    depth, _, d, d3 = ada_w.shape
    b = c.shape[0]
    rows = 8
    cp = jnp.zeros((rows, d), F32).at[:b].set(c)
    w = ada_w.reshape(depth * 2, d, d3)
    bias = ada_b.reshape(depth * 2, 1, d3)
    tn = 1024
    out = pl.pallas_call(
        _adaln_kernel,
        grid=(depth * 2, d3 // tn),
        in_specs=[
            pl.BlockSpec((rows, d), lambda i, j: (0, 0)),
            pl.BlockSpec((1, d, tn), lambda i, j: (i, 0, j)),
            pl.BlockSpec((1, 1, tn), lambda i, j: (i, 0, j)),
        ],
        out_specs=pl.BlockSpec((1, rows, tn), lambda i, j: (i, 0, j)),
        out_shape=jax.ShapeDtypeStruct((depth * 2, rows, d3), F32),
        compiler_params=pltpu.CompilerParams(vmem_limit_bytes=VMEM_LIMIT),
        name="adaln",
    )(cp, w, bias)
    return out[:, :b].reshape(depth, 2, b, d3)


def _inproj_kernel(x_ref, g_ref, sc_ref, sh_ref, cs_ref, place_ref, ones_ref, w_ref,
                   attn_ref, dnqkv_ref, gate_ref, dz_ref, cv_ref):
    h = _rms_modulate(x_ref[0], g_ref[...], sc_ref[0], sh_ref[0]).astype(BF16)
    a = _dot(h, w_ref[:, 0:ATTN_W])
    tab = _dot_tn(jnp.concatenate(_split3(cs_ref[0]), axis=0), place_ref[...]) + ones_ref[...]
    rc, ra, rb = tab[:, 0:LANES], tab[:, LANES:2 * LANES], tab[:, 2 * LANES:3 * LANES]
    half = ROT_DIM // 2
    for j in range((ATTN_QW + ATTN_KVW) // LANES):
        t = a[:, j * LANES:(j + 1) * LANES]
        r = t * rc + pltpu.roll(t, LANES - half, 1) * ra + pltpu.roll(t, half, 1) * rb
        if j < ATTN_QW // LANES:
            r = r * (HEAD_DIM ** -0.5)
        attn_ref[0, :, j * LANES:(j + 1) * LANES] = r.astype(BF16)
    attn_ref[0, :, ATTN_QW + ATTN_KVW:ATTN_W] = a[:, ATTN_QW + ATTN_KVW:ATTN_W].astype(BF16)
    o = ATTN_W
    dnqkv_ref[0] = _dot(h, w_ref[:, o:o + 3 * DN_W])
    o += 3 * DN_W
    gate_ref[0] = _dot(h, w_ref[:, o:o + LANES])
    o += LANES
    dz_ref[0] = _dot(h, w_ref[:, o:o + DN_W])
    o += DN_W
    cv_ref[0] = _dot(h, w_ref[:, o:o + 2 * CV_C])


def _rope_placement():
    half = ROT_DIM // 2
    place = np.zeros((2 * half, 3 * LANES), np.float32)
    ones = np.zeros((1, 3 * LANES), np.float32)
    for lane in range(LANES):
        dim = lane % HEAD_DIM
        if dim < ROT_DIM:
            place[dim % half, lane] = 1.0
            if dim < half:
                place[half + dim, LANES + lane] = -1.0
            else:
                place[half + dim - half, 2 * LANES + lane] = 1.0
        else:
            ones[0, lane] = 1.0
    return np.tile(place, (3, 1)), ones


def _inproj(x, g, sc, sh, cos_sin, w):
    b, s, d = x.shape
    tm = TM_PROJ
    n = w.shape[1]
    row = lambda width: pl.BlockSpec((1, tm, width), lambda bi, i: (bi, i, 0))
    vec = pl.BlockSpec((1, 1, d), lambda bi, i: (bi, 0, 0))
    const = lambda shape: pl.BlockSpec(shape, lambda bi, i: (0,) * len(shape))
    widths = (ATTN_W, 3 * DN_W, LANES, DN_W, 2 * CV_C)
    dtypes = (BF16, F32, F32, F32, F32)
    place, ones = _rope_placement()
    return pl.pallas_call(
        _inproj_kernel,
        grid=(b, s // tm),
        in_specs=[row(d), const((1, d)), vec, vec,
                  pl.BlockSpec((1, ROT_DIM, tm), lambda bi, i: (bi, 0, i)), const(place.shape), const(ones.shape),
                  const((d, n))],
        out_specs=[row(wd) for wd in widths],
        out_shape=[jax.ShapeDtypeStruct((b, s, wd), dt) for wd, dt in zip(widths, dtypes)],
        compiler_params=pltpu.CompilerParams(vmem_limit_bytes=VMEM_LIMIT),
        name="inproj",
    )(x, g, sc, sh, cos_sin, jnp.asarray(place, dtype=BF16), jnp.asarray(ones), w)


def _attn_kernel(sink_ref, cur_ref, prev_ref, o_ref):
    i = pl.program_id(1)
    blk = ATTN_BLOCK
    cur = cur_ref[0]
    prev = prev_ref[0]
    kv_all = jnp.concatenate([prev[:, ATTN_QW:], cur[:, ATTN_QW:]], axis=0)
    band_w = 2 * blk
    qi = lax.broadcasted_iota(jnp.int32, (blk, 2 * band_w), 0)
    kj = lax.broadcasted_iota(jnp.int32, (blk, 2 * band_w), 1) % band_w
    diff = qi + blk - kj
    band = (diff >= 0) & (diff < blk)
    band_first = band & (kj >= jnp.where(i == 0, blk, 0))
    first_head = lax.broadcasted_iota(jnp.int32, (blk, LANES), 1) < HEAD_DIM
    zeros = jnp.zeros((band_w, HEAD_DIM), BF16)

    def blockdiag(t):
        return jnp.concatenate([jnp.concatenate([t, zeros], axis=1), jnp.concatenate([zeros, t], axis=1)], axis=0)

    units = [(r, j, pr) for r in range(cur.shape[0] // blk) for j in range(ATTN_HKV) for pr in range(ATTN_GROUP // 2)]

    def scores(r, j, pr):
        kb = kv_all[r * blk:(r + 2) * blk, j * HEAD_DIM:(j + 1) * HEAD_DIM]
        slab = (j * (ATTN_GROUP // 2) + pr) * LANES
        return _dot_nt(cur[r * blk:(r + 1) * blk, slab:slab + LANES], blockdiag(kb))

    lane_row = lax.broadcasted_iota(jnp.int32, (1, 2 * band_w), 1)
    not_row0 = lax.broadcasted_iota(jnp.int32, (band_w, HEAD_DIM), 0) > 0
    def value_rows(r, j):
        vb = kv_all[r * blk:(r + 2) * blk, ATTN_KVW + j * HEAD_DIM:ATTN_KVW + (j + 1) * HEAD_DIM]
        return blockdiag(jnp.where(not_row0, vb, jnp.zeros_like(vb)))

    for g0 in range(0, len(units), ATTN_GROUP_UNITS):
        group = units[g0:g0 + ATTN_GROUP_UNITS]
        raw = [scores(*u) for u in group]
        sc = []
        for (r, j, pr), s_raw in zip(group, raw):
            hq = j * ATTN_GROUP + 2 * pr
            fill = jnp.where(lane_row == 0, sink_ref[hq], jnp.where(lane_row == band_w, sink_ref[hq + 1], -jnp.inf))
            sc.append(jnp.where(band_first if r == 0 else band, s_raw, fill))
        halves = [[s_[:, t * band_w:(t + 1) * band_w] for t in range(2)] for s_ in sc]
        mx = [[jnp.max(h_, axis=-1, keepdims=True) for h_ in hs] for hs in halves]
        p = [[jnp.exp(h_ - m_) for h_, m_ in zip(hs, ms)] for hs, ms in zip(halves, mx)]
        rs = [[1.0 / jnp.sum(p_, axis=-1, keepdims=True) for p_ in ps] for ps in p]
        p16 = [jnp.concatenate([p_.astype(BF16) for p_ in ps], axis=1) for ps in p]
        outs = [_dot(p16[n], value_rows(r, j)) for n, (r, j, pr) in enumerate(group)]
        for n, (r, j, pr) in enumerate(group):
            o = outs[n] * jnp.where(first_head, rs[n][0], rs[n][1])
            slab = (j * (ATTN_GROUP // 2) + pr) * LANES
            o_ref[0, r * blk:(r + 1) * blk, slab:slab + LANES] = o.astype(BF16)


def _attention(attn, sinks):
    b, s, _ = attn.shape
    tq = TQ_ATTN
    per = tq // ATTN_BLOCK
    return pl.pallas_call(
        _attn_kernel,
        grid=(b, s // tq),
        in_specs=[
            pl.BlockSpec(memory_space=pltpu.SMEM),
            pl.BlockSpec((1, tq, ATTN_W), lambda bi, i: (bi, i, 0)),
            pl.BlockSpec((1, ATTN_BLOCK, ATTN_W), lambda bi, i: (bi, jnp.maximum(i * per - 1, 0), 0)),
        ],
        out_specs=pl.BlockSpec((1, tq, ATTN_QW), lambda bi, i: (bi, i, 0)),
        out_shape=jax.ShapeDtypeStruct((b, s, ATTN_QW), BF16),
        compiler_params=pltpu.CompilerParams(vmem_limit_bytes=VMEM_LIMIT),
        name="attention",
    )(sinks, attn, attn)


def _deltanet_kernel(cur_ref, prev_ref, gate_ref, dz_ref, cw_ref, alog_ref, dtb_ref, ng_ref,
                     expand_ref, tril_ref, slow_ref, ones_ref, o_ref,
                     xext, q_s, k_s, v_s, b_s, g_s, o_s, state):
    i = pl.program_id(0)
    n_seq, rows = cur_ref.shape[0], cur_ref.shape[1]
    ck = DN_CHUNK

    @pl.when(i == 0)
    def _():
        state[...] = jnp.zeros_like(state)

    ones_blk = ones_ref[...]

    def l2n(t):
        ss = _dot((t * t).astype(BF16), ones_blk)
        return t * lax.rsqrt(ss + EPS)

    for bi in range(n_seq):
        seq = slice(bi * rows, (bi + 1) * rows)
        xext[bi, 0:DN_HALO, :] = jnp.where(i > 0, prev_ref[bi], 0.0)
        xext[bi, DN_HALO:DN_HALO + rows, :] = cur_ref[bi]
        acc = cw_ref[0:1, :] * xext[bi, pl.ds(DN_HALO - DN_CONV + 1, rows), :]
        for t in range(1, DN_CONV):
            acc = acc + cw_ref[t:t + 1, :] * xext[bi, pl.ds(DN_HALO - DN_CONV + 1 + t, rows), :]
        qkv = _silu(acc)
        q_s[seq, :] = l2n(qkv[:, 0:DN_W]) * (DN_D ** -0.5)
        k_s[seq, :] = l2n(qkv[:, DN_W:2 * DN_W])
        v_s[seq, :] = qkv[:, 2 * DN_W:3 * DN_W]
        ge = _dot(jnp.concatenate(_split3(gate_ref[bi]), axis=1), expand_ref[...])
        b_s[seq, :] = _sigmoid(ge[:, 0:DN_W])
        da = ge[:, DN_W:2 * DN_W] + dtb_ref[...]
        softplus = jnp.maximum(da, 0.0) + jnp.log1p(jnp.exp(-jnp.abs(da)))
        g_s[seq, :] = -jnp.exp(alog_ref[...]) * softplus

    tril = tril_ref[...]
    slow = slow_ref[...]
    ri = lax.broadcasted_iota(jnp.int32, (ck, DN_W), 0)
    ci = lax.broadcasted_iota(jnp.int32, (ck, DN_W), 1) % ck
    incl = ri >= ci

    pw = 2 * DN_D
    n_pair = DN_W // pw
    r2 = lax.broadcasted_iota(jnp.int32, (ck, pw), 0)
    l2 = lax.broadcasted_iota(jnp.int32, (ck, pw), 1)
    c2 = l2 % ck
    strict = r2 > c2
    eye = jnp.where(r2 == c2, 1.0, 0.0)
    first = l2 < DN_D
    masks = []
    blk = 1
    while blk < ck:
        masks.append((r2 // (2 * blk) == c2 // (2 * blk)) & ((r2 // blk) % 2 == 1) & ((c2 // blk) % 2 == 0))
        blk *= 2
    same_head = (lax.broadcasted_iota(jnp.int32, (pw, pw), 0) // DN_D) == (lax.broadcasted_iota(jnp.int32, (pw, pw), 1) // DN_D)

    def blockdiag(t):
        z = jnp.zeros_like(t)
        return jnp.concatenate([jnp.where(first, t, z), jnp.where(first, z, t)], axis=0)

    nc = n_seq * rows // ck
    per_seq = rows // ck
    slabs = [slice(p * pw, (p + 1) * pw) for p in range(n_pair)]
    kbeta16, k16, q16, dec, vbeta, kbg, qg, kg, egl = ([] for _ in range(9))
    for c in range(nc):
        r0 = c * ck
        q = q_s[r0:r0 + ck, :]
        k = k_s[r0:r0 + ck, :]
        beta = b_s[r0:r0 + ck, :]
        g = g_s[r0:r0 + ck, :]
        cums = _dot(tril, jnp.concatenate([jnp.concatenate(_split3(g), axis=0),
                                           jnp.concatenate(_split3(g * slow), axis=0)], axis=1))
        gc = cums[:, 0:DN_W]
        gdiff = cums[:, DN_W:]
        decay = jnp.exp(jnp.where(incl, gdiff, -jnp.inf))
        egc = jnp.exp(gc)
        glast = gc[ck - 1:ck, :]
        kbeta = k * beta
        per_slab = ((kbeta16, kbeta.astype(BF16)), (k16, k.astype(BF16)), (q16, q.astype(BF16)), (dec, decay),
                    (vbeta, (v_s[r0:r0 + ck, :] * beta).astype(BF16)), (kbg, (kbeta * egc).astype(BF16)),
                    (qg, q * egc), (kg, (k * jnp.exp(glast - gc)).astype(BF16)), (egl, jnp.exp(glast)))
        for dst, full in per_slab:
            dst.extend(full[:, sl] for sl in slabs)
    inst = range(nc * n_pair)
    bdk = [blockdiag(k16[n]) for n in inst]
    lower = [jnp.where(strict, _dot_nt(kbeta16[n], bdk[n]) * dec[n], 0.0) for n in inst]
    a_intra = [(_dot_nt(q16[n], bdk[n]) * dec[n]).astype(BF16) for n in inst]
    lower16 = [t.astype(BF16) for t in lower]
    tinv = [eye - jnp.where(masks[0], lower[n], 0.0) for n in inst]
    for m in masks[1:]:
        t16 = [t.astype(BF16) for t in tinv]
        x16 = [_dot(jnp.where(m, lower16[n], jnp.zeros_like(lower16[n])), blockdiag(t16[n])).astype(BF16) for n in inst]
        tinv = [tinv[n] - _dot(t16[n], blockdiag(x16[n])) for n in inst]
    t16 = [t.astype(BF16) for t in tinv]
    w16 = [_dot(t16[n], blockdiag(kbg[n])).astype(BF16) for n in inst]
    u16 = [_dot(t16[n], blockdiag(vbeta[n])).astype(BF16) for n in inst]
    kw16 = [jnp.where(same_head, _dot_tn(kg[n], w16[n]), 0.0).astype(BF16) for n in inst]
    ku = [jnp.where(same_head, _dot_tn(kg[n], u16[n]), 0.0) for n in inst]
    qeff16 = [(qg[n] - _dot(a_intra[n], blockdiag(w16[n]))).astype(BF16) for n in inst]
    au = [_dot(a_intra[n], blockdiag(u16[n])) for n in inst]
    st = [[state[bi, p] for p in range(n_pair)] for bi in range(n_seq)]
    for c in range(per_seq):
        for bi in range(n_seq):
            outs = []
            for p in range(n_pair):
                n = (bi * per_seq + c) * n_pair + p
                s16 = st[bi][p].astype(BF16)
                outs.append(_dot(qeff16[n], s16) + au[n])
                st[bi][p] = st[bi][p] * egl[n] - _dot(kw16[n], s16) + ku[n]
            r0 = (bi * per_seq + c) * ck
            o_s[r0:r0 + ck, :] = jnp.concatenate(outs, axis=1)
    for bi in range(n_seq):
        for p in range(n_pair):
            state[bi, p] = st[bi][p]
    for bi in range(n_seq):
        o = o_s[bi * rows:(bi + 1) * rows, :]
        ms = _dot((o * o).astype(BF16), ones_blk) * (1.0 / DN_D)
        y = o * lax.rsqrt(ms + EPS) * ng_ref[...]
        o_ref[bi] = (y * _silu(dz_ref[bi])).astype(BF16)


def _deltanet(dnqkv, gate, dz, conv_w, a_log, dt_bias, norm_g):
    b, s, _ = dnqkv.shape
    rows = R_DN
    ck = DN_CHUNK
    lane_head = np.arange(2 * DN_W) // DN_D
    expand = np.tile((np.arange(LANES)[:, None] == lane_head[None, :]).astype(np.float32), (3, 1))
    tril = np.tile(np.tril(np.ones((ck, ck), np.float32)), (1, 3))
    slow = (np.arange(ck)[:, None] > (np.arange(DN_W)[None, :] % ck)).astype(np.float32)
    ones_blk = (np.arange(DN_W)[:, None] // DN_D == np.arange(DN_W)[None, :] // DN_D).astype(np.float32)
    rep = lambda t: jnp.repeat(t.astype(F32), DN_D).reshape(1, DN_W)
    const = lambda shape: pl.BlockSpec(shape, lambda i: (0,) * len(shape))
    row = lambda width: pl.BlockSpec((b, rows, width), lambda i: (0, i, 0))
    per = rows // DN_HALO
    return pl.pallas_call(
        _deltanet_kernel,
        grid=(s // rows,),
        in_specs=[
            row(3 * DN_W),
            pl.BlockSpec((b, DN_HALO, 3 * DN_W), lambda i: (0, jnp.maximum(i * per - 1, 0), 0)),
            row(LANES), row(DN_W),
            const((DN_CONV, 3 * DN_W)), const((1, DN_W)), const((1, DN_W)), const((1, DN_W)),
            const((3 * LANES, 2 * DN_W)), const((ck, 3 * ck)), const((ck, DN_W)), const((DN_W, DN_W)),
        ],
        out_specs=row(DN_W),
        out_shape=jax.ShapeDtypeStruct((b, s, DN_W), BF16),
        scratch_shapes=[
            pltpu.VMEM((b, rows + DN_HALO, 3 * DN_W), F32),
            *[pltpu.VMEM((b * rows, DN_W), F32) for _ in range(6)],
            pltpu.VMEM((b, DN_W // (2 * DN_D), 2 * DN_D, 2 * DN_D), F32),
        ],
        compiler_params=pltpu.CompilerParams(vmem_limit_bytes=VMEM_LIMIT, dimension_semantics=("arbitrary",)),
        name="deltanet",
    )(dnqkv, dnqkv, gate, dz, conv_w, rep(a_log), rep(dt_bias), jnp.tile(norm_g.astype(F32), DN_H).reshape(1, DN_W),
      jnp.asarray(expand, dtype=BF16), jnp.asarray(tril, dtype=BF16), jnp.asarray(slow), jnp.asarray(ones_blk, dtype=BF16))


def _conformer_kernel(cur_ref, prev_ref, w_ref, b_ref, lg_ref, lb_ref, o_ref, u_s, sh_s):
    i = pl.program_id(1)
    rows = cur_ref.shape[1]
    prev = jnp.where(i > 0, prev_ref[0], 0.0)
    u_s[0:CV_HALO, :] = prev[:, 0:CV_C] * _sigmoid(prev[:, CV_C:])
    cur = cur_ref[0]
    u_s[CV_HALO:CV_HALO + rows, :] = cur[:, 0:CV_C] * _sigmoid(cur[:, CV_C:])
    span = rows + CV_HALO - SUBLANES
    for ph in range(1, SUBLANES):
        sh_s[ph - 1, 0:span, :] = u_s[pl.ds(ph, span), :]
    base = CV_HALO - CV_K + 1
    acc = b_ref[...]
    for t in range(CV_K):
        blk, ph = divmod(base + t, SUBLANES)
        src = u_s if ph == 0 else sh_s.at[ph - 1]
        acc = acc + w_ref[t:t + 1, :] * src[blk * SUBLANES:blk * SUBLANES + rows, :]
    mu = jnp.mean(acc, axis=-1, keepdims=True)
    xc = acc - mu
    y = xc * lax.rsqrt(jnp.mean(xc * xc, axis=-1, keepdims=True) + EPS)
    o_ref[0] = _silu(y * lg_ref[...] + lb_ref[...]).astype(BF16)


def _conformer(cv, w, bias, ln_g, ln_b):
    b, s, _ = cv.shape
    rows = R_CV
    per = rows // CV_HALO
    const = lambda shape: pl.BlockSpec(shape, lambda bi, i: (0,) * len(shape))
    return pl.pallas_call(
        _conformer_kernel,
        grid=(b, s // rows),
        in_specs=[
            pl.BlockSpec((1, rows, 2 * CV_C), lambda bi, i: (bi, i, 0)),
            pl.BlockSpec((1, CV_HALO, 2 * CV_C), lambda bi, i: (bi, jnp.maximum(i * per - 1, 0), 0)),
            const((CV_K, CV_C)), const((1, CV_C)), const((1, CV_C)), const((1, CV_C)),
        ],
        out_specs=pl.BlockSpec((1, rows, CV_C), lambda bi, i: (bi, i, 0)),
        out_shape=jax.ShapeDtypeStruct((b, s, CV_C), BF16),
        scratch_shapes=[pltpu.VMEM((rows + CV_HALO, CV_C), F32),
                        pltpu.VMEM((SUBLANES - 1, rows + CV_HALO - SUBLANES, CV_C), F32)],
        compiler_params=pltpu.CompilerParams(vmem_limit_bytes=VMEM_LIMIT),
        name="conformer",
    )(cv, cv, w, bias.reshape(1, CV_C), ln_g.reshape(1, CV_C), ln_b.reshape(1, CV_C))


def _mix_residual(x_ref, ya_ref, yd_ref, yc_ref, wa_ref, wd_ref, wc_ref, gt1_ref):
    y = _dot(ya_ref[...], wa_ref[...]) + _dot(yd_ref[...], wd_ref[...]) + _dot(yc_ref[...], wc_ref[...])
    return x_ref[...] + gt1_ref[0] * y


def _finish(x1, acc, gt2_ref, fg_ref, o_ref, final_norm):
    x2 = x1 + gt2_ref[0] * acc
    if final_norm:
        x2 = x2 * lax.rsqrt(jnp.mean(x2 * x2, axis=-1, keepdims=True) + EPS) * fg_ref[...]
    o_ref[...] = x2


def _dense_ffn_kernel(x_ref, ya_ref, yd_ref, yc_ref, wa_ref, wd_ref, wc_ref, gt1_ref, g_ref, sc_ref, sh_ref, gt2_ref,
                      fg_ref, wg_ref, wu_ref, wdn_ref, o_ref, *, final_norm):
    x1 = _mix_residual(x_ref, ya_ref, yd_ref, yc_ref, wa_ref, wd_ref, wc_ref, gt1_ref)
    h = _rms_modulate(x1, g_ref[...], sc_ref[0], sh_ref[0]).astype(BF16)
    ff = wg_ref.shape[1]
    acc = None
    for c0 in range(0, ff, FF_PIECE):
        c1 = min(c0 + FF_PIECE, ff)
        a = (_silu(_dot(h, wg_ref[:, c0:c1])) * _dot(h, wu_ref[:, c0:c1])).astype(BF16)
        part = _dot(a, wdn_ref[c0:c1, :])
        acc = part if acc is None else acc + part
    _finish(x1, acc, gt2_ref, fg_ref, o_ref, final_norm)


def _route_kernel(x_ref, ya_ref, yd_ref, yc_ref, wa_ref, wd_ref, wc_ref, gt1_ref, g_ref, sc_ref, sh_ref,
                  rwt_ref, rb_ref, upper_ref, x1_ref, h_ref, comb_ref, pos_ref, cnt_ref):
    subs = [slice(n * MOE_SUB, (n + 1) * MOE_SUB) for n in range(x_ref.shape[0] // MOE_SUB)]
    wa, wd, wc = wa_ref[...], wd_ref[...], wc_ref[...]
    y = [_dot(ya_ref[sl, :], wa) + _dot(yd_ref[sl, :], wd) + _dot(yc_ref[sl, :], wc) for sl in subs]
    x1 = [x_ref[sl, :] + gt1_ref[0] * y_ for sl, y_ in zip(subs, y)]
    h = [_rms_modulate(x_, g_ref[...], sc_ref[0], sh_ref[0]) for x_ in x1]
    for sl, x_, h_ in zip(subs, x1, h):
        x1_ref[sl, :] = x_
        h_ref[sl, :] = h_.astype(BF16)
    w_hi, w_mid, _ = _split3(rwt_ref[...])
    w_cat = jnp.concatenate([w_hi, w_hi, w_mid], axis=1)
    pieces = [_split3(h_) for h_ in h]
    logits = [_dot_nt(w_cat, jnp.concatenate([hi, mid, hi], axis=1)) + rb_ref[...] for hi, mid, _ in pieces]
    row = lax.broadcasted_iota(jnp.int32, logits[0].shape, 0)
    lane = lax.broadcasted_iota(jnp.int32, (N_EXPERTS, LANES), 1)
    cnt = jnp.zeros((N_EXPERTS, LANES), F32)
    for n, (sl, lg) in enumerate(zip(subs, logits)):
        m1 = jnp.max(lg, axis=0, keepdims=True)
        i1 = jnp.min(jnp.where(lg == m1, row, N_EXPERTS), axis=0, keepdims=True)
        rest = jnp.where(row == i1, -jnp.inf, lg)
        m2 = jnp.max(rest, axis=0, keepdims=True)
        i2 = jnp.min(jnp.where(rest == m2, row, N_EXPERTS), axis=0, keepdims=True)
        e2 = jnp.exp(m2 - m1)
        comb_ref[:, sl] = jnp.where(row == i1, 1.0 / (1.0 + e2), 0.0) + jnp.where(row == i2, e2 / (1.0 + e2), 0.0)
        sel = jnp.where((row == i1) | (row == i2), 1.0, 0.0)
        rank = _dot(sel.astype(BF16), upper_ref[...])
        pos_ref[:, sl] = jnp.where(sel > 0.0, rank, -1.0)
        cnt = cnt + jnp.where(lane == n, jnp.sum(sel, axis=1, keepdims=True), 0.0)
    cnt_ref[0] = cnt.astype(jnp.int32)


def _route(x, ys, w_out, mods, g, rwt, rb, s):
    t, d = x.shape
    tm = TM_FFN
    gt1, sc, sh, _ = mods
    tok = lambda width: pl.BlockSpec((tm, width), lambda i: (i, 0))
    const = lambda shape: pl.BlockSpec(shape, lambda i: (0,) * len(shape))
    vec = pl.BlockSpec((1, 1, d), lambda i: ((i * tm) // s, 0, 0))
    col = pl.BlockSpec((N_EXPERTS, tm), lambda i: (0, i))
    upper = np.triu(np.ones((MOE_SUB, MOE_SUB), np.float32), 1)
    return pl.pallas_call(
        _route_kernel,
        grid=(t // tm,),
        in_specs=[tok(d), tok(ATTN_QW), tok(DN_W), tok(CV_C), const((ATTN_QW, d)), const((DN_W, d)), const((CV_C, d)),
                  vec, const((1, d)), vec, vec, const((N_EXPERTS, d)), const((N_EXPERTS, 1)), const((MOE_SUB, MOE_SUB))],
        out_specs=[tok(d), tok(d), col, col, pl.BlockSpec((1, N_EXPERTS, LANES), lambda i: (i, 0, 0))],
        out_shape=[jax.ShapeDtypeStruct((t, d), F32), jax.ShapeDtypeStruct((t, d), BF16),
                   jax.ShapeDtypeStruct((N_EXPERTS, t), F32), jax.ShapeDtypeStruct((N_EXPERTS, t), F32),
                   jax.ShapeDtypeStruct((t // tm, N_EXPERTS, LANES), jnp.int32)],
        compiler_params=pltpu.CompilerParams(vmem_limit_bytes=VMEM_LIMIT),
        name="route",
    )(x, *ys, *w_out, gt1, g, sc, sh, rwt, rb, jnp.asarray(upper, dtype=BF16))


def _moe_sparse_kernel(cnt_ref, off_ref, tot_ref, base_ref, h_ref, x1_hbm, comb_ref, pos_ref, gt2_ref, fg_ref,
                       wg_hbm, wu_hbm, wdn_hbm, o_hbm, xg_s, wrow_s, yw_s, w_s, w_sem, acc_s, x1_s, out_s, io_sem,
                       *, final_norm):
    i = pl.program_id(0)
    e = pl.program_id(1)
    n_e = pl.num_programs(1)
    n_sub = h_ref.shape[0] // MOE_SUB
    d = h_ref.shape[1]
    win_row = lax.broadcasted_iota(jnp.int32, (MOE_WIN, MOE_SUB), 0)

    step = i * n_e + e
    n_steps = pl.num_programs(0) * n_e

    def weight_copies(expert, slot):
        return [pltpu.make_async_copy(src.at[expert], w_s.at[slot, k], w_sem.at[slot, k])
                for k, src in enumerate((wg_hbm, wu_hbm, wdn_hbm))]

    @pl.when(step == 0)
    def _():
        for ahead in range(W_SLOTS - 1):
            for cp in weight_copies(ahead % N_EXPERTS, ahead):
                cp.start()

    @pl.when(step + (W_SLOTS - 1) < n_steps)
    def _():
        for cp in weight_copies(lax.rem(e + (W_SLOTS - 1), n_e), lax.rem(step + (W_SLOTS - 1), W_SLOTS)):
            cp.start()

    slot = lax.rem(step, W_SLOTS)
    for cp in weight_copies(e, slot):
        cp.wait()
    wg_ref, wu_ref, wdn_ref = (w_s.at[slot, k] for k in range(3))

    def gather(sub, p):
        k = (i * n_sub + sub) * N_EXPERTS + e
        lanes = slice(sub * MOE_SUB, (sub + 1) * MOE_SUB)
        pos = pos_ref[pl.ds(e, 1), lanes].astype(jnp.int32)
        sel = win_row + p * MOE_WIN == pos
        r0 = pl.multiple_of(off_ref[k] + p * MOE_WIN, PACK_ROWS)
        rows = _dot(jnp.where(sel, 1.0, 0.0).astype(BF16), h_ref[lanes, :])
        xg_s[pl.ds(r0, MOE_WIN), :] = rows.astype(BF16)
        w = jnp.sum(jnp.where(sel, comb_ref[pl.ds(e, 1), lanes], 0.0), axis=1, keepdims=True)
        wrow_s[pl.ds(r0, MOE_WIN), :] = jnp.broadcast_to(w, (MOE_WIN, LANES))

    most = cnt_ref[i * n_sub * N_EXPERTS + e]
    for sub in range(1, n_sub):
        most = jnp.maximum(most, cnt_ref[(i * n_sub + sub) * N_EXPERTS + e])
    for p in range(MOE_SUB // MOE_WIN - 1, 0, -1):
        @pl.when(most > p * MOE_WIN)
        def _():
            for sub in range(n_sub):
                @pl.when(cnt_ref[(i * n_sub + sub) * N_EXPERTS + e] > p * MOE_WIN)
                def _():
                    gather(sub, p)
    for sub in range(n_sub):
        gather(sub, 0)

    total = pl.multiple_of(tot_ref[i * N_EXPERTS + e], PACK_ROWS)
    base = pl.multiple_of(base_ref[i * N_EXPERTS + e], PACK_ROWS)
    xg_s[pl.ds(total, MOE_TAIL), :] = jnp.zeros((MOE_TAIL, d), BF16)
    wrow_s[pl.ds(total, MOE_TAIL), :] = jnp.zeros((MOE_TAIL, LANES), F32)

    def ffn(r0, rows):
        xc = xg_s[pl.ds(r0, rows), :]
        a = (_silu(_dot(xc, wg_ref[...])) * _dot(xc, wu_ref[...])).astype(BF16)
        y = _dot(a, wdn_ref[...]) * jnp.tile(wrow_s[pl.ds(r0, rows), :], (1, d // LANES))
        yw_s[pl.ds(base + r0, rows), :] = y.astype(BF16)

    n_full = total // MOE_CHUNK
    rem = total - n_full * MOE_CHUNK
    n_chunks = n_full + jnp.where(rem > MOE_CHUNK // 2, 1, 0)
    ffn(0, MOE_CHUNK)

    def full_chunk(c, carry):
        ffn(pl.multiple_of(c * MOE_CHUNK, MOE_CHUNK), MOE_CHUNK)
        return carry

    lax.fori_loop(1, n_chunks, full_chunk, 0)
    half = (rem > 0) & (rem <= MOE_CHUNK // 2) & (n_full > 0)

    @pl.when(half)
    def _():
        ffn(pl.multiple_of(n_full * MOE_CHUNK, MOE_CHUNK), MOE_CHUNK // 2)

    covered = jnp.maximum(n_chunks, 1) * MOE_CHUNK + jnp.where(half, MOE_CHUNK // 2, 0)
    yw_s[pl.ds(pl.multiple_of(base + covered, PACK_ROWS), MOE_TAIL), :] = jnp.zeros((MOE_TAIL, d), BF16)

    @pl.when(e == n_e - 1)
    def _():
        row0 = i * (n_sub * MOE_SUB)

        def rows_of(ref, sub):
            return ref.at[pl.ds(pl.multiple_of(row0 + sub * MOE_SUB, MOE_SUB), MOE_SUB), :]

        def x1_copy(sub):
            return pltpu.make_async_copy(rows_of(x1_hbm, sub), x1_s.at[sub % 2], io_sem.at[0, sub % 2])

        def out_copy(sub):
            return pltpu.make_async_copy(out_s.at[sub % 2], rows_of(o_hbm, sub), io_sem.at[1, sub % 2])

        x1_copy(0).start()
        for sub in range(n_sub):
            if sub + 1 < n_sub:
                x1_copy(sub + 1).start()
            lanes = slice(sub * MOE_SUB, (sub + 1) * MOE_SUB)
            first = (i * n_sub + sub) * N_EXPERTS

            def windows(p):
                sels, rows = [], []
                for ex in range(N_EXPERTS):
                    r0 = pl.multiple_of(base_ref[i * N_EXPERTS + ex] + off_ref[first + ex] + p * MOE_WIN, PACK_ROWS)
                    pos = pos_ref[ex:ex + 1, lanes].astype(jnp.int32)
                    sels.append(jnp.where(win_row + p * MOE_WIN == pos, 1.0, 0.0).astype(BF16))
                    rows.append(yw_s[pl.ds(r0, MOE_WIN), :])
                return _dot_tn(jnp.concatenate(sels, axis=0), jnp.concatenate(rows, axis=0))

            acc_s[...] = windows(0)
            most = cnt_ref[first]
            for ex in range(1, N_EXPERTS):
                most = jnp.maximum(most, cnt_ref[first + ex])
            for p in range(1, MOE_SUB // MOE_WIN):
                @pl.when(most > p * MOE_WIN)
                def _():
                    acc_s[...] += windows(p)
            x1_copy(sub).wait()
            x2 = x1_s[sub % 2] + gt2_ref[0] * acc_s[...]
            if final_norm:
                x2 = x2 * lax.rsqrt(jnp.mean(x2 * x2, axis=-1, keepdims=True) + EPS) * fg_ref[...]
            if sub >= 2:
                out_copy(sub - 2).wait()
            out_s[sub % 2] = x2
            out_copy(sub).start()
        for sub in range(max(n_sub - 2, 0), n_sub):
            out_copy(sub).wait()


def _dense_ffn(x, ys, w_out, mods, g, final_g, wg, wu, wdn, s, final_norm):
    t, d = x.shape
    tm = TM_FFN
    ff = wg.shape[1]
    gt1, sc, sh, gt2 = mods
    tok = lambda width: pl.BlockSpec((tm, width), lambda i: (i, 0))
    const = lambda shape: pl.BlockSpec(shape, lambda i: (0,) * len(shape), pipeline_mode=pl.Buffered(1))
    vec = pl.BlockSpec((1, 1, d), lambda i: ((i * tm) // s, 0, 0))
    return pl.pallas_call(
        functools.partial(_dense_ffn_kernel, final_norm=final_norm),
        grid=(t // tm,),
        in_specs=[tok(d), tok(ATTN_QW), tok(DN_W), tok(CV_C), const((ATTN_QW, d)), const((DN_W, d)), const((CV_C, d)),
                  vec, const((1, d)), vec, vec, vec, const((1, d)),
                  const((d, ff)), const((d, ff)), const((ff, d))],
        out_specs=tok(d),
        out_shape=jax.ShapeDtypeStruct((t, d), F32),
        compiler_params=pltpu.CompilerParams(vmem_limit_bytes=VMEM_LIMIT),
        name="dense_ffn",
    )(x, *ys, *w_out, gt1, g, sc, sh, gt2, final_g, wg, wu, wdn)


def _moe_sparse(h, x1, comb, pos, cnt, gt2, final_g, wg, wu, wdn, s, final_norm):
    t, d = x1.shape
    ts = TS_MOE
    n_e, _, ffe = wg.shape
    n_sub = ts // MOE_SUB
    counts = cnt[:, :, :TM_FFN // MOE_SUB].transpose(0, 2, 1).reshape(t // ts, n_sub, n_e)
    padded = -(-counts // PACK_ROWS) * PACK_ROWS
    offs = jnp.cumsum(padded, axis=1) - padded
    tots = jnp.sum(padded, axis=1)
    bases = jnp.cumsum(tots, axis=1) - tots
    cap_one = -(-(ts + n_sub * PACK_ROWS + MOE_TAIL) // PACK_ROWS) * PACK_ROWS
    cap_all = -(-(N_TOP * ts + n_e * n_sub * PACK_ROWS + MOE_CHUNK + MOE_TAIL) // PACK_ROWS) * PACK_ROWS
    tok = lambda width: pl.BlockSpec((ts, width), lambda i, e, *_: (i, 0))
    col = pl.BlockSpec((n_e, ts), lambda i, e, *_: (0, i))
    hbm = pl.BlockSpec(memory_space=pl.ANY)
    assert ffe == d, "the weight ring holds the three expert matrices in one (3, d, ffe) slot"
    flat = lambda a: a.reshape(-1).astype(jnp.int32)
    grid_spec = pltpu.PrefetchScalarGridSpec(
        num_scalar_prefetch=4,
        grid=(t // ts, n_e),
        in_specs=[tok(d), hbm, col, col,
                  pl.BlockSpec((1, 1, d), lambda i, e, *_: ((i * ts) // s, 0, 0)),
                  pl.BlockSpec((1, d), lambda i, e, *_: (0, 0)),
                  hbm, hbm, hbm],
        out_specs=hbm,
        scratch_shapes=[pltpu.VMEM((cap_one, d), BF16), pltpu.VMEM((cap_one, LANES), F32),
                        pltpu.VMEM((cap_all, d), BF16),
                        pltpu.VMEM((W_SLOTS, 3, d, ffe), BF16), pltpu.SemaphoreType.DMA((W_SLOTS, 3)),
                        pltpu.VMEM((MOE_SUB, d), F32), pltpu.VMEM((2, MOE_SUB, d), F32), pltpu.VMEM((2, MOE_SUB, d), F32),
                        pltpu.SemaphoreType.DMA((2, 2))],
    )
    return pl.pallas_call(
        functools.partial(_moe_sparse_kernel, final_norm=final_norm),
        grid_spec=grid_spec,
        out_shape=jax.ShapeDtypeStruct((t, d), F32),
        compiler_params=pltpu.CompilerParams(vmem_limit_bytes=VMEM_LIMIT_MOE,
                                             dimension_semantics=("arbitrary", "arbitrary")),
        name="moe_sparse",
    )(flat(counts), flat(offs), flat(tots), flat(bases), h, x1, comb, pos, gt2, final_g, wg, wu, wdn)


def _rope_cos_sin(positions):
    inv_freq = ROPE_THETA ** (-jnp.arange(0, ROT_DIM, 2, dtype=F32) / ROT_DIM)
    ang = positions.astype(F32)[:, None, :] * inv_freq[None, :, None]
    return jnp.concatenate([jnp.cos(ang), jnp.sin(ang)], axis=1)


def _pack_w_in(w):
    d = w.shape[0]
    o1 = ATTN_W
    o2 = o1 + 3 * DN_W
    o3 = o2 + 2 * DN_H
    o4 = o3 + DN_W
    gates = jnp.zeros((d, LANES), w.dtype).at[:, :2 * DN_H].set(w[:, o2:o3])
    return jnp.concatenate([w[:, :o2], gates, w[:, o3:o4], w[:, o4:]], axis=1).astype(BF16)


def kernel(x, c, positions, ada_w, ada_b, norm_mix_g, norm_ffn_g, w_in, attn_sinks, dn_conv_w, dn_a_log, dn_dt_bias, dn_norm_g, cv_dw_w, cv_dw_b, cv_ln_g, cv_ln_b, w_out, ffn_w_gate, ffn_w_up, ffn_w_down, router_w, router_b, moe_w_gate, moe_w_up, moe_w_down, final_norm_g):
    b, s, d = x.shape
    depth = w_in.shape[0]
    mod = _adaln(c, ada_w, ada_b)
    cos_sin = _rope_cos_sin(positions)
    final_g = final_norm_g.reshape(1, d)
    vec = lambda t: t.reshape(b, 1, d)
    for l in range(depth):
        sh1, sc1, gt1 = (vec(t) for t in jnp.split(mod[l, 0], 3, axis=-1))
        sh2, sc2, gt2 = (vec(t) for t in jnp.split(mod[l, 1], 3, axis=-1))
        attn, dnqkv, gate, dz, cv = _inproj(x, norm_mix_g[l].reshape(1, d), sc1, sh1, cos_sin, _pack_w_in(w_in[l]))
        y_attn = _attention(attn, attn_sinks[l].astype(F32))
        y_dn = _deltanet(dnqkv, gate, dz, dn_conv_w[l], dn_a_log[l], dn_dt_bias[l], dn_norm_g[l])
        y_cv = _conformer(cv, cv_dw_w[l], cv_dw_b[l], cv_ln_g[l], cv_ln_b[l])
        t = b * s
        ys = (y_attn.reshape(t, ATTN_QW), y_dn.reshape(t, DN_W), y_cv.reshape(t, CV_C))
        wo = w_out[l].astype(BF16)
        wos = (wo[:ATTN_QW], wo[ATTN_QW:ATTN_QW + DN_W], wo[ATTN_QW + DN_W:])
        mods = (gt1, sc2, sh2, gt2)
        g2 = norm_ffn_g[l].reshape(1, d)
        last = l == depth - 1
        j = l // 2
        if l % 2 == 0:
            x2 = _dense_ffn(x.reshape(t, d), ys, wos, mods, g2, final_g, ffn_w_gate[j].astype(BF16),
                            ffn_w_up[j].astype(BF16), ffn_w_down[j].astype(BF16), s, last)
        else:
            x1, h2, comb, pos, cnt = _route(x.reshape(t, d), ys, wos, mods, g2, router_w[j].astype(F32).T,
                                            router_b[j].astype(F32).reshape(N_EXPERTS, 1), s)
            x2 = _moe_sparse(h2, x1, comb, pos, cnt, gt2, final_g, moe_w_gate[j].astype(BF16),
                             moe_w_up[j].astype(BF16), moe_w_down[j].astype(BF16), s, last)
        x = x2.reshape(b, s, d)
    return x
```

```python
import functools

import numpy as np
import jax
import jax.numpy as jnp
from jax import lax
from jax.experimental import pallas as pl
from jax.experimental.pallas import tpu as pltpu

F32 = jnp.float32
BF16 = jnp.bfloat16
HIGHEST = lax.Precision.HIGHEST

HEAD_DIM = 64
ATTN_HQ = 8
ATTN_HKV = 2
ATTN_GROUP = ATTN_HQ // ATTN_HKV
ATTN_BLOCK = 128
ROT_DIM = HEAD_DIM // 4
ROPE_THETA = 500000.0
DN_H = 4
DN_D = 64
DN_CONV = 4
DN_CHUNK = 64
CV_C = 256
CV_K = 31
ATTN_QW = ATTN_HQ * HEAD_DIM
ATTN_KVW = ATTN_HKV * HEAD_DIM
ATTN_W = ATTN_QW + 2 * ATTN_KVW
DN_W = DN_H * DN_D
N_EXPERTS = 8
EPS = 1e-6
LANES = 128
SUBLANES = 8
PACK_ROWS = 16
VMEM_LIMIT = 48 * 1024 * 1024
VMEM_LIMIT_MOE = 56 * 1024 * 1024

ADA_TN = 1024
TM_PROJ = 1024
TQ_ATTN = 1024
ATTN_GROUP_UNITS = 8
R_DN = 512
R_CV = 1024
TM_FFN = 512
TM_ROUTE = 1024
FF_PIECE = 1536
CV_HALO = 32
DN_HALO = 8
TS_MOE = 2048
MOE_SUB = 256
MOE_WIN = 128
MOE_CHUNK = 256
MOE_TAIL = 384
N_TOP = 2
W_SLOTS = 3


def _dot(a, b, precision=None):
    return jnp.dot(a, b, preferred_element_type=F32, precision=precision)


def _dot_nt(a, b):
    return lax.dot_general(a, b, (((1,), (1,)), ((), ())), preferred_element_type=F32)


def _dot_tn(a, b):
    return lax.dot_general(a, b, (((0,), (0,)), ((), ())), preferred_element_type=F32)


def _split3(x):
    hi = x.astype(BF16)
    rest = x - hi.astype(F32)
    mid = rest.astype(BF16)
    return hi, mid, (rest - mid.astype(F32)).astype(BF16)


def _sigmoid(x):
    return 1.0 / (1.0 + jnp.exp(-x))


def _silu(x):
    return x * _sigmoid(x)


def _rms_modulate(x, g, sc, sh):
    y = x * lax.rsqrt(jnp.mean(x * x, axis=-1, keepdims=True) + EPS)
    return (y * g) * (1.0 + sc) + sh


def _adaln_kernel(c_ref, w_ref, b_ref, o_ref):
    o_ref[0] = _dot(_silu(c_ref[...]), w_ref[0], HIGHEST) + b_ref[0]


def _adaln(c, ada_w, ada_b):
    depth, _, d, d3 = ada_w.shape
    b = c.shape[0]
    rows = SUBLANES
    cp = jnp.zeros((rows, d), F32).at[:b].set(c)
    w = ada_w.reshape(depth * 2, d, d3)
    bias = ada_b.reshape(depth * 2, 1, d3)
    tn = ADA_TN
    out = pl.pallas_call(
        _adaln_kernel,
        grid=(depth * 2, d3 // tn),
        in_specs=[
            pl.BlockSpec((rows, d), lambda i, j: (0, 0)),
            pl.BlockSpec((1, d, tn), lambda i, j: (i, 0, j)),
            pl.BlockSpec((1, 1, tn), lambda i, j: (i, 0, j)),
        ],
        out_specs=pl.BlockSpec((1, rows, tn), lambda i, j: (i, 0, j)),
        out_shape=jax.ShapeDtypeStruct((depth * 2, rows, d3), F32),
        compiler_params=pltpu.CompilerParams(vmem_limit_bytes=VMEM_LIMIT),
        name="adaln",
    )(cp, w, bias)
    return out[:, :b].reshape(depth, 2, b, d3)


def _inproj_kernel(x_ref, g_ref, sc_ref, sh_ref, cs_ref, place_ref, ones_ref, w_ref,
                   attn_ref, dnqkv_ref, gate_ref, dz_ref, cv_ref):
    h = _rms_modulate(x_ref[0], g_ref[...], sc_ref[0], sh_ref[0]).astype(BF16)
    a = _dot(h, w_ref[:, 0:ATTN_W])
    tab = _dot_tn(jnp.concatenate(_split3(cs_ref[0]), axis=0), place_ref[...]) + ones_ref[...]
    rc, ra, rb = tab[:, 0:LANES], tab[:, LANES:2 * LANES], tab[:, 2 * LANES:3 * LANES]
    half = ROT_DIM // 2
    for j in range((ATTN_QW + ATTN_KVW) // LANES):
        t = a[:, j * LANES:(j + 1) * LANES]
        r = t * rc + pltpu.roll(t, LANES - half, 1) * ra + pltpu.roll(t, half, 1) * rb
        if j < ATTN_QW // LANES:
            r = r * (HEAD_DIM ** -0.5)
        attn_ref[0, :, j * LANES:(j + 1) * LANES] = r.astype(BF16)
    attn_ref[0, :, ATTN_QW + ATTN_KVW:ATTN_W] = a[:, ATTN_QW + ATTN_KVW:ATTN_W].astype(BF16)
    o = ATTN_W
    dnqkv_ref[0] = _dot(h, w_ref[:, o:o + 3 * DN_W])
    o += 3 * DN_W
    gate_ref[0] = _dot(h, w_ref[:, o:o + LANES])
    o += LANES
    dz_ref[0] = _dot(h, w_ref[:, o:o + DN_W])
    o += DN_W
    cv_ref[0] = _dot(h, w_ref[:, o:o + 2 * CV_C])


def _rope_placement():
    half = ROT_DIM // 2
    place = np.zeros((2 * half, 3 * LANES), np.float32)
    ones = np.zeros((1, 3 * LANES), np.float32)
    for lane in range(LANES):
        dim = lane % HEAD_DIM
        if dim < ROT_DIM:
            place[dim % half, lane] = 1.0
            if dim < half:
                place[half + dim, LANES + lane] = -1.0
            else:
                place[half + dim - half, 2 * LANES + lane] = 1.0
        else:
            ones[0, lane] = 1.0
    return np.tile(place, (3, 1)), ones


def _inproj(x, g, sc, sh, cos_sin, w):
    b, s, d = x.shape
    tm = TM_PROJ
    n = w.shape[1]
    row = lambda width: pl.BlockSpec((1, tm, width), lambda bi, i: (bi, i, 0))
    vec = pl.BlockSpec((1, 1, d), lambda bi, i: (bi, 0, 0))
    const = lambda shape: pl.BlockSpec(shape, lambda bi, i: (0,) * len(shape))
    widths = (ATTN_W, 3 * DN_W, LANES, DN_W, 2 * CV_C)
    dtypes = (BF16, F32, F32, F32, F32)
    place, ones = _rope_placement()
    return pl.pallas_call(
        _inproj_kernel,
        grid=(b, s // tm),
        in_specs=[row(d), const((1, d)), vec, vec,
                  pl.BlockSpec((1, ROT_DIM, tm), lambda bi, i: (bi, 0, i)), const(place.shape), const(ones.shape),
                  const((d, n))],
        out_specs=[row(wd) for wd in widths],
        out_shape=[jax.ShapeDtypeStruct((b, s, wd), dt) for wd, dt in zip(widths, dtypes)],
        compiler_params=pltpu.CompilerParams(vmem_limit_bytes=VMEM_LIMIT),
        name="inproj",
    )(x, g, sc, sh, cos_sin, jnp.asarray(place, dtype=BF16), jnp.asarray(ones), w)


def _attn_kernel(sink_ref, cur_ref, prev_ref, o_ref):
    i = pl.program_id(1)
    blk = ATTN_BLOCK
    cur = cur_ref[0]
    prev = prev_ref[0]
    kv_all = jnp.concatenate([prev[:, ATTN_QW:], cur[:, ATTN_QW:]], axis=0)
    band_w = 2 * blk
    qi = lax.broadcasted_iota(jnp.int32, (blk, 2 * band_w), 0)
    kj = lax.broadcasted_iota(jnp.int32, (blk, 2 * band_w), 1) % band_w
    diff = qi + blk - kj
    band = (diff >= 0) & (diff < blk)
    band_first = band & (kj >= jnp.where(i == 0, blk, 0))
    first_head = lax.broadcasted_iota(jnp.int32, (blk, LANES), 1) < HEAD_DIM
    zeros = jnp.zeros((band_w, HEAD_DIM), BF16)

    def blockdiag(t):
        return jnp.concatenate([jnp.concatenate([t, zeros], axis=1), jnp.concatenate([zeros, t], axis=1)], axis=0)

    units = [(r, j, pr) for r in range(cur.shape[0] // blk) for j in range(ATTN_HKV) for pr in range(ATTN_GROUP // 2)]

    def scores(r, j, pr):
        kb = kv_all[r * blk:(r + 2) * blk, j * HEAD_DIM:(j + 1) * HEAD_DIM]
        slab = (j * (ATTN_GROUP // 2) + pr) * LANES
        return _dot_nt(cur[r * blk:(r + 1) * blk, slab:slab + LANES], blockdiag(kb))

    lane_row = lax.broadcasted_iota(jnp.int32, (1, 2 * band_w), 1)
    not_row0 = lax.broadcasted_iota(jnp.int32, (band_w, HEAD_DIM), 0) > 0
    def value_rows(r, j):
        vb = kv_all[r * blk:(r + 2) * blk, ATTN_KVW + j * HEAD_DIM:ATTN_KVW + (j + 1) * HEAD_DIM]
        return blockdiag(jnp.where(not_row0, vb, jnp.zeros_like(vb)))

    for g0 in range(0, len(units), ATTN_GROUP_UNITS):
        group = units[g0:g0 + ATTN_GROUP_UNITS]
        raw = [scores(*u) for u in group]
        sc = []
        for (r, j, pr), s_raw in zip(group, raw):
            hq = j * ATTN_GROUP + 2 * pr
            fill = jnp.where(lane_row == 0, sink_ref[hq], jnp.where(lane_row == band_w, sink_ref[hq + 1], -jnp.inf))
            sc.append(jnp.where(band_first if r == 0 else band, s_raw, fill))
        halves = [[s_[:, t * band_w:(t + 1) * band_w] for t in range(2)] for s_ in sc]
        mx = [[jnp.max(h_, axis=-1, keepdims=True) for h_ in hs] for hs in halves]
        p = [[jnp.exp(h_ - m_) for h_, m_ in zip(hs, ms)] for hs, ms in zip(halves, mx)]
        rs = [[1.0 / jnp.sum(p_, axis=-1, keepdims=True) for p_ in ps] for ps in p]
        p16 = [jnp.concatenate([p_.astype(BF16) for p_ in ps], axis=1) for ps in p]
        outs = [_dot(p16[n], value_rows(r, j)) for n, (r, j, pr) in enumerate(group)]
        for n, (r, j, pr) in enumerate(group):
            o = outs[n] * jnp.where(first_head, rs[n][0], rs[n][1])
            slab = (j * (ATTN_GROUP // 2) + pr) * LANES
            o_ref[0, r * blk:(r + 1) * blk, slab:slab + LANES] = o.astype(BF16)


def _attention(attn, sinks):
    b, s, _ = attn.shape
    tq = TQ_ATTN
    per = tq // ATTN_BLOCK
    return pl.pallas_call(
        _attn_kernel,
        grid=(b, s // tq),
        in_specs=[
            pl.BlockSpec(memory_space=pltpu.SMEM),
            pl.BlockSpec((1, tq, ATTN_W), lambda bi, i: (bi, i, 0)),
            pl.BlockSpec((1, ATTN_BLOCK, ATTN_W), lambda bi, i: (bi, jnp.maximum(i * per - 1, 0), 0)),
        ],
        out_specs=pl.BlockSpec((1, tq, ATTN_QW), lambda bi, i: (bi, i, 0)),
        out_shape=jax.ShapeDtypeStruct((b, s, ATTN_QW), BF16),
        compiler_params=pltpu.CompilerParams(vmem_limit_bytes=VMEM_LIMIT),
        name="attention",
    )(sinks, attn, attn)


def _deltanet_kernel(cur_ref, prev_ref, gate_ref, dz_ref, cw_ref, alog_ref, dtb_ref, ng_ref,
                     expand_ref, tril_ref, slow_ref, ones_ref, o_ref,
                     xext, q_s, k_s, v_s, b_s, g_s, o_s, state):
    i = pl.program_id(0)
    n_seq, rows = cur_ref.shape[0], cur_ref.shape[1]
    ck = DN_CHUNK

    @pl.when(i == 0)
    def _():
        state[...] = jnp.zeros_like(state)

    ones_blk = ones_ref[...]

    def l2n(t):
        ss = _dot((t * t).astype(BF16), ones_blk)
        return t * lax.rsqrt(ss + EPS)

    for bi in range(n_seq):
        seq = slice(bi * rows, (bi + 1) * rows)
        xext[bi, 0:DN_HALO, :] = jnp.where(i > 0, prev_ref[bi], 0.0)
        xext[bi, DN_HALO:DN_HALO + rows, :] = cur_ref[bi]
        acc = cw_ref[0:1, :] * xext[bi, pl.ds(DN_HALO - DN_CONV + 1, rows), :]
        for t in range(1, DN_CONV):
            acc = acc + cw_ref[t:t + 1, :] * xext[bi, pl.ds(DN_HALO - DN_CONV + 1 + t, rows), :]
        qkv = _silu(acc)
        q_s[seq, :] = l2n(qkv[:, 0:DN_W]) * (DN_D ** -0.5)
        k_s[seq, :] = l2n(qkv[:, DN_W:2 * DN_W])
        v_s[seq, :] = qkv[:, 2 * DN_W:3 * DN_W]
        ge = _dot(jnp.concatenate(_split3(gate_ref[bi]), axis=1), expand_ref[...])
        b_s[seq, :] = _sigmoid(ge[:, 0:DN_W])
        da = ge[:, DN_W:2 * DN_W] + dtb_ref[...]
        softplus = jnp.maximum(da, 0.0) + jnp.log1p(jnp.exp(-jnp.abs(da)))
        g_s[seq, :] = -jnp.exp(alog_ref[...]) * softplus

    tril = tril_ref[...]
    slow = slow_ref[...]
    ri = lax.broadcasted_iota(jnp.int32, (ck, DN_W), 0)
    ci = lax.broadcasted_iota(jnp.int32, (ck, DN_W), 1) % ck
    incl = ri >= ci

    pw = 2 * DN_D
    n_pair = DN_W // pw
    r2 = lax.broadcasted_iota(jnp.int32, (ck, pw), 0)
    l2 = lax.broadcasted_iota(jnp.int32, (ck, pw), 1)
    c2 = l2 % ck
    strict = r2 > c2
    eye = jnp.where(r2 == c2, 1.0, 0.0)
    first = l2 < DN_D
    masks = []
    blk = 1
    while blk < ck:
        masks.append((r2 // (2 * blk) == c2 // (2 * blk)) & ((r2 // blk) % 2 == 1) & ((c2 // blk) % 2 == 0))
        blk *= 2
    same_head = (lax.broadcasted_iota(jnp.int32, (pw, pw), 0) // DN_D) == (lax.broadcasted_iota(jnp.int32, (pw, pw), 1) // DN_D)

    def blockdiag(t):
        z = jnp.zeros_like(t)
        return jnp.concatenate([jnp.where(first, t, z), jnp.where(first, z, t)], axis=0)

    nc = n_seq * rows // ck
    per_seq = rows // ck
    slabs = [slice(p * pw, (p + 1) * pw) for p in range(n_pair)]
    kbeta16, k16, q16, dec, vbeta, kbg, qg, kg, egl = ([] for _ in range(9))
    for c in range(nc):
        r0 = c * ck
        q = q_s[r0:r0 + ck, :]
        k = k_s[r0:r0 + ck, :]
        beta = b_s[r0:r0 + ck, :]
        g = g_s[r0:r0 + ck, :]
        cums = _dot(tril, jnp.concatenate([jnp.concatenate(_split3(g), axis=0),
                                           jnp.concatenate(_split3(g * slow), axis=0)], axis=1))
        gc = cums[:, 0:DN_W]
        gdiff = cums[:, DN_W:]
        decay = jnp.exp(jnp.where(incl, gdiff, -jnp.inf))
        egc = jnp.exp(gc)
        glast = gc[ck - 1:ck, :]
        kbeta = k * beta
        per_slab = ((kbeta16, kbeta.astype(BF16)), (k16, k.astype(BF16)), (q16, q.astype(BF16)), (dec, decay),
                    (vbeta, (v_s[r0:r0 + ck, :] * beta).astype(BF16)), (kbg, (kbeta * egc).astype(BF16)),
                    (qg, q * egc), (kg, (k * jnp.exp(glast - gc)).astype(BF16)), (egl, jnp.exp(glast)))
        for dst, full in per_slab:
            dst.extend(full[:, sl] for sl in slabs)
    inst = range(nc * n_pair)
    bdk = [blockdiag(k16[n]) for n in inst]
    lower = [jnp.where(strict, _dot_nt(kbeta16[n], bdk[n]) * dec[n], 0.0) for n in inst]
    a_intra = [(_dot_nt(q16[n], bdk[n]) * dec[n]).astype(BF16) for n in inst]
    lower16 = [t.astype(BF16) for t in lower]
    tinv = [eye - jnp.where(masks[0], lower[n], 0.0) for n in inst]
    for m in masks[1:]:
        t16 = [t.astype(BF16) for t in tinv]
        x16 = [_dot(jnp.where(m, lower16[n], jnp.zeros_like(lower16[n])), blockdiag(t16[n])).astype(BF16) for n in inst]
        tinv = [tinv[n] - _dot(t16[n], blockdiag(x16[n])) for n in inst]
    t16 = [t.astype(BF16) for t in tinv]
    w16 = [_dot(t16[n], blockdiag(kbg[n])).astype(BF16) for n in inst]
    u16 = [_dot(t16[n], blockdiag(vbeta[n])).astype(BF16) for n in inst]
    kw16 = [jnp.where(same_head, _dot_tn(kg[n], w16[n]), 0.0).astype(BF16) for n in inst]
    ku = [jnp.where(same_head, _dot_tn(kg[n], u16[n]), 0.0) for n in inst]
    qeff16 = [(qg[n] - _dot(a_intra[n], blockdiag(w16[n]))).astype(BF16) for n in inst]
    au = [_dot(a_intra[n], blockdiag(u16[n])) for n in inst]
    st = [[state[bi, p] for p in range(n_pair)] for bi in range(n_seq)]
    for c in range(per_seq):
        for bi in range(n_seq):
            outs = []
            for p in range(n_pair):
                n = (bi * per_seq + c) * n_pair + p
                s16 = st[bi][p].astype(BF16)
                outs.append(_dot(qeff16[n], s16) + au[n])
                st[bi][p] = st[bi][p] * egl[n] - _dot(kw16[n], s16) + ku[n]
            r0 = (bi * per_seq + c) * ck
            o_s[r0:r0 + ck, :] = jnp.concatenate(outs, axis=1)
    for bi in range(n_seq):
        for p in range(n_pair):
            state[bi, p] = st[bi][p]
    for bi in range(n_seq):
        o = o_s[bi * rows:(bi + 1) * rows, :]
        ms = _dot((o * o).astype(BF16), ones_blk) * (1.0 / DN_D)
        y = o * lax.rsqrt(ms + EPS) * ng_ref[...]
        o_ref[bi] = (y * _silu(dz_ref[bi])).astype(BF16)


def _deltanet(dnqkv, gate, dz, conv_w, a_log, dt_bias, norm_g):
    b, s, _ = dnqkv.shape
    rows = R_DN
    ck = DN_CHUNK
    lane_head = np.arange(2 * DN_W) // DN_D
    expand = np.tile((np.arange(LANES)[:, None] == lane_head[None, :]).astype(np.float32), (3, 1))
    tril = np.tile(np.tril(np.ones((ck, ck), np.float32)), (1, 3))
    slow = (np.arange(ck)[:, None] > (np.arange(DN_W)[None, :] % ck)).astype(np.float32)
    ones_blk = (np.arange(DN_W)[:, None] // DN_D == np.arange(DN_W)[None, :] // DN_D).astype(np.float32)
    rep = lambda t: jnp.repeat(t.astype(F32), DN_D).reshape(1, DN_W)
    const = lambda shape: pl.BlockSpec(shape, lambda i: (0,) * len(shape))
    row = lambda width: pl.BlockSpec((b, rows, width), lambda i: (0, i, 0))
    per = rows // DN_HALO
    return pl.pallas_call(
        _deltanet_kernel,
        grid=(s // rows,),
        in_specs=[
            row(3 * DN_W),
            pl.BlockSpec((b, DN_HALO, 3 * DN_W), lambda i: (0, jnp.maximum(i * per - 1, 0), 0)),
            row(LANES), row(DN_W),
            const((DN_CONV, 3 * DN_W)), const((1, DN_W)), const((1, DN_W)), const((1, DN_W)),
            const((3 * LANES, 2 * DN_W)), const((ck, 3 * ck)), const((ck, DN_W)), const((DN_W, DN_W)),
        ],
        out_specs=row(DN_W),
        out_shape=jax.ShapeDtypeStruct((b, s, DN_W), BF16),
        scratch_shapes=[
            pltpu.VMEM((b, rows + DN_HALO, 3 * DN_W), F32),
            *[pltpu.VMEM((b * rows, DN_W), F32) for _ in range(6)],
            pltpu.VMEM((b, DN_W // (2 * DN_D), 2 * DN_D, 2 * DN_D), F32),
        ],
        compiler_params=pltpu.CompilerParams(vmem_limit_bytes=VMEM_LIMIT, dimension_semantics=("arbitrary",)),
        name="deltanet",
    )(dnqkv, dnqkv, gate, dz, conv_w, rep(a_log), rep(dt_bias), jnp.tile(norm_g.astype(F32), DN_H).reshape(1, DN_W),
      jnp.asarray(expand, dtype=BF16), jnp.asarray(tril, dtype=BF16), jnp.asarray(slow), jnp.asarray(ones_blk, dtype=BF16))


def _conformer_kernel(cur_ref, prev_ref, w_ref, b_ref, lg_ref, lb_ref, o_ref, u_s, sh_s):
    i = pl.program_id(1)
    rows = cur_ref.shape[1]
    prev = jnp.where(i > 0, prev_ref[0], 0.0)
    u_s[0:CV_HALO, :] = prev[:, 0:CV_C] * _sigmoid(prev[:, CV_C:])
    cur = cur_ref[0]
    u_s[CV_HALO:CV_HALO + rows, :] = cur[:, 0:CV_C] * _sigmoid(cur[:, CV_C:])
    span = rows + CV_HALO - SUBLANES
    for ph in range(1, SUBLANES):
        sh_s[ph - 1, 0:span, :] = u_s[pl.ds(ph, span), :]
    base = CV_HALO - CV_K + 1
    acc = b_ref[...]
    for t in range(CV_K):
        blk, ph = divmod(base + t, SUBLANES)
        src = u_s if ph == 0 else sh_s.at[ph - 1]
        acc = acc + w_ref[t:t + 1, :] * src[blk * SUBLANES:blk * SUBLANES + rows, :]
    mu = jnp.mean(acc, axis=-1, keepdims=True)
    xc = acc - mu
    y = xc * lax.rsqrt(jnp.mean(xc * xc, axis=-1, keepdims=True) + EPS)
    o_ref[0] = _silu(y * lg_ref[...] + lb_ref[...]).astype(BF16)


def _conformer(cv, w, bias, ln_g, ln_b):
    b, s, _ = cv.shape
    rows = R_CV
    per = rows // CV_HALO
    const = lambda shape: pl.BlockSpec(shape, lambda bi, i: (0,) * len(shape))
    return pl.pallas_call(
        _conformer_kernel,
        grid=(b, s // rows),
        in_specs=[
            pl.BlockSpec((1, rows, 2 * CV_C), lambda bi, i: (bi, i, 0)),
            pl.BlockSpec((1, CV_HALO, 2 * CV_C), lambda bi, i: (bi, jnp.maximum(i * per - 1, 0), 0)),
            const((CV_K, CV_C)), const((1, CV_C)), const((1, CV_C)), const((1, CV_C)),
        ],
        out_specs=pl.BlockSpec((1, rows, CV_C), lambda bi, i: (bi, i, 0)),
        out_shape=jax.ShapeDtypeStruct((b, s, CV_C), BF16),
        scratch_shapes=[pltpu.VMEM((rows + CV_HALO, CV_C), F32),
                        pltpu.VMEM((SUBLANES - 1, rows + CV_HALO - SUBLANES, CV_C), F32)],
        compiler_params=pltpu.CompilerParams(vmem_limit_bytes=VMEM_LIMIT),
        name="conformer",
    )(cv, cv, w, bias.reshape(1, CV_C), ln_g.reshape(1, CV_C), ln_b.reshape(1, CV_C))


def _mix_residual(x_ref, ya_ref, yd_ref, yc_ref, wa_ref, wd_ref, wc_ref, gt1_ref):
    y = _dot(ya_ref[...], wa_ref[...]) + _dot(yd_ref[...], wd_ref[...]) + _dot(yc_ref[...], wc_ref[...])
    return x_ref[...] + gt1_ref[0] * y


def _finish(x1, acc, gt2_ref, fg_ref, o_ref, final_norm):
    x2 = x1 + gt2_ref[0] * acc
    if final_norm:
        x2 = x2 * lax.rsqrt(jnp.mean(x2 * x2, axis=-1, keepdims=True) + EPS) * fg_ref[...]
    o_ref[...] = x2


def _dense_ffn_kernel(x_ref, ya_ref, yd_ref, yc_ref, wa_ref, wd_ref, wc_ref, gt1_ref, g_ref, sc_ref, sh_ref, gt2_ref,
                      fg_ref, wg_ref, wu_ref, wdn_ref, o_ref, *, final_norm):
    x1 = _mix_residual(x_ref, ya_ref, yd_ref, yc_ref, wa_ref, wd_ref, wc_ref, gt1_ref)
    h = _rms_modulate(x1, g_ref[...], sc_ref[0], sh_ref[0]).astype(BF16)
    ff = wg_ref.shape[1]
    acc = None
    for c0 in range(0, ff, FF_PIECE):
        c1 = min(c0 + FF_PIECE, ff)
        a = (_silu(_dot(h, wg_ref[:, c0:c1])) * _dot(h, wu_ref[:, c0:c1])).astype(BF16)
        part = _dot(a, wdn_ref[c0:c1, :])
        acc = part if acc is None else acc + part
    _finish(x1, acc, gt2_ref, fg_ref, o_ref, final_norm)


def _route_kernel(x_ref, ya_ref, yd_ref, yc_ref, wa_ref, wd_ref, wc_ref, gt1_ref, g_ref, sc_ref, sh_ref,
                  rwt_ref, rb_ref, upper_ref, x1_ref, h_ref, comb_ref, pos_ref, cnt_ref):
    subs = [slice(n * MOE_SUB, (n + 1) * MOE_SUB) for n in range(x_ref.shape[0] // MOE_SUB)]
    wa, wd, wc = wa_ref[...], wd_ref[...], wc_ref[...]
    y = [_dot(ya_ref[sl, :], wa) + _dot(yd_ref[sl, :], wd) + _dot(yc_ref[sl, :], wc) for sl in subs]
    x1 = [x_ref[sl, :] + gt1_ref[0] * y_ for sl, y_ in zip(subs, y)]
    h = [_rms_modulate(x_, g_ref[...], sc_ref[0], sh_ref[0]) for x_ in x1]
    for sl, x_, h_ in zip(subs, x1, h):
        x1_ref[sl, :] = x_
        h_ref[sl, :] = h_.astype(BF16)
    w_hi, w_mid, _ = _split3(rwt_ref[...])
    w_cat = jnp.concatenate([w_hi, w_hi, w_mid], axis=1)
    pieces = [_split3(h_) for h_ in h]
    logits = [_dot_nt(w_cat, jnp.concatenate([hi, mid, hi], axis=1)) + rb_ref[...] for hi, mid, _ in pieces]
    row = lax.broadcasted_iota(jnp.int32, logits[0].shape, 0)
    lane = lax.broadcasted_iota(jnp.int32, (N_EXPERTS, LANES), 1)
    cnt = jnp.zeros((N_EXPERTS, LANES), F32)
    for n, (sl, lg) in enumerate(zip(subs, logits)):
        m1 = jnp.max(lg, axis=0, keepdims=True)
        i1 = jnp.min(jnp.where(lg == m1, row, N_EXPERTS), axis=0, keepdims=True)
        rest = jnp.where(row == i1, -jnp.inf, lg)
        m2 = jnp.max(rest, axis=0, keepdims=True)
        i2 = jnp.min(jnp.where(rest == m2, row, N_EXPERTS), axis=0, keepdims=True)
        e2 = jnp.exp(m2 - m1)
        comb_ref[:, sl] = jnp.where(row == i1, 1.0 / (1.0 + e2), 0.0) + jnp.where(row == i2, e2 / (1.0 + e2), 0.0)
        sel = jnp.where((row == i1) | (row == i2), 1.0, 0.0)
        rank = _dot(sel.astype(BF16), upper_ref[...])
        pos_ref[:, sl] = jnp.where(sel > 0.0, rank, -1.0)
        cnt = cnt + jnp.where(lane == n, jnp.sum(sel, axis=1, keepdims=True), 0.0)
    cnt_ref[0] = cnt.astype(jnp.int32)


def _route(x, ys, w_out, mods, g, rwt, rb, s):
    t, d = x.shape
    tm = TM_ROUTE
    gt1, sc, sh, _ = mods
    tok = lambda width: pl.BlockSpec((tm, width), lambda i: (i, 0))
    const = lambda shape: pl.BlockSpec(shape, lambda i: (0,) * len(shape))
    vec = pl.BlockSpec((1, 1, d), lambda i: ((i * tm) // s, 0, 0))
    col = pl.BlockSpec((N_EXPERTS, tm), lambda i: (0, i))
    upper = np.triu(np.ones((MOE_SUB, MOE_SUB), np.float32), 1)
    return pl.pallas_call(
        _route_kernel,
        grid=(t // tm,),
        in_specs=[tok(d), tok(ATTN_QW), tok(DN_W), tok(CV_C), const((ATTN_QW, d)), const((DN_W, d)), const((CV_C, d)),
                  vec, const((1, d)), vec, vec, const((N_EXPERTS, d)), const((N_EXPERTS, 1)), const((MOE_SUB, MOE_SUB))],
        out_specs=[tok(d), tok(d), col, col, pl.BlockSpec((1, N_EXPERTS, LANES), lambda i: (i, 0, 0))],
        out_shape=[jax.ShapeDtypeStruct((t, d), F32), jax.ShapeDtypeStruct((t, d), BF16),
                   jax.ShapeDtypeStruct((N_EXPERTS, t), F32), jax.ShapeDtypeStruct((N_EXPERTS, t), F32),
                   jax.ShapeDtypeStruct((t // tm, N_EXPERTS, LANES), jnp.int32)],
        compiler_params=pltpu.CompilerParams(vmem_limit_bytes=VMEM_LIMIT),
        name="route",
    )(x, *ys, *w_out, gt1, g, sc, sh, rwt, rb, jnp.asarray(upper, dtype=BF16))


def _moe_sparse_kernel(cnt_ref, off_ref, tot_ref, base_ref, h_ref, x1_hbm, comb_ref, pos_ref, gt2_ref, fg_ref,
                       wg_hbm, wu_hbm, wdn_hbm, o_hbm, xg_s, wrow_s, yw_s, w_s, w_sem, acc_s, x1_s, out_s, io_sem,
                       *, final_norm):
    i = pl.program_id(0)
    e = pl.program_id(1)
    n_e = pl.num_programs(1)
    n_sub = h_ref.shape[0] // MOE_SUB
    d = h_ref.shape[1]
    win_row = lax.broadcasted_iota(jnp.int32, (MOE_WIN, MOE_SUB), 0)

    step = i * n_e + e
    n_steps = pl.num_programs(0) * n_e

    def weight_copies(expert, slot):
        return [pltpu.make_async_copy(src.at[expert], w_s.at[slot, k], w_sem.at[slot, k])
                for k, src in enumerate((wg_hbm, wu_hbm, wdn_hbm))]

    @pl.when(step == 0)
    def _():
        for ahead in range(W_SLOTS - 1):
            for cp in weight_copies(ahead % N_EXPERTS, ahead):
                cp.start()

    @pl.when(step + (W_SLOTS - 1) < n_steps)
    def _():
        for cp in weight_copies(lax.rem(e + (W_SLOTS - 1), n_e), lax.rem(step + (W_SLOTS - 1), W_SLOTS)):
            cp.start()

    slot = lax.rem(step, W_SLOTS)
    for cp in weight_copies(e, slot):
        cp.wait()
    wg_ref, wu_ref, wdn_ref = (w_s.at[slot, k] for k in range(3))

    def gather(sub, p):
        k = (i * n_sub + sub) * N_EXPERTS + e
        lanes = slice(sub * MOE_SUB, (sub + 1) * MOE_SUB)
        pos = pos_ref[pl.ds(e, 1), lanes].astype(jnp.int32)
        sel = win_row + p * MOE_WIN == pos
        r0 = pl.multiple_of(off_ref[k] + p * MOE_WIN, PACK_ROWS)
        rows = _dot(jnp.where(sel, 1.0, 0.0).astype(BF16), h_ref[lanes, :])
        xg_s[pl.ds(r0, MOE_WIN), :] = rows.astype(BF16)
        w = jnp.sum(jnp.where(sel, comb_ref[pl.ds(e, 1), lanes], 0.0), axis=1, keepdims=True)
        wrow_s[pl.ds(r0, MOE_WIN), :] = jnp.broadcast_to(w, (MOE_WIN, LANES))

    most = cnt_ref[i * n_sub * N_EXPERTS + e]
    for sub in range(1, n_sub):
        most = jnp.maximum(most, cnt_ref[(i * n_sub + sub) * N_EXPERTS + e])
    for p in range(MOE_SUB // MOE_WIN - 1, 0, -1):
        @pl.when(most > p * MOE_WIN)
        def _():
            for sub in range(n_sub):
                @pl.when(cnt_ref[(i * n_sub + sub) * N_EXPERTS + e] > p * MOE_WIN)
                def _():
                    gather(sub, p)
    for sub in range(n_sub):
        gather(sub, 0)

    total = pl.multiple_of(tot_ref[i * N_EXPERTS + e], PACK_ROWS)
    base = pl.multiple_of(base_ref[i * N_EXPERTS + e], PACK_ROWS)
    xg_s[pl.ds(total, MOE_TAIL), :] = jnp.zeros((MOE_TAIL, d), BF16)
    wrow_s[pl.ds(total, MOE_TAIL), :] = jnp.zeros((MOE_TAIL, LANES), F32)

    def ffn(r0, rows):
        xc = xg_s[pl.ds(r0, rows), :]
        a = (_silu(_dot(xc, wg_ref[...])) * _dot(xc, wu_ref[...])).astype(BF16)
        y = _dot(a, wdn_ref[...]) * jnp.tile(wrow_s[pl.ds(r0, rows), :], (1, d // LANES))
        yw_s[pl.ds(base + r0, rows), :] = y.astype(BF16)

    n_full = total // MOE_CHUNK
    rem = total - n_full * MOE_CHUNK
    n_chunks = n_full + jnp.where(rem > MOE_CHUNK // 2, 1, 0)
    ffn(0, MOE_CHUNK)

    def full_chunk(c, carry):
        ffn(pl.multiple_of(c * MOE_CHUNK, MOE_CHUNK), MOE_CHUNK)
        return carry

    lax.fori_loop(1, n_chunks, full_chunk, 0)
    half = (rem > 0) & (rem <= MOE_CHUNK // 2) & (n_full > 0)

    @pl.when(half)
    def _():
        ffn(pl.multiple_of(n_full * MOE_CHUNK, MOE_CHUNK), MOE_CHUNK // 2)

    covered = jnp.maximum(n_chunks, 1) * MOE_CHUNK + jnp.where(half, MOE_CHUNK // 2, 0)
    yw_s[pl.ds(pl.multiple_of(base + covered, PACK_ROWS), MOE_TAIL), :] = jnp.zeros((MOE_TAIL, d), BF16)

    @pl.when(e == n_e - 1)
    def _():
        row0 = i * (n_sub * MOE_SUB)

        def rows_of(ref, sub):
            return ref.at[pl.ds(pl.multiple_of(row0 + sub * MOE_SUB, MOE_SUB), MOE_SUB), :]

        def x1_copy(sub):
            return pltpu.make_async_copy(rows_of(x1_hbm, sub), x1_s.at[sub % 2], io_sem.at[0, sub % 2])

        def out_copy(sub):
            return pltpu.make_async_copy(out_s.at[sub % 2], rows_of(o_hbm, sub), io_sem.at[1, sub % 2])

        x1_copy(0).start()
        for sub in range(n_sub):
            if sub + 1 < n_sub:
                x1_copy(sub + 1).start()
            lanes = slice(sub * MOE_SUB, (sub + 1) * MOE_SUB)
            first = (i * n_sub + sub) * N_EXPERTS

            def windows(p):
                sels, rows = [], []
                for ex in range(N_EXPERTS):
                    r0 = pl.multiple_of(base_ref[i * N_EXPERTS + ex] + off_ref[first + ex] + p * MOE_WIN, PACK_ROWS)
                    pos = pos_ref[ex:ex + 1, lanes].astype(jnp.int32)
                    sels.append(jnp.where(win_row + p * MOE_WIN == pos, 1.0, 0.0).astype(BF16))
                    rows.append(yw_s[pl.ds(r0, MOE_WIN), :])
                return _dot_tn(jnp.concatenate(sels, axis=0), jnp.concatenate(rows, axis=0))

            acc_s[...] = windows(0)
            most = cnt_ref[first]
            for ex in range(1, N_EXPERTS):
                most = jnp.maximum(most, cnt_ref[first + ex])
            for p in range(1, MOE_SUB // MOE_WIN):
                @pl.when(most > p * MOE_WIN)
                def _():
                    acc_s[...] += windows(p)
            x1_copy(sub).wait()
            x2 = x1_s[sub % 2] + gt2_ref[0] * acc_s[...]
            if final_norm:
                x2 = x2 * lax.rsqrt(jnp.mean(x2 * x2, axis=-1, keepdims=True) + EPS) * fg_ref[...]
            if sub >= 2:
                out_copy(sub - 2).wait()
            out_s[sub % 2] = x2
            out_copy(sub).start()
        for sub in range(max(n_sub - 2, 0), n_sub):
            out_copy(sub).wait()


def _dense_ffn(x, ys, w_out, mods, g, final_g, wg, wu, wdn, s, final_norm):
    t, d = x.shape
    tm = TM_FFN
    ff = wg.shape[1]
    gt1, sc, sh, gt2 = mods
    tok = lambda width: pl.BlockSpec((tm, width), lambda i: (i, 0))
    const = lambda shape: pl.BlockSpec(shape, lambda i: (0,) * len(shape), pipeline_mode=pl.Buffered(1))
    vec = pl.BlockSpec((1, 1, d), lambda i: ((i * tm) // s, 0, 0))
    return pl.pallas_call(
        functools.partial(_dense_ffn_kernel, final_norm=final_norm),
        grid=(t // tm,),
        in_specs=[tok(d), tok(ATTN_QW), tok(DN_W), tok(CV_C), const((ATTN_QW, d)), const((DN_W, d)), const((CV_C, d)),
                  vec, const((1, d)), vec, vec, vec, const((1, d)),
                  const((d, ff)), const((d, ff)), const((ff, d))],
        out_specs=tok(d),
        out_shape=jax.ShapeDtypeStruct((t, d), F32),
        compiler_params=pltpu.CompilerParams(vmem_limit_bytes=VMEM_LIMIT),
        name="dense_ffn",
    )(x, *ys, *w_out, gt1, g, sc, sh, gt2, final_g, wg, wu, wdn)


def _moe_sparse(h, x1, comb, pos, cnt, gt2, final_g, wg, wu, wdn, s, final_norm):
    t, d = x1.shape
    ts = TS_MOE
    n_e, _, ffe = wg.shape
    n_sub = ts // MOE_SUB
    counts = cnt[:, :, :TM_ROUTE // MOE_SUB].transpose(0, 2, 1).reshape(t // ts, n_sub, n_e)
    padded = -(-counts // PACK_ROWS) * PACK_ROWS
    offs = jnp.cumsum(padded, axis=1) - padded
    tots = jnp.sum(padded, axis=1)
    bases = jnp.cumsum(tots, axis=1) - tots
    cap_one = -(-(ts + n_sub * PACK_ROWS + MOE_TAIL) // PACK_ROWS) * PACK_ROWS
    cap_all = -(-(N_TOP * ts + n_e * n_sub * PACK_ROWS + MOE_CHUNK + MOE_TAIL) // PACK_ROWS) * PACK_ROWS
    tok = lambda width: pl.BlockSpec((ts, width), lambda i, e, *_: (i, 0))
    col = pl.BlockSpec((n_e, ts), lambda i, e, *_: (0, i))
    hbm = pl.BlockSpec(memory_space=pl.ANY)
    assert ffe == d, "the weight ring holds the three expert matrices in one (3, d, ffe) slot"
    flat = lambda a: a.reshape(-1).astype(jnp.int32)
    grid_spec = pltpu.PrefetchScalarGridSpec(
        num_scalar_prefetch=4,
        grid=(t // ts, n_e),
        in_specs=[tok(d), hbm, col, col,
                  pl.BlockSpec((1, 1, d), lambda i, e, *_: ((i * ts) // s, 0, 0)),
                  pl.BlockSpec((1, d), lambda i, e, *_: (0, 0)),
                  hbm, hbm, hbm],
        out_specs=hbm,
        scratch_shapes=[pltpu.VMEM((cap_one, d), BF16), pltpu.VMEM((cap_one, LANES), F32),
                        pltpu.VMEM((cap_all, d), BF16),
                        pltpu.VMEM((W_SLOTS, 3, d, ffe), BF16), pltpu.SemaphoreType.DMA((W_SLOTS, 3)),
                        pltpu.VMEM((MOE_SUB, d), F32), pltpu.VMEM((2, MOE_SUB, d), F32), pltpu.VMEM((2, MOE_SUB, d), F32),
                        pltpu.SemaphoreType.DMA((2, 2))],
    )
    return pl.pallas_call(
        functools.partial(_moe_sparse_kernel, final_norm=final_norm),
        grid_spec=grid_spec,
        out_shape=jax.ShapeDtypeStruct((t, d), F32),
        compiler_params=pltpu.CompilerParams(vmem_limit_bytes=VMEM_LIMIT_MOE,
                                             dimension_semantics=("arbitrary", "arbitrary")),
        name="moe_sparse",
    )(flat(counts), flat(offs), flat(tots), flat(bases), h, x1, comb, pos, gt2, final_g, wg, wu, wdn)


def _rope_cos_sin(positions):
    inv_freq = ROPE_THETA ** (-jnp.arange(0, ROT_DIM, 2, dtype=F32) / ROT_DIM)
    ang = positions.astype(F32)[:, None, :] * inv_freq[None, :, None]
    return jnp.concatenate([jnp.cos(ang), jnp.sin(ang)], axis=1)


def _pack_w_in(w):
    d = w.shape[0]
    o1 = ATTN_W
    o2 = o1 + 3 * DN_W
    o3 = o2 + 2 * DN_H
    o4 = o3 + DN_W
    gates = jnp.zeros((d, LANES), w.dtype).at[:, :2 * DN_H].set(w[:, o2:o3])
    return jnp.concatenate([w[:, :o2], gates, w[:, o3:o4], w[:, o4:]], axis=1).astype(BF16)


def kernel(x, c, positions, ada_w, ada_b, norm_mix_g, norm_ffn_g, w_in, attn_sinks, dn_conv_w, dn_a_log, dn_dt_bias, dn_norm_g, cv_dw_w, cv_dw_b, cv_ln_g, cv_ln_b, w_out, ffn_w_gate, ffn_w_up, ffn_w_down, router_w, router_b, moe_w_gate, moe_w_up, moe_w_down, final_norm_g):
    b, s, d = x.shape
    depth = w_in.shape[0]
    mod = _adaln(c, ada_w, ada_b)
    cos_sin = _rope_cos_sin(positions)
    final_g = final_norm_g.reshape(1, d)
    vec = lambda t: t.reshape(b, 1, d)
    for l in range(depth):
        sh1, sc1, gt1 = (vec(t) for t in jnp.split(mod[l, 0], 3, axis=-1))
        sh2, sc2, gt2 = (vec(t) for t in jnp.split(mod[l, 1], 3, axis=-1))
        attn, dnqkv, gate, dz, cv = _inproj(x, norm_mix_g[l].reshape(1, d), sc1, sh1, cos_sin, _pack_w_in(w_in[l]))
        y_attn = _attention(attn, attn_sinks[l].astype(F32))
        y_dn = _deltanet(dnqkv, gate, dz, dn_conv_w[l], dn_a_log[l], dn_dt_bias[l], dn_norm_g[l])
        y_cv = _conformer(cv, cv_dw_w[l], cv_dw_b[l], cv_ln_g[l], cv_ln_b[l])
        t = b * s
        ys = (y_attn.reshape(t, ATTN_QW), y_dn.reshape(t, DN_W), y_cv.reshape(t, CV_C))
        wo = w_out[l].astype(BF16)
        wos = (wo[:ATTN_QW], wo[ATTN_QW:ATTN_QW + DN_W], wo[ATTN_QW + DN_W:])
        mods = (gt1, sc2, sh2, gt2)
        g2 = norm_ffn_g[l].reshape(1, d)
        last = l == depth - 1
        j = l // 2
        if l % 2 == 0:
            x2 = _dense_ffn(x.reshape(t, d), ys, wos, mods, g2, final_g, ffn_w_gate[j].astype(BF16),
                            ffn_w_up[j].astype(BF16), ffn_w_down[j].astype(BF16), s, last)
        else:
            x1, h2, comb, pos, cnt = _route(x.reshape(t, d), ys, wos, mods, g2, router_w[j].astype(F32).T,
                                            router_b[j].astype(F32).reshape(N_EXPERTS, 1), s)
            x2 = _moe_sparse(h2, x1, comb, pos, cnt, gt2, final_g, moe_w_gate[j].astype(BF16),
                             moe_w_up[j].astype(BF16), moe_w_down[j].astype(BF16), s, last)
        x = x2.reshape(b, s, d)
    return x
```

```python
import functools

import numpy as np
import jax
import jax.numpy as jnp
from jax import lax
from jax.experimental import pallas as pl
from jax.experimental.pallas import tpu as pltpu

F32 = jnp.float32
BF16 = jnp.bfloat16
HIGHEST = lax.Precision.HIGHEST

HEAD_DIM = 64
ATTN_HQ = 8
ATTN_HKV = 2
ATTN_GROUP = ATTN_HQ // ATTN_HKV
ATTN_BLOCK = 128
ROT_DIM = HEAD_DIM // 4
ROPE_THETA = 500000.0
DN_H = 4
DN_D = 64
DN_CONV = 4
DN_CHUNK = 64
CV_C = 256
CV_K = 31
ATTN_QW = ATTN_HQ * HEAD_DIM
ATTN_KVW = ATTN_HKV * HEAD_DIM
ATTN_W = ATTN_QW + 2 * ATTN_KVW
DN_W = DN_H * DN_D
N_EXPERTS = 8
EPS = 1e-6
LANES = 128
SUBLANES = 8
PACK_ROWS = 16
VMEM_LIMIT = 48 * 1024 * 1024
VMEM_LIMIT_MOE = 56 * 1024 * 1024

ADA_TN = 1024
TM_PROJ = 1024
TQ_ATTN = 1024
ATTN_GROUP_UNITS = 32
R_DN = 512
R_CV = 1024
TM_FFN = 512
TM_ROUTE = 1024
FFN_ROWS = 256
FF_PIECE = 1536
CV_HALO = 32
DN_HALO = 8
TS_MOE = 2048
MOE_SUB = 256
MOE_WIN = 128
MOE_CHUNK = 256
MOE_TAIL = 384
N_TOP = 2
W_SLOTS = 3


def _dot(a, b, precision=None):
    return jnp.dot(a, b, preferred_element_type=F32, precision=precision)


def _dot_nt(a, b):
    return lax.dot_general(a, b, (((1,), (1,)), ((), ())), preferred_element_type=F32)


def _dot_tn(a, b):
    return lax.dot_general(a, b, (((0,), (0,)), ((), ())), preferred_element_type=F32)


def _split3(x):
    hi = x.astype(BF16)
    rest = x - hi.astype(F32)
    mid = rest.astype(BF16)
    return hi, mid, (rest - mid.astype(F32)).astype(BF16)


def _sigmoid(x):
    return 1.0 / (1.0 + jnp.exp(-x))


def _silu(x):
    return x * _sigmoid(x)


def _rms_modulate(x, g, sc, sh):
    y = x * lax.rsqrt(jnp.mean(x * x, axis=-1, keepdims=True) + EPS)
    return (y * g) * (1.0 + sc) + sh


def _adaln_kernel(c_ref, w_ref, b_ref, o_ref):
    o_ref[0] = _dot(_silu(c_ref[...]), w_ref[0], HIGHEST) + b_ref[0]


def _adaln(c, ada_w, ada_b):
    depth, _, d, d3 = ada_w.shape
    b = c.shape[0]
    rows = SUBLANES
    cp = jnp.zeros((rows, d), F32).at[:b].set(c)
    w = ada_w.reshape(depth * 2, d, d3)
    bias = ada_b.reshape(depth * 2, 1, d3)
    tn = ADA_TN
    out = pl.pallas_call(
        _adaln_kernel,
        grid=(depth * 2, d3 // tn),
        in_specs=[
            pl.BlockSpec((rows, d), lambda i, j: (0, 0)),
            pl.BlockSpec((1, d, tn), lambda i, j: (i, 0, j)),
            pl.BlockSpec((1, 1, tn), lambda i, j: (i, 0, j)),
        ],
        out_specs=pl.BlockSpec((1, rows, tn), lambda i, j: (i, 0, j)),
        out_shape=jax.ShapeDtypeStruct((depth * 2, rows, d3), F32),
        compiler_params=pltpu.CompilerParams(vmem_limit_bytes=VMEM_LIMIT),
        name="adaln",
    )(cp, w, bias)
    return out[:, :b].reshape(depth, 2, b, d3)


def _inproj_kernel(x_ref, g_ref, sc_ref, sh_ref, cs_ref, place_ref, ones_ref, w_ref,
                   attn_ref, dnqkv_ref, gate_ref, dz_ref, cv_ref):
    h = _rms_modulate(x_ref[0], g_ref[...], sc_ref[0], sh_ref[0]).astype(BF16)
    a = _dot(h, w_ref[:, 0:ATTN_W])
    tab = _dot_tn(jnp.concatenate(_split3(cs_ref[0]), axis=0), place_ref[...]) + ones_ref[...]
    rc, ra, rb = tab[:, 0:LANES], tab[:, LANES:2 * LANES], tab[:, 2 * LANES:3 * LANES]
    half = ROT_DIM // 2
    for j in range((ATTN_QW + ATTN_KVW) // LANES):
        t = a[:, j * LANES:(j + 1) * LANES]
        r = t * rc + pltpu.roll(t, LANES - half, 1) * ra + pltpu.roll(t, half, 1) * rb
        if j < ATTN_QW // LANES:
            r = r * (HEAD_DIM ** -0.5)
        attn_ref[0, :, j * LANES:(j + 1) * LANES] = r.astype(BF16)
    attn_ref[0, :, ATTN_QW + ATTN_KVW:ATTN_W] = a[:, ATTN_QW + ATTN_KVW:ATTN_W].astype(BF16)
    o = ATTN_W
    dnqkv_ref[0] = _dot(h, w_ref[:, o:o + 3 * DN_W])
    o += 3 * DN_W
    gate_ref[0] = _dot(h, w_ref[:, o:o + LANES])
    o += LANES
    dz_ref[0] = _dot(h, w_ref[:, o:o + DN_W])
    o += DN_W
    cv_ref[0] = _dot(h, w_ref[:, o:o + 2 * CV_C])


def _rope_placement():
    half = ROT_DIM // 2
    place = np.zeros((2 * half, 3 * LANES), np.float32)
    ones = np.zeros((1, 3 * LANES), np.float32)
    for lane in range(LANES):
        dim = lane % HEAD_DIM
        if dim < ROT_DIM:
            place[dim % half, lane] = 1.0
            if dim < half:
                place[half + dim, LANES + lane] = -1.0
            else:
                place[half + dim - half, 2 * LANES + lane] = 1.0
        else:
            ones[0, lane] = 1.0
    return np.tile(place, (3, 1)), ones


def _inproj(x, g, sc, sh, cos_sin, w):
    b, s, d = x.shape
    tm = TM_PROJ
    n = w.shape[1]
    row = lambda width: pl.BlockSpec((1, tm, width), lambda bi, i: (bi, i, 0))
    vec = pl.BlockSpec((1, 1, d), lambda bi, i: (bi, 0, 0))
    const = lambda shape: pl.BlockSpec(shape, lambda bi, i: (0,) * len(shape))
    widths = (ATTN_W, 3 * DN_W, LANES, DN_W, 2 * CV_C)
    dtypes = (BF16, F32, F32, F32, F32)
    place, ones = _rope_placement()
    return pl.pallas_call(
        _inproj_kernel,
        grid=(b, s // tm),
        in_specs=[row(d), const((1, d)), vec, vec,
                  pl.BlockSpec((1, ROT_DIM, tm), lambda bi, i: (bi, 0, i)), const(place.shape), const(ones.shape),
                  const((d, n))],
        out_specs=[row(wd) for wd in widths],
        out_shape=[jax.ShapeDtypeStruct((b, s, wd), dt) for wd, dt in zip(widths, dtypes)],
        compiler_params=pltpu.CompilerParams(vmem_limit_bytes=VMEM_LIMIT),
        name="inproj",
    )(x, g, sc, sh, cos_sin, jnp.asarray(place, dtype=BF16), jnp.asarray(ones), w)


def _attn_kernel(sink_ref, cur_ref, prev_ref, o_ref):
    i = pl.program_id(1)
    blk = ATTN_BLOCK
    cur = cur_ref[0]
    prev = prev_ref[0]
    kv_all = jnp.concatenate([prev[:, ATTN_QW:], cur[:, ATTN_QW:]], axis=0)
    band_w = 2 * blk
    qi = lax.broadcasted_iota(jnp.int32, (blk, 2 * band_w), 0)
    kj = lax.broadcasted_iota(jnp.int32, (blk, 2 * band_w), 1) % band_w
    diff = qi + blk - kj
    band = (diff >= 0) & (diff < blk)
    band_first = band & (kj >= jnp.where(i == 0, blk, 0))
    first_head = lax.broadcasted_iota(jnp.int32, (blk, LANES), 1) < HEAD_DIM
    zeros = jnp.zeros((band_w, HEAD_DIM), BF16)

    def blockdiag(t):
        return jnp.concatenate([jnp.concatenate([t, zeros], axis=1), jnp.concatenate([zeros, t], axis=1)], axis=0)

    units = [(r, j, pr) for r in range(cur.shape[0] // blk) for j in range(ATTN_HKV) for pr in range(ATTN_GROUP // 2)]

    def scores(r, j, pr):
        kb = kv_all[r * blk:(r + 2) * blk, j * HEAD_DIM:(j + 1) * HEAD_DIM]
        slab = (j * (ATTN_GROUP // 2) + pr) * LANES
        return _dot_nt(cur[r * blk:(r + 1) * blk, slab:slab + LANES], blockdiag(kb))

    lane_row = lax.broadcasted_iota(jnp.int32, (1, 2 * band_w), 1)
    not_row0 = lax.broadcasted_iota(jnp.int32, (band_w, HEAD_DIM), 0) > 0
    def value_rows(r, j):
        vb = kv_all[r * blk:(r + 2) * blk, ATTN_KVW + j * HEAD_DIM:ATTN_KVW + (j + 1) * HEAD_DIM]
        return blockdiag(jnp.where(not_row0, vb, jnp.zeros_like(vb)))

    for g0 in range(0, len(units), ATTN_GROUP_UNITS):
        group = units[g0:g0 + ATTN_GROUP_UNITS]
        raw = [scores(*u) for u in group]
        sc = []
        for (r, j, pr), s_raw in zip(group, raw):
            hq = j * ATTN_GROUP + 2 * pr
            fill = jnp.where(lane_row == 0, sink_ref[hq], jnp.where(lane_row == band_w, sink_ref[hq + 1], -jnp.inf))
            sc.append(jnp.where(band_first if r == 0 else band, s_raw, fill))
        halves = [[s_[:, t * band_w:(t + 1) * band_w] for t in range(2)] for s_ in sc]
        mx = [[jnp.max(h_, axis=-1, keepdims=True) for h_ in hs] for hs in halves]
        p = [[jnp.exp(h_ - m_) for h_, m_ in zip(hs, ms)] for hs, ms in zip(halves, mx)]
        rs = [[1.0 / jnp.sum(p_, axis=-1, keepdims=True) for p_ in ps] for ps in p]
        p16 = [jnp.concatenate([p_.astype(BF16) for p_ in ps], axis=1) for ps in p]
        outs = [_dot(p16[n], value_rows(r, j)) for n, (r, j, pr) in enumerate(group)]
        for n, (r, j, pr) in enumerate(group):
            o = outs[n] * jnp.where(first_head, rs[n][0], rs[n][1])
            slab = (j * (ATTN_GROUP // 2) + pr) * LANES
            o_ref[0, r * blk:(r + 1) * blk, slab:slab + LANES] = o.astype(BF16)


def _attention(attn, sinks):
    b, s, _ = attn.shape
    tq = TQ_ATTN
    per = tq // ATTN_BLOCK
    return pl.pallas_call(
        _attn_kernel,
        grid=(b, s // tq),
        in_specs=[
            pl.BlockSpec(memory_space=pltpu.SMEM),
            pl.BlockSpec((1, tq, ATTN_W), lambda bi, i: (bi, i, 0)),
            pl.BlockSpec((1, ATTN_BLOCK, ATTN_W), lambda bi, i: (bi, jnp.maximum(i * per - 1, 0), 0)),
        ],
        out_specs=pl.BlockSpec((1, tq, ATTN_QW), lambda bi, i: (bi, i, 0)),
        out_shape=jax.ShapeDtypeStruct((b, s, ATTN_QW), BF16),
        compiler_params=pltpu.CompilerParams(vmem_limit_bytes=VMEM_LIMIT),
        name="attention",
    )(sinks, attn, attn)


def _deltanet_kernel(cur_ref, prev_ref, gate_ref, dz_ref, cw_ref, alog_ref, dtb_ref, ng_ref,
                     expand_ref, tril_ref, slow_ref, ones_ref, o_ref,
                     xext, q_s, k_s, v_s, b_s, g_s, o_s, state):
    i = pl.program_id(0)
    n_seq, rows = cur_ref.shape[0], cur_ref.shape[1]
    ck = DN_CHUNK

    @pl.when(i == 0)
    def _():
        state[...] = jnp.zeros_like(state)

    ones_blk = ones_ref[...]

    def l2n(t):
        ss = _dot((t * t).astype(BF16), ones_blk)
        return t * lax.rsqrt(ss + EPS)

    for bi in range(n_seq):
        seq = slice(bi * rows, (bi + 1) * rows)
        xext[bi, 0:DN_HALO, :] = jnp.where(i > 0, prev_ref[bi], 0.0)
        xext[bi, DN_HALO:DN_HALO + rows, :] = cur_ref[bi]
        acc = cw_ref[0:1, :] * xext[bi, pl.ds(DN_HALO - DN_CONV + 1, rows), :]
        for t in range(1, DN_CONV):
            acc = acc + cw_ref[t:t + 1, :] * xext[bi, pl.ds(DN_HALO - DN_CONV + 1 + t, rows), :]
        qkv = _silu(acc)
        q_s[seq, :] = l2n(qkv[:, 0:DN_W]) * (DN_D ** -0.5)
        k_s[seq, :] = l2n(qkv[:, DN_W:2 * DN_W])
        v_s[seq, :] = qkv[:, 2 * DN_W:3 * DN_W]
        ge = _dot(jnp.concatenate(_split3(gate_ref[bi]), axis=1), expand_ref[...])
        b_s[seq, :] = _sigmoid(ge[:, 0:DN_W])
        da = ge[:, DN_W:2 * DN_W] + dtb_ref[...]
        softplus = jnp.maximum(da, 0.0) + jnp.log1p(jnp.exp(-jnp.abs(da)))
        g_s[seq, :] = -jnp.exp(alog_ref[...]) * softplus

    tril = tril_ref[...]
    slow = slow_ref[...]
    ri = lax.broadcasted_iota(jnp.int32, (ck, DN_W), 0)
    ci = lax.broadcasted_iota(jnp.int32, (ck, DN_W), 1) % ck
    incl = ri >= ci

    pw = 2 * DN_D
    n_pair = DN_W // pw
    r2 = lax.broadcasted_iota(jnp.int32, (ck, pw), 0)
    l2 = lax.broadcasted_iota(jnp.int32, (ck, pw), 1)
    c2 = l2 % ck
    strict = r2 > c2
    eye = jnp.where(r2 == c2, 1.0, 0.0)
    first = l2 < DN_D
    masks = []
    blk = 1
    while blk < ck:
        masks.append((r2 // (2 * blk) == c2 // (2 * blk)) & ((r2 // blk) % 2 == 1) & ((c2 // blk) % 2 == 0))
        blk *= 2
    same_head = (lax.broadcasted_iota(jnp.int32, (pw, pw), 0) // DN_D) == (lax.broadcasted_iota(jnp.int32, (pw, pw), 1) // DN_D)

    def blockdiag(t):
        z = jnp.zeros_like(t)
        return jnp.concatenate([jnp.where(first, t, z), jnp.where(first, z, t)], axis=0)

    nc = n_seq * rows // ck
    per_seq = rows // ck
    slabs = [slice(p * pw, (p + 1) * pw) for p in range(n_pair)]
    kbeta16, k16, q16, dec, vbeta, kbg, qg, kg, egl = ([] for _ in range(9))
    for c in range(nc):
        r0 = c * ck
        q = q_s[r0:r0 + ck, :]
        k = k_s[r0:r0 + ck, :]
        beta = b_s[r0:r0 + ck, :]
        g = g_s[r0:r0 + ck, :]
        cums = _dot(tril, jnp.concatenate([jnp.concatenate(_split3(g), axis=0),
                                           jnp.concatenate(_split3(g * slow), axis=0)], axis=1))
        gc = cums[:, 0:DN_W]
        gdiff = cums[:, DN_W:]
        decay = jnp.exp(jnp.where(incl, gdiff, -jnp.inf))
        egc = jnp.exp(gc)
        glast = gc[ck - 1:ck, :]
        kbeta = k * beta
        per_slab = ((kbeta16, kbeta.astype(BF16)), (k16, k.astype(BF16)), (q16, q.astype(BF16)), (dec, decay),
                    (vbeta, (v_s[r0:r0 + ck, :] * beta).astype(BF16)), (kbg, (kbeta * egc).astype(BF16)),
                    (qg, q * egc), (kg, (k * jnp.exp(glast - gc)).astype(BF16)), (egl, jnp.exp(glast)))
        for dst, full in per_slab:
            dst.extend(full[:, sl] for sl in slabs)
    inst = range(nc * n_pair)
    bdk = [blockdiag(k16[n]) for n in inst]
    lower = [jnp.where(strict, _dot_nt(kbeta16[n], bdk[n]) * dec[n], 0.0) for n in inst]
    a_intra = [(_dot_nt(q16[n], bdk[n]) * dec[n]).astype(BF16) for n in inst]
    lower16 = [t.astype(BF16) for t in lower]
    tinv = [eye - jnp.where(masks[0], lower[n], 0.0) for n in inst]
    for m in masks[1:]:
        t16 = [t.astype(BF16) for t in tinv]
        x16 = [_dot(jnp.where(m, lower16[n], jnp.zeros_like(lower16[n])), blockdiag(t16[n])).astype(BF16) for n in inst]
        tinv = [tinv[n] - _dot(t16[n], blockdiag(x16[n])) for n in inst]
    t16 = [t.astype(BF16) for t in tinv]
    w16 = [_dot(t16[n], blockdiag(kbg[n])).astype(BF16) for n in inst]
    u16 = [_dot(t16[n], blockdiag(vbeta[n])).astype(BF16) for n in inst]
    kw16 = [jnp.where(same_head, _dot_tn(kg[n], w16[n]), 0.0).astype(BF16) for n in inst]
    ku = [jnp.where(same_head, _dot_tn(kg[n], u16[n]), 0.0) for n in inst]
    qeff16 = [(qg[n] - _dot(a_intra[n], blockdiag(w16[n]))).astype(BF16) for n in inst]
    au = [_dot(a_intra[n], blockdiag(u16[n])) for n in inst]
    st = [[state[bi, p] for p in range(n_pair)] for bi in range(n_seq)]
    for c in range(per_seq):
        for bi in range(n_seq):
            outs = []
            for p in range(n_pair):
                n = (bi * per_seq + c) * n_pair + p
                s16 = st[bi][p].astype(BF16)
                outs.append(_dot(qeff16[n], s16) + au[n])
                st[bi][p] = st[bi][p] * egl[n] - _dot(kw16[n], s16) + ku[n]
            r0 = (bi * per_seq + c) * ck
            o_s[r0:r0 + ck, :] = jnp.concatenate(outs, axis=1)
    for bi in range(n_seq):
        for p in range(n_pair):
            state[bi, p] = st[bi][p]
    for bi in range(n_seq):
        o = o_s[bi * rows:(bi + 1) * rows, :]
        ms = _dot((o * o).astype(BF16), ones_blk) * (1.0 / DN_D)
        y = o * lax.rsqrt(ms + EPS) * ng_ref[...]
        o_ref[bi] = (y * _silu(dz_ref[bi])).astype(BF16)


def _deltanet(dnqkv, gate, dz, conv_w, a_log, dt_bias, norm_g):
    b, s, _ = dnqkv.shape
    rows = R_DN
    ck = DN_CHUNK
    lane_head = np.arange(2 * DN_W) // DN_D
    expand = np.tile((np.arange(LANES)[:, None] == lane_head[None, :]).astype(np.float32), (3, 1))
    tril = np.tile(np.tril(np.ones((ck, ck), np.float32)), (1, 3))
    slow = (np.arange(ck)[:, None] > (np.arange(DN_W)[None, :] % ck)).astype(np.float32)
    ones_blk = (np.arange(DN_W)[:, None] // DN_D == np.arange(DN_W)[None, :] // DN_D).astype(np.float32)
    rep = lambda t: jnp.repeat(t.astype(F32), DN_D).reshape(1, DN_W)
    const = lambda shape: pl.BlockSpec(shape, lambda i: (0,) * len(shape))
    row = lambda width: pl.BlockSpec((b, rows, width), lambda i: (0, i, 0))
    per = rows // DN_HALO
    return pl.pallas_call(
        _deltanet_kernel,
        grid=(s // rows,),
        in_specs=[
            row(3 * DN_W),
            pl.BlockSpec((b, DN_HALO, 3 * DN_W), lambda i: (0, jnp.maximum(i * per - 1, 0), 0)),
            row(LANES), row(DN_W),
            const((DN_CONV, 3 * DN_W)), const((1, DN_W)), const((1, DN_W)), const((1, DN_W)),
            const((3 * LANES, 2 * DN_W)), const((ck, 3 * ck)), const((ck, DN_W)), const((DN_W, DN_W)),
        ],
        out_specs=row(DN_W),
        out_shape=jax.ShapeDtypeStruct((b, s, DN_W), BF16),
        scratch_shapes=[
            pltpu.VMEM((b, rows + DN_HALO, 3 * DN_W), F32),
            *[pltpu.VMEM((b * rows, DN_W), F32) for _ in range(6)],
            pltpu.VMEM((b, DN_W // (2 * DN_D), 2 * DN_D, 2 * DN_D), F32),
        ],
        compiler_params=pltpu.CompilerParams(vmem_limit_bytes=VMEM_LIMIT, dimension_semantics=("arbitrary",)),
        name="deltanet",
    )(dnqkv, dnqkv, gate, dz, conv_w, rep(a_log), rep(dt_bias), jnp.tile(norm_g.astype(F32), DN_H).reshape(1, DN_W),
      jnp.asarray(expand, dtype=BF16), jnp.asarray(tril, dtype=BF16), jnp.asarray(slow), jnp.asarray(ones_blk, dtype=BF16))


def _conformer_kernel(cur_ref, prev_ref, w_ref, b_ref, lg_ref, lb_ref, o_ref, u_s, sh_s):
    i = pl.program_id(1)
    rows = cur_ref.shape[1]
    prev = jnp.where(i > 0, prev_ref[0], 0.0)
    u_s[0:CV_HALO, :] = prev[:, 0:CV_C] * _sigmoid(prev[:, CV_C:])
    cur = cur_ref[0]
    u_s[CV_HALO:CV_HALO + rows, :] = cur[:, 0:CV_C] * _sigmoid(cur[:, CV_C:])
    span = rows + CV_HALO - SUBLANES
    for ph in range(1, SUBLANES):
        sh_s[ph - 1, 0:span, :] = u_s[pl.ds(ph, span), :]
    base = CV_HALO - CV_K + 1
    acc = b_ref[...]
    for t in range(CV_K):
        blk, ph = divmod(base + t, SUBLANES)
        src = u_s if ph == 0 else sh_s.at[ph - 1]
        acc = acc + w_ref[t:t + 1, :] * src[blk * SUBLANES:blk * SUBLANES + rows, :]
    mu = jnp.mean(acc, axis=-1, keepdims=True)
    xc = acc - mu
    y = xc * lax.rsqrt(jnp.mean(xc * xc, axis=-1, keepdims=True) + EPS)
    o_ref[0] = _silu(y * lg_ref[...] + lb_ref[...]).astype(BF16)


def _conformer(cv, w, bias, ln_g, ln_b):
    b, s, _ = cv.shape
    rows = R_CV
    per = rows // CV_HALO
    const = lambda shape: pl.BlockSpec(shape, lambda bi, i: (0,) * len(shape))
    return pl.pallas_call(
        _conformer_kernel,
        grid=(b, s // rows),
        in_specs=[
            pl.BlockSpec((1, rows, 2 * CV_C), lambda bi, i: (bi, i, 0)),
            pl.BlockSpec((1, CV_HALO, 2 * CV_C), lambda bi, i: (bi, jnp.maximum(i * per - 1, 0), 0)),
            const((CV_K, CV_C)), const((1, CV_C)), const((1, CV_C)), const((1, CV_C)),
        ],
        out_specs=pl.BlockSpec((1, rows, CV_C), lambda bi, i: (bi, i, 0)),
        out_shape=jax.ShapeDtypeStruct((b, s, CV_C), BF16),
        scratch_shapes=[pltpu.VMEM((rows + CV_HALO, CV_C), F32),
                        pltpu.VMEM((SUBLANES - 1, rows + CV_HALO - SUBLANES, CV_C), F32)],
        compiler_params=pltpu.CompilerParams(vmem_limit_bytes=VMEM_LIMIT),
        name="conformer",
    )(cv, cv, w, bias.reshape(1, CV_C), ln_g.reshape(1, CV_C), ln_b.reshape(1, CV_C))


def _dense_ffn_kernel(x_ref, ya_ref, yd_ref, yc_ref, wa_ref, wd_ref, wc_ref, gt1_ref, g_ref, sc_ref, sh_ref, gt2_ref,
                      fg_ref, wg_ref, wu_ref, wdn_ref, o_ref, *, final_norm):
    rows = [slice(r0, r0 + FFN_ROWS) for r0 in range(0, x_ref.shape[0], FFN_ROWS)]
    wa, wd, wc = wa_ref[...], wd_ref[...], wc_ref[...]
    y = [_dot(ya_ref[sl, :], wa) + _dot(yd_ref[sl, :], wd) + _dot(yc_ref[sl, :], wc) for sl in rows]
    x1 = [x_ref[sl, :] + gt1_ref[0] * y_ for sl, y_ in zip(rows, y)]
    h = [_rms_modulate(x_, g_ref[...], sc_ref[0], sh_ref[0]).astype(BF16) for x_ in x1]
    ff = wg_ref.shape[1]
    acc = [None] * len(rows)
    for c0 in range(0, ff, FF_PIECE):
        c1 = min(c0 + FF_PIECE, ff)
        a = [(_silu(_dot(h_, wg_ref[:, c0:c1])) * _dot(h_, wu_ref[:, c0:c1])).astype(BF16) for h_ in h]
        part = [_dot(a_, wdn_ref[c0:c1, :]) for a_ in a]
        acc = [p if q is None else q + p for p, q in zip(part, acc)]
    for sl, x_, acc_ in zip(rows, x1, acc):
        x2 = x_ + gt2_ref[0] * acc_
        if final_norm:
            x2 = x2 * lax.rsqrt(jnp.mean(x2 * x2, axis=-1, keepdims=True) + EPS) * fg_ref[...]
        o_ref[sl, :] = x2


def _route_kernel(x_ref, ya_ref, yd_ref, yc_ref, wa_ref, wd_ref, wc_ref, gt1_ref, g_ref, sc_ref, sh_ref,
                  rwt_ref, rb_ref, upper_ref, x1_ref, h_ref, comb_ref, pos_ref, cnt_ref):
    subs = [slice(n * MOE_SUB, (n + 1) * MOE_SUB) for n in range(x_ref.shape[0] // MOE_SUB)]
    wa, wd, wc = wa_ref[...], wd_ref[...], wc_ref[...]
    y = [_dot(ya_ref[sl, :], wa) + _dot(yd_ref[sl, :], wd) + _dot(yc_ref[sl, :], wc) for sl in subs]
    x1 = [x_ref[sl, :] + gt1_ref[0] * y_ for sl, y_ in zip(subs, y)]
    h = [_rms_modulate(x_, g_ref[...], sc_ref[0], sh_ref[0]) for x_ in x1]
    for sl, x_, h_ in zip(subs, x1, h):
        x1_ref[sl, :] = x_
        h_ref[sl, :] = h_.astype(BF16)
    w_hi, w_mid, _ = _split3(rwt_ref[...])
    w_cat = jnp.concatenate([w_hi, w_hi, w_mid], axis=1)
    pieces = [_split3(h_) for h_ in h]
    logits = [_dot_nt(w_cat, jnp.concatenate([hi, mid, hi], axis=1)) + rb_ref[...] for hi, mid, _ in pieces]
    row = lax.broadcasted_iota(jnp.int32, logits[0].shape, 0)
    lane = lax.broadcasted_iota(jnp.int32, (N_EXPERTS, LANES), 1)
    cnt = jnp.zeros((N_EXPERTS, LANES), F32)
    for n, (sl, lg) in enumerate(zip(subs, logits)):
        m1 = jnp.max(lg, axis=0, keepdims=True)
        i1 = jnp.min(jnp.where(lg == m1, row, N_EXPERTS), axis=0, keepdims=True)
        rest = jnp.where(row == i1, -jnp.inf, lg)
        m2 = jnp.max(rest, axis=0, keepdims=True)
        i2 = jnp.min(jnp.where(rest == m2, row, N_EXPERTS), axis=0, keepdims=True)
        e2 = jnp.exp(m2 - m1)
        comb_ref[:, sl] = jnp.where(row == i1, 1.0 / (1.0 + e2), 0.0) + jnp.where(row == i2, e2 / (1.0 + e2), 0.0)
        sel = jnp.where((row == i1) | (row == i2), 1.0, 0.0)
        rank = _dot(sel.astype(BF16), upper_ref[...])
        pos_ref[:, sl] = jnp.where(sel > 0.0, rank, -1.0)
        cnt = cnt + jnp.where(lane == n, jnp.sum(sel, axis=1, keepdims=True), 0.0)
    cnt_ref[0] = cnt.astype(jnp.int32)


def _route(x, ys, w_out, mods, g, rwt, rb, s):
    t, d = x.shape
    tm = TM_ROUTE
    assert s % tm == 0, "a tile must not straddle two sequences (one adaLN row set per tile)"
    gt1, sc, sh, _ = mods
    tok = lambda width: pl.BlockSpec((tm, width), lambda i: (i, 0))
    const = lambda shape: pl.BlockSpec(shape, lambda i: (0,) * len(shape))
    vec = pl.BlockSpec((1, 1, d), lambda i: ((i * tm) // s, 0, 0))
    col = pl.BlockSpec((N_EXPERTS, tm), lambda i: (0, i))
    upper = np.triu(np.ones((MOE_SUB, MOE_SUB), np.float32), 1)
    return pl.pallas_call(
        _route_kernel,
        grid=(t // tm,),
        in_specs=[tok(d), tok(ATTN_QW), tok(DN_W), tok(CV_C), const((ATTN_QW, d)), const((DN_W, d)), const((CV_C, d)),
                  vec, const((1, d)), vec, vec, const((N_EXPERTS, d)), const((N_EXPERTS, 1)), const((MOE_SUB, MOE_SUB))],
        out_specs=[tok(d), tok(d), col, col, pl.BlockSpec((1, N_EXPERTS, LANES), lambda i: (i, 0, 0))],
        out_shape=[jax.ShapeDtypeStruct((t, d), F32), jax.ShapeDtypeStruct((t, d), BF16),
                   jax.ShapeDtypeStruct((N_EXPERTS, t), F32), jax.ShapeDtypeStruct((N_EXPERTS, t), F32),
                   jax.ShapeDtypeStruct((t // tm, N_EXPERTS, LANES), jnp.int32)],
        compiler_params=pltpu.CompilerParams(vmem_limit_bytes=VMEM_LIMIT),
        name="route",
    )(x, *ys, *w_out, gt1, g, sc, sh, rwt, rb, jnp.asarray(upper, dtype=BF16))


def _moe_sparse_kernel(cnt_ref, off_ref, tot_ref, base_ref, h_ref, x1_hbm, comb_ref, pos_ref, gt2_ref, fg_ref,
                       wg_hbm, wu_hbm, wdn_hbm, o_hbm, xg_s, wrow_s, yw_s, w_s, w_sem, acc_s, x1_s, out_s, io_sem,
                       *, final_norm):
    i = pl.program_id(0)
    e = pl.program_id(1)
    n_e = pl.num_programs(1)
    n_sub = h_ref.shape[0] // MOE_SUB
    d = h_ref.shape[1]
    win_row = lax.broadcasted_iota(jnp.int32, (MOE_WIN, MOE_SUB), 0)

    step = i * n_e + e
    n_steps = pl.num_programs(0) * n_e

    def weight_copies(expert, slot):
        return [pltpu.make_async_copy(src.at[expert], w_s.at[slot, k], w_sem.at[slot, k])
                for k, src in enumerate((wg_hbm, wu_hbm, wdn_hbm))]

    @pl.when(step == 0)
    def _():
        for ahead in range(W_SLOTS - 1):
            for cp in weight_copies(ahead % N_EXPERTS, ahead):
                cp.start()

    @pl.when(step + (W_SLOTS - 1) < n_steps)
    def _():
        for cp in weight_copies(lax.rem(e + (W_SLOTS - 1), n_e), lax.rem(step + (W_SLOTS - 1), W_SLOTS)):
            cp.start()

    slot = lax.rem(step, W_SLOTS)
    for cp in weight_copies(e, slot):
        cp.wait()
    wg_ref, wu_ref, wdn_ref = (w_s.at[slot, k] for k in range(3))

    def gather(sub, p):
        k = (i * n_sub + sub) * N_EXPERTS + e
        lanes = slice(sub * MOE_SUB, (sub + 1) * MOE_SUB)
        pos = pos_ref[pl.ds(e, 1), lanes].astype(jnp.int32)
        sel = win_row + p * MOE_WIN == pos
        r0 = pl.multiple_of(off_ref[k] + p * MOE_WIN, PACK_ROWS)
        rows = _dot(jnp.where(sel, 1.0, 0.0).astype(BF16), h_ref[lanes, :])
        xg_s[pl.ds(r0, MOE_WIN), :] = rows.astype(BF16)
        w = jnp.sum(jnp.where(sel, comb_ref[pl.ds(e, 1), lanes], 0.0), axis=1, keepdims=True)
        wrow_s[pl.ds(r0, MOE_WIN), :] = jnp.broadcast_to(w, (MOE_WIN, LANES))

    most = cnt_ref[i * n_sub * N_EXPERTS + e]
    for sub in range(1, n_sub):
        most = jnp.maximum(most, cnt_ref[(i * n_sub + sub) * N_EXPERTS + e])
    for p in range(MOE_SUB // MOE_WIN - 1, 0, -1):
        @pl.when(most > p * MOE_WIN)
        def _():
            for sub in range(n_sub):
                @pl.when(cnt_ref[(i * n_sub + sub) * N_EXPERTS + e] > p * MOE_WIN)
                def _():
                    gather(sub, p)
    for sub in range(n_sub):
        gather(sub, 0)

    total = pl.multiple_of(tot_ref[i * N_EXPERTS + e], PACK_ROWS)
    base = pl.multiple_of(base_ref[i * N_EXPERTS + e], PACK_ROWS)
    xg_s[pl.ds(total, MOE_TAIL), :] = jnp.zeros((MOE_TAIL, d), BF16)
    wrow_s[pl.ds(total, MOE_TAIL), :] = jnp.zeros((MOE_TAIL, LANES), F32)

    def ffn(r0, rows):
        xc = xg_s[pl.ds(r0, rows), :]
        a = (_silu(_dot(xc, wg_ref[...])) * _dot(xc, wu_ref[...])).astype(BF16)
        y = _dot(a, wdn_ref[...]) * jnp.tile(wrow_s[pl.ds(r0, rows), :], (1, d // LANES))
        yw_s[pl.ds(base + r0, rows), :] = y.astype(BF16)

    n_full = total // MOE_CHUNK
    rem = total - n_full * MOE_CHUNK
    n_chunks = n_full + jnp.where(rem > MOE_CHUNK // 2, 1, 0)
    ffn(0, MOE_CHUNK)

    def full_chunk(c, carry):
        ffn(pl.multiple_of(c * MOE_CHUNK, MOE_CHUNK), MOE_CHUNK)
        return carry

    lax.fori_loop(1, n_chunks, full_chunk, 0)
    half = (rem > 0) & (rem <= MOE_CHUNK // 2) & (n_full > 0)

    @pl.when(half)
    def _():
        ffn(pl.multiple_of(n_full * MOE_CHUNK, MOE_CHUNK), MOE_CHUNK // 2)

    covered = jnp.maximum(n_chunks, 1) * MOE_CHUNK + jnp.where(half, MOE_CHUNK // 2, 0)
    yw_s[pl.ds(pl.multiple_of(base + covered, PACK_ROWS), MOE_TAIL), :] = jnp.zeros((MOE_TAIL, d), BF16)

    @pl.when(e == n_e - 1)
    def _():
        row0 = i * (n_sub * MOE_SUB)

        def rows_of(ref, sub):
            return ref.at[pl.ds(pl.multiple_of(row0 + sub * MOE_SUB, MOE_SUB), MOE_SUB), :]

        def x1_copy(sub):
            return pltpu.make_async_copy(rows_of(x1_hbm, sub), x1_s.at[sub % 2], io_sem.at[0, sub % 2])

        def out_copy(sub):
            return pltpu.make_async_copy(out_s.at[sub % 2], rows_of(o_hbm, sub), io_sem.at[1, sub % 2])

        x1_copy(0).start()
        for sub in range(n_sub):
            if sub + 1 < n_sub:
                x1_copy(sub + 1).start()
            lanes = slice(sub * MOE_SUB, (sub + 1) * MOE_SUB)
            first = (i * n_sub + sub) * N_EXPERTS

            def windows(p):
                sels, rows = [], []
                for ex in range(N_EXPERTS):
                    r0 = pl.multiple_of(base_ref[i * N_EXPERTS + ex] + off_ref[first + ex] + p * MOE_WIN, PACK_ROWS)
                    pos = pos_ref[ex:ex + 1, lanes].astype(jnp.int32)
                    sels.append(jnp.where(win_row + p * MOE_WIN == pos, 1.0, 0.0).astype(BF16))
                    rows.append(yw_s[pl.ds(r0, MOE_WIN), :])
                return _dot_tn(jnp.concatenate(sels, axis=0), jnp.concatenate(rows, axis=0))

            acc_s[...] = windows(0)
            most = cnt_ref[first]
            for ex in range(1, N_EXPERTS):
                most = jnp.maximum(most, cnt_ref[first + ex])
            for p in range(1, MOE_SUB // MOE_WIN):
                @pl.when(most > p * MOE_WIN)
                def _():
                    acc_s[...] += windows(p)
            x1_copy(sub).wait()
            x2 = x1_s[sub % 2] + gt2_ref[0] * acc_s[...]
            if final_norm:
                x2 = x2 * lax.rsqrt(jnp.mean(x2 * x2, axis=-1, keepdims=True) + EPS) * fg_ref[...]
            if sub >= 2:
                out_copy(sub - 2).wait()
            out_s[sub % 2] = x2
            out_copy(sub).start()
        for sub in range(max(n_sub - 2, 0), n_sub):
            out_copy(sub).wait()


def _dense_ffn(x, ys, w_out, mods, g, final_g, wg, wu, wdn, s, final_norm):
    t, d = x.shape
    tm = TM_FFN
    assert s % tm == 0, "a tile must not straddle two sequences (one adaLN row set per tile)"
    ff = wg.shape[1]
    gt1, sc, sh, gt2 = mods
    tok = lambda width: pl.BlockSpec((tm, width), lambda i: (i, 0))
    const = lambda shape: pl.BlockSpec(shape, lambda i: (0,) * len(shape), pipeline_mode=pl.Buffered(1))
    vec = pl.BlockSpec((1, 1, d), lambda i: ((i * tm) // s, 0, 0))
    return pl.pallas_call(
        functools.partial(_dense_ffn_kernel, final_norm=final_norm),
        grid=(t // tm,),
        in_specs=[tok(d), tok(ATTN_QW), tok(DN_W), tok(CV_C), const((ATTN_QW, d)), const((DN_W, d)), const((CV_C, d)),
                  vec, const((1, d)), vec, vec, vec, const((1, d)),
                  const((d, ff)), const((d, ff)), const((ff, d))],
        out_specs=tok(d),
        out_shape=jax.ShapeDtypeStruct((t, d), F32),
        compiler_params=pltpu.CompilerParams(vmem_limit_bytes=VMEM_LIMIT),
        name="dense_ffn",
    )(x, *ys, *w_out, gt1, g, sc, sh, gt2, final_g, wg, wu, wdn)


def _moe_sparse(h, x1, comb, pos, cnt, gt2, final_g, wg, wu, wdn, s, final_norm):
    t, d = x1.shape
    ts = TS_MOE
    n_e, _, ffe = wg.shape
    n_sub = ts // MOE_SUB
    assert s % ts == 0 and t % ts == 0, "a super-tile must not straddle two sequences (one adaLN gate row per tile)"
    counts = cnt[:, :, :TM_ROUTE // MOE_SUB].transpose(0, 2, 1).reshape(t // ts, n_sub, n_e)
    padded = -(-counts // PACK_ROWS) * PACK_ROWS
    offs = jnp.cumsum(padded, axis=1) - padded
    tots = jnp.sum(padded, axis=1)
    bases = jnp.cumsum(tots, axis=1) - tots
    cap_one = -(-(ts + n_sub * PACK_ROWS + MOE_TAIL) // PACK_ROWS) * PACK_ROWS
    cap_all = -(-(N_TOP * ts + n_e * n_sub * PACK_ROWS + MOE_CHUNK + MOE_TAIL) // PACK_ROWS) * PACK_ROWS
    tok = lambda width: pl.BlockSpec((ts, width), lambda i, e, *_: (i, 0))
    col = pl.BlockSpec((n_e, ts), lambda i, e, *_: (0, i))
    hbm = pl.BlockSpec(memory_space=pl.ANY)
    assert ffe == d, "the weight ring holds the three expert matrices in one (3, d, ffe) slot"
    flat = lambda a: a.reshape(-1).astype(jnp.int32)
    grid_spec = pltpu.PrefetchScalarGridSpec(
        num_scalar_prefetch=4,
        grid=(t // ts, n_e),
        in_specs=[tok(d), hbm, col, col,
                  pl.BlockSpec((1, 1, d), lambda i, e, *_: ((i * ts) // s, 0, 0)),
                  pl.BlockSpec((1, d), lambda i, e, *_: (0, 0)),
                  hbm, hbm, hbm],
        out_specs=hbm,
        scratch_shapes=[pltpu.VMEM((cap_one, d), BF16), pltpu.VMEM((cap_one, LANES), F32),
                        pltpu.VMEM((cap_all, d), BF16),
                        pltpu.VMEM((W_SLOTS, 3, d, ffe), BF16), pltpu.SemaphoreType.DMA((W_SLOTS, 3)),
                        pltpu.VMEM((MOE_SUB, d), F32), pltpu.VMEM((2, MOE_SUB, d), F32), pltpu.VMEM((2, MOE_SUB, d), F32),
                        pltpu.SemaphoreType.DMA((2, 2))],
    )
    return pl.pallas_call(
        functools.partial(_moe_sparse_kernel, final_norm=final_norm),
        grid_spec=grid_spec,
        out_shape=jax.ShapeDtypeStruct((t, d), F32),
        compiler_params=pltpu.CompilerParams(vmem_limit_bytes=VMEM_LIMIT_MOE,
                                             dimension_semantics=("arbitrary", "arbitrary")),
        name="moe_sparse",
    )(flat(counts), flat(offs), flat(tots), flat(bases), h, x1, comb, pos, gt2, final_g, wg, wu, wdn)


def _rope_cos_sin(positions):
    inv_freq = ROPE_THETA ** (-jnp.arange(0, ROT_DIM, 2, dtype=F32) / ROT_DIM)
    ang = positions.astype(F32)[:, None, :] * inv_freq[None, :, None]
    return jnp.concatenate([jnp.cos(ang), jnp.sin(ang)], axis=1)


def _pack_w_in(w):
    d = w.shape[0]
    o1 = ATTN_W
    o2 = o1 + 3 * DN_W
    o3 = o2 + 2 * DN_H
    o4 = o3 + DN_W
    gates = jnp.zeros((d, LANES), w.dtype).at[:, :2 * DN_H].set(w[:, o2:o3])
    return jnp.concatenate([w[:, :o2], gates, w[:, o3:o4], w[:, o4:]], axis=1).astype(BF16)


def kernel(x, c, positions, ada_w, ada_b, norm_mix_g, norm_ffn_g, w_in, attn_sinks, dn_conv_w, dn_a_log, dn_dt_bias, dn_norm_g, cv_dw_w, cv_dw_b, cv_ln_g, cv_ln_b, w_out, ffn_w_gate, ffn_w_up, ffn_w_down, router_w, router_b, moe_w_gate, moe_w_up, moe_w_down, final_norm_g):
    b, s, d = x.shape
    depth = w_in.shape[0]
    mod = _adaln(c, ada_w, ada_b)
    cos_sin = _rope_cos_sin(positions)
    final_g = final_norm_g.reshape(1, d)
    vec = lambda t: t.reshape(b, 1, d)
    for l in range(depth):
        sh1, sc1, gt1 = (vec(t) for t in jnp.split(mod[l, 0], 3, axis=-1))
        sh2, sc2, gt2 = (vec(t) for t in jnp.split(mod[l, 1], 3, axis=-1))
        attn, dnqkv, gate, dz, cv = _inproj(x, norm_mix_g[l].reshape(1, d), sc1, sh1, cos_sin, _pack_w_in(w_in[l]))
        y_attn = _attention(attn, attn_sinks[l].astype(F32))
        y_dn = _deltanet(dnqkv, gate, dz, dn_conv_w[l], dn_a_log[l], dn_dt_bias[l], dn_norm_g[l])
        y_cv = _conformer(cv, cv_dw_w[l], cv_dw_b[l], cv_ln_g[l], cv_ln_b[l])
        t = b * s
        ys = (y_attn.reshape(t, ATTN_QW), y_dn.reshape(t, DN_W), y_cv.reshape(t, CV_C))
        wo = w_out[l].astype(BF16)
        wos = (wo[:ATTN_QW], wo[ATTN_QW:ATTN_QW + DN_W], wo[ATTN_QW + DN_W:])
        mods = (gt1, sc2, sh2, gt2)
        g2 = norm_ffn_g[l].reshape(1, d)
        last = l == depth - 1
        j = l // 2
        if l % 2 == 0:
            x2 = _dense_ffn(x.reshape(t, d), ys, wos, mods, g2, final_g, ffn_w_gate[j].astype(BF16),
                            ffn_w_up[j].astype(BF16), ffn_w_down[j].astype(BF16), s, last)
        else:
            x1, h2, comb, pos, cnt = _route(x.reshape(t, d), ys, wos, mods, g2, router_w[j].astype(F32).T,
                                            router_b[j].astype(F32).reshape(N_EXPERTS, 1), s)
            x2 = _moe_sparse(h2, x1, comb, pos, cnt, gt2, final_g, moe_w_gate[j].astype(BF16),
                             moe_w_up[j].astype(BF16), moe_w_down[j].astype(BF16), s, last)
        x = x2.reshape(b, s, d)
    return x
```

```python
import functools

import numpy as np
import jax
import jax.numpy as jnp
from jax import lax
from jax.experimental import pallas as pl
from jax.experimental.pallas import tpu as pltpu

F32 = jnp.float32
BF16 = jnp.bfloat16
HIGHEST = lax.Precision.HIGHEST

HEAD_DIM = 64
ATTN_HQ = 8
ATTN_HKV = 2
ATTN_GROUP = ATTN_HQ // ATTN_HKV
ATTN_BLOCK = 128
ROT_DIM = HEAD_DIM // 4
ROPE_THETA = 500000.0
DN_H = 4
DN_D = 64
DN_CONV = 4
DN_CHUNK = 64
CV_C = 256
CV_K = 31
ATTN_QW = ATTN_HQ * HEAD_DIM
ATTN_KVW = ATTN_HKV * HEAD_DIM
ATTN_W = ATTN_QW + 2 * ATTN_KVW
DN_W = DN_H * DN_D
N_EXPERTS = 8
EPS = 1e-6
LANES = 128
SUBLANES = 8
PACK_ROWS = 16
VMEM_LIMIT = 48 * 1024 * 1024
VMEM_LIMIT_BIG = 56 * 1024 * 1024

ADA_TN = 1024
TM_PROJ = 1024
TQ_ATTN = 2048
ATTN_GROUP_UNITS = 64
R_DN = 512
R_CV = 1024
TM_FFN = 1024
TM_ROUTE = 1024
FFN_ROWS = 256
FF_PIECE = 1536
CV_HALO = 32
DN_HALO = 8
TS_MOE = 2048
MOE_SUB = 256
MOE_WIN = 128
MOE_CHUNK = 256
MOE_TAIL = 384
N_TOP = 2
W_SLOTS = 3


def _dot(a, b, precision=None):
    return jnp.dot(a, b, preferred_element_type=F32, precision=precision)


def _dot_nt(a, b):
    return lax.dot_general(a, b, (((1,), (1,)), ((), ())), preferred_element_type=F32)


def _dot_tn(a, b):
    return lax.dot_general(a, b, (((0,), (0,)), ((), ())), preferred_element_type=F32)


def _split3(x):
    hi = x.astype(BF16)
    rest = x - hi.astype(F32)
    mid = rest.astype(BF16)
    return hi, mid, (rest - mid.astype(F32)).astype(BF16)


def _sigmoid(x):
    return 1.0 / (1.0 + jnp.exp(-x))


def _silu(x):
    return x * _sigmoid(x)


def _rms_modulate(x, g, sc, sh):
    y = x * lax.rsqrt(jnp.mean(x * x, axis=-1, keepdims=True) + EPS)
    return (y * g) * (1.0 + sc) + sh


def _adaln_kernel(c_ref, w_ref, b_ref, o_ref):
    o_ref[0] = _dot(_silu(c_ref[...]), w_ref[0], HIGHEST) + b_ref[0]


def _adaln(c, ada_w, ada_b):
    depth, _, d, d3 = ada_w.shape
    b = c.shape[0]
    rows = SUBLANES
    cp = jnp.zeros((rows, d), F32).at[:b].set(c)
    w = ada_w.reshape(depth * 2, d, d3)
    bias = ada_b.reshape(depth * 2, 1, d3)
    tn = ADA_TN
    out = pl.pallas_call(
        _adaln_kernel,
        grid=(depth * 2, d3 // tn),
        in_specs=[
            pl.BlockSpec((rows, d), lambda i, j: (0, 0)),
            pl.BlockSpec((1, d, tn), lambda i, j: (i, 0, j)),
            pl.BlockSpec((1, 1, tn), lambda i, j: (i, 0, j)),
        ],
        out_specs=pl.BlockSpec((1, rows, tn), lambda i, j: (i, 0, j)),
        out_shape=jax.ShapeDtypeStruct((depth * 2, rows, d3), F32),
        compiler_params=pltpu.CompilerParams(vmem_limit_bytes=VMEM_LIMIT),
        name="adaln",
    )(cp, w, bias)
    return out[:, :b].reshape(depth, 2, b, d3)


def _inproj_kernel(x_ref, g_ref, sc_ref, sh_ref, cs_ref, place_ref, ones_ref, w_ref,
                   attn_ref, dnqkv_ref, gate_ref, dz_ref, cv_ref):
    h = _rms_modulate(x_ref[0], g_ref[...], sc_ref[0], sh_ref[0]).astype(BF16)
    a = _dot(h, w_ref[:, 0:ATTN_W])
    tab = _dot_tn(jnp.concatenate(_split3(cs_ref[0]), axis=0), place_ref[...]) + ones_ref[...]
    rc, ra, rb = tab[:, 0:LANES], tab[:, LANES:2 * LANES], tab[:, 2 * LANES:3 * LANES]
    half = ROT_DIM // 2
    for j in range((ATTN_QW + ATTN_KVW) // LANES):
        t = a[:, j * LANES:(j + 1) * LANES]
        r = t * rc + pltpu.roll(t, LANES - half, 1) * ra + pltpu.roll(t, half, 1) * rb
        if j < ATTN_QW // LANES:
            r = r * (HEAD_DIM ** -0.5)
        attn_ref[0, :, j * LANES:(j + 1) * LANES] = r.astype(BF16)
    attn_ref[0, :, ATTN_QW + ATTN_KVW:ATTN_W] = a[:, ATTN_QW + ATTN_KVW:ATTN_W].astype(BF16)
    o = ATTN_W
    dnqkv_ref[0] = _dot(h, w_ref[:, o:o + 3 * DN_W])
    o += 3 * DN_W
    gate_ref[0] = _dot(h, w_ref[:, o:o + LANES])
    o += LANES
    dz_ref[0] = _dot(h, w_ref[:, o:o + DN_W])
    o += DN_W
    cv_ref[0] = _dot(h, w_ref[:, o:o + 2 * CV_C])


def _rope_placement():
    half = ROT_DIM // 2
    place = np.zeros((2 * half, 3 * LANES), np.float32)
    ones = np.zeros((1, 3 * LANES), np.float32)
    for lane in range(LANES):
        dim = lane % HEAD_DIM
        if dim < ROT_DIM:
            place[dim % half, lane] = 1.0
            if dim < half:
                place[half + dim, LANES + lane] = -1.0
            else:
                place[half + dim - half, 2 * LANES + lane] = 1.0
        else:
            ones[0, lane] = 1.0
    return np.tile(place, (3, 1)), ones


def _inproj(x, g, sc, sh, cos_sin, w):
    b, s, d = x.shape
    tm = TM_PROJ
    n = w.shape[1]
    row = lambda width: pl.BlockSpec((1, tm, width), lambda bi, i: (bi, i, 0))
    vec = pl.BlockSpec((1, 1, d), lambda bi, i: (bi, 0, 0))
    const = lambda shape: pl.BlockSpec(shape, lambda bi, i: (0,) * len(shape))
    widths = (ATTN_W, 3 * DN_W, LANES, DN_W, 2 * CV_C)
    dtypes = (BF16, F32, F32, F32, F32)
    place, ones = _rope_placement()
    return pl.pallas_call(
        _inproj_kernel,
        grid=(b, s // tm),
        in_specs=[row(d), const((1, d)), vec, vec,
                  pl.BlockSpec((1, ROT_DIM, tm), lambda bi, i: (bi, 0, i)), const(place.shape), const(ones.shape),
                  const((d, n))],
        out_specs=[row(wd) for wd in widths],
        out_shape=[jax.ShapeDtypeStruct((b, s, wd), dt) for wd, dt in zip(widths, dtypes)],
        compiler_params=pltpu.CompilerParams(vmem_limit_bytes=VMEM_LIMIT),
        name="inproj",
    )(x, g, sc, sh, cos_sin, jnp.asarray(place, dtype=BF16), jnp.asarray(ones), w)


def _attn_kernel(sink_ref, cur_ref, prev_ref, o_ref):
    i = pl.program_id(1)
    blk = ATTN_BLOCK
    cur = cur_ref[0]
    prev = prev_ref[0]
    kv_all = jnp.concatenate([prev[:, ATTN_QW:], cur[:, ATTN_QW:]], axis=0)
    band_w = 2 * blk
    qi = lax.broadcasted_iota(jnp.int32, (blk, 2 * band_w), 0)
    kj = lax.broadcasted_iota(jnp.int32, (blk, 2 * band_w), 1) % band_w
    diff = qi + blk - kj
    band = (diff >= 0) & (diff < blk)
    band_first = band & (kj >= jnp.where(i == 0, blk, 0))
    first_head = lax.broadcasted_iota(jnp.int32, (blk, LANES), 1) < HEAD_DIM
    zeros = jnp.zeros((band_w, HEAD_DIM), BF16)

    def blockdiag(t):
        return jnp.concatenate([jnp.concatenate([t, zeros], axis=1), jnp.concatenate([zeros, t], axis=1)], axis=0)

    units = [(r, j, pr) for r in range(cur.shape[0] // blk) for j in range(ATTN_HKV) for pr in range(ATTN_GROUP // 2)]

    def scores(r, j, pr):
        kb = kv_all[r * blk:(r + 2) * blk, j * HEAD_DIM:(j + 1) * HEAD_DIM]
        slab = (j * (ATTN_GROUP // 2) + pr) * LANES
        return _dot_nt(cur[r * blk:(r + 1) * blk, slab:slab + LANES], blockdiag(kb))

    lane_row = lax.broadcasted_iota(jnp.int32, (1, 2 * band_w), 1)
    not_row0 = lax.broadcasted_iota(jnp.int32, (band_w, HEAD_DIM), 0) > 0
    def value_rows(r, j):
        vb = kv_all[r * blk:(r + 2) * blk, ATTN_KVW + j * HEAD_DIM:ATTN_KVW + (j + 1) * HEAD_DIM]
        return blockdiag(jnp.where(not_row0, vb, jnp.zeros_like(vb)))

    for g0 in range(0, len(units), ATTN_GROUP_UNITS):
        group = units[g0:g0 + ATTN_GROUP_UNITS]
        raw = [scores(*u) for u in group]
        sc = []
        for (r, j, pr), s_raw in zip(group, raw):
            hq = j * ATTN_GROUP + 2 * pr
            fill = jnp.where(lane_row == 0, sink_ref[hq], jnp.where(lane_row == band_w, sink_ref[hq + 1], -jnp.inf))
            sc.append(jnp.where(band_first if r == 0 else band, s_raw, fill))
        halves = [[s_[:, t * band_w:(t + 1) * band_w] for t in range(2)] for s_ in sc]
        mx = [[jnp.max(h_, axis=-1, keepdims=True) for h_ in hs] for hs in halves]
        p = [[jnp.exp(h_ - m_) for h_, m_ in zip(hs, ms)] for hs, ms in zip(halves, mx)]
        rs = [[1.0 / jnp.sum(p_, axis=-1, keepdims=True) for p_ in ps] for ps in p]
        p16 = [jnp.concatenate([p_.astype(BF16) for p_ in ps], axis=1) for ps in p]
        outs = [_dot(p16[n], value_rows(r, j)) for n, (r, j, pr) in enumerate(group)]
        for n, (r, j, pr) in enumerate(group):
            o = outs[n] * jnp.where(first_head, rs[n][0], rs[n][1])
            slab = (j * (ATTN_GROUP // 2) + pr) * LANES
            o_ref[0, r * blk:(r + 1) * blk, slab:slab + LANES] = o.astype(BF16)


def _attention(attn, sinks):
    b, s, _ = attn.shape
    tq = TQ_ATTN
    per = tq // ATTN_BLOCK
    return pl.pallas_call(
        _attn_kernel,
        grid=(b, s // tq),
        in_specs=[
            pl.BlockSpec(memory_space=pltpu.SMEM),
            pl.BlockSpec((1, tq, ATTN_W), lambda bi, i: (bi, i, 0)),
            pl.BlockSpec((1, ATTN_BLOCK, ATTN_W), lambda bi, i: (bi, jnp.maximum(i * per - 1, 0), 0)),
        ],
        out_specs=pl.BlockSpec((1, tq, ATTN_QW), lambda bi, i: (bi, i, 0)),
        out_shape=jax.ShapeDtypeStruct((b, s, ATTN_QW), BF16),
        compiler_params=pltpu.CompilerParams(vmem_limit_bytes=VMEM_LIMIT),
        name="attention",
    )(sinks, attn, attn)


def _deltanet_kernel(cur_ref, prev_ref, gate_ref, dz_ref, cw_ref, alog_ref, dtb_ref, ng_ref,
                     expand_ref, tril_ref, slow_ref, ones_ref, o_ref,
                     xext, q_s, k_s, v_s, b_s, g_s, o_s, state):
    i = pl.program_id(0)
    n_seq, rows = cur_ref.shape[0], cur_ref.shape[1]
    ck = DN_CHUNK

    @pl.when(i == 0)
    def _():
        state[...] = jnp.zeros_like(state)

    ones_blk = ones_ref[...]

    def l2n(t):
        ss = _dot((t * t).astype(BF16), ones_blk)
        return t * lax.rsqrt(ss + EPS)

    for bi in range(n_seq):
        seq = slice(bi * rows, (bi + 1) * rows)
        xext[bi, 0:DN_HALO, :] = jnp.where(i > 0, prev_ref[bi], 0.0)
        xext[bi, DN_HALO:DN_HALO + rows, :] = cur_ref[bi]
        acc = cw_ref[0:1, :] * xext[bi, pl.ds(DN_HALO - DN_CONV + 1, rows), :]
        for t in range(1, DN_CONV):
            acc = acc + cw_ref[t:t + 1, :] * xext[bi, pl.ds(DN_HALO - DN_CONV + 1 + t, rows), :]
        qkv = _silu(acc)
        q_s[seq, :] = l2n(qkv[:, 0:DN_W]) * (DN_D ** -0.5)
        k_s[seq, :] = l2n(qkv[:, DN_W:2 * DN_W])
        v_s[seq, :] = qkv[:, 2 * DN_W:3 * DN_W]
        ge = _dot(jnp.concatenate(_split3(gate_ref[bi]), axis=1), expand_ref[...])
        b_s[seq, :] = _sigmoid(ge[:, 0:DN_W])
        da = ge[:, DN_W:2 * DN_W] + dtb_ref[...]
        softplus = jnp.maximum(da, 0.0) + jnp.log1p(jnp.exp(-jnp.abs(da)))
        g_s[seq, :] = -jnp.exp(alog_ref[...]) * softplus

    tril = tril_ref[...]
    slow = slow_ref[...]
    ri = lax.broadcasted_iota(jnp.int32, (ck, DN_W), 0)
    ci = lax.broadcasted_iota(jnp.int32, (ck, DN_W), 1) % ck
    incl = ri >= ci

    pw = 2 * DN_D
    n_pair = DN_W // pw
    r2 = lax.broadcasted_iota(jnp.int32, (ck, pw), 0)
    l2 = lax.broadcasted_iota(jnp.int32, (ck, pw), 1)
    c2 = l2 % ck
    strict = r2 > c2
    eye = jnp.where(r2 == c2, 1.0, 0.0)
    first = l2 < DN_D
    masks = []
    blk = 1
    while blk < ck:
        masks.append((r2 // (2 * blk) == c2 // (2 * blk)) & ((r2 // blk) % 2 == 1) & ((c2 // blk) % 2 == 0))
        blk *= 2
    same_head = (lax.broadcasted_iota(jnp.int32, (pw, pw), 0) // DN_D) == (lax.broadcasted_iota(jnp.int32, (pw, pw), 1) // DN_D)

    def blockdiag(t):
        z = jnp.zeros_like(t)
        return jnp.concatenate([jnp.where(first, t, z), jnp.where(first, z, t)], axis=0)

    nc = n_seq * rows // ck
    per_seq = rows // ck
    slabs = [slice(p * pw, (p + 1) * pw) for p in range(n_pair)]
    kbeta16, k16, q16, dec, vbeta, kbg, qg, kg, egl = ([] for _ in range(9))
    for c in range(nc):
        r0 = c * ck
        q = q_s[r0:r0 + ck, :]
        k = k_s[r0:r0 + ck, :]
        beta = b_s[r0:r0 + ck, :]
        g = g_s[r0:r0 + ck, :]
        cums = _dot(tril, jnp.concatenate([jnp.concatenate(_split3(g), axis=0),
                                           jnp.concatenate(_split3(g * slow), axis=0)], axis=1))
        gc = cums[:, 0:DN_W]
        gdiff = cums[:, DN_W:]
        decay = jnp.exp(jnp.where(incl, gdiff, -jnp.inf))
        egc = jnp.exp(gc)
        glast = gc[ck - 1:ck, :]
        kbeta = k * beta
        per_slab = ((kbeta16, kbeta.astype(BF16)), (k16, k.astype(BF16)), (q16, q.astype(BF16)), (dec, decay),
                    (vbeta, (v_s[r0:r0 + ck, :] * beta).astype(BF16)), (kbg, (kbeta * egc).astype(BF16)),
                    (qg, q * egc), (kg, (k * jnp.exp(glast - gc)).astype(BF16)), (egl, jnp.exp(glast)))
        for dst, full in per_slab:
            dst.extend(full[:, sl] for sl in slabs)
    inst = range(nc * n_pair)
    bdk = [blockdiag(k16[n]) for n in inst]
    lower = [jnp.where(strict, _dot_nt(kbeta16[n], bdk[n]) * dec[n], 0.0) for n in inst]
    a_intra = [(_dot_nt(q16[n], bdk[n]) * dec[n]).astype(BF16) for n in inst]
    lower16 = [t.astype(BF16) for t in lower]
    tinv = [eye - jnp.where(masks[0], lower[n], 0.0) for n in inst]
    for m in masks[1:]:
        t16 = [t.astype(BF16) for t in tinv]
        x16 = [_dot(jnp.where(m, lower16[n], jnp.zeros_like(lower16[n])), blockdiag(t16[n])).astype(BF16) for n in inst]
        tinv = [tinv[n] - _dot(t16[n], blockdiag(x16[n])) for n in inst]
    t16 = [t.astype(BF16) for t in tinv]
    w16 = [_dot(t16[n], blockdiag(kbg[n])).astype(BF16) for n in inst]
    u16 = [_dot(t16[n], blockdiag(vbeta[n])).astype(BF16) for n in inst]
    kw16 = [jnp.where(same_head, _dot_tn(kg[n], w16[n]), 0.0).astype(BF16) for n in inst]
    ku = [jnp.where(same_head, _dot_tn(kg[n], u16[n]), 0.0) for n in inst]
    qeff16 = [(qg[n] - _dot(a_intra[n], blockdiag(w16[n]))).astype(BF16) for n in inst]
    au = [_dot(a_intra[n], blockdiag(u16[n])) for n in inst]
    st = [[state[bi, p] for p in range(n_pair)] for bi in range(n_seq)]
    for c in range(per_seq):
        for bi in range(n_seq):
            outs = []
            for p in range(n_pair):
                n = (bi * per_seq + c) * n_pair + p
                s16 = st[bi][p].astype(BF16)
                outs.append(_dot(qeff16[n], s16) + au[n])
                st[bi][p] = st[bi][p] * egl[n] - _dot(kw16[n], s16) + ku[n]
            r0 = (bi * per_seq + c) * ck
            o_s[r0:r0 + ck, :] = jnp.concatenate(outs, axis=1)
    for bi in range(n_seq):
        for p in range(n_pair):
            state[bi, p] = st[bi][p]
    for bi in range(n_seq):
        o = o_s[bi * rows:(bi + 1) * rows, :]
        ms = _dot((o * o).astype(BF16), ones_blk) * (1.0 / DN_D)
        y = o * lax.rsqrt(ms + EPS) * ng_ref[...]
        o_ref[bi] = (y * _silu(dz_ref[bi])).astype(BF16)


def _deltanet(dnqkv, gate, dz, conv_w, a_log, dt_bias, norm_g):
    b, s, _ = dnqkv.shape
    rows = R_DN
    ck = DN_CHUNK
    lane_head = np.arange(2 * DN_W) // DN_D
    expand = np.tile((np.arange(LANES)[:, None] == lane_head[None, :]).astype(np.float32), (3, 1))
    tril = np.tile(np.tril(np.ones((ck, ck), np.float32)), (1, 3))
    slow = (np.arange(ck)[:, None] > (np.arange(DN_W)[None, :] % ck)).astype(np.float32)
    ones_blk = (np.arange(DN_W)[:, None] // DN_D == np.arange(DN_W)[None, :] // DN_D).astype(np.float32)
    rep = lambda t: jnp.repeat(t.astype(F32), DN_D).reshape(1, DN_W)
    const = lambda shape: pl.BlockSpec(shape, lambda i: (0,) * len(shape))
    row = lambda width: pl.BlockSpec((b, rows, width), lambda i: (0, i, 0))
    per = rows // DN_HALO
    return pl.pallas_call(
        _deltanet_kernel,
        grid=(s // rows,),
        in_specs=[
            row(3 * DN_W),
            pl.BlockSpec((b, DN_HALO, 3 * DN_W), lambda i: (0, jnp.maximum(i * per - 1, 0), 0)),
            row(LANES), row(DN_W),
            const((DN_CONV, 3 * DN_W)), const((1, DN_W)), const((1, DN_W)), const((1, DN_W)),
            const((3 * LANES, 2 * DN_W)), const((ck, 3 * ck)), const((ck, DN_W)), const((DN_W, DN_W)),
        ],
        out_specs=row(DN_W),
        out_shape=jax.ShapeDtypeStruct((b, s, DN_W), BF16),
        scratch_shapes=[
            pltpu.VMEM((b, rows + DN_HALO, 3 * DN_W), F32),
            *[pltpu.VMEM((b * rows, DN_W), F32) for _ in range(6)],
            pltpu.VMEM((b, DN_W // (2 * DN_D), 2 * DN_D, 2 * DN_D), F32),
        ],
        compiler_params=pltpu.CompilerParams(vmem_limit_bytes=VMEM_LIMIT, dimension_semantics=("arbitrary",)),
        name="deltanet",
    )(dnqkv, dnqkv, gate, dz, conv_w, rep(a_log), rep(dt_bias), jnp.tile(norm_g.astype(F32), DN_H).reshape(1, DN_W),
      jnp.asarray(expand, dtype=BF16), jnp.asarray(tril, dtype=BF16), jnp.asarray(slow), jnp.asarray(ones_blk, dtype=BF16))


def _conformer_kernel(cur_ref, prev_ref, w_ref, b_ref, lg_ref, lb_ref, o_ref, u_s, sh_s):
    i = pl.program_id(1)
    rows = cur_ref.shape[1]
    prev = jnp.where(i > 0, prev_ref[0], 0.0)
    u_s[0:CV_HALO, :] = prev[:, 0:CV_C] * _sigmoid(prev[:, CV_C:])
    cur = cur_ref[0]
    u_s[CV_HALO:CV_HALO + rows, :] = cur[:, 0:CV_C] * _sigmoid(cur[:, CV_C:])
    span = rows + CV_HALO - SUBLANES
    for ph in range(1, SUBLANES):
        sh_s[ph - 1, 0:span, :] = u_s[pl.ds(ph, span), :]
    base = CV_HALO - CV_K + 1
    acc = b_ref[...]
    for t in range(CV_K):
        blk, ph = divmod(base + t, SUBLANES)
        src = u_s if ph == 0 else sh_s.at[ph - 1]
        acc = acc + w_ref[t:t + 1, :] * src[blk * SUBLANES:blk * SUBLANES + rows, :]
    mu = jnp.mean(acc, axis=-1, keepdims=True)
    xc = acc - mu
    y = xc * lax.rsqrt(jnp.mean(xc * xc, axis=-1, keepdims=True) + EPS)
    o_ref[0] = _silu(y * lg_ref[...] + lb_ref[...]).astype(BF16)


def _conformer(cv, w, bias, ln_g, ln_b):
    b, s, _ = cv.shape
    rows = R_CV
    per = rows // CV_HALO
    const = lambda shape: pl.BlockSpec(shape, lambda bi, i: (0,) * len(shape))
    return pl.pallas_call(
        _conformer_kernel,
        grid=(b, s // rows),
        in_specs=[
            pl.BlockSpec((1, rows, 2 * CV_C), lambda bi, i: (bi, i, 0)),
            pl.BlockSpec((1, CV_HALO, 2 * CV_C), lambda bi, i: (bi, jnp.maximum(i * per - 1, 0), 0)),
            const((CV_K, CV_C)), const((1, CV_C)), const((1, CV_C)), const((1, CV_C)),
        ],
        out_specs=pl.BlockSpec((1, rows, CV_C), lambda bi, i: (bi, i, 0)),
        out_shape=jax.ShapeDtypeStruct((b, s, CV_C), BF16),
        scratch_shapes=[pltpu.VMEM((rows + CV_HALO, CV_C), F32),
                        pltpu.VMEM((SUBLANES - 1, rows + CV_HALO - SUBLANES, CV_C), F32)],
        compiler_params=pltpu.CompilerParams(vmem_limit_bytes=VMEM_LIMIT),
        name="conformer",
    )(cv, cv, w, bias.reshape(1, CV_C), ln_g.reshape(1, CV_C), ln_b.reshape(1, CV_C))


def _dense_ffn_kernel(x_ref, ya_ref, yd_ref, yc_ref, wa_ref, wd_ref, wc_ref, gt1_ref, g_ref, sc_ref, sh_ref, gt2_ref,
                      fg_ref, wg_ref, wu_ref, wdn_ref, o_ref, *, final_norm):
    rows = [slice(r0, r0 + FFN_ROWS) for r0 in range(0, x_ref.shape[0], FFN_ROWS)]
    wa, wd, wc = wa_ref[...], wd_ref[...], wc_ref[...]
    y = [_dot(ya_ref[sl, :], wa) + _dot(yd_ref[sl, :], wd) + _dot(yc_ref[sl, :], wc) for sl in rows]
    x1 = [x_ref[sl, :] + gt1_ref[0] * y_ for sl, y_ in zip(rows, y)]
    h = [_rms_modulate(x_, g_ref[...], sc_ref[0], sh_ref[0]).astype(BF16) for x_ in x1]
    ff = wg_ref.shape[1]
    acc = [None] * len(rows)
    for c0 in range(0, ff, FF_PIECE):
        c1 = min(c0 + FF_PIECE, ff)
        a = [(_silu(_dot(h_, wg_ref[:, c0:c1])) * _dot(h_, wu_ref[:, c0:c1])).astype(BF16) for h_ in h]
        part = [_dot(a_, wdn_ref[c0:c1, :]) for a_ in a]
        acc = [p if q is None else q + p for p, q in zip(part, acc)]
    for sl, x_, acc_ in zip(rows, x1, acc):
        x2 = x_ + gt2_ref[0] * acc_
        if final_norm:
            x2 = x2 * lax.rsqrt(jnp.mean(x2 * x2, axis=-1, keepdims=True) + EPS) * fg_ref[...]
        o_ref[sl, :] = x2


def _route_kernel(x_ref, ya_ref, yd_ref, yc_ref, wa_ref, wd_ref, wc_ref, gt1_ref, g_ref, sc_ref, sh_ref,
                  rwt_ref, rb_ref, upper_ref, x1_ref, h_ref, comb_ref, pos_ref, cnt_ref):
    subs = [slice(n * MOE_SUB, (n + 1) * MOE_SUB) for n in range(x_ref.shape[0] // MOE_SUB)]
    wa, wd, wc = wa_ref[...], wd_ref[...], wc_ref[...]
    y = [_dot(ya_ref[sl, :], wa) + _dot(yd_ref[sl, :], wd) + _dot(yc_ref[sl, :], wc) for sl in subs]
    x1 = [x_ref[sl, :] + gt1_ref[0] * y_ for sl, y_ in zip(subs, y)]
    h = [_rms_modulate(x_, g_ref[...], sc_ref[0], sh_ref[0]) for x_ in x1]
    for sl, x_, h_ in zip(subs, x1, h):
        x1_ref[sl, :] = x_
        h_ref[sl, :] = h_.astype(BF16)
    w_hi, w_mid, _ = _split3(rwt_ref[...])
    w_cat = jnp.concatenate([w_hi, w_hi, w_mid], axis=1)
    pieces = [_split3(h_) for h_ in h]
    logits = [_dot_nt(w_cat, jnp.concatenate([hi, mid, hi], axis=1)) + rb_ref[...] for hi, mid, _ in pieces]
    row = lax.broadcasted_iota(jnp.int32, logits[0].shape, 0)
    lane = lax.broadcasted_iota(jnp.int32, (N_EXPERTS, LANES), 1)
    cnt = jnp.zeros((N_EXPERTS, LANES), F32)
    for n, (sl, lg) in enumerate(zip(subs, logits)):
        m1 = jnp.max(lg, axis=0, keepdims=True)
        i1 = jnp.min(jnp.where(lg == m1, row, N_EXPERTS), axis=0, keepdims=True)
        rest = jnp.where(row == i1, -jnp.inf, lg)
        m2 = jnp.max(rest, axis=0, keepdims=True)
        i2 = jnp.min(jnp.where(rest == m2, row, N_EXPERTS), axis=0, keepdims=True)
        e2 = jnp.exp(m2 - m1)
        comb_ref[:, sl] = jnp.where(row == i1, 1.0 / (1.0 + e2), 0.0) + jnp.where(row == i2, e2 / (1.0 + e2), 0.0)
        sel = jnp.where((row == i1) | (row == i2), 1.0, 0.0)
        rank = _dot(sel.astype(BF16), upper_ref[...])
        pos_ref[:, sl] = jnp.where(sel > 0.0, rank, -1.0)
        cnt = cnt + jnp.where(lane == n, jnp.sum(sel, axis=1, keepdims=True), 0.0)
    cnt_ref[0] = cnt.astype(jnp.int32)


def _route(x, ys, w_out, mods, g, rwt, rb, s):
    t, d = x.shape
    tm = TM_ROUTE
    assert s % tm == 0, "a tile must not straddle two sequences (one adaLN row set per tile)"
    gt1, sc, sh, _ = mods
    tok = lambda width: pl.BlockSpec((tm, width), lambda i: (i, 0))
    const = lambda shape: pl.BlockSpec(shape, lambda i: (0,) * len(shape))
    vec = pl.BlockSpec((1, 1, d), lambda i: ((i * tm) // s, 0, 0))
    col = pl.BlockSpec((N_EXPERTS, tm), lambda i: (0, i))
    upper = np.triu(np.ones((MOE_SUB, MOE_SUB), np.float32), 1)
    return pl.pallas_call(
        _route_kernel,
        grid=(t // tm,),
        in_specs=[tok(d), tok(ATTN_QW), tok(DN_W), tok(CV_C), const((ATTN_QW, d)), const((DN_W, d)), const((CV_C, d)),
                  vec, const((1, d)), vec, vec, const((N_EXPERTS, d)), const((N_EXPERTS, 1)), const((MOE_SUB, MOE_SUB))],
        out_specs=[tok(d), tok(d), col, col, pl.BlockSpec((1, N_EXPERTS, LANES), lambda i: (i, 0, 0))],
        out_shape=[jax.ShapeDtypeStruct((t, d), F32), jax.ShapeDtypeStruct((t, d), BF16),
                   jax.ShapeDtypeStruct((N_EXPERTS, t), F32), jax.ShapeDtypeStruct((N_EXPERTS, t), F32),
                   jax.ShapeDtypeStruct((t // tm, N_EXPERTS, LANES), jnp.int32)],
        compiler_params=pltpu.CompilerParams(vmem_limit_bytes=VMEM_LIMIT),
        name="route",
    )(x, *ys, *w_out, gt1, g, sc, sh, rwt, rb, jnp.asarray(upper, dtype=BF16))


def _moe_sparse_kernel(cnt_ref, off_ref, tot_ref, base_ref, h_ref, x1_hbm, comb_ref, pos_ref, gt2_ref, fg_ref,
                       wg_hbm, wu_hbm, wdn_hbm, o_hbm, xg_s, wrow_s, yw_s, w_s, w_sem, acc_s, x1_s, out_s, io_sem,
                       *, final_norm):
    i = pl.program_id(0)
    e = pl.program_id(1)
    n_e = pl.num_programs(1)
    n_sub = h_ref.shape[0] // MOE_SUB
    d = h_ref.shape[1]
    win_row = lax.broadcasted_iota(jnp.int32, (MOE_WIN, MOE_SUB), 0)

    step = i * n_e + e
    n_steps = pl.num_programs(0) * n_e

    def weight_copies(expert, slot):
        return [pltpu.make_async_copy(src.at[expert], w_s.at[slot, k], w_sem.at[slot, k])
                for k, src in enumerate((wg_hbm, wu_hbm, wdn_hbm))]

    @pl.when(step == 0)
    def _():
        for ahead in range(W_SLOTS - 1):
            for cp in weight_copies(ahead % N_EXPERTS, ahead):
                cp.start()

    @pl.when(step + (W_SLOTS - 1) < n_steps)
    def _():
        for cp in weight_copies(lax.rem(e + (W_SLOTS - 1), n_e), lax.rem(step + (W_SLOTS - 1), W_SLOTS)):
            cp.start()

    slot = lax.rem(step, W_SLOTS)
    for cp in weight_copies(e, slot):
        cp.wait()
    wg_ref, wu_ref, wdn_ref = (w_s.at[slot, k] for k in range(3))

    def gather(sub, p):
        k = (i * n_sub + sub) * N_EXPERTS + e
        lanes = slice(sub * MOE_SUB, (sub + 1) * MOE_SUB)
        pos = pos_ref[pl.ds(e, 1), lanes].astype(jnp.int32)
        sel = win_row + p * MOE_WIN == pos
        r0 = pl.multiple_of(off_ref[k] + p * MOE_WIN, PACK_ROWS)
        rows = _dot(jnp.where(sel, 1.0, 0.0).astype(BF16), h_ref[lanes, :])
        xg_s[pl.ds(r0, MOE_WIN), :] = rows.astype(BF16)
        w = jnp.sum(jnp.where(sel, comb_ref[pl.ds(e, 1), lanes], 0.0), axis=1, keepdims=True)
        wrow_s[pl.ds(r0, MOE_WIN), :] = jnp.broadcast_to(w, (MOE_WIN, LANES))

    most = cnt_ref[i * n_sub * N_EXPERTS + e]
    for sub in range(1, n_sub):
        most = jnp.maximum(most, cnt_ref[(i * n_sub + sub) * N_EXPERTS + e])
    for p in range(MOE_SUB // MOE_WIN - 1, 0, -1):
        @pl.when(most > p * MOE_WIN)
        def _():
            for sub in range(n_sub):
                @pl.when(cnt_ref[(i * n_sub + sub) * N_EXPERTS + e] > p * MOE_WIN)
                def _():
                    gather(sub, p)
    for sub in range(n_sub):
        gather(sub, 0)

    total = pl.multiple_of(tot_ref[i * N_EXPERTS + e], PACK_ROWS)
    base = pl.multiple_of(base_ref[i * N_EXPERTS + e], PACK_ROWS)
    xg_s[pl.ds(total, MOE_TAIL), :] = jnp.zeros((MOE_TAIL, d), BF16)
    wrow_s[pl.ds(total, MOE_TAIL), :] = jnp.zeros((MOE_TAIL, LANES), F32)

    def ffn(r0, rows):
        xc = xg_s[pl.ds(r0, rows), :]
        a = (_silu(_dot(xc, wg_ref[...])) * _dot(xc, wu_ref[...])).astype(BF16)
        y = _dot(a, wdn_ref[...]) * jnp.tile(wrow_s[pl.ds(r0, rows), :], (1, d // LANES))
        yw_s[pl.ds(base + r0, rows), :] = y.astype(BF16)

    n_full = total // MOE_CHUNK
    rem = total - n_full * MOE_CHUNK
    n_chunks = n_full + jnp.where(rem > MOE_CHUNK // 2, 1, 0)
    ffn(0, MOE_CHUNK)

    def full_chunk(c, carry):
        ffn(pl.multiple_of(c * MOE_CHUNK, MOE_CHUNK), MOE_CHUNK)
        return carry

    lax.fori_loop(1, n_chunks, full_chunk, 0)
    half = (rem > 0) & (rem <= MOE_CHUNK // 2) & (n_full > 0)

    @pl.when(half)
    def _():
        ffn(pl.multiple_of(n_full * MOE_CHUNK, MOE_CHUNK), MOE_CHUNK // 2)

    covered = jnp.maximum(n_chunks, 1) * MOE_CHUNK + jnp.where(half, MOE_CHUNK // 2, 0)
    yw_s[pl.ds(pl.multiple_of(base + covered, PACK_ROWS), MOE_TAIL), :] = jnp.zeros((MOE_TAIL, d), BF16)

    @pl.when(e == n_e - 1)
    def _():
        row0 = i * (n_sub * MOE_SUB)

        def rows_of(ref, sub):
            return ref.at[pl.ds(pl.multiple_of(row0 + sub * MOE_SUB, MOE_SUB), MOE_SUB), :]

        def x1_copy(sub):
            return pltpu.make_async_copy(rows_of(x1_hbm, sub), x1_s.at[sub % 2], io_sem.at[0, sub % 2])

        def out_copy(sub):
            return pltpu.make_async_copy(out_s.at[sub % 2], rows_of(o_hbm, sub), io_sem.at[1, sub % 2])

        x1_copy(0).start()
        for sub in range(n_sub):
            if sub + 1 < n_sub:
                x1_copy(sub + 1).start()
            lanes = slice(sub * MOE_SUB, (sub + 1) * MOE_SUB)
            first = (i * n_sub + sub) * N_EXPERTS

            def windows(p):
                sels, rows = [], []
                for ex in range(N_EXPERTS):
                    r0 = pl.multiple_of(base_ref[i * N_EXPERTS + ex] + off_ref[first + ex] + p * MOE_WIN, PACK_ROWS)
                    pos = pos_ref[ex:ex + 1, lanes].astype(jnp.int32)
                    sels.append(jnp.where(win_row + p * MOE_WIN == pos, 1.0, 0.0).astype(BF16))
                    rows.append(yw_s[pl.ds(r0, MOE_WIN), :])
                return _dot_tn(jnp.concatenate(sels, axis=0), jnp.concatenate(rows, axis=0))

            acc_s[...] = windows(0)
            most = cnt_ref[first]
            for ex in range(1, N_EXPERTS):
                most = jnp.maximum(most, cnt_ref[first + ex])
            for p in range(1, MOE_SUB // MOE_WIN):
                @pl.when(most > p * MOE_WIN)
                def _():
                    acc_s[...] += windows(p)
            x1_copy(sub).wait()
            x2 = x1_s[sub % 2] + gt2_ref[0] * acc_s[...]
            if final_norm:
                x2 = x2 * lax.rsqrt(jnp.mean(x2 * x2, axis=-1, keepdims=True) + EPS) * fg_ref[...]
            if sub >= 2:
                out_copy(sub - 2).wait()
            out_s[sub % 2] = x2
            out_copy(sub).start()
        for sub in range(max(n_sub - 2, 0), n_sub):
            out_copy(sub).wait()


def _dense_ffn(x, ys, w_out, mods, g, final_g, wg, wu, wdn, s, final_norm):
    t, d = x.shape
    tm = TM_FFN
    assert s % tm == 0, "a tile must not straddle two sequences (one adaLN row set per tile)"
    ff = wg.shape[1]
    gt1, sc, sh, gt2 = mods
    tok = lambda width: pl.BlockSpec((tm, width), lambda i: (i, 0))
    const = lambda shape: pl.BlockSpec(shape, lambda i: (0,) * len(shape), pipeline_mode=pl.Buffered(1))
    vec = pl.BlockSpec((1, 1, d), lambda i: ((i * tm) // s, 0, 0))
    return pl.pallas_call(
        functools.partial(_dense_ffn_kernel, final_norm=final_norm),
        grid=(t // tm,),
        in_specs=[tok(d), tok(ATTN_QW), tok(DN_W), tok(CV_C), const((ATTN_QW, d)), const((DN_W, d)), const((CV_C, d)),
                  vec, const((1, d)), vec, vec, vec, const((1, d)),
                  const((d, ff)), const((d, ff)), const((ff, d))],
        out_specs=tok(d),
        out_shape=jax.ShapeDtypeStruct((t, d), F32),
        compiler_params=pltpu.CompilerParams(vmem_limit_bytes=VMEM_LIMIT_BIG),
        name="dense_ffn",
    )(x, *ys, *w_out, gt1, g, sc, sh, gt2, final_g, wg, wu, wdn)


def _moe_sparse(h, x1, comb, pos, cnt, gt2, final_g, wg, wu, wdn, s, final_norm):
    t, d = x1.shape
    ts = TS_MOE
    n_e, _, ffe = wg.shape
    n_sub = ts // MOE_SUB
    assert s % ts == 0 and t % ts == 0, "a super-tile must not straddle two sequences (one adaLN gate row per tile)"
    counts = cnt[:, :, :TM_ROUTE // MOE_SUB].transpose(0, 2, 1).reshape(t // ts, n_sub, n_e)
    padded = -(-counts // PACK_ROWS) * PACK_ROWS
    offs = jnp.cumsum(padded, axis=1) - padded
    tots = jnp.sum(padded, axis=1)
    bases = jnp.cumsum(tots, axis=1) - tots
    cap_one = -(-(ts + n_sub * PACK_ROWS + MOE_TAIL) // PACK_ROWS) * PACK_ROWS
    cap_all = -(-(N_TOP * ts + n_e * n_sub * PACK_ROWS + MOE_CHUNK + MOE_TAIL) // PACK_ROWS) * PACK_ROWS
    tok = lambda width: pl.BlockSpec((ts, width), lambda i, e, *_: (i, 0))
    col = pl.BlockSpec((n_e, ts), lambda i, e, *_: (0, i))
    hbm = pl.BlockSpec(memory_space=pl.ANY)
    assert ffe == d, "the weight ring holds the three expert matrices in one (3, d, ffe) slot"
    flat = lambda a: a.reshape(-1).astype(jnp.int32)
    grid_spec = pltpu.PrefetchScalarGridSpec(
        num_scalar_prefetch=4,
        grid=(t // ts, n_e),
        in_specs=[tok(d), hbm, col, col,
                  pl.BlockSpec((1, 1, d), lambda i, e, *_: ((i * ts) // s, 0, 0)),
                  pl.BlockSpec((1, d), lambda i, e, *_: (0, 0)),
                  hbm, hbm, hbm],
        out_specs=hbm,
        scratch_shapes=[pltpu.VMEM((cap_one, d), BF16), pltpu.VMEM((cap_one, LANES), F32),
                        pltpu.VMEM((cap_all, d), BF16),
                        pltpu.VMEM((W_SLOTS, 3, d, ffe), BF16), pltpu.SemaphoreType.DMA((W_SLOTS, 3)),
                        pltpu.VMEM((MOE_SUB, d), F32), pltpu.VMEM((2, MOE_SUB, d), F32), pltpu.VMEM((2, MOE_SUB, d), F32),
                        pltpu.SemaphoreType.DMA((2, 2))],
    )
    return pl.pallas_call(
        functools.partial(_moe_sparse_kernel, final_norm=final_norm),
        grid_spec=grid_spec,
        out_shape=jax.ShapeDtypeStruct((t, d), F32),
        compiler_params=pltpu.CompilerParams(vmem_limit_bytes=VMEM_LIMIT_BIG,
                                             dimension_semantics=("arbitrary", "arbitrary")),
        name="moe_sparse",
    )(flat(counts), flat(offs), flat(tots), flat(bases), h, x1, comb, pos, gt2, final_g, wg, wu, wdn)


def _rope_cos_sin(positions):
    inv_freq = ROPE_THETA ** (-jnp.arange(0, ROT_DIM, 2, dtype=F32) / ROT_DIM)
    ang = positions.astype(F32)[:, None, :] * inv_freq[None, :, None]
    return jnp.concatenate([jnp.cos(ang), jnp.sin(ang)], axis=1)


def _pack_w_in(w):
    d = w.shape[0]
    o1 = ATTN_W
    o2 = o1 + 3 * DN_W
    o3 = o2 + 2 * DN_H
    o4 = o3 + DN_W
    gates = jnp.zeros((d, LANES), w.dtype).at[:, :2 * DN_H].set(w[:, o2:o3])
    return jnp.concatenate([w[:, :o2], gates, w[:, o3:o4], w[:, o4:]], axis=1).astype(BF16)


def kernel(x, c, positions, ada_w, ada_b, norm_mix_g, norm_ffn_g, w_in, attn_sinks, dn_conv_w, dn_a_log, dn_dt_bias, dn_norm_g, cv_dw_w, cv_dw_b, cv_ln_g, cv_ln_b, w_out, ffn_w_gate, ffn_w_up, ffn_w_down, router_w, router_b, moe_w_gate, moe_w_up, moe_w_down, final_norm_g):
    b, s, d = x.shape
    depth = w_in.shape[0]
    mod = _adaln(c, ada_w, ada_b)
    cos_sin = _rope_cos_sin(positions)
    final_g = final_norm_g.reshape(1, d)
    vec = lambda t: t.reshape(b, 1, d)
    for l in range(depth):
        sh1, sc1, gt1 = (vec(t) for t in jnp.split(mod[l, 0], 3, axis=-1))
        sh2, sc2, gt2 = (vec(t) for t in jnp.split(mod[l, 1], 3, axis=-1))
        attn, dnqkv, gate, dz, cv = _inproj(x, norm_mix_g[l].reshape(1, d), sc1, sh1, cos_sin, _pack_w_in(w_in[l]))
        y_attn = _attention(attn, attn_sinks[l].astype(F32))
        y_dn = _deltanet(dnqkv, gate, dz, dn_conv_w[l], dn_a_log[l], dn_dt_bias[l], dn_norm_g[l])
        y_cv = _conformer(cv, cv_dw_w[l], cv_dw_b[l], cv_ln_g[l], cv_ln_b[l])
        t = b * s
        ys = (y_attn.reshape(t, ATTN_QW), y_dn.reshape(t, DN_W), y_cv.reshape(t, CV_C))
        wo = w_out[l].astype(BF16)
        wos = (wo[:ATTN_QW], wo[ATTN_QW:ATTN_QW + DN_W], wo[ATTN_QW + DN_W:])
        mods = (gt1, sc2, sh2, gt2)
        g2 = norm_ffn_g[l].reshape(1, d)
        last = l == depth - 1
        j = l // 2
        if l % 2 == 0:
            x2 = _dense_ffn(x.reshape(t, d), ys, wos, mods, g2, final_g, ffn_w_gate[j].astype(BF16),
                            ffn_w_up[j].astype(BF16), ffn_w_down[j].astype(BF16), s, last)
        else:
            x1, h2, comb, pos, cnt = _route(x.reshape(t, d), ys, wos, mods, g2, router_w[j].astype(F32).T,
                                            router_b[j].astype(F32).reshape(N_EXPERTS, 1), s)
            x2 = _moe_sparse(h2, x1, comb, pos, cnt, gt2, final_g, moe_w_gate[j].astype(BF16),
                             moe_w_up[j].astype(BF16), moe_w_down[j].astype(BF16), s, last)
        x = x2.reshape(b, s, d)
    return x
```

```python
import functools

import numpy as np
import jax
import jax.numpy as jnp
from jax import lax
from jax.experimental import pallas as pl
from jax.experimental.pallas import tpu as pltpu

F32 = jnp.float32
BF16 = jnp.bfloat16
HIGHEST = lax.Precision.HIGHEST

HEAD_DIM = 64
ATTN_HQ = 8
ATTN_HKV = 2
ATTN_GROUP = ATTN_HQ // ATTN_HKV
ATTN_BLOCK = 128
ROT_DIM = HEAD_DIM // 4
ROPE_THETA = 500000.0
DN_H = 4
DN_D = 64
DN_CONV = 4
DN_CHUNK = 64
CV_C = 256
CV_K = 31
ATTN_QW = ATTN_HQ * HEAD_DIM
ATTN_KVW = ATTN_HKV * HEAD_DIM
ATTN_W = ATTN_QW + 2 * ATTN_KVW
DN_W = DN_H * DN_D
N_EXPERTS = 8
EPS = 1e-6
LANES = 128
SUBLANES = 8
PACK_ROWS = 16
VMEM_LIMIT = 48 * 1024 * 1024
VMEM_LIMIT_BIG = 56 * 1024 * 1024

ADA_TN = 3072
TM_PROJ = 1024
TQ_ATTN = 2048
ATTN_GROUP_UNITS = 64
R_DN = 512
R_CV = 2048
TM_FFN = 1024
TM_ROUTE = 1024
FFN_ROWS = 256
FF_PIECE = 1536
CV_HALO = 32
DN_HALO = 8
TS_MOE = 2048
MOE_SUB = 256
MOE_WIN = 128
MOE_CHUNK = 256
MOE_TAIL = 384
N_TOP = 2
W_SLOTS = 3


def _dot(a, b, precision=None):
    return jnp.dot(a, b, preferred_element_type=F32, precision=precision)


def _dot_nt(a, b):
    return lax.dot_general(a, b, (((1,), (1,)), ((), ())), preferred_element_type=F32)


def _dot_tn(a, b):
    return lax.dot_general(a, b, (((0,), (0,)), ((), ())), preferred_element_type=F32)


def _split3(x):
    hi = x.astype(BF16)
    rest = x - hi.astype(F32)
    mid = rest.astype(BF16)
    return hi, mid, (rest - mid.astype(F32)).astype(BF16)


def _sigmoid(x):
    return 1.0 / (1.0 + jnp.exp(-x))


def _silu(x):
    return x * _sigmoid(x)


def _rms_modulate(x, g, sc, sh):
    y = x * lax.rsqrt(jnp.mean(x * x, axis=-1, keepdims=True) + EPS)
    return (y * g) * (1.0 + sc) + sh


def _adaln_kernel(c_ref, w_ref, b_ref, o_ref):
    o_ref[0] = _dot(_silu(c_ref[...]), w_ref[0], HIGHEST) + b_ref[0]


def _adaln(c, ada_w, ada_b):
    depth, _, d, d3 = ada_w.shape
    b = c.shape[0]
    rows = SUBLANES
    cp = jnp.zeros((rows, d), F32).at[:b].set(c)
    w = ada_w.reshape(depth * 2, d, d3)
    bias = ada_b.reshape(depth * 2, 1, d3)
    tn = ADA_TN
    out = pl.pallas_call(
        _adaln_kernel,
        grid=(depth * 2, d3 // tn),
        in_specs=[
            pl.BlockSpec((rows, d), lambda i, j: (0, 0)),
            pl.BlockSpec((1, d, tn), lambda i, j: (i, 0, j)),
            pl.BlockSpec((1, 1, tn), lambda i, j: (i, 0, j)),
        ],
        out_specs=pl.BlockSpec((1, rows, tn), lambda i, j: (i, 0, j)),
        out_shape=jax.ShapeDtypeStruct((depth * 2, rows, d3), F32),
        compiler_params=pltpu.CompilerParams(vmem_limit_bytes=VMEM_LIMIT),
        name="adaln",
    )(cp, w, bias)
    return out[:, :b].reshape(depth, 2, b, d3)


def _inproj_kernel(x_ref, g_ref, sc_ref, sh_ref, cs_ref, place_ref, ones_ref, w_ref,
                   attn_ref, dnqkv_ref, gate_ref, dz_ref, cv_ref):
    h = _rms_modulate(x_ref[0], g_ref[...], sc_ref[0], sh_ref[0]).astype(BF16)
    a = _dot(h, w_ref[:, 0:ATTN_W])
    tab = _dot_tn(jnp.concatenate(_split3(cs_ref[0]), axis=0), place_ref[...]) + ones_ref[...]
    rc, ra, rb = tab[:, 0:LANES], tab[:, LANES:2 * LANES], tab[:, 2 * LANES:3 * LANES]
    half = ROT_DIM // 2
    for j in range((ATTN_QW + ATTN_KVW) // LANES):
        t = a[:, j * LANES:(j + 1) * LANES]
        r = t * rc + pltpu.roll(t, LANES - half, 1) * ra + pltpu.roll(t, half, 1) * rb
        if j < ATTN_QW // LANES:
            r = r * (HEAD_DIM ** -0.5)
        attn_ref[0, :, j * LANES:(j + 1) * LANES] = r.astype(BF16)
    attn_ref[0, :, ATTN_QW + ATTN_KVW:ATTN_W] = a[:, ATTN_QW + ATTN_KVW:ATTN_W].astype(BF16)
    o = ATTN_W
    dnqkv_ref[0] = _dot(h, w_ref[:, o:o + 3 * DN_W])
    o += 3 * DN_W
    gate_ref[0] = _dot(h, w_ref[:, o:o + LANES])
    o += LANES
    dz_ref[0] = _dot(h, w_ref[:, o:o + DN_W])
    o += DN_W
    cv_ref[0] = _dot(h, w_ref[:, o:o + 2 * CV_C])


def _rope_placement():
    half = ROT_DIM // 2
    place = np.zeros((2 * half, 3 * LANES), np.float32)
    ones = np.zeros((1, 3 * LANES), np.float32)
    for lane in range(LANES):
        dim = lane % HEAD_DIM
        if dim < ROT_DIM:
            place[dim % half, lane] = 1.0
            if dim < half:
                place[half + dim, LANES + lane] = -1.0
            else:
                place[half + dim - half, 2 * LANES + lane] = 1.0
        else:
            ones[0, lane] = 1.0
    return np.tile(place, (3, 1)), ones


def _inproj(x, g, sc, sh, cos_sin, w):
    b, s, d = x.shape
    tm = TM_PROJ
    n = w.shape[1]
    row = lambda width: pl.BlockSpec((1, tm, width), lambda bi, i: (bi, i, 0))
    vec = pl.BlockSpec((1, 1, d), lambda bi, i: (bi, 0, 0))
    const = lambda shape: pl.BlockSpec(shape, lambda bi, i: (0,) * len(shape))
    widths = (ATTN_W, 3 * DN_W, LANES, DN_W, 2 * CV_C)
    dtypes = (BF16, F32, F32, F32, F32)
    place, ones = _rope_placement()
    return pl.pallas_call(
        _inproj_kernel,
        grid=(b, s // tm),
        in_specs=[row(d), const((1, d)), vec, vec,
                  pl.BlockSpec((1, ROT_DIM, tm), lambda bi, i: (bi, 0, i)), const(place.shape), const(ones.shape),
                  const((d, n))],
        out_specs=[row(wd) for wd in widths],
        out_shape=[jax.ShapeDtypeStruct((b, s, wd), dt) for wd, dt in zip(widths, dtypes)],
        compiler_params=pltpu.CompilerParams(vmem_limit_bytes=VMEM_LIMIT),
        name="inproj",
    )(x, g, sc, sh, cos_sin, jnp.asarray(place, dtype=BF16), jnp.asarray(ones), w)


def _attn_kernel(sink_ref, cur_ref, prev_ref, o_ref):
    i = pl.program_id(1)
    blk = ATTN_BLOCK
    cur = cur_ref[0]
    prev = prev_ref[0]
    kv_all = jnp.concatenate([prev[:, ATTN_QW:], cur[:, ATTN_QW:]], axis=0)
    band_w = 2 * blk
    qi = lax.broadcasted_iota(jnp.int32, (blk, 2 * band_w), 0)
    kj = lax.broadcasted_iota(jnp.int32, (blk, 2 * band_w), 1) % band_w
    diff = qi + blk - kj
    band = (diff >= 0) & (diff < blk)
    band_first = band & (kj >= jnp.where(i == 0, blk, 0))
    first_head = lax.broadcasted_iota(jnp.int32, (blk, LANES), 1) < HEAD_DIM
    zeros = jnp.zeros((band_w, HEAD_DIM), BF16)

    def blockdiag(t):
        return jnp.concatenate([jnp.concatenate([t, zeros], axis=1), jnp.concatenate([zeros, t], axis=1)], axis=0)

    units = [(r, j, pr) for r in range(cur.shape[0] // blk) for j in range(ATTN_HKV) for pr in range(ATTN_GROUP // 2)]

    def scores(r, j, pr):
        kb = kv_all[r * blk:(r + 2) * blk, j * HEAD_DIM:(j + 1) * HEAD_DIM]
        slab = (j * (ATTN_GROUP // 2) + pr) * LANES
        return _dot_nt(cur[r * blk:(r + 1) * blk, slab:slab + LANES], blockdiag(kb))

    lane_row = lax.broadcasted_iota(jnp.int32, (1, 2 * band_w), 1)
    not_row0 = lax.broadcasted_iota(jnp.int32, (band_w, HEAD_DIM), 0) > 0
    def value_rows(r, j):
        vb = kv_all[r * blk:(r + 2) * blk, ATTN_KVW + j * HEAD_DIM:ATTN_KVW + (j + 1) * HEAD_DIM]
        return blockdiag(jnp.where(not_row0, vb, jnp.zeros_like(vb)))

    for g0 in range(0, len(units), ATTN_GROUP_UNITS):
        group = units[g0:g0 + ATTN_GROUP_UNITS]
        raw = [scores(*u) for u in group]
        sc = []
        for (r, j, pr), s_raw in zip(group, raw):
            hq = j * ATTN_GROUP + 2 * pr
            fill = jnp.where(lane_row == 0, sink_ref[hq], jnp.where(lane_row == band_w, sink_ref[hq + 1], -jnp.inf))
            sc.append(jnp.where(band_first if r == 0 else band, s_raw, fill))
        halves = [[s_[:, t * band_w:(t + 1) * band_w] for t in range(2)] for s_ in sc]
        mx = [[jnp.max(h_, axis=-1, keepdims=True) for h_ in hs] for hs in halves]
        p = [[jnp.exp(h_ - m_) for h_, m_ in zip(hs, ms)] for hs, ms in zip(halves, mx)]
        rs = [[1.0 / jnp.sum(p_, axis=-1, keepdims=True) for p_ in ps] for ps in p]
        p16 = [jnp.concatenate([p_.astype(BF16) for p_ in ps], axis=1) for ps in p]
        outs = [_dot(p16[n], value_rows(r, j)) for n, (r, j, pr) in enumerate(group)]
        for n, (r, j, pr) in enumerate(group):
            o = outs[n] * jnp.where(first_head, rs[n][0], rs[n][1])
            slab = (j * (ATTN_GROUP // 2) + pr) * LANES
            o_ref[0, r * blk:(r + 1) * blk, slab:slab + LANES] = o.astype(BF16)


def _attention(attn, sinks):
    b, s, _ = attn.shape
    tq = TQ_ATTN
    per = tq // ATTN_BLOCK
    return pl.pallas_call(
        _attn_kernel,
        grid=(b, s // tq),
        in_specs=[
            pl.BlockSpec(memory_space=pltpu.SMEM),
            pl.BlockSpec((1, tq, ATTN_W), lambda bi, i: (bi, i, 0)),
            pl.BlockSpec((1, ATTN_BLOCK, ATTN_W), lambda bi, i: (bi, jnp.maximum(i * per - 1, 0), 0)),
        ],
        out_specs=pl.BlockSpec((1, tq, ATTN_QW), lambda bi, i: (bi, i, 0)),
        out_shape=jax.ShapeDtypeStruct((b, s, ATTN_QW), BF16),
        compiler_params=pltpu.CompilerParams(vmem_limit_bytes=VMEM_LIMIT),
        name="attention",
    )(sinks, attn, attn)


def _deltanet_kernel(cur_ref, prev_ref, gate_ref, dz_ref, cw_ref, alog_ref, dtb_ref, ng_ref,
                     expand_ref, tril_ref, slow_ref, ones_ref, o_ref,
                     xext, q_s, k_s, v_s, b_s, g_s, o_s, state):
    i = pl.program_id(0)
    n_seq, rows = cur_ref.shape[0], cur_ref.shape[1]
    ck = DN_CHUNK

    @pl.when(i == 0)
    def _():
        state[...] = jnp.zeros_like(state)

    ones_blk = ones_ref[...]

    def l2n(t):
        ss = _dot((t * t).astype(BF16), ones_blk)
        return t * lax.rsqrt(ss + EPS)

    for bi in range(n_seq):
        seq = slice(bi * rows, (bi + 1) * rows)
        xext[bi, 0:DN_HALO, :] = jnp.where(i > 0, prev_ref[bi], 0.0)
        xext[bi, DN_HALO:DN_HALO + rows, :] = cur_ref[bi]
        acc = cw_ref[0:1, :] * xext[bi, pl.ds(DN_HALO - DN_CONV + 1, rows), :]
        for t in range(1, DN_CONV):
            acc = acc + cw_ref[t:t + 1, :] * xext[bi, pl.ds(DN_HALO - DN_CONV + 1 + t, rows), :]
        qkv = _silu(acc)
        q_s[seq, :] = l2n(qkv[:, 0:DN_W]) * (DN_D ** -0.5)
        k_s[seq, :] = l2n(qkv[:, DN_W:2 * DN_W])
        v_s[seq, :] = qkv[:, 2 * DN_W:3 * DN_W]
        ge = _dot(jnp.concatenate(_split3(gate_ref[bi]), axis=1), expand_ref[...])
        b_s[seq, :] = _sigmoid(ge[:, 0:DN_W])
        da = ge[:, DN_W:2 * DN_W] + dtb_ref[...]
        softplus = jnp.maximum(da, 0.0) + jnp.log1p(jnp.exp(-jnp.abs(da)))
        g_s[seq, :] = -jnp.exp(alog_ref[...]) * softplus

    tril = tril_ref[...]
    slow = slow_ref[...]
    ri = lax.broadcasted_iota(jnp.int32, (ck, DN_W), 0)
    ci = lax.broadcasted_iota(jnp.int32, (ck, DN_W), 1) % ck
    incl = ri >= ci

    pw = 2 * DN_D
    n_pair = DN_W // pw
    r2 = lax.broadcasted_iota(jnp.int32, (ck, pw), 0)
    l2 = lax.broadcasted_iota(jnp.int32, (ck, pw), 1)
    c2 = l2 % ck
    strict = r2 > c2
    eye = jnp.where(r2 == c2, 1.0, 0.0)
    first = l2 < DN_D
    masks = []
    blk = 1
    while blk < ck:
        masks.append((r2 // (2 * blk) == c2 // (2 * blk)) & ((r2 // blk) % 2 == 1) & ((c2 // blk) % 2 == 0))
        blk *= 2
    same_head = (lax.broadcasted_iota(jnp.int32, (pw, pw), 0) // DN_D) == (lax.broadcasted_iota(jnp.int32, (pw, pw), 1) // DN_D)

    def blockdiag(t):
        z = jnp.zeros_like(t)
        return jnp.concatenate([jnp.where(first, t, z), jnp.where(first, z, t)], axis=0)

    nc = n_seq * rows // ck
    per_seq = rows // ck
    slabs = [slice(p * pw, (p + 1) * pw) for p in range(n_pair)]
    kbeta16, k16, q16, dec, vbeta, kbg, qg, kg, egl = ([] for _ in range(9))
    for c in range(nc):
        r0 = c * ck
        q = q_s[r0:r0 + ck, :]
        k = k_s[r0:r0 + ck, :]
        beta = b_s[r0:r0 + ck, :]
        g = g_s[r0:r0 + ck, :]
        cums = _dot(tril, jnp.concatenate([jnp.concatenate(_split3(g), axis=0),
                                           jnp.concatenate(_split3(g * slow), axis=0)], axis=1))
        gc = cums[:, 0:DN_W]
        gdiff = cums[:, DN_W:]
        decay = jnp.exp(jnp.where(incl, gdiff, -jnp.inf))
        egc = jnp.exp(gc)
        glast = gc[ck - 1:ck, :]
        kbeta = k * beta
        per_slab = ((kbeta16, kbeta.astype(BF16)), (k16, k.astype(BF16)), (q16, q.astype(BF16)), (dec, decay),
                    (vbeta, (v_s[r0:r0 + ck, :] * beta).astype(BF16)), (kbg, (kbeta * egc).astype(BF16)),
                    (qg, q * egc), (kg, (k * jnp.exp(glast - gc)).astype(BF16)), (egl, jnp.exp(glast)))
        for dst, full in per_slab:
            dst.extend(full[:, sl] for sl in slabs)
    inst = range(nc * n_pair)
    bdk = [blockdiag(k16[n]) for n in inst]
    lower = [jnp.where(strict, _dot_nt(kbeta16[n], bdk[n]) * dec[n], 0.0) for n in inst]
    a_intra = [(_dot_nt(q16[n], bdk[n]) * dec[n]).astype(BF16) for n in inst]
    lower16 = [t.astype(BF16) for t in lower]
    tinv = [eye - jnp.where(masks[0], lower[n], 0.0) for n in inst]
    for m in masks[1:]:
        t16 = [t.astype(BF16) for t in tinv]
        x16 = [_dot(jnp.where(m, lower16[n], jnp.zeros_like(lower16[n])), blockdiag(t16[n])).astype(BF16) for n in inst]
        tinv = [tinv[n] - _dot(t16[n], blockdiag(x16[n])) for n in inst]
    t16 = [t.astype(BF16) for t in tinv]
    w16 = [_dot(t16[n], blockdiag(kbg[n])).astype(BF16) for n in inst]
    u16 = [_dot(t16[n], blockdiag(vbeta[n])).astype(BF16) for n in inst]
    kw16 = [jnp.where(same_head, _dot_tn(kg[n], w16[n]), 0.0).astype(BF16) for n in inst]
    ku = [jnp.where(same_head, _dot_tn(kg[n], u16[n]), 0.0) for n in inst]
    qeff16 = [(qg[n] - _dot(a_intra[n], blockdiag(w16[n]))).astype(BF16) for n in inst]
    au = [_dot(a_intra[n], blockdiag(u16[n])) for n in inst]
    st = [[state[bi, p] for p in range(n_pair)] for bi in range(n_seq)]
    for c in range(per_seq):
        for bi in range(n_seq):
            outs = []
            for p in range(n_pair):
                n = (bi * per_seq + c) * n_pair + p
                s16 = st[bi][p].astype(BF16)
                outs.append(_dot(qeff16[n], s16) + au[n])
                st[bi][p] = st[bi][p] * egl[n] - _dot(kw16[n], s16) + ku[n]
            r0 = (bi * per_seq + c) * ck
            o_s[r0:r0 + ck, :] = jnp.concatenate(outs, axis=1)
    for bi in range(n_seq):
        for p in range(n_pair):
            state[bi, p] = st[bi][p]
    for bi in range(n_seq):
        o = o_s[bi * rows:(bi + 1) * rows, :]
        ms = _dot((o * o).astype(BF16), ones_blk) * (1.0 / DN_D)
        y = o * lax.rsqrt(ms + EPS) * ng_ref[...]
        o_ref[bi] = (y * _silu(dz_ref[bi])).astype(BF16)


def _deltanet(dnqkv, gate, dz, conv_w, a_log, dt_bias, norm_g):
    b, s, _ = dnqkv.shape
    rows = R_DN
    ck = DN_CHUNK
    lane_head = np.arange(2 * DN_W) // DN_D
    expand = np.tile((np.arange(LANES)[:, None] == lane_head[None, :]).astype(np.float32), (3, 1))
    tril = np.tile(np.tril(np.ones((ck, ck), np.float32)), (1, 3))
    slow = (np.arange(ck)[:, None] > (np.arange(DN_W)[None, :] % ck)).astype(np.float32)
    ones_blk = (np.arange(DN_W)[:, None] // DN_D == np.arange(DN_W)[None, :] // DN_D).astype(np.float32)
    rep = lambda t: jnp.repeat(t.astype(F32), DN_D).reshape(1, DN_W)
    const = lambda shape: pl.BlockSpec(shape, lambda i: (0,) * len(shape))
    row = lambda width: pl.BlockSpec((b, rows, width), lambda i: (0, i, 0))
    per = rows // DN_HALO
    return pl.pallas_call(
        _deltanet_kernel,
        grid=(s // rows,),
        in_specs=[
            row(3 * DN_W),
            pl.BlockSpec((b, DN_HALO, 3 * DN_W), lambda i: (0, jnp.maximum(i * per - 1, 0), 0)),
            row(LANES), row(DN_W),
            const((DN_CONV, 3 * DN_W)), const((1, DN_W)), const((1, DN_W)), const((1, DN_W)),
            const((3 * LANES, 2 * DN_W)), const((ck, 3 * ck)), const((ck, DN_W)), const((DN_W, DN_W)),
        ],
        out_specs=row(DN_W),
        out_shape=jax.ShapeDtypeStruct((b, s, DN_W), BF16),
        scratch_shapes=[
            pltpu.VMEM((b, rows + DN_HALO, 3 * DN_W), F32),
            *[pltpu.VMEM((b * rows, DN_W), F32) for _ in range(6)],
            pltpu.VMEM((b, DN_W // (2 * DN_D), 2 * DN_D, 2 * DN_D), F32),
        ],
        compiler_params=pltpu.CompilerParams(vmem_limit_bytes=VMEM_LIMIT, dimension_semantics=("arbitrary",)),
        name="deltanet",
    )(dnqkv, dnqkv, gate, dz, conv_w, rep(a_log), rep(dt_bias), jnp.tile(norm_g.astype(F32), DN_H).reshape(1, DN_W),
      jnp.asarray(expand, dtype=BF16), jnp.asarray(tril, dtype=BF16), jnp.asarray(slow), jnp.asarray(ones_blk, dtype=BF16))


def _conformer_kernel(cur_ref, prev_ref, w_ref, b_ref, lg_ref, lb_ref, o_ref, u_s, sh_s):
    i = pl.program_id(1)
    rows = cur_ref.shape[1]
    prev = jnp.where(i > 0, prev_ref[0], 0.0)
    u_s[0:CV_HALO, :] = prev[:, 0:CV_C] * _sigmoid(prev[:, CV_C:])
    cur = cur_ref[0]
    u_s[CV_HALO:CV_HALO + rows, :] = cur[:, 0:CV_C] * _sigmoid(cur[:, CV_C:])
    span = rows + CV_HALO - SUBLANES
    for ph in range(1, SUBLANES):
        sh_s[ph - 1, 0:span, :] = u_s[pl.ds(ph, span), :]
    base = CV_HALO - CV_K + 1
    acc = b_ref[...]
    for t in range(CV_K):
        blk, ph = divmod(base + t, SUBLANES)
        src = u_s if ph == 0 else sh_s.at[ph - 1]
        acc = acc + w_ref[t:t + 1, :] * src[blk * SUBLANES:blk * SUBLANES + rows, :]
    mu = jnp.mean(acc, axis=-1, keepdims=True)
    xc = acc - mu
    y = xc * lax.rsqrt(jnp.mean(xc * xc, axis=-1, keepdims=True) + EPS)
    o_ref[0] = _silu(y * lg_ref[...] + lb_ref[...]).astype(BF16)


def _conformer(cv, w, bias, ln_g, ln_b):
    b, s, _ = cv.shape
    rows = R_CV
    per = rows // CV_HALO
    const = lambda shape: pl.BlockSpec(shape, lambda bi, i: (0,) * len(shape))
    return pl.pallas_call(
        _conformer_kernel,
        grid=(b, s // rows),
        in_specs=[
            pl.BlockSpec((1, rows, 2 * CV_C), lambda bi, i: (bi, i, 0)),
            pl.BlockSpec((1, CV_HALO, 2 * CV_C), lambda bi, i: (bi, jnp.maximum(i * per - 1, 0), 0)),
            const((CV_K, CV_C)), const((1, CV_C)), const((1, CV_C)), const((1, CV_C)),
        ],
        out_specs=pl.BlockSpec((1, rows, CV_C), lambda bi, i: (bi, i, 0)),
        out_shape=jax.ShapeDtypeStruct((b, s, CV_C), BF16),
        scratch_shapes=[pltpu.VMEM((rows + CV_HALO, CV_C), F32),
                        pltpu.VMEM((SUBLANES - 1, rows + CV_HALO - SUBLANES, CV_C), F32)],
        compiler_params=pltpu.CompilerParams(vmem_limit_bytes=VMEM_LIMIT),
        name="conformer",
    )(cv, cv, w, bias.reshape(1, CV_C), ln_g.reshape(1, CV_C), ln_b.reshape(1, CV_C))


def _dense_ffn_kernel(x_ref, ya_ref, yd_ref, yc_ref, wa_ref, wd_ref, wc_ref, gt1_ref, g_ref, sc_ref, sh_ref, gt2_ref,
                      fg_ref, wg_ref, wu_ref, wdn_ref, o_ref, *, final_norm):
    rows = [slice(r0, r0 + FFN_ROWS) for r0 in range(0, x_ref.shape[0], FFN_ROWS)]
    wa, wd, wc = wa_ref[...], wd_ref[...], wc_ref[...]
    y = [_dot(ya_ref[sl, :], wa) + _dot(yd_ref[sl, :], wd) + _dot(yc_ref[sl, :], wc) for sl in rows]
    x1 = [x_ref[sl, :] + gt1_ref[0] * y_ for sl, y_ in zip(rows, y)]
    h = [_rms_modulate(x_, g_ref[...], sc_ref[0], sh_ref[0]).astype(BF16) for x_ in x1]
    ff = wg_ref.shape[1]
    acc = [None] * len(rows)
    for c0 in range(0, ff, FF_PIECE):
        c1 = min(c0 + FF_PIECE, ff)
        a = [(_silu(_dot(h_, wg_ref[:, c0:c1])) * _dot(h_, wu_ref[:, c0:c1])).astype(BF16) for h_ in h]
        part = [_dot(a_, wdn_ref[c0:c1, :]) for a_ in a]
        acc = [p if q is None else q + p for p, q in zip(part, acc)]
    for sl, x_, acc_ in zip(rows, x1, acc):
        x2 = x_ + gt2_ref[0] * acc_
        if final_norm:
            x2 = x2 * lax.rsqrt(jnp.mean(x2 * x2, axis=-1, keepdims=True) + EPS) * fg_ref[...]
        o_ref[sl, :] = x2


def _route_kernel(x_ref, ya_ref, yd_ref, yc_ref, wa_ref, wd_ref, wc_ref, gt1_ref, g_ref, sc_ref, sh_ref,
                  rwt_ref, rb_ref, upper_ref, x1_ref, h_ref, comb_ref, pos_ref, cnt_ref):
    subs = [slice(n * MOE_SUB, (n + 1) * MOE_SUB) for n in range(x_ref.shape[0] // MOE_SUB)]
    wa, wd, wc = wa_ref[...], wd_ref[...], wc_ref[...]
    y = [_dot(ya_ref[sl, :], wa) + _dot(yd_ref[sl, :], wd) + _dot(yc_ref[sl, :], wc) for sl in subs]
    x1 = [x_ref[sl, :] + gt1_ref[0] * y_ for sl, y_ in zip(subs, y)]
    h = [_rms_modulate(x_, g_ref[...], sc_ref[0], sh_ref[0]) for x_ in x1]
    for sl, x_, h_ in zip(subs, x1, h):
        x1_ref[sl, :] = x_
        h_ref[sl, :] = h_.astype(BF16)
    w_hi, w_mid, _ = _split3(rwt_ref[...])
    w_cat = jnp.concatenate([w_hi, w_hi, w_mid], axis=1)
    pieces = [_split3(h_) for h_ in h]
    logits = [_dot_nt(w_cat, jnp.concatenate([hi, mid, hi], axis=1)) + rb_ref[...] for hi, mid, _ in pieces]
    row = lax.broadcasted_iota(jnp.int32, logits[0].shape, 0)
    lane = lax.broadcasted_iota(jnp.int32, (N_EXPERTS, LANES), 1)
    cnt = jnp.zeros((N_EXPERTS, LANES), F32)
    for n, (sl, lg) in enumerate(zip(subs, logits)):
        m1 = jnp.max(lg, axis=0, keepdims=True)
        i1 = jnp.min(jnp.where(lg == m1, row, N_EXPERTS), axis=0, keepdims=True)
        rest = jnp.where(row == i1, -jnp.inf, lg)
        m2 = jnp.max(rest, axis=0, keepdims=True)
        i2 = jnp.min(jnp.where(rest == m2, row, N_EXPERTS), axis=0, keepdims=True)
        e2 = jnp.exp(m2 - m1)
        comb_ref[:, sl] = jnp.where(row == i1, 1.0 / (1.0 + e2), 0.0) + jnp.where(row == i2, e2 / (1.0 + e2), 0.0)
        sel = jnp.where((row == i1) | (row == i2), 1.0, 0.0)
        rank = _dot(sel.astype(BF16), upper_ref[...])
        pos_ref[:, sl] = jnp.where(sel > 0.0, rank, -1.0)
        cnt = cnt + jnp.where(lane == n, jnp.sum(sel, axis=1, keepdims=True), 0.0)
    cnt_ref[0] = cnt.astype(jnp.int32)


def _route(x, ys, w_out, mods, g, rwt, rb, s):
    t, d = x.shape
    tm = TM_ROUTE
    assert s % tm == 0, "a tile must not straddle two sequences (one adaLN row set per tile)"
    gt1, sc, sh, _ = mods
    tok = lambda width: pl.BlockSpec((tm, width), lambda i: (i, 0))
    const = lambda shape: pl.BlockSpec(shape, lambda i: (0,) * len(shape))
    vec = pl.BlockSpec((1, 1, d), lambda i: ((i * tm) // s, 0, 0))
    col = pl.BlockSpec((N_EXPERTS, tm), lambda i: (0, i))
    upper = np.triu(np.ones((MOE_SUB, MOE_SUB), np.float32), 1)
    return pl.pallas_call(
        _route_kernel,
        grid=(t // tm,),
        in_specs=[tok(d), tok(ATTN_QW), tok(DN_W), tok(CV_C), const((ATTN_QW, d)), const((DN_W, d)), const((CV_C, d)),
                  vec, const((1, d)), vec, vec, const((N_EXPERTS, d)), const((N_EXPERTS, 1)), const((MOE_SUB, MOE_SUB))],
        out_specs=[tok(d), tok(d), col, col, pl.BlockSpec((1, N_EXPERTS, LANES), lambda i: (i, 0, 0))],
        out_shape=[jax.ShapeDtypeStruct((t, d), F32), jax.ShapeDtypeStruct((t, d), BF16),
                   jax.ShapeDtypeStruct((N_EXPERTS, t), F32), jax.ShapeDtypeStruct((N_EXPERTS, t), F32),
                   jax.ShapeDtypeStruct((t // tm, N_EXPERTS, LANES), jnp.int32)],
        compiler_params=pltpu.CompilerParams(vmem_limit_bytes=VMEM_LIMIT),
        name="route",
    )(x, *ys, *w_out, gt1, g, sc, sh, rwt, rb, jnp.asarray(upper, dtype=BF16))


def _moe_sparse_kernel(cnt_ref, off_ref, tot_ref, base_ref, h_ref, x1_hbm, comb_ref, pos_ref, gt2_ref, fg_ref,
                       wg_hbm, wu_hbm, wdn_hbm, o_hbm, xg_s, wrow_s, yw_s, w_s, w_sem, acc_s, x1_s, out_s, io_sem,
                       *, final_norm):
    i = pl.program_id(0)
    e = pl.program_id(1)
    n_e = pl.num_programs(1)
    n_sub = h_ref.shape[0] // MOE_SUB
    d = h_ref.shape[1]
    win_row = lax.broadcasted_iota(jnp.int32, (MOE_WIN, MOE_SUB), 0)

    step = i * n_e + e
    n_steps = pl.num_programs(0) * n_e

    def weight_copies(expert, slot):
        return [pltpu.make_async_copy(src.at[expert], w_s.at[slot, k], w_sem.at[slot, k])
                for k, src in enumerate((wg_hbm, wu_hbm, wdn_hbm))]

    @pl.when(step == 0)
    def _():
        for ahead in range(W_SLOTS - 1):
            for cp in weight_copies(ahead % N_EXPERTS, ahead):
                cp.start()

    @pl.when(step + (W_SLOTS - 1) < n_steps)
    def _():
        for cp in weight_copies(lax.rem(e + (W_SLOTS - 1), n_e), lax.rem(step + (W_SLOTS - 1), W_SLOTS)):
            cp.start()

    slot = lax.rem(step, W_SLOTS)
    for cp in weight_copies(e, slot):
        cp.wait()
    wg_ref, wu_ref, wdn_ref = (w_s.at[slot, k] for k in range(3))

    def gather(sub, p):
        k = (i * n_sub + sub) * N_EXPERTS + e
        lanes = slice(sub * MOE_SUB, (sub + 1) * MOE_SUB)
        pos = pos_ref[pl.ds(e, 1), lanes].astype(jnp.int32)
        sel = win_row + p * MOE_WIN == pos
        r0 = pl.multiple_of(off_ref[k] + p * MOE_WIN, PACK_ROWS)
        rows = _dot(jnp.where(sel, 1.0, 0.0).astype(BF16), h_ref[lanes, :])
        xg_s[pl.ds(r0, MOE_WIN), :] = rows.astype(BF16)
        w = jnp.sum(jnp.where(sel, comb_ref[pl.ds(e, 1), lanes], 0.0), axis=1, keepdims=True)
        wrow_s[pl.ds(r0, MOE_WIN), :] = jnp.broadcast_to(w, (MOE_WIN, LANES))

    most = cnt_ref[i * n_sub * N_EXPERTS + e]
    for sub in range(1, n_sub):
        most = jnp.maximum(most, cnt_ref[(i * n_sub + sub) * N_EXPERTS + e])
    for p in range(MOE_SUB // MOE_WIN - 1, 0, -1):
        @pl.when(most > p * MOE_WIN)
        def _():
            for sub in range(n_sub):
                @pl.when(cnt_ref[(i * n_sub + sub) * N_EXPERTS + e] > p * MOE_WIN)
                def _():
                    gather(sub, p)
    for sub in range(n_sub):
        gather(sub, 0)

    total = pl.multiple_of(tot_ref[i * N_EXPERTS + e], PACK_ROWS)
    base = pl.multiple_of(base_ref[i * N_EXPERTS + e], PACK_ROWS)
    xg_s[pl.ds(total, MOE_TAIL), :] = jnp.zeros((MOE_TAIL, d), BF16)
    wrow_s[pl.ds(total, MOE_TAIL), :] = jnp.zeros((MOE_TAIL, LANES), F32)

    def ffn(r0, rows):
        xc = xg_s[pl.ds(r0, rows), :]
        a = (_silu(_dot(xc, wg_ref[...])) * _dot(xc, wu_ref[...])).astype(BF16)
        y = _dot(a, wdn_ref[...]) * jnp.tile(wrow_s[pl.ds(r0, rows), :], (1, d // LANES))
        yw_s[pl.ds(base + r0, rows), :] = y.astype(BF16)

    n_full = total // MOE_CHUNK
    rem = total - n_full * MOE_CHUNK
    n_chunks = n_full + jnp.where(rem > MOE_CHUNK // 2, 1, 0)
    ffn(0, MOE_CHUNK)

    def full_chunk(c, carry):
        ffn(pl.multiple_of(c * MOE_CHUNK, MOE_CHUNK), MOE_CHUNK)
        return carry

    lax.fori_loop(1, n_chunks, full_chunk, 0)
    half = (rem > 0) & (rem <= MOE_CHUNK // 2) & (n_full > 0)

    @pl.when(half)
    def _():
        ffn(pl.multiple_of(n_full * MOE_CHUNK, MOE_CHUNK), MOE_CHUNK // 2)

    covered = jnp.maximum(n_chunks, 1) * MOE_CHUNK + jnp.where(half, MOE_CHUNK // 2, 0)
    yw_s[pl.ds(pl.multiple_of(base + covered, PACK_ROWS), MOE_TAIL), :] = jnp.zeros((MOE_TAIL, d), BF16)

    @pl.when(e == n_e - 1)
    def _():
        row0 = i * (n_sub * MOE_SUB)

        def rows_of(ref, sub):
            return ref.at[pl.ds(pl.multiple_of(row0 + sub * MOE_SUB, MOE_SUB), MOE_SUB), :]

        def x1_copy(sub):
            return pltpu.make_async_copy(rows_of(x1_hbm, sub), x1_s.at[sub % 2], io_sem.at[0, sub % 2])

        def out_copy(sub):
            return pltpu.make_async_copy(out_s.at[sub % 2], rows_of(o_hbm, sub), io_sem.at[1, sub % 2])

        x1_copy(0).start()
        for sub in range(n_sub):
            if sub + 1 < n_sub:
                x1_copy(sub + 1).start()
            lanes = slice(sub * MOE_SUB, (sub + 1) * MOE_SUB)
            first = (i * n_sub + sub) * N_EXPERTS

            def windows(p):
                sels, rows = [], []
                for ex in range(N_EXPERTS):
                    r0 = pl.multiple_of(base_ref[i * N_EXPERTS + ex] + off_ref[first + ex] + p * MOE_WIN, PACK_ROWS)
                    pos = pos_ref[ex:ex + 1, lanes].astype(jnp.int32)
                    sels.append(jnp.where(win_row + p * MOE_WIN == pos, 1.0, 0.0).astype(BF16))
                    rows.append(yw_s[pl.ds(r0, MOE_WIN), :])
                return _dot_tn(jnp.concatenate(sels, axis=0), jnp.concatenate(rows, axis=0))

            acc_s[...] = windows(0)
            most = cnt_ref[first]
            for ex in range(1, N_EXPERTS):
                most = jnp.maximum(most, cnt_ref[first + ex])
            for p in range(1, MOE_SUB // MOE_WIN):
                @pl.when(most > p * MOE_WIN)
                def _():
                    acc_s[...] += windows(p)
            x1_copy(sub).wait()
            x2 = x1_s[sub % 2] + gt2_ref[0] * acc_s[...]
            if final_norm:
                x2 = x2 * lax.rsqrt(jnp.mean(x2 * x2, axis=-1, keepdims=True) + EPS) * fg_ref[...]
            if sub >= 2:
                out_copy(sub - 2).wait()
            out_s[sub % 2] = x2
            out_copy(sub).start()
        for sub in range(max(n_sub - 2, 0), n_sub):
            out_copy(sub).wait()


def _dense_ffn(x, ys, w_out, mods, g, final_g, wg, wu, wdn, s, final_norm):
    t, d = x.shape
    tm = TM_FFN
    assert s % tm == 0, "a tile must not straddle two sequences (one adaLN row set per tile)"
    ff = wg.shape[1]
    gt1, sc, sh, gt2 = mods
    tok = lambda width: pl.BlockSpec((tm, width), lambda i: (i, 0))
    const = lambda shape: pl.BlockSpec(shape, lambda i: (0,) * len(shape), pipeline_mode=pl.Buffered(1))
    vec = pl.BlockSpec((1, 1, d), lambda i: ((i * tm) // s, 0, 0))
    return pl.pallas_call(
        functools.partial(_dense_ffn_kernel, final_norm=final_norm),
        grid=(t // tm,),
        in_specs=[tok(d), tok(ATTN_QW), tok(DN_W), tok(CV_C), const((ATTN_QW, d)), const((DN_W, d)), const((CV_C, d)),
                  vec, const((1, d)), vec, vec, vec, const((1, d)),
                  const((d, ff)), const((d, ff)), const((ff, d))],
        out_specs=tok(d),
        out_shape=jax.ShapeDtypeStruct((t, d), F32),
        compiler_params=pltpu.CompilerParams(vmem_limit_bytes=VMEM_LIMIT_BIG),
        name="dense_ffn",
    )(x, *ys, *w_out, gt1, g, sc, sh, gt2, final_g, wg, wu, wdn)


def _moe_sparse(h, x1, comb, pos, cnt, gt2, final_g, wg, wu, wdn, s, final_norm):
    t, d = x1.shape
    ts = TS_MOE
    n_e, _, ffe = wg.shape
    n_sub = ts // MOE_SUB
    assert s % ts == 0 and t % ts == 0, "a super-tile must not straddle two sequences (one adaLN gate row per tile)"
    counts = cnt[:, :, :TM_ROUTE // MOE_SUB].transpose(0, 2, 1).reshape(t // ts, n_sub, n_e)
    padded = -(-counts // PACK_ROWS) * PACK_ROWS
    offs = jnp.cumsum(padded, axis=1) - padded
    tots = jnp.sum(padded, axis=1)
    bases = jnp.cumsum(tots, axis=1) - tots
    cap_one = -(-(ts + n_sub * PACK_ROWS + MOE_TAIL) // PACK_ROWS) * PACK_ROWS
    cap_all = -(-(N_TOP * ts + n_e * n_sub * PACK_ROWS + MOE_CHUNK + MOE_TAIL) // PACK_ROWS) * PACK_ROWS
    tok = lambda width: pl.BlockSpec((ts, width), lambda i, e, *_: (i, 0))
    col = pl.BlockSpec((n_e, ts), lambda i, e, *_: (0, i))
    hbm = pl.BlockSpec(memory_space=pl.ANY)
    assert ffe == d, "the weight ring holds the three expert matrices in one (3, d, ffe) slot"
    flat = lambda a: a.reshape(-1).astype(jnp.int32)
    grid_spec = pltpu.PrefetchScalarGridSpec(
        num_scalar_prefetch=4,
        grid=(t // ts, n_e),
        in_specs=[tok(d), hbm, col, col,
                  pl.BlockSpec((1, 1, d), lambda i, e, *_: ((i * ts) // s, 0, 0)),
                  pl.BlockSpec((1, d), lambda i, e, *_: (0, 0)),
                  hbm, hbm, hbm],
        out_specs=hbm,
        scratch_shapes=[pltpu.VMEM((cap_one, d), BF16), pltpu.VMEM((cap_one, LANES), F32),
                        pltpu.VMEM((cap_all, d), BF16),
                        pltpu.VMEM((W_SLOTS, 3, d, ffe), BF16), pltpu.SemaphoreType.DMA((W_SLOTS, 3)),
                        pltpu.VMEM((MOE_SUB, d), F32), pltpu.VMEM((2, MOE_SUB, d), F32), pltpu.VMEM((2, MOE_SUB, d), F32),
                        pltpu.SemaphoreType.DMA((2, 2))],
    )
    return pl.pallas_call(
        functools.partial(_moe_sparse_kernel, final_norm=final_norm),
        grid_spec=grid_spec,
        out_shape=jax.ShapeDtypeStruct((t, d), F32),
        compiler_params=pltpu.CompilerParams(vmem_limit_bytes=VMEM_LIMIT_BIG,
                                             dimension_semantics=("arbitrary", "arbitrary")),
        name="moe_sparse",
    )(flat(counts), flat(offs), flat(tots), flat(bases), h, x1, comb, pos, gt2, final_g, wg, wu, wdn)


def _rope_cos_sin(positions):
    inv_freq = ROPE_THETA ** (-jnp.arange(0, ROT_DIM, 2, dtype=F32) / ROT_DIM)
    ang = positions.astype(F32)[:, None, :] * inv_freq[None, :, None]
    return jnp.concatenate([jnp.cos(ang), jnp.sin(ang)], axis=1)


def _pack_w_in(w):
    d = w.shape[0]
    o1 = ATTN_W
    o2 = o1 + 3 * DN_W
    o3 = o2 + 2 * DN_H
    o4 = o3 + DN_W
    gates = jnp.zeros((d, LANES), w.dtype).at[:, :2 * DN_H].set(w[:, o2:o3])
    return jnp.concatenate([w[:, :o2], gates, w[:, o3:o4], w[:, o4:]], axis=1).astype(BF16)


def kernel(x, c, positions, ada_w, ada_b, norm_mix_g, norm_ffn_g, w_in, attn_sinks, dn_conv_w, dn_a_log, dn_dt_bias, dn_norm_g, cv_dw_w, cv_dw_b, cv_ln_g, cv_ln_b, w_out, ffn_w_gate, ffn_w_up, ffn_w_down, router_w, router_b, moe_w_gate, moe_w_up, moe_w_down, final_norm_g):
    b, s, d = x.shape
    depth = w_in.shape[0]
    mod = _adaln(c, ada_w, ada_b)
    cos_sin = _rope_cos_sin(positions)
    final_g = final_norm_g.reshape(1, d)
    vec = lambda t: t.reshape(b, 1, d)
    for l in range(depth):
        sh1, sc1, gt1 = (vec(t) for t in jnp.split(mod[l, 0], 3, axis=-1))
        sh2, sc2, gt2 = (vec(t) for t in jnp.split(mod[l, 1], 3, axis=-1))
        attn, dnqkv, gate, dz, cv = _inproj(x, norm_mix_g[l].reshape(1, d), sc1, sh1, cos_sin, _pack_w_in(w_in[l]))
        y_attn = _attention(attn, attn_sinks[l].astype(F32))
        y_dn = _deltanet(dnqkv, gate, dz, dn_conv_w[l], dn_a_log[l], dn_dt_bias[l], dn_norm_g[l])
        y_cv = _conformer(cv, cv_dw_w[l], cv_dw_b[l], cv_ln_g[l], cv_ln_b[l])
        t = b * s
        ys = (y_attn.reshape(t, ATTN_QW), y_dn.reshape(t, DN_W), y_cv.reshape(t, CV_C))
        wo = w_out[l].astype(BF16)
        wos = (wo[:ATTN_QW], wo[ATTN_QW:ATTN_QW + DN_W], wo[ATTN_QW + DN_W:])
        mods = (gt1, sc2, sh2, gt2)
        g2 = norm_ffn_g[l].reshape(1, d)
        last = l == depth - 1
        j = l // 2
        if l % 2 == 0:
            x2 = _dense_ffn(x.reshape(t, d), ys, wos, mods, g2, final_g, ffn_w_gate[j].astype(BF16),
                            ffn_w_up[j].astype(BF16), ffn_w_down[j].astype(BF16), s, last)
        else:
            x1, h2, comb, pos, cnt = _route(x.reshape(t, d), ys, wos, mods, g2, router_w[j].astype(F32).T,
                                            router_b[j].astype(F32).reshape(N_EXPERTS, 1), s)
            x2 = _moe_sparse(h2, x1, comb, pos, cnt, gt2, final_g, moe_w_gate[j].astype(BF16),
                             moe_w_up[j].astype(BF16), moe_w_down[j].astype(BF16), s, last)
        x = x2.reshape(b, s, d)
    return x
```
